```python
import jax, jax.numpy as jnp
from jax import lax
import numpy as np

D_MODEL = 2048
BATCH = 16
SEQ = 2048
DEPTH = 2

PLE_DIM = 256
FOX_HEADS = 16
FOX_HEAD_DIM = 64
FOX_W = FOX_HEADS * FOX_HEAD_DIM
Q_BLOCK = 128
RWKV_HEADS = 16
RWKV_HEAD_SIZE = 64
RWKV_W = RWKV_HEADS * RWKV_HEAD_SIZE
RWKV_DECAY_LORA = 64
RWKV_ICLR_LORA = 64
RWKV_SHIFT_COLS = 3 * RWKV_W + RWKV_DECAY_LORA + RWKV_ICLR_LORA
AB_COLS = 4 * FOX_W + FOX_HEADS + RWKV_SHIFT_COLS + RWKV_W
AB_OUT = FOX_W + RWKV_W
GMLP_W = D_MODEL
GMLP_GROUPS = 16
GMLP_GROUP_CH = GMLP_W // GMLP_GROUPS
GMLP_CHUNK = 128
C_COLS = 3 * GMLP_W
N_AB_LAYERS = (DEPTH + 1) // 2
N_C_LAYERS = DEPTH // 2
RMS_EPS = 1e-6
LN_EPS = 1e-5
RWKV_GN_EPS = 64e-5

kernel_name = 'hybrid_fox_rwkv7_gmlp_sandwich_ple'


def _split(x, sizes):
    offs = []
    o = 0
    for s in sizes[:-1]:
        o += s
        offs.append(o)
    return jnp.split(x, offs, axis=-1)


def rms_norm(x, g):
    xf = x.astype(jnp.float32)
    y = xf * lax.rsqrt(jnp.mean(xf * xf, axis=-1, keepdims=True) + RMS_EPS)
    return (y * g.astype(jnp.float32)).astype(x.dtype)


def layer_norm(x, g, b):
    xf = x.astype(jnp.float32)
    mu = jnp.mean(xf, axis=-1, keepdims=True)
    var = jnp.mean(jnp.square(xf - mu), axis=-1, keepdims=True)
    y = (xf - mu) * lax.rsqrt(var + LN_EPS) * g.astype(jnp.float32) + b.astype(jnp.float32)
    return y.astype(x.dtype)


def forgetting_attention(q, k, v, log_f):
    T = q.shape[1]
    c = jnp.cumsum(log_f, axis=1).transpose(0, 2, 1)
    scale = FOX_HEAD_DIM ** -0.5
    outs = []
    for blk in range(T // Q_BLOCK):
        q0 = blk * Q_BLOCK
        q1 = q0 + Q_BLOCK
        s = jnp.einsum('bqhd,bkhd->bhqk', q[:, q0:q1], k[:, :q1]).astype(jnp.float32) * scale
        bias = c[:, :, q0:q1, None] - c[:, :, None, :q1]
        causal = (q0 + jnp.arange(Q_BLOCK))[:, None] >= jnp.arange(q1)[None, :]
        s = jnp.where(causal, s + bias, -jnp.inf)
        pr = jax.nn.softmax(s, axis=-1)
        outs.append(jnp.einsum('bhqk,bkhd->bqhd', pr.astype(v.dtype), v[:, :q1]))
    return jnp.concatenate(outs, axis=1)


def rwkv7_time_mix(sh, mu, w0, w2, a0, a2, k_k, k_a, r_k, ln_g, ln_b):
    dt = sh.dtype
    B, T, _ = sh.shape
    H, N = RWKV_HEADS, RWKV_HEAD_SIZE
    prev = jnp.pad(sh, ((0, 0), (1, 0), (0, 0)))[:, :-1]
    sh = (sh + (prev - sh) * mu).astype(jnp.float32)
    r, k, v, w_lo, a_lo = _split(sh, (RWKV_W, RWKV_W, RWKV_W, RWKV_DECAY_LORA, RWKV_ICLR_LORA))
    w_raw = -jax.nn.softplus(-(w0.astype(jnp.float32) + jnp.tanh(w_lo) @ w2.astype(jnp.float32))) - 0.5
    decay = jnp.exp(-jnp.exp(w_raw))
    a = jax.nn.sigmoid(a0.astype(jnp.float32) + a_lo @ a2.astype(jnp.float32))
    kk = (k * k_k.astype(jnp.float32)).reshape(B, T, H, N)
    kk = kk / jnp.maximum(jnp.sqrt(jnp.sum(kk * kk, axis=-1, keepdims=True)), 1e-12)
    k = k * (1.0 + (a - 1.0) * k_a.astype(jnp.float32))
    heads = lambda z: z.reshape(B, T, H, N)
    r_h, k_h, v_h, a_h, w_h = heads(r), heads(k), heads(v), heads(a), heads(decay)
    xs = tuple(z.transpose(1, 0, 2, 3) for z in (r_h, w_h, k_h, v_h, kk, a_h))

    def step(S, inp):
        r_t, w_t, k_t, v_t, kk_t, a_t = inp
        sa = jnp.einsum('bhij,bhj->bhi', S, kk_t)
        S = S * w_t[:, :, None, :] - sa[..., None] * (kk_t * a_t)[:, :, None, :] + v_t[..., None] * k_t[:, :, None, :]
        y = jnp.einsum('bhij,bhj->bhi', S, r_t)
        return S, y

    S0 = jnp.zeros((B, H, N, N), jnp.float32)
    _, y = lax.scan(step, S0, xs)
    y = y.transpose(1, 0, 2, 3)
    mean = jnp.mean(y, axis=-1, keepdims=True)
    var = jnp.mean(jnp.square(y - mean), axis=-1, keepdims=True)
    y = ((y - mean) * lax.rsqrt(var + RWKV_GN_EPS)).reshape(B, T, RWKV_W)
    y = y * ln_g.astype(jnp.float32) + ln_b.astype(jnp.float32)
    bonus = jnp.sum(r_h * k_h * r_k.astype(jnp.float32), axis=-1, keepdims=True) * v_h
    y = y + bonus.reshape(B, T, RWKV_W)
    return y.astype(dt)


def chunked_spatial_gating(u, v, ln_g, ln_b, w_s, b_s):
    B, T, _ = v.shape
    u = jax.nn.gelu(u, approximate=False)
    v = layer_norm(jax.nn.gelu(v, approximate=False), ln_g, ln_b)
    vc = v.reshape(B, T // GMLP_CHUNK, GMLP_CHUNK, GMLP_GROUPS, GMLP_GROUP_CH)
    causal = jnp.tril(jnp.ones((GMLP_CHUNK, GMLP_CHUNK), w_s.dtype))
    mixed = jnp.einsum('gts,bnsgc->bntgc', w_s * causal, vc) + b_s.T[:, :, None]
    return u * mixed.reshape(B, T, GMLP_W)


def _fwd_setup_inputs(seed: int = 0) -> dict:
    key = jax.random.key(seed)
    ks = iter(jax.random.split(key, 40))
    nrm = lambda shape, scale: scale * jax.random.normal(next(ks), shape, jnp.float32)
    uni = lambda shape, lo, hi: jax.random.uniform(next(ks), shape, jnp.float32, lo, hi)
    E, C = N_AB_LAYERS, N_C_LAYERS
    return {
        'x': nrm((BATCH, SEQ, D_MODEL), 1.0),
        'p': nrm((DEPTH, BATCH, SEQ, PLE_DIM), 1.0),
        'norm_pre': 1.0 + nrm((DEPTH, D_MODEL), 0.02),
        'norm_post': 1.0 + nrm((DEPTH, D_MODEL), 0.02),
        'ab_w_in': nrm((E, D_MODEL, AB_COLS), D_MODEL ** -0.5),
        'fox_f_bias': uni((E, FOX_HEADS), 1.0, 5.0),
        'rwkv_mu': uni((E, RWKV_SHIFT_COLS), 0.0, 1.0),
        'rwkv_w0': uni((E, RWKV_W), -6.0, 1.0),
        'rwkv_w2': nrm((E, RWKV_DECAY_LORA, RWKV_W), 0.1),
        'rwkv_a0': nrm((E, RWKV_W), 0.1),
        'rwkv_a2': nrm((E, RWKV_ICLR_LORA, RWKV_W), 0.1),
        'rwkv_k_k': 0.85 + nrm((E, RWKV_W), 0.05),
        'rwkv_k_a': 1.0 + nrm((E, RWKV_W), 0.05),
        'rwkv_r_k': nrm((E, RWKV_HEADS, RWKV_HEAD_SIZE), 0.1),
        'rwkv_ln_g': 1.0 + nrm((E, RWKV_W), 0.02),
        'rwkv_ln_b': nrm((E, RWKV_W), 0.02),
        'ab_w_out': nrm((E, AB_OUT, D_MODEL), AB_OUT ** -0.5),
        'c_w_in': nrm((C, D_MODEL, C_COLS), D_MODEL ** -0.5),
        'c_ln_g': 1.0 + nrm((C, GMLP_W), 0.02),
        'c_ln_b': nrm((C, GMLP_W), 0.02),
        'c_w_s': nrm((C, GMLP_GROUPS, GMLP_CHUNK, GMLP_CHUNK), GMLP_CHUNK ** -0.5),
        'c_b_s': 1.0 + nrm((C, GMLP_GROUPS, GMLP_CHUNK), 0.01),
        'c_w_out': nrm((C, GMLP_W, D_MODEL), GMLP_W ** -0.5),
        'ple_w_proj': nrm((DEPTH, PLE_DIM, D_MODEL), PLE_DIM ** -0.5),
        'ple_w_gate': nrm((DEPTH, D_MODEL, D_MODEL), D_MODEL ** -0.5),
    }


def _fwd_reference(x, p, norm_pre, norm_post, ab_w_in, fox_f_bias, rwkv_mu, rwkv_w0, rwkv_w2, rwkv_a0, rwkv_a2,
              rwkv_k_k, rwkv_k_a, rwkv_r_k, rwkv_ln_g, rwkv_ln_b, ab_w_out, c_w_in, c_ln_g, c_ln_b, c_w_s,
              c_b_s, c_w_out, ple_w_proj, ple_w_gate):
    B, T, _ = x.shape
    h = x
    for i in range(DEPTH):
        j = i // 2
        xn = rms_norm(h, norm_pre[i])
        if i % 2 == 0:
            proj = xn @ ab_w_in[j]
            qa, ka, va, fa, ga, shb, gb = _split(
                proj, (FOX_W, FOX_W, FOX_W, FOX_HEADS, FOX_W, RWKV_SHIFT_COLS, RWKV_W))
            log_f = jax.nn.log_sigmoid((fa + fox_f_bias[j]).astype(jnp.float32))
            hd = lambda z: z.reshape(B, T, FOX_HEADS, FOX_HEAD_DIM)
            oa = forgetting_attention(hd(qa), hd(ka), hd(va), log_f).reshape(B, T, FOX_W)
            ob = rwkv7_time_mix(shb, rwkv_mu[j], rwkv_w0[j], rwkv_w2[j], rwkv_a0[j], rwkv_a2[j],
                                rwkv_k_k[j], rwkv_k_a[j], rwkv_r_k[j], rwkv_ln_g[j], rwkv_ln_b[j])
            y = jnp.concatenate([oa * jax.nn.silu(ga), ob * jax.nn.silu(gb)], axis=-1) @ ab_w_out[j]
        else:
            proj = xn @ c_w_in[j]
            u, v, g = _split(proj, (GMLP_W, GMLP_W, GMLP_W))
            oc = chunked_spatial_gating(u, v, c_ln_g[j], c_ln_b[j], c_w_s[j], c_b_s[j])
            y = (oc * jax.nn.silu(g)) @ c_w_out[j]
        h = h + rms_norm(y, norm_post[i])
        h = h + (p[i] @ ple_w_proj[i]) * jax.nn.sigmoid(h @ ple_w_gate[i])
    return h


import jax as _jax
import jax.numpy as _jnp

TWIN_FORMAT = 'train_step'
FWD_PARAMS = ['x', 'p', 'norm_pre', 'norm_post', 'ab_w_in', 'fox_f_bias', 'rwkv_mu', 'rwkv_w0', 'rwkv_w2', 'rwkv_a0', 'rwkv_a2', 'rwkv_k_k', 'rwkv_k_a', 'rwkv_r_k', 'rwkv_ln_g', 'rwkv_ln_b', 'ab_w_out', 'c_w_in', 'c_ln_g', 'c_ln_b', 'c_w_s', 'c_b_s', 'c_w_out', 'ple_w_proj', 'ple_w_gate']
TWIN_WEIGHTS = ['norm_pre', 'norm_post', 'ab_w_in', 'fox_f_bias', 'rwkv_mu', 'rwkv_w0', 'rwkv_w2', 'rwkv_a0', 'rwkv_a2', 'rwkv_k_k', 'rwkv_k_a', 'rwkv_r_k', 'rwkv_ln_g', 'rwkv_ln_b', 'ab_w_out', 'c_w_in', 'c_ln_g', 'c_ln_b', 'c_w_s', 'c_b_s', 'c_w_out', 'ple_w_proj', 'ple_w_gate']
TWIN_DIFF_INPUT = 'x'
TWIN_INPUTS = ['x', 'p', 'norm_pre', 'norm_post', 'ab_w_in', 'fox_f_bias', 'rwkv_mu', 'rwkv_w0', 'rwkv_w2', 'rwkv_a0', 'rwkv_a2', 'rwkv_k_k', 'rwkv_k_a', 'rwkv_r_k', 'rwkv_ln_g', 'rwkv_ln_b', 'ab_w_out', 'c_w_in', 'c_ln_g', 'c_ln_b', 'c_w_s', 'c_b_s', 'c_w_out', 'ple_w_proj', 'ple_w_gate', 'loss_target', 'm_norm_pre', 'm_norm_post', 'm_ab_w_in', 'm_fox_f_bias', 'm_rwkv_mu', 'm_rwkv_w0', 'm_rwkv_w2', 'm_rwkv_a0', 'm_rwkv_a2', 'm_rwkv_k_k', 'm_rwkv_k_a', 'm_rwkv_r_k', 'm_rwkv_ln_g', 'm_rwkv_ln_b', 'm_ab_w_out', 'm_c_w_in', 'm_c_ln_g', 'm_c_ln_b', 'm_c_w_s', 'm_c_b_s', 'm_c_w_out', 'm_ple_w_proj', 'm_ple_w_gate', 'v_norm_pre', 'v_norm_post', 'v_ab_w_in', 'v_fox_f_bias', 'v_rwkv_mu', 'v_rwkv_w0', 'v_rwkv_w2', 'v_rwkv_a0', 'v_rwkv_a2', 'v_rwkv_k_k', 'v_rwkv_k_a', 'v_rwkv_r_k', 'v_rwkv_ln_g', 'v_rwkv_ln_b', 'v_ab_w_out', 'v_c_w_in', 'v_c_ln_g', 'v_c_ln_b', 'v_c_w_s', 'v_c_b_s', 'v_c_w_out', 'v_ple_w_proj', 'v_ple_w_gate']
TWIN_OUTPUTS = ['loss', 'grad_x', 'grad_norm_pre', 'grad_norm_post', 'grad_ab_w_in', 'grad_fox_f_bias', 'grad_rwkv_mu', 'grad_rwkv_w0', 'grad_rwkv_w2', 'grad_rwkv_a0', 'grad_rwkv_a2', 'grad_rwkv_k_k', 'grad_rwkv_k_a', 'grad_rwkv_r_k', 'grad_rwkv_ln_g', 'grad_rwkv_ln_b', 'grad_ab_w_out', 'grad_c_w_in', 'grad_c_ln_g', 'grad_c_ln_b', 'grad_c_w_s', 'grad_c_b_s', 'grad_c_w_out', 'grad_ple_w_proj', 'grad_ple_w_gate', 'delta_norm_pre', 'delta_norm_post', 'delta_ab_w_in', 'delta_fox_f_bias', 'delta_rwkv_mu', 'delta_rwkv_w0', 'delta_rwkv_w2', 'delta_rwkv_a0', 'delta_rwkv_a2', 'delta_rwkv_k_k', 'delta_rwkv_k_a', 'delta_rwkv_r_k', 'delta_rwkv_ln_g', 'delta_rwkv_ln_b', 'delta_ab_w_out', 'delta_c_w_in', 'delta_c_ln_g', 'delta_c_ln_b', 'delta_c_w_s', 'delta_c_b_s', 'delta_c_w_out', 'delta_ple_w_proj', 'delta_ple_w_gate', 'new_m_norm_pre', 'new_m_norm_post', 'new_m_ab_w_in', 'new_m_fox_f_bias', 'new_m_rwkv_mu', 'new_m_rwkv_w0', 'new_m_rwkv_w2', 'new_m_rwkv_a0', 'new_m_rwkv_a2', 'new_m_rwkv_k_k', 'new_m_rwkv_k_a', 'new_m_rwkv_r_k', 'new_m_rwkv_ln_g', 'new_m_rwkv_ln_b', 'new_m_ab_w_out', 'new_m_c_w_in', 'new_m_c_ln_g', 'new_m_c_ln_b', 'new_m_c_w_s', 'new_m_c_b_s', 'new_m_c_w_out', 'new_m_ple_w_proj', 'new_m_ple_w_gate', 'new_v_norm_pre', 'new_v_norm_post', 'new_v_ab_w_in', 'new_v_fox_f_bias', 'new_v_rwkv_mu', 'new_v_rwkv_w0', 'new_v_rwkv_w2', 'new_v_rwkv_a0', 'new_v_rwkv_a2', 'new_v_rwkv_k_k', 'new_v_rwkv_k_a', 'new_v_rwkv_r_k', 'new_v_rwkv_ln_g', 'new_v_rwkv_ln_b', 'new_v_ab_w_out', 'new_v_c_w_in', 'new_v_c_ln_g', 'new_v_c_ln_b', 'new_v_c_w_s', 'new_v_c_b_s', 'new_v_c_w_out', 'new_v_ple_w_proj', 'new_v_ple_w_gate']
TWIN_LEAF_KINDS = {'loss': 'loss', 'grad_x': 'grad_x', 'grad_norm_pre': 'grad_w', 'grad_norm_post': 'grad_w', 'grad_ab_w_in': 'grad_w', 'grad_fox_f_bias': 'grad_w', 'grad_rwkv_mu': 'grad_w', 'grad_rwkv_w0': 'grad_w', 'grad_rwkv_w2': 'grad_w', 'grad_rwkv_a0': 'grad_w', 'grad_rwkv_a2': 'grad_w', 'grad_rwkv_k_k': 'grad_w', 'grad_rwkv_k_a': 'grad_w', 'grad_rwkv_r_k': 'grad_w', 'grad_rwkv_ln_g': 'grad_w', 'grad_rwkv_ln_b': 'grad_w', 'grad_ab_w_out': 'grad_w', 'grad_c_w_in': 'grad_w', 'grad_c_ln_g': 'grad_w', 'grad_c_ln_b': 'grad_w', 'grad_c_w_s': 'grad_w', 'grad_c_b_s': 'grad_w', 'grad_c_w_out': 'grad_w', 'grad_ple_w_proj': 'grad_w', 'grad_ple_w_gate': 'grad_w', 'delta_norm_pre': 'delta_w', 'delta_norm_post': 'delta_w', 'delta_ab_w_in': 'delta_w', 'delta_fox_f_bias': 'delta_w', 'delta_rwkv_mu': 'delta_w', 'delta_rwkv_w0': 'delta_w', 'delta_rwkv_w2': 'delta_w', 'delta_rwkv_a0': 'delta_w', 'delta_rwkv_a2': 'delta_w', 'delta_rwkv_k_k': 'delta_w', 'delta_rwkv_k_a': 'delta_w', 'delta_rwkv_r_k': 'delta_w', 'delta_rwkv_ln_g': 'delta_w', 'delta_rwkv_ln_b': 'delta_w', 'delta_ab_w_out': 'delta_w', 'delta_c_w_in': 'delta_w', 'delta_c_ln_g': 'delta_w', 'delta_c_ln_b': 'delta_w', 'delta_c_w_s': 'delta_w', 'delta_c_b_s': 'delta_w', 'delta_c_w_out': 'delta_w', 'delta_ple_w_proj': 'delta_w', 'delta_ple_w_gate': 'delta_w', 'new_m_norm_pre': 'new_m', 'new_m_norm_post': 'new_m', 'new_m_ab_w_in': 'new_m', 'new_m_fox_f_bias': 'new_m', 'new_m_rwkv_mu': 'new_m', 'new_m_rwkv_w0': 'new_m', 'new_m_rwkv_w2': 'new_m', 'new_m_rwkv_a0': 'new_m', 'new_m_rwkv_a2': 'new_m', 'new_m_rwkv_k_k': 'new_m', 'new_m_rwkv_k_a': 'new_m', 'new_m_rwkv_r_k': 'new_m', 'new_m_rwkv_ln_g': 'new_m', 'new_m_rwkv_ln_b': 'new_m', 'new_m_ab_w_out': 'new_m', 'new_m_c_w_in': 'new_m', 'new_m_c_ln_g': 'new_m', 'new_m_c_ln_b': 'new_m', 'new_m_c_w_s': 'new_m', 'new_m_c_b_s': 'new_m', 'new_m_c_w_out': 'new_m', 'new_m_ple_w_proj': 'new_m', 'new_m_ple_w_gate': 'new_m', 'new_v_norm_pre': 'new_v', 'new_v_norm_post': 'new_v', 'new_v_ab_w_in': 'new_v', 'new_v_fox_f_bias': 'new_v', 'new_v_rwkv_mu': 'new_v', 'new_v_rwkv_w0': 'new_v', 'new_v_rwkv_w2': 'new_v', 'new_v_rwkv_a0': 'new_v', 'new_v_rwkv_a2': 'new_v', 'new_v_rwkv_k_k': 'new_v', 'new_v_rwkv_k_a': 'new_v', 'new_v_rwkv_r_k': 'new_v', 'new_v_rwkv_ln_g': 'new_v', 'new_v_rwkv_ln_b': 'new_v', 'new_v_ab_w_out': 'new_v', 'new_v_c_w_in': 'new_v', 'new_v_c_ln_g': 'new_v', 'new_v_c_ln_b': 'new_v', 'new_v_c_w_s': 'new_v', 'new_v_c_b_s': 'new_v', 'new_v_c_w_out': 'new_v', 'new_v_ple_w_proj': 'new_v', 'new_v_ple_w_gate': 'new_v'}


def _forward(args):
    return _fwd_reference(*[args[k] for k in FWD_PARAMS])


def _output_shape():
    out = _jax.eval_shape(lambda: _forward(_fwd_setup_inputs(0)))
    return out.shape, out.dtype

N_MICROBATCH = 1
ADAM_LR = 0.001
ADAM_B1 = 0.9
ADAM_B2 = 0.999
ADAM_EPS = 1e-08
ADAM_WD = 0.01
ADAM_STEP = 10
PER_EXAMPLE_BATCH_AXIS = {'x': 0, 'p': 1, 'loss_target': 0}
SHARED_INPUTS = []
_WEIGHT_DTYPES = {'norm_pre': _jnp.float32, 'norm_post': _jnp.float32, 'ab_w_in': _jnp.float32, 'fox_f_bias': _jnp.float32, 'rwkv_mu': _jnp.float32, 'rwkv_w0': _jnp.float32, 'rwkv_w2': _jnp.float32, 'rwkv_a0': _jnp.float32, 'rwkv_a2': _jnp.float32, 'rwkv_k_k': _jnp.float32, 'rwkv_k_a': _jnp.float32, 'rwkv_r_k': _jnp.float32, 'rwkv_ln_g': _jnp.float32, 'rwkv_ln_b': _jnp.float32, 'ab_w_out': _jnp.float32, 'c_w_in': _jnp.float32, 'c_ln_g': _jnp.float32, 'c_ln_b': _jnp.float32, 'c_w_s': _jnp.float32, 'c_b_s': _jnp.float32, 'c_w_out': _jnp.float32, 'ple_w_proj': _jnp.float32, 'ple_w_gate': _jnp.float32}
MOMENT_SCALE = {'norm_pre': 4.429829e-01, 'norm_post': 1.634312e+01, 'ab_w_in': 2.789534e-01, 'fox_f_bias': 6.222235e-01, 'rwkv_mu': 6.234576e-01, 'rwkv_w0': 1.481978e-01, 'rwkv_w2': 1.834079e-02, 'rwkv_a0': 1.517715e-01, 'rwkv_a2': 1.334458e-01, 'rwkv_k_k': 3.997500e-01, 'rwkv_k_a': 3.933083e-01, 'rwkv_r_k': 7.813430e-01, 'rwkv_ln_g': 3.831223e-01, 'rwkv_ln_b': 9.809580e-01, 'ab_w_out': 2.755194e-01, 'c_w_in': 1.785078e-01, 'c_ln_g': 9.885610e-02, 'c_ln_b': 9.818863e-02, 'c_w_s': 9.711649e-02, 'c_b_s': 1.430961e-01, 'c_w_out': 2.380955e-01, 'ple_w_proj': 2.429326e-01, 'ple_w_gate': 9.366989e-02}


def _to_microbatches(a, axis):
    t = _jnp.moveaxis(a, axis, 0)
    t = t.reshape((N_MICROBATCH, t.shape[0] // N_MICROBATCH) + t.shape[1:])
    return _jnp.moveaxis(t, 1, axis + 1)


def setup_inputs(seed: int = 0) -> dict:
    inp = _fwd_setup_inputs(seed)
    key = _jax.random.fold_in(_jax.random.key(seed), 7919)
    shape, _ = _output_shape()
    out = dict(inp)
    out["loss_target"] = _jax.random.normal(_jax.random.fold_in(key, 0), shape, _jnp.float32)
    for i, name in enumerate(TWIN_WEIGHTS):
        w = inp[name].astype(_jnp.float32)
        if MOMENT_SCALE is None:
            s = _jnp.sqrt(_jnp.mean(_jnp.square(w)) + 1e-30)
        else:
            s = MOMENT_SCALE[name]
        km, kv = _jax.random.split(_jax.random.fold_in(key, i + 1))
        out[name] = w
        out["m_" + name] = s * _jax.random.normal(km, w.shape, _jnp.float32)
        out["v_" + name] = (s * s) * _jax.random.uniform(kv, w.shape, _jnp.float32, 0.5, 1.5)
    if N_MICROBATCH > 1:
        for name, axis in PER_EXAMPLE_BATCH_AXIS.items():
            out[name] = _to_microbatches(out[name], axis)
    return {'x': out['x'], 'p': out['p'], 'norm_pre': out['norm_pre'], 'norm_post': out['norm_post'], 'ab_w_in': out['ab_w_in'], 'fox_f_bias': out['fox_f_bias'], 'rwkv_mu': out['rwkv_mu'], 'rwkv_w0': out['rwkv_w0'], 'rwkv_w2': out['rwkv_w2'], 'rwkv_a0': out['rwkv_a0'], 'rwkv_a2': out['rwkv_a2'], 'rwkv_k_k': out['rwkv_k_k'], 'rwkv_k_a': out['rwkv_k_a'], 'rwkv_r_k': out['rwkv_r_k'], 'rwkv_ln_g': out['rwkv_ln_g'], 'rwkv_ln_b': out['rwkv_ln_b'], 'ab_w_out': out['ab_w_out'], 'c_w_in': out['c_w_in'], 'c_ln_g': out['c_ln_g'], 'c_ln_b': out['c_ln_b'], 'c_w_s': out['c_w_s'], 'c_b_s': out['c_b_s'], 'c_w_out': out['c_w_out'], 'ple_w_proj': out['ple_w_proj'], 'ple_w_gate': out['ple_w_gate'], 'loss_target': out['loss_target'], 'm_norm_pre': out['m_norm_pre'], 'm_norm_post': out['m_norm_post'], 'm_ab_w_in': out['m_ab_w_in'], 'm_fox_f_bias': out['m_fox_f_bias'], 'm_rwkv_mu': out['m_rwkv_mu'], 'm_rwkv_w0': out['m_rwkv_w0'], 'm_rwkv_w2': out['m_rwkv_w2'], 'm_rwkv_a0': out['m_rwkv_a0'], 'm_rwkv_a2': out['m_rwkv_a2'], 'm_rwkv_k_k': out['m_rwkv_k_k'], 'm_rwkv_k_a': out['m_rwkv_k_a'], 'm_rwkv_r_k': out['m_rwkv_r_k'], 'm_rwkv_ln_g': out['m_rwkv_ln_g'], 'm_rwkv_ln_b': out['m_rwkv_ln_b'], 'm_ab_w_out': out['m_ab_w_out'], 'm_c_w_in': out['m_c_w_in'], 'm_c_ln_g': out['m_c_ln_g'], 'm_c_ln_b': out['m_c_ln_b'], 'm_c_w_s': out['m_c_w_s'], 'm_c_b_s': out['m_c_b_s'], 'm_c_w_out': out['m_c_w_out'], 'm_ple_w_proj': out['m_ple_w_proj'], 'm_ple_w_gate': out['m_ple_w_gate'], 'v_norm_pre': out['v_norm_pre'], 'v_norm_post': out['v_norm_post'], 'v_ab_w_in': out['v_ab_w_in'], 'v_fox_f_bias': out['v_fox_f_bias'], 'v_rwkv_mu': out['v_rwkv_mu'], 'v_rwkv_w0': out['v_rwkv_w0'], 'v_rwkv_w2': out['v_rwkv_w2'], 'v_rwkv_a0': out['v_rwkv_a0'], 'v_rwkv_a2': out['v_rwkv_a2'], 'v_rwkv_k_k': out['v_rwkv_k_k'], 'v_rwkv_k_a': out['v_rwkv_k_a'], 'v_rwkv_r_k': out['v_rwkv_r_k'], 'v_rwkv_ln_g': out['v_rwkv_ln_g'], 'v_rwkv_ln_b': out['v_rwkv_ln_b'], 'v_ab_w_out': out['v_ab_w_out'], 'v_c_w_in': out['v_c_w_in'], 'v_c_ln_g': out['v_c_ln_g'], 'v_c_ln_b': out['v_c_ln_b'], 'v_c_w_s': out['v_c_w_s'], 'v_c_b_s': out['v_c_b_s'], 'v_c_w_out': out['v_c_w_out'], 'v_ple_w_proj': out['v_ple_w_proj'], 'v_ple_w_gate': out['v_ple_w_gate']}


def _loss(weights, diff, rest, loss_target):
    with _jax.named_scope("forward"):
        args = {**rest, TWIN_DIFF_INPUT: diff, **{k: w.astype(_WEIGHT_DTYPES[k]) for k, w in weights.items()}}
        y = _forward(args)
    with _jax.named_scope("loss_head"):
        err = _jnp.square(y.astype(_jnp.float32) - loss_target)
        return 0.5 * _jnp.sum(_jnp.mean(err, axis=-1)) if err.ndim else 0.5 * err


def _adamw(w, g, m, v):
    m = ADAM_B1 * m + (1.0 - ADAM_B1) * g
    v = ADAM_B2 * v + (1.0 - ADAM_B2) * _jnp.square(g)
    m_hat = m / (1.0 - ADAM_B1 ** ADAM_STEP)
    v_hat = v / (1.0 - ADAM_B2 ** ADAM_STEP)
    delta = -ADAM_LR * (m_hat / (_jnp.sqrt(v_hat) + ADAM_EPS) + ADAM_WD * w)
    return delta, m, v


def reference(x, p, norm_pre, norm_post, ab_w_in, fox_f_bias, rwkv_mu, rwkv_w0, rwkv_w2, rwkv_a0, rwkv_a2, rwkv_k_k, rwkv_k_a, rwkv_r_k, rwkv_ln_g, rwkv_ln_b, ab_w_out, c_w_in, c_ln_g, c_ln_b, c_w_s, c_b_s, c_w_out, ple_w_proj, ple_w_gate, loss_target, m_norm_pre, m_norm_post, m_ab_w_in, m_fox_f_bias, m_rwkv_mu, m_rwkv_w0, m_rwkv_w2, m_rwkv_a0, m_rwkv_a2, m_rwkv_k_k, m_rwkv_k_a, m_rwkv_r_k, m_rwkv_ln_g, m_rwkv_ln_b, m_ab_w_out, m_c_w_in, m_c_ln_g, m_c_ln_b, m_c_w_s, m_c_b_s, m_c_w_out, m_ple_w_proj, m_ple_w_gate, v_norm_pre, v_norm_post, v_ab_w_in, v_fox_f_bias, v_rwkv_mu, v_rwkv_w0, v_rwkv_w2, v_rwkv_a0, v_rwkv_a2, v_rwkv_k_k, v_rwkv_k_a, v_rwkv_r_k, v_rwkv_ln_g, v_rwkv_ln_b, v_ab_w_out, v_c_w_in, v_c_ln_g, v_c_ln_b, v_c_w_s, v_c_b_s, v_c_w_out, v_ple_w_proj, v_ple_w_gate):
    given = dict(x=x, p=p, norm_pre=norm_pre, norm_post=norm_post, ab_w_in=ab_w_in, fox_f_bias=fox_f_bias, rwkv_mu=rwkv_mu, rwkv_w0=rwkv_w0, rwkv_w2=rwkv_w2, rwkv_a0=rwkv_a0, rwkv_a2=rwkv_a2, rwkv_k_k=rwkv_k_k, rwkv_k_a=rwkv_k_a, rwkv_r_k=rwkv_r_k, rwkv_ln_g=rwkv_ln_g, rwkv_ln_b=rwkv_ln_b, ab_w_out=ab_w_out, c_w_in=c_w_in, c_ln_g=c_ln_g, c_ln_b=c_ln_b, c_w_s=c_w_s, c_b_s=c_b_s, c_w_out=c_w_out, ple_w_proj=ple_w_proj, ple_w_gate=ple_w_gate, loss_target=loss_target, m_norm_pre=m_norm_pre, m_norm_post=m_norm_post, m_ab_w_in=m_ab_w_in, m_fox_f_bias=m_fox_f_bias, m_rwkv_mu=m_rwkv_mu, m_rwkv_w0=m_rwkv_w0, m_rwkv_w2=m_rwkv_w2, m_rwkv_a0=m_rwkv_a0, m_rwkv_a2=m_rwkv_a2, m_rwkv_k_k=m_rwkv_k_k, m_rwkv_k_a=m_rwkv_k_a, m_rwkv_r_k=m_rwkv_r_k, m_rwkv_ln_g=m_rwkv_ln_g, m_rwkv_ln_b=m_rwkv_ln_b, m_ab_w_out=m_ab_w_out, m_c_w_in=m_c_w_in, m_c_ln_g=m_c_ln_g, m_c_ln_b=m_c_ln_b, m_c_w_s=m_c_w_s, m_c_b_s=m_c_b_s, m_c_w_out=m_c_w_out, m_ple_w_proj=m_ple_w_proj, m_ple_w_gate=m_ple_w_gate, v_norm_pre=v_norm_pre, v_norm_post=v_norm_post, v_ab_w_in=v_ab_w_in, v_fox_f_bias=v_fox_f_bias, v_rwkv_mu=v_rwkv_mu, v_rwkv_w0=v_rwkv_w0, v_rwkv_w2=v_rwkv_w2, v_rwkv_a0=v_rwkv_a0, v_rwkv_a2=v_rwkv_a2, v_rwkv_k_k=v_rwkv_k_k, v_rwkv_k_a=v_rwkv_k_a, v_rwkv_r_k=v_rwkv_r_k, v_rwkv_ln_g=v_rwkv_ln_g, v_rwkv_ln_b=v_rwkv_ln_b, v_ab_w_out=v_ab_w_out, v_c_w_in=v_c_w_in, v_c_ln_g=v_c_ln_g, v_c_ln_b=v_c_ln_b, v_c_w_s=v_c_w_s, v_c_b_s=v_c_b_s, v_c_w_out=v_c_w_out, v_ple_w_proj=v_ple_w_proj, v_ple_w_gate=v_ple_w_gate)
    weights = {n: given[n] for n in TWIN_WEIGHTS}
    shared = {n: given[n] for n in SHARED_INPUTS}
    per_example = {n: given[n] for n in ['x', 'p']}
    grad_fn = _jax.value_and_grad(_loss, argnums=(0, 1))

    def one_microbatch(ex, loss_target):
        ex = dict(ex)
        diff = ex.pop(TWIN_DIFF_INPUT)
        return grad_fn(weights, diff, {**shared, **ex}, loss_target)

    if N_MICROBATCH == 1:
        loss, (grad_w, grad_x) = one_microbatch(per_example, given["loss_target"])
    else:
        def body(carry, xs):
            loss_sum, grad_sum = carry
            l_k, (gw_k, gx_k) = one_microbatch(xs[0], xs[1])
            with _jax.named_scope("update"):
                return (loss_sum + l_k, _jax.tree.map(_jnp.add, grad_sum, gw_k)), gx_k

        init = (_jnp.zeros((), _jnp.float32), _jax.tree.map(_jnp.zeros_like, weights))
        (loss, grad_w), grad_x = _jax.lax.scan(body, init, (per_example, given["loss_target"]))
    with _jax.named_scope("update"):
        delta_w, new_m, new_v = {}, {}, {}
        for n in TWIN_WEIGHTS:
            delta_w[n], new_m[n], new_v[n] = _adamw(weights[n], grad_w[n], given["m_" + n], given["v_" + n])
    return (loss, grad_x, *[grad_w[n] for n in TWIN_WEIGHTS], *[delta_w[n] for n in TWIN_WEIGHTS],
            *[new_m[n] for n in TWIN_WEIGHTS], *[new_v[n] for n in TWIN_WEIGHTS])
```

```python
import functools

import jax
import jax.numpy as jnp
from jax import lax
from jax.experimental import pallas as pl
from jax.experimental.pallas import tpu as pltpu

F32 = jnp.float32
BF16 = jnp.bfloat16
HI = lax.Precision.HIGHEST

HEAD = 64
LANE = 128
LORA = 64
GMLP_CHUNK = 128
RMS_EPS = 1e-6
LN_EPS = 1e-5
GN_EPS = 64e-5
VMEM_LIMIT = 56 * 1024 * 1024
N_DEV = 8

ADAM_LR = 0.001
ADAM_B1 = 0.9
ADAM_B2 = 0.999
ADAM_EPS = 1e-08
ADAM_WD = 0.01
ADAM_STEP = 10

NT_DIMS = (((1,), (1,)), ((), ()))
TN_DIMS = (((0,), (0,)), ((), ()))


def _cparams(sem):
    return pltpu.CompilerParams(dimension_semantics=sem, vmem_limit_bytes=VMEM_LIMIT)


def _tile(n, target, q=LANE):
    if n <= target:
        return n
    best = None
    for d in range(q, target + 1, q):
        if n % d == 0:
            best = d
    assert best is not None, (n, target, q)
    return best


def _sigmoid(x):
    return 1.0 / (1.0 + jnp.exp(-x))


def _silu_and_grad(x):
    s = _sigmoid(x)
    return x * s, s * (1.0 + x * (1.0 - s))


def _gelu_and_grad(x):
    cdf = 0.5 * (1.0 + lax.erf(x * 0.7071067811865476))
    pdf = jnp.exp(-0.5 * x * x) * 0.3989422804014327
    return x * cdf, cdf + x * pdf


def _iota(shape, dim):
    return lax.broadcasted_iota(jnp.int32, shape, dim)


def _head_ones():
    return (_iota((LANE, LANE), 0) // HEAD == _iota((LANE, LANE), 1) // HEAD).astype(F32)


def _head_eye():
    return (_iota((HEAD, LANE), 0) == _iota((HEAD, LANE), 1) % HEAD).astype(F32)


def _hsum(x, ones):
    return jnp.dot(x, ones, precision=HI, preferred_element_type=F32)


def _colsum(x):
    return jnp.sum(x, axis=0, keepdims=True)


def matmul(a, b, *, name, ta=False, tb=False, add=None, out_dtype=F32, tm=512, tn=1024, tk=1024):
    m, k = (a.shape[1], a.shape[0]) if ta else a.shape
    n = b.shape[0] if tb else b.shape[1]
    assert (b.shape[1] if tb else b.shape[0]) == k
    tm, tn, tk = _tile(m, tm), _tile(n, tn), _tile(k, tk)
    nk = k // tk

    def body(*refs):
        if add is None:
            a_ref, b_ref, o_ref, acc_ref = refs
        else:
            a_ref, b_ref, add_ref, o_ref, acc_ref = refs
        kk = pl.program_id(2)

        @pl.when(kk == 0)
        def _():
            acc_ref[...] = jnp.zeros_like(acc_ref)

        dims = (((0 if ta else 1,), (1 if tb else 0,)), ((), ()))
        acc_ref[...] += lax.dot_general(a_ref[...].astype(BF16), b_ref[...].astype(BF16), dims,
                                        preferred_element_type=F32)

        @pl.when(kk == nk - 1)
        def _():
            r = acc_ref[...]
            if add is not None:
                r = r + add_ref[...].astype(F32)
            o_ref[...] = r.astype(out_dtype)

    a_spec = pl.BlockSpec((tk, tm), lambda i, j, kk: (kk, i)) if ta else pl.BlockSpec((tm, tk), lambda i, j, kk: (i, kk))
    b_spec = pl.BlockSpec((tn, tk), lambda i, j, kk: (j, kk)) if tb else pl.BlockSpec((tk, tn), lambda i, j, kk: (kk, j))
    o_spec = pl.BlockSpec((tm, tn), lambda i, j, kk: (i, j))
    in_specs = [a_spec, b_spec] + ([o_spec] if add is not None else [])
    args = (a, b) + ((add,) if add is not None else ())
    return pl.pallas_call(
        body, name=name, grid=(m // tm, n // tn, nk), in_specs=in_specs, out_specs=o_spec,
        out_shape=jax.ShapeDtypeStruct((m, n), out_dtype), scratch_shapes=[pltpu.VMEM((tm, tn), F32)],
        compiler_params=_cparams(("parallel", "parallel", "arbitrary")),
    )(*args)


def _row_spec(tm, width, cb=0):
    return pl.BlockSpec((tm, width), lambda i: (i, cb))


def _full_spec(shape):
    return pl.BlockSpec(shape, lambda i: (0,) * len(shape))


def rms_fwd(x, g, *, name):
    n, d = x.shape
    tm = _tile(n, 256, 8)

    def body(x_ref, g_ref, o_ref):
        xv = x_ref[...]
        r = lax.rsqrt(jnp.mean(xv * xv, axis=-1, keepdims=True) + RMS_EPS)
        o_ref[...] = (xv * r * g_ref[...]).astype(BF16)

    return pl.pallas_call(
        body, name=name, grid=(n // tm,), in_specs=[_row_spec(tm, d), _full_spec((1, d))],
        out_specs=_row_spec(tm, d), out_shape=jax.ShapeDtypeStruct((n, d), BF16),
        compiler_params=_cparams(("parallel",)),
    )(x, g)


def rms_res_fwd(y, res, g, *, name):
    n, d = y.shape
    tm = _tile(n, 256, 8)

    def body(y_ref, res_ref, g_ref, h_ref, hb_ref):
        yv = y_ref[...]
        r = lax.rsqrt(jnp.mean(yv * yv, axis=-1, keepdims=True) + RMS_EPS)
        h = res_ref[...] + yv * r * g_ref[...]
        h_ref[...] = h
        hb_ref[...] = h.astype(BF16)

    return pl.pallas_call(
        body, name=name, grid=(n // tm,), in_specs=[_row_spec(tm, d), _row_spec(tm, d), _full_spec((1, d))],
        out_specs=[_row_spec(tm, d), _row_spec(tm, d)],
        out_shape=[jax.ShapeDtypeStruct((n, d), F32), jax.ShapeDtypeStruct((n, d), BF16)],
        compiler_params=_cparams(("parallel",)),
    )(y, res, g)


def rms_bwd(x, dy, g, *, name, res=None, out_dtype=F32):
    n, d = x.shape
    tm = _tile(n, 256, 8)

    def body(*refs):
        if res is None:
            x_ref, dy_ref, g_ref, dx_ref, dg_ref = refs
        else:
            x_ref, dy_ref, g_ref, res_ref, dx_ref, dg_ref = refs

        @pl.when(pl.program_id(0) == 0)
        def _():
            dg_ref[...] = jnp.zeros_like(dg_ref)

        xv = x_ref[...]
        dyv = dy_ref[...].astype(F32)
        r = lax.rsqrt(jnp.mean(xv * xv, axis=-1, keepdims=True) + RMS_EPS)
        xhat = xv * r
        dxh = dyv * g_ref[...]
        dx = r * (dxh - xhat * jnp.mean(dxh * xhat, axis=-1, keepdims=True))
        if res is not None:
            dx = dx + res_ref[...]
        dx_ref[...] = dx.astype(out_dtype)
        dg_ref[...] += _colsum(dyv * xhat)

    in_specs = [_row_spec(tm, d), _row_spec(tm, d), _full_spec((1, d))] + ([_row_spec(tm, d)] if res is not None else [])
    args = (x, dy, g) + ((res,) if res is not None else ())
    return pl.pallas_call(
        body, name=name, grid=(n // tm,), in_specs=in_specs, out_specs=[_row_spec(tm, d), _full_spec((1, d))],
        out_shape=[jax.ShapeDtypeStruct((n, d), out_dtype), jax.ShapeDtypeStruct((1, d), F32)],
        compiler_params=_cparams(("arbitrary",)),
    )(*args)


def ple_fwd(h, a, z, *, name):
    n, d = h.shape
    tm = _tile(n, 256, 8)

    def body(h_ref, a_ref, z_ref, o_ref):
        o_ref[...] = h_ref[...] + a_ref[...] * _sigmoid(z_ref[...])

    return pl.pallas_call(
        body, name=name, grid=(n // tm,), in_specs=[_row_spec(tm, d)] * 3, out_specs=_row_spec(tm, d),
        out_shape=jax.ShapeDtypeStruct((n, d), F32), compiler_params=_cparams(("parallel",)),
    )(h, a, z)


def ple_bwd(dh, a, z, *, name):
    n, d = dh.shape
    tm = _tile(n, 256, 8)

    def body(dh_ref, a_ref, z_ref, da_ref, dz_ref):
        s = _sigmoid(z_ref[...])
        dhv = dh_ref[...]
        da_ref[...] = (dhv * s).astype(BF16)
        dz_ref[...] = (dhv * a_ref[...] * s * (1.0 - s)).astype(BF16)

    return pl.pallas_call(
        body, name=name, grid=(n // tm,), in_specs=[_row_spec(tm, d)] * 3, out_specs=[_row_spec(tm, d)] * 2,
        out_shape=[jax.ShapeDtypeStruct((n, d), BF16)] * 2, compiler_params=_cparams(("parallel",)),
    )(dh, a, z)


def loss_head(h, target, *, name):
    n, d = h.shape
    tm = _tile(n, 256, 8)

    def body(h_ref, t_ref, dh_ref, loss_ref):
        @pl.when(pl.program_id(0) == 0)
        def _():
            loss_ref[...] = jnp.zeros_like(loss_ref)

        e = h_ref[...] - t_ref[...]
        dh_ref[...] = e * (1.0 / d)
        loss_ref[...] += 0.5 * jnp.sum(jnp.sum(e * e, axis=-1, keepdims=True) * (1.0 / d), axis=0, keepdims=True)

    return pl.pallas_call(
        body, name=name, grid=(n // tm,), in_specs=[_row_spec(tm, d)] * 2,
        out_specs=[_row_spec(tm, d), _full_spec((1, 1))],
        out_shape=[jax.ShapeDtypeStruct((n, d), F32), jax.ShapeDtypeStruct((1, 1), F32)],
        compiler_params=_cparams(("arbitrary",)),
    )(h, target)


def _log_sigmoid(z):
    return jnp.minimum(z, 0.0) - jnp.log1p(jnp.exp(-jnp.abs(z)))


def fox_gate_fwd(proj, fbias, bsz, seq, nw):
    cc = _tile(seq, 256, 8)

    def body(f_ref, b_ref, c_ref):
        low = (_iota((cc, cc), 0) >= _iota((cc, cc), 1)).astype(F32)
        carry = jnp.zeros((1, LANE), F32)
        for ci in range(seq // cc):
            rows = slice(ci * cc, (ci + 1) * cc)
            lf = _log_sigmoid(f_ref[rows, :] + b_ref[...])
            c_ref[rows, :] = jnp.dot(low, lf, precision=HI, preferred_element_type=F32) + carry
            carry = carry + _colsum(lf)

    return pl.pallas_call(
        body, name="fox_gate_fwd", grid=(bsz,),
        in_specs=[pl.BlockSpec((seq, LANE), lambda b: (b, 8 * nw + 1)), _full_spec((1, LANE))],
        out_specs=pl.BlockSpec((seq, LANE), lambda b: (b, 0)),
        out_shape=jax.ShapeDtypeStruct((bsz * seq, LANE), F32), compiler_params=_cparams(("parallel",)),
    )(proj, fbias)


def fox_gate_bwd(proj, fbias, dc, bsz, seq, nw):
    cc = _tile(seq, 256, 8)
    nc = seq // cc

    def body(f_ref, b_ref, dc_ref, df_ref, db_ref):
        @pl.when(pl.program_id(0) == 0)
        def _():
            db_ref[...] = jnp.zeros_like(db_ref)

        upp = (_iota((cc, cc), 0) <= _iota((cc, cc), 1)).astype(F32)
        carry = jnp.zeros((1, LANE), F32)
        dbias = jnp.zeros((1, LANE), F32)
        for ci in reversed(range(nc)):
            rows = slice(ci * cc, (ci + 1) * cc)
            blk = dc_ref[rows, :]
            dlf = jnp.dot(upp, blk, precision=HI, preferred_element_type=F32) + carry
            carry = carry + _colsum(blk)
            df = dlf * _sigmoid(-(f_ref[rows, :] + b_ref[...]))
            df_ref[rows, :] = df.astype(BF16)
            dbias = dbias + _colsum(df)
        db_ref[...] += dbias

    return pl.pallas_call(
        body, name="fox_gate_bwd", grid=(bsz,),
        in_specs=[pl.BlockSpec((seq, LANE), lambda b: (b, 8 * nw + 1)), _full_spec((1, LANE)),
                  pl.BlockSpec((seq, LANE), lambda b: (b, 0))],
        out_specs=[pl.BlockSpec((seq, LANE), lambda b: (b, 0)), _full_spec((1, LANE))],
        out_shape=[jax.ShapeDtypeStruct((bsz * seq, LANE), BF16), jax.ShapeDtypeStruct((1, LANE), F32)],
        compiler_params=_cparams(("arbitrary",)),
    )(proj, fbias, dc)


def _attn_scores(qs, kh, ccol, crow, r0, kend, tq):
    s = lax.dot_general(qs[r0:kend], kh[:kend], NT_DIMS, preferred_element_type=F32)
    s = s + ccol[r0:kend] - crow[:, :kend]
    causal = _iota((tq, kend), 1) <= r0 + _iota((tq, kend), 0)
    return jnp.where(causal, s, -jnp.inf)


def _head_column(c_blk, h):
    return jnp.sum(jnp.where(_iota(c_blk.shape, 1) == h, c_blk, 0.0), axis=1, keepdims=True)


def fox_attn_fwd(proj, c, crow, bsz, seq, nw):
    tq = _tile(seq, 256, 8)
    scale = HEAD ** -0.5

    def body(q_ref, k_ref, v_ref, g_ref, c_ref, crow_ref, o_ref, y_ref):
        hp = pl.program_id(1)
        c_blk = c_ref[...]
        for hh in range(2):
            sl = slice(hh * HEAD, (hh + 1) * HEAD)
            ccol = _head_column(c_blk, 2 * hp + hh)
            crow_h = crow_ref[0, 0, hh:hh + 1, :]
            qs = (q_ref[:, sl] * scale).astype(BF16)
            kh = k_ref[:, sl].astype(BF16)
            vh = v_ref[:, sl].astype(BF16)
            for qi in range(seq // tq):
                r0, kend = qi * tq, (qi + 1) * tq
                s = _attn_scores(qs, kh, ccol, crow_h, r0, kend, tq)
                p = jnp.exp(s - jnp.max(s, axis=-1, keepdims=True))
                o = jnp.dot(p.astype(BF16), vh[:kend], preferred_element_type=F32) / jnp.sum(p, axis=-1, keepdims=True)
                o_ref[r0:kend, sl] = o
                y_ref[r0:kend, sl] = (o * _silu_and_grad(g_ref[r0:kend, sl])[0]).astype(BF16)

    blk = lambda cb: pl.BlockSpec((seq, LANE), lambda b, j: (b, cb * nw + j))
    return pl.pallas_call(
        body, name="fox_attn_fwd", grid=(bsz, nw),
        in_specs=[blk(0), blk(1), blk(2), blk(3), pl.BlockSpec((seq, LANE), lambda b, j: (b, 0)),
                  pl.BlockSpec((1, 1, 8, seq), lambda b, j: (b, j, 0, 0))],
        out_specs=[pl.BlockSpec((seq, LANE), lambda b, j: (b, j))] * 2,
        out_shape=[jax.ShapeDtypeStruct((bsz * seq, nw * LANE), F32), jax.ShapeDtypeStruct((bsz * seq, nw * LANE), BF16)],
        compiler_params=_cparams(("parallel", "parallel")),
    )(proj, proj, proj, proj, c, crow)


def fox_attn_bwd(proj, c, crow, oa, dycat, bsz, seq, nw):
    tq = _tile(seq, 256, 8)
    scale = HEAD ** -0.5

    def body(q_ref, k_ref, v_ref, g_ref, c_ref, crow_ref, o_ref, dy_ref,
             dq_ref, dk_ref, dv_ref, dg_ref, dc_ref, dk_acc, dv_acc, dc_acc):
        hp = pl.program_id(1)
        c_blk = c_ref[...]
        dc_ref[...] = jnp.zeros_like(dc_ref)
        for hh in range(2):
            sl = slice(hh * HEAD, (hh + 1) * HEAD)
            ccol = _head_column(c_blk, 2 * hp + hh)
            crow_h = crow_ref[0, 0, hh:hh + 1, :]
            qs = (q_ref[:, sl] * scale).astype(BF16)
            kh = k_ref[:, sl].astype(BF16)
            vh = v_ref[:, sl].astype(BF16)
            oh = o_ref[:, sl]
            dyh = dy_ref[:, sl].astype(F32)
            silu, dsilu = _silu_and_grad(g_ref[:, sl])
            dg_ref[:, sl] = (dyh * oh * dsilu).astype(BF16)
            do = dyh * silu
            dvec = jnp.sum(do * oh, axis=-1, keepdims=True)
            dob = do.astype(BF16)
            dk_acc[...] = jnp.zeros_like(dk_acc)
            dv_acc[...] = jnp.zeros_like(dv_acc)
            dc_acc[...] = jnp.zeros_like(dc_acc)
            for qi in range(seq // tq):
                r0, kend = qi * tq, (qi + 1) * tq
                s = _attn_scores(qs, kh, ccol, crow_h, r0, kend, tq)
                p = jnp.exp(s - jnp.max(s, axis=-1, keepdims=True))
                p = p / jnp.sum(p, axis=-1, keepdims=True)
                dp = lax.dot_general(dob[r0:kend], vh[:kend], NT_DIMS, preferred_element_type=F32)
                ds = p * (dp - dvec[r0:kend])
                dsb = ds.astype(BF16)
                dq_ref[r0:kend, sl] = (jnp.dot(dsb, kh[:kend], preferred_element_type=F32) * scale).astype(BF16)
                dk_acc[0:kend, :] += lax.dot_general(dsb, qs[r0:kend], TN_DIMS, preferred_element_type=F32)
                dv_acc[0:kend, :] += lax.dot_general(p.astype(BF16), dob[r0:kend], TN_DIMS, preferred_element_type=F32)
                dc_acc[:, 0:kend] += -_colsum(ds)
                rowsum = lax.dot_general(jnp.ones((8, kend), F32), ds, NT_DIMS, precision=HI, preferred_element_type=F32)
                dc_acc[:, r0:kend] += rowsum[0:1, :]
            dk_ref[:, sl] = dk_acc[...].astype(BF16)
            dv_ref[:, sl] = dv_acc[...].astype(BF16)
            dc_ref[0, 0, hh:hh + 1, :] = dc_acc[...]

    blk = lambda cb: pl.BlockSpec((seq, LANE), lambda b, j: (b, cb * nw + j))
    own = pl.BlockSpec((seq, LANE), lambda b, j: (b, j))
    n = bsz * seq
    return pl.pallas_call(
        body, name="fox_attn_bwd", grid=(bsz, nw),
        in_specs=[blk(0), blk(1), blk(2), blk(3), pl.BlockSpec((seq, LANE), lambda b, j: (b, 0)),
                  pl.BlockSpec((1, 1, 8, seq), lambda b, j: (b, j, 0, 0)), own, own],
        out_specs=[own] * 4 + [pl.BlockSpec((1, 1, 8, seq), lambda b, j: (b, j, 0, 0))],
        out_shape=[jax.ShapeDtypeStruct((n, nw * LANE), BF16)] * 4 + [jax.ShapeDtypeStruct((bsz, nw, 8, seq), F32)],
        scratch_shapes=[pltpu.VMEM((seq, HEAD), F32), pltpu.VMEM((seq, HEAD), F32), pltpu.VMEM((1, seq), F32)],
        compiler_params=_cparams(("parallel", "parallel")),
    )(proj, proj, proj, proj, c, crow, oa, dycat)


def _shift_mix(x, mu):
    prev = jnp.where(_iota(x.shape, 0) == 0, 0.0, pltpu.roll(x, 1, 0))
    return x + (prev - x) * mu, prev


def _rwkv_pre_math(r_in, k_in, v_in, wa_in, mu_r, mu_k, mu_v, mu_wa, w0, w2, a0, a2, k_k, k_a, ones):
    r, prev_r = _shift_mix(r_in, mu_r)
    k, prev_k = _shift_mix(k_in, mu_k)
    v, prev_v = _shift_mix(v_in, mu_v)
    wa, prev_wa = _shift_mix(wa_in, mu_wa)
    tw = jnp.tanh(wa[:, :LORA])
    alo = wa[:, LORA:]
    sg = _sigmoid(w0 + jnp.dot(tw, w2, precision=HI, preferred_element_type=F32))
    e = sg * 0.6065306597126334
    w = jnp.exp(-e)
    a = _sigmoid(a0 + jnp.dot(alo, a2, precision=HI, preferred_element_type=F32))
    kkraw = k * k_k
    nrm = jnp.sqrt(_hsum(kkraw * kkraw, ones))
    den = jnp.maximum(nrm, 1e-12)
    kk = kkraw / den
    k2 = k * (1.0 + (a - 1.0) * k_a)
    return dict(r=r, k=k, v=v, prev_r=prev_r, prev_k=prev_k, prev_v=prev_v, prev_wa=prev_wa, tw=tw, alo=alo,
                sg=sg, e=e, w=w, a=a, nrm=nrm, den=den, kk=kk, k2=k2)


def _rwkv_pre_specs(seq, nw):
    tok = lambda cb: pl.BlockSpec((seq, LANE), lambda b, j: (b, cb * nw + j))
    par = pl.BlockSpec((1, LANE), lambda b, j: (0, j))
    lora = pl.BlockSpec((LORA, LANE), lambda b, j: (0, j))
    return [tok(4), tok(5), tok(6), pl.BlockSpec((seq, LANE), lambda b, j: (b, 8 * nw)),
            par, par, par, pl.BlockSpec((1, LANE), lambda b, j: (0, 0)), par, lora, par, lora, par, par]


def rwkv_pre_fwd(proj, prm, bsz, seq, nw):
    def body(r_ref, k_ref, v_ref, wa_ref, mur, muk, muv, muwa, w0, w2, a0, a2, kk_, ka_,
             ro, wo, ko, vo, kko, kao):
        m = _rwkv_pre_math(r_ref[...], k_ref[...], v_ref[...], wa_ref[...], mur[...], muk[...], muv[...], muwa[...],
                           w0[...], w2[...], a0[...], a2[...], kk_[...], ka_[...], _head_ones())
        ro[...] = m["r"]
        wo[...] = m["w"]
        ko[...] = m["k2"]
        vo[...] = m["v"]
        kko[...] = m["kk"]
        kao[...] = m["kk"] * m["a"]

    n = bsz * seq
    own = pl.BlockSpec((seq, LANE), lambda b, j: (b, j))
    return pl.pallas_call(
        body, name="rwkv_pre_fwd", grid=(bsz, nw), in_specs=_rwkv_pre_specs(seq, nw), out_specs=[own] * 6,
        out_shape=[jax.ShapeDtypeStruct((n, nw * LANE), F32)] * 6, compiler_params=_cparams(("parallel", "parallel")),
    )(proj, proj, proj, proj, prm["mu_r"], prm["mu_k"], prm["mu_v"], prm["mu_wa"], prm["w0"], prm["w2"],
      prm["a0"], prm["a2"], prm["k_k"], prm["k_a"])


def rwkv_pre_bwd(proj, prm, grads_scan, grads_bonus, bsz, seq, nw):
    wid = nw * LANE

    def body(r_ref, k_ref, v_ref, wa_ref, mur, muk, muv, muwa, w0, w2, a0, a2, kk_, ka_,
             dr_s, dw_s, dk_s, dv_s, dkk_s, dka_s, dr_b, dk_b, dv_b,
             dxr, dxk, dxv, dxwa, dmur, dmuk, dmuv, dmuwa, dw0, dw2, da0, da2, dkk_p, dka_p, dwa_acc):
        b, j = pl.program_id(0), pl.program_id(1)
        cols = pl.ds(pl.multiple_of(j * LANE, LANE), LANE)
        acc_refs = (dmur, dmuk, dmuv, dmuwa, dw0, dw2, da0, da2, dkk_p, dka_p)

        @pl.when(jnp.logical_and(b == 0, j == 0))
        def _():
            for ref in acc_refs:
                ref[...] = jnp.zeros_like(ref)

        @pl.when(j == 0)
        def _():
            dwa_acc[...] = jnp.zeros_like(dwa_acc)

        ones = _head_ones()
        r_in, k_in, v_in, wa_in = r_ref[...], k_ref[...], v_ref[...], wa_ref[...]
        m = _rwkv_pre_math(r_in, k_in, v_in, wa_in, mur[...], muk[...], muv[...], muwa[...],
                           w0[...], w2[...], a0[...], a2[...], kk_[...], ka_[...], ones)
        k, a, kk = m["k"], m["a"], m["kk"]
        dr = dr_s[...] + dr_b[...]
        dk2 = dk_s[...] + dk_b[...]
        dv = dv_s[...] + dv_b[...]
        dka = dka_s[...]
        da = dka * kk + dk2 * k * ka_[...]
        dkk = dkk_s[...] + dka * a
        dkkraw = jnp.where(m["nrm"] > 1e-12, dkk - kk * _hsum(dkk * kk, ones), dkk) / m["den"]
        dk = dkkraw * kk_[...] + dk2 * (1.0 + (a - 1.0) * ka_[...])
        dkk_p[:, cols] += _colsum(dkkraw * k)
        dka_p[:, cols] += _colsum(dk2 * k * (a - 1.0))
        dza = da * a * (1.0 - a)
        da0[:, cols] += _colsum(dza)
        da2[:, cols] += lax.dot_general(m["alo"], dza, TN_DIMS, precision=HI, preferred_element_type=F32)
        dalo = lax.dot_general(dza, a2[...], NT_DIMS, precision=HI, preferred_element_type=F32)
        dzw = -dw_s[...] * m["w"] * m["e"] * (1.0 - m["sg"])
        dw0[:, cols] += _colsum(dzw)
        dw2[:, cols] += lax.dot_general(m["tw"], dzw, TN_DIMS, precision=HI, preferred_element_type=F32)
        dtw = lax.dot_general(dzw, w2[...], NT_DIMS, precision=HI, preferred_element_type=F32)
        dwa_acc[:, 0:LORA] += dtw * (1.0 - m["tw"] * m["tw"])
        dwa_acc[:, LORA:LANE] += dalo

        def shift_bwd(dxs, x, prev, mu):
            g = dxs * mu
            nxt = jnp.where(_iota(g.shape, 0) == seq - 1, 0.0, pltpu.roll(g, seq - 1, 0))
            return dxs * (1.0 - mu) + nxt, _colsum(dxs * (prev - x))

        for dxs, x, prev, mu, out, dmu in ((dr, r_in, m["prev_r"], mur, dxr, dmur), (dk, k_in, m["prev_k"], muk, dxk, dmuk),
                                           (dv, v_in, m["prev_v"], muv, dxv, dmuv)):
            dx, dm = shift_bwd(dxs, x, prev, mu[...])
            out[...] = dx.astype(BF16)
            dmu[:, cols] += dm

        @pl.when(j == nw - 1)
        def _():
            dx, dm = shift_bwd(dwa_acc[...], wa_in, m["prev_wa"], muwa[...])
            dxwa[...] = dx.astype(BF16)
            dmuwa[...] += dm

    n = bsz * seq
    own = pl.BlockSpec((seq, LANE), lambda b, j: (b, j))
    whole = lambda shape: pl.BlockSpec(shape, lambda b, j: (0,) * len(shape))
    row = jax.ShapeDtypeStruct((1, wid), F32)
    return pl.pallas_call(
        body, name="rwkv_pre_bwd", grid=(bsz, nw), in_specs=_rwkv_pre_specs(seq, nw) + [own] * 9,
        out_specs=[own] * 3 + [pl.BlockSpec((seq, LANE), lambda b, j: (b, 0))]
        + [whole((1, wid))] * 3 + [whole((1, LANE)), whole((1, wid)), whole((LORA, wid)), whole((1, wid)),
                                   whole((LORA, wid)), whole((1, wid)), whole((1, wid))],
        out_shape=[jax.ShapeDtypeStruct((n, wid), BF16)] * 3 + [jax.ShapeDtypeStruct((n, LANE), BF16)]
        + [row] * 3 + [jax.ShapeDtypeStruct((1, LANE), F32), row, jax.ShapeDtypeStruct((LORA, wid), F32), row,
                       jax.ShapeDtypeStruct((LORA, wid), F32), row, row],
        scratch_shapes=[pltpu.VMEM((seq, LANE), F32)],
        compiler_params=_cparams(("arbitrary", "arbitrary")),
    )(proj, proj, proj, proj, prm["mu_r"], prm["mu_k"], prm["mu_v"], prm["mu_wa"], prm["w0"], prm["w2"],
      prm["a0"], prm["a2"], prm["k_k"], prm["k_a"], *grads_scan, *grads_bonus)


def _scan_group(nw):
    return _tile(nw, 4, 1)


ROWS = 8


def _cols8(tile, eye, ones):
    big = _hsum(jnp.concatenate([tile[q:q + 1, :] * eye for q in range(ROWS)], axis=0), ones)
    return [big[q * HEAD:(q + 1) * HEAD, :] for q in range(ROWS)]


def _rows8(mats, eye, ones):
    big = _hsum(jnp.concatenate(mats, axis=0), ones)
    return _stack_rows([_colsum(big[q * HEAD:(q + 1) * HEAD, :] * eye) for q in range(ROWS)])


def _stack_rows(rows):
    rid = _iota((ROWS, LANE), 0)
    out = jnp.zeros((ROWS, LANE), F32)
    for q in range(ROWS):
        out = jnp.where(rid == q, rows[q], out)
    return out


def rwkv_scan_fwd(r, w, k, v, kk, ka, bsz, seq, nw):
    grp = _scan_group(nw)
    tc = _tile(seq, 32, 8)
    nt = seq // tc

    def body(r_ref, w_ref, k_ref, v_ref, kk_ref, ka_ref, y_ref, st_ref, s_ref):
        @pl.when(pl.program_id(2) == 0)
        def _():
            s_ref[...] = jnp.zeros_like(s_ref)

        ones, eye = _head_ones(), _head_eye()

        def step(i8, carry):
            base = pl.multiple_of(i8 * ROWS, ROWS)
            for g in range(grp):
                cs = slice(g * LANE, (g + 1) * LANE)
                tile = lambda ref: ref[pl.ds(base, ROWS), cs]
                rt, wt, kt, kkt, kat = tile(r_ref), tile(w_ref), tile(k_ref), tile(kk_ref), tile(ka_ref)
                vcols = _cols8(tile(v_ref), eye, ones)
                s = s_ref[g]
                outs = []
                for q in range(ROWS):
                    row = lambda t: t[q:q + 1, :]
                    st_ref[base + q, g] = s
                    sa = _hsum(s * row(kkt), ones)
                    s = s * row(wt) - sa * row(kat) + vcols[q] * row(kt)
                    outs.append(s * row(rt))
                y_ref[pl.ds(base, ROWS), cs] = _rows8(outs, eye, ones)
                s_ref[g] = s
            return carry

        lax.fori_loop(0, tc // ROWS, step, 0)

    tok = pl.BlockSpec((tc, grp * LANE), lambda b, g, t: (b * nt + t, g))
    n = bsz * seq
    return pl.pallas_call(
        body, name="rwkv_scan_fwd", grid=(bsz, nw // grp, nt), in_specs=[tok] * 6,
        out_specs=[tok, pl.BlockSpec((tc, grp, HEAD, LANE), lambda b, g, t: (b * nt + t, g, 0, 0))],
        out_shape=[jax.ShapeDtypeStruct((n, nw * LANE), F32), jax.ShapeDtypeStruct((n, nw, HEAD, LANE), F32)],
        scratch_shapes=[pltpu.VMEM((grp, HEAD, LANE), F32)],
        compiler_params=_cparams(("parallel", "parallel", "arbitrary")),
    )(r, w, k, v, kk, ka)


def rwkv_scan_bwd(r, w, k, v, kk, ka, states, dy, bsz, seq, nw):
    grp = _scan_group(nw)
    tc = _tile(seq, 32, 8)
    nt = seq // tc
    nblk = tc // ROWS

    def body(r_ref, w_ref, k_ref, v_ref, kk_ref, ka_ref, st_ref, dy_ref,
             dr_ref, dw_ref, dk_ref, dv_ref, dkk_ref, dka_ref, ds_ref):
        @pl.when(pl.program_id(2) == 0)
        def _():
            ds_ref[...] = jnp.zeros_like(ds_ref)

        ones, eye = _head_ones(), _head_eye()

        def step(ii, carry):
            base = pl.multiple_of((nblk - 1 - ii) * ROWS, ROWS)
            for g in range(grp):
                cs = slice(g * LANE, (g + 1) * LANE)
                tile = lambda ref: ref[pl.ds(base, ROWS), cs]
                rt, wt, kt, kkt, kat = tile(r_ref), tile(w_ref), tile(k_ref), tile(kk_ref), tile(ka_ref)
                vcols = _cols8(tile(v_ref), eye, ones)
                dycols = _cols8(tile(dy_ref), eye, ones)
                prev = [st_ref[base + q, g] for q in range(ROWS)]
                sa_all = _hsum(jnp.concatenate([prev[q] * kkt[q:q + 1, :] for q in range(ROWS)], axis=0), ones)
                ds = ds_ref[g]
                drs, dks, dkas, dws, dkks, dvm = ([None] * ROWS for _ in range(6))
                for q in reversed(range(ROWS)):
                    row = lambda t: t[q:q + 1, :]
                    sp = prev[q]
                    sa = sa_all[q * HEAD:(q + 1) * HEAD, :]
                    st = sp * row(wt) - sa * row(kat) + vcols[q] * row(kt)
                    ds = ds + dycols[q] * row(rt)
                    drs[q] = _colsum(st * dycols[q])
                    dvm[q] = ds * row(kt)
                    dks[q] = _colsum(ds * vcols[q])
                    dsa = -_hsum(ds * row(kat), ones)
                    dkas[q] = -_colsum(ds * sa)
                    dws[q] = _colsum(ds * sp)
                    dkks[q] = _colsum(sp * dsa)
                    ds = ds * row(wt) + dsa * row(kkt)
                ds_ref[g] = ds
                rows = pl.ds(base, ROWS)
                dr_ref[rows, cs] = _stack_rows(drs)
                dk_ref[rows, cs] = _stack_rows(dks)
                dka_ref[rows, cs] = _stack_rows(dkas)
                dw_ref[rows, cs] = _stack_rows(dws)
                dkk_ref[rows, cs] = _stack_rows(dkks)
                dv_ref[rows, cs] = _rows8(dvm, eye, ones)
            return carry

        lax.fori_loop(0, nblk, step, 0)

    tok = pl.BlockSpec((tc, grp * LANE), lambda b, g, t: (b * nt + nt - 1 - t, g))
    n = bsz * seq
    return pl.pallas_call(
        body, name="rwkv_scan_bwd", grid=(bsz, nw // grp, nt),
        in_specs=[tok] * 6 + [pl.BlockSpec((tc, grp, HEAD, LANE), lambda b, g, t: (b * nt + nt - 1 - t, g, 0, 0)), tok],
        out_specs=[tok] * 6, out_shape=[jax.ShapeDtypeStruct((n, nw * LANE), F32)] * 6,
        scratch_shapes=[pltpu.VMEM((grp, HEAD, LANE), F32)],
        compiler_params=_cparams(("parallel", "parallel", "arbitrary")),
    )(r, w, k, v, kk, ka, states, dy)


def _rwkv_post_math(y, r, k2, v, ln_g, ln_b, r_k, ones):
    d = y - _hsum(y, ones) * (1.0 / HEAD)
    rstd = lax.rsqrt(_hsum(d * d, ones) * (1.0 / HEAD) + GN_EPS)
    yn = d * rstd
    s = _hsum(r * k2 * r_k, ones)
    return yn, rstd, s, yn * ln_g + ln_b + s * v


def _rwkv_post_specs(tm, nw):
    tok = pl.BlockSpec((tm, LANE), lambda i, j: (i, j))
    par = pl.BlockSpec((1, LANE), lambda i, j: (0, j))
    return [tok] * 4 + [pl.BlockSpec((tm, LANE), lambda i, j: (i, 7 * nw + j)), par, par, par]


def rwkv_post_fwd(y, r, k2, v, proj, prm, nw):
    n = y.shape[0]
    tm = _tile(n, 512, 8)

    def body(y_ref, r_ref, k_ref, v_ref, g_ref, lg, lb, rk, o_ref):
        ob = _rwkv_post_math(y_ref[...], r_ref[...], k_ref[...], v_ref[...], lg[...], lb[...], rk[...], _head_ones())[3]
        o_ref[...] = (ob * _silu_and_grad(g_ref[...])[0]).astype(BF16)

    return pl.pallas_call(
        body, name="rwkv_post_fwd", grid=(n // tm, nw), in_specs=_rwkv_post_specs(tm, nw),
        out_specs=pl.BlockSpec((tm, LANE), lambda i, j: (i, j)), out_shape=jax.ShapeDtypeStruct((n, nw * LANE), BF16),
        compiler_params=_cparams(("parallel", "parallel")),
    )(y, r, k2, v, proj, prm["ln_g"], prm["ln_b"], prm["r_k"])


def rwkv_post_bwd(y, r, k2, v, proj, prm, dycat, nw):
    n = y.shape[0]
    tm = _tile(n, 512, 8)
    wid = nw * LANE

    def body(y_ref, r_ref, k_ref, v_ref, g_ref, lg, lb, rk, dyc_ref,
             dy_ref, dr_ref, dk_ref, dv_ref, dg_ref, dlg, dlb, drk):
        i, j = pl.program_id(0), pl.program_id(1)
        cols = pl.ds(pl.multiple_of(j * LANE, LANE), LANE)

        @pl.when(jnp.logical_and(i == 0, j == 0))
        def _():
            for ref in (dlg, dlb, drk):
                ref[...] = jnp.zeros_like(ref)

        ones = _head_ones()
        rr, kr, vr = r_ref[...], k_ref[...], v_ref[...]
        yn, rstd, s, ob = _rwkv_post_math(y_ref[...], rr, kr, vr, lg[...], lb[...], rk[...], ones)
        silu, dsilu = _silu_and_grad(g_ref[...])
        dyc = dyc_ref[...].astype(F32)
        dg_ref[...] = (dyc * ob * dsilu).astype(BF16)
        dob = dyc * silu
        dlg[:, cols] += _colsum(dob * yn)
        dlb[:, cols] += _colsum(dob)
        dyn = dob * lg[...]
        dy_ref[...] = rstd * (dyn - _hsum(dyn, ones) * (1.0 / HEAD) - yn * _hsum(dyn * yn, ones) * (1.0 / HEAD))
        dv_ref[...] = dob * s
        dsum = _hsum(dob * vr, ones)
        dr_ref[...] = dsum * kr * rk[...]
        dk_ref[...] = dsum * rr * rk[...]
        drk[:, cols] += _colsum(dsum * rr * kr)

    tok = pl.BlockSpec((tm, LANE), lambda i, j: (i, j))
    whole = pl.BlockSpec((1, wid), lambda i, j: (0, 0))
    return pl.pallas_call(
        body, name="rwkv_post_bwd", grid=(n // tm, nw),
        in_specs=_rwkv_post_specs(tm, nw) + [pl.BlockSpec((tm, LANE), lambda i, j: (i, nw + j))],
        out_specs=[tok] * 5 + [whole] * 3,
        out_shape=[jax.ShapeDtypeStruct((n, wid), F32)] * 4 + [jax.ShapeDtypeStruct((n, wid), BF16)]
        + [jax.ShapeDtypeStruct((1, wid), F32)] * 3,
        compiler_params=_cparams(("arbitrary", "arbitrary")),
    )(y, r, k2, v, proj, prm["ln_g"], prm["ln_b"], prm["r_k"], dycat)


def _sgu_math(blk, ln_g, ln_b, ws_ref, bb_ref, mixed_ref, d, ngr):
    u, v, g = blk[:, 0:d], blk[:, d:2 * d], blk[:, 2 * d:3 * d]
    gu, dgu = _gelu_and_grad(u)
    gv, dgv = _gelu_and_grad(v)
    cen = gv - jnp.mean(gv, axis=-1, keepdims=True)
    rstd = lax.rsqrt(jnp.mean(cen * cen, axis=-1, keepdims=True) + LN_EPS)
    vhat = cen * rstd
    vn = vhat * ln_g + ln_b
    tril = _iota((GMLP_CHUNK, GMLP_CHUNK), 0) >= _iota((GMLP_CHUNK, GMLP_CHUNK), 1)
    for gi in range(ngr):
        cs = slice(gi * LANE, (gi + 1) * LANE)
        wm = jnp.where(tril, ws_ref[gi], 0.0)
        mixed_ref[:, cs] = jnp.dot(wm, vn[:, cs], precision=HI, preferred_element_type=F32) + bb_ref[gi]
    return dict(g=g, gu=gu, dgu=dgu, dgv=dgv, rstd=rstd, vhat=vhat, vn=vn, tril=tril)


def sgu_fwd(proj, ln_g, ln_b, w_s, b_bc):
    n, d3 = proj.shape
    d = d3 // 3
    ngr = d // LANE

    def body(p_ref, lg, lb, ws_ref, bb_ref, o_ref, mixed_ref):
        m = _sgu_math(p_ref[...], lg[...], lb[...], ws_ref, bb_ref, mixed_ref, d, ngr)
        o_ref[...] = (m["gu"] * mixed_ref[...] * _silu_and_grad(m["g"])[0]).astype(BF16)

    return pl.pallas_call(
        body, name="sgu_fwd", grid=(n // GMLP_CHUNK,),
        in_specs=[_row_spec(GMLP_CHUNK, d3), _full_spec((1, d)), _full_spec((1, d)),
                  _full_spec((ngr, GMLP_CHUNK, GMLP_CHUNK)), _full_spec((ngr, GMLP_CHUNK, LANE))],
        out_specs=_row_spec(GMLP_CHUNK, d), out_shape=jax.ShapeDtypeStruct((n, d), BF16),
        scratch_shapes=[pltpu.VMEM((GMLP_CHUNK, d), F32)], compiler_params=_cparams(("parallel",)),
    )(proj, ln_g, ln_b, w_s, b_bc)


def sgu_bwd(proj, ln_g, ln_b, w_s, b_bc, dyin):
    n, d3 = proj.shape
    d = d3 // 3
    ngr = d // LANE
    nsteps = n // GMLP_CHUNK

    def body(p_ref, lg, lb, ws_ref, bb_ref, dy_ref, dp_ref, dws_ref, dbs_ref, dlg_ref, dlb_ref,
             mixed_ref, dvn_ref, dbacc_ref):
        step = pl.program_id(0)

        @pl.when(step == 0)
        def _():
            for ref in (dws_ref, dlg_ref, dlb_ref, dbacc_ref):
                ref[...] = jnp.zeros_like(ref)

        m = _sgu_math(p_ref[...], lg[...], lb[...], ws_ref, bb_ref, mixed_ref, d, ngr)
        silu, dsilu = _silu_and_grad(m["g"])
        dyv = dy_ref[...].astype(F32)
        mixed = mixed_ref[...]
        dp_ref[:, 2 * d:3 * d] = (dyv * m["gu"] * mixed * dsilu).astype(BF16)
        doc = dyv * silu
        dp_ref[:, 0:d] = (doc * mixed * m["dgu"]).astype(BF16)
        dmixed = doc * m["gu"]
        for gi in range(ngr):
            cs = slice(gi * LANE, (gi + 1) * LANE)
            dm = dmixed[:, cs]
            wm = jnp.where(m["tril"], ws_ref[gi], 0.0)
            dws_ref[gi] += jnp.where(m["tril"], lax.dot_general(dm, m["vn"][:, cs], NT_DIMS, precision=HI,
                                                                preferred_element_type=F32), 0.0)
            dbacc_ref[gi] += dm
            dvn_ref[:, cs] = lax.dot_general(wm, dm, TN_DIMS, precision=HI, preferred_element_type=F32)
        dvn = dvn_ref[...]
        dlg_ref[...] += _colsum(dvn * m["vhat"])
        dlb_ref[...] += _colsum(dvn)
        dvh = dvn * lg[...]
        dgv = m["rstd"] * (dvh - jnp.mean(dvh, axis=-1, keepdims=True)
                           - m["vhat"] * jnp.mean(dvh * m["vhat"], axis=-1, keepdims=True))
        dp_ref[:, d:2 * d] = (dgv * m["dgv"]).astype(BF16)

        @pl.when(step == nsteps - 1)
        def _():
            dbs_ref[...] = jnp.sum(dbacc_ref[...], axis=-1, keepdims=True)

    return pl.pallas_call(
        body, name="sgu_bwd", grid=(nsteps,),
        in_specs=[_row_spec(GMLP_CHUNK, d3), _full_spec((1, d)), _full_spec((1, d)),
                  _full_spec((ngr, GMLP_CHUNK, GMLP_CHUNK)), _full_spec((ngr, GMLP_CHUNK, LANE)), _row_spec(GMLP_CHUNK, d)],
        out_specs=[_row_spec(GMLP_CHUNK, d3), _full_spec((ngr, GMLP_CHUNK, GMLP_CHUNK)), _full_spec((ngr, GMLP_CHUNK, 1)),
                   _full_spec((1, d)), _full_spec((1, d))],
        out_shape=[jax.ShapeDtypeStruct((n, d3), BF16), jax.ShapeDtypeStruct((ngr, GMLP_CHUNK, GMLP_CHUNK), F32),
                   jax.ShapeDtypeStruct((ngr, GMLP_CHUNK, 1), F32), jax.ShapeDtypeStruct((1, d), F32),
                   jax.ShapeDtypeStruct((1, d), F32)],
        scratch_shapes=[pltpu.VMEM((GMLP_CHUNK, d), F32), pltpu.VMEM((GMLP_CHUNK, d), F32),
                        pltpu.VMEM((ngr, GMLP_CHUNK, LANE), F32)],
        compiler_params=_cparams(("arbitrary",)),
    )(proj, ln_g, ln_b, w_s, b_bc, dyin)


def _ab_segments(wid, heads):
    o = {}
    o["qkv"] = (0, 3 * wid)
    o["f"] = (3 * wid, 3 * wid + heads)
    o["ga"] = (3 * wid + heads, 4 * wid + heads)
    o["rkv"] = (4 * wid + heads, 7 * wid + heads)
    o["wa"] = (7 * wid + heads, 7 * wid + heads + 2 * LORA)
    o["gb"] = (7 * wid + heads + 2 * LORA, 8 * wid + heads + 2 * LORA)
    return o


def _ab_reorder(w, wid, heads):
    seg = _ab_segments(wid, heads)
    cut = lambda name: w[:, seg[name][0]:seg[name][1]]
    pad = jnp.zeros((w.shape[0], LANE - heads), w.dtype)
    return jnp.concatenate([cut("qkv"), cut("ga"), cut("rkv"), cut("gb"), cut("wa"), cut("f"), pad], axis=1)


def _ab_restore(g, wid, heads):
    return jnp.concatenate([g[:, 0:3 * wid], g[:, 8 * wid + LANE:8 * wid + LANE + heads], g[:, 3 * wid:4 * wid],
                            g[:, 4 * wid:7 * wid], g[:, 8 * wid:8 * wid + LANE], g[:, 7 * wid:8 * wid]], axis=1)


def local_step(x, p, target, wts):
    bsz, seq, d = x.shape
    n = bsz * seq
    wid = d // 2
    nw = wid // LANE
    heads = wid // HEAD
    x2 = x.reshape(n, d)
    tgt = target.reshape(n, d)
    p0, p1 = p[0].reshape(n, -1), p[1].reshape(n, -1)
    row = lambda a: a.reshape(1, -1)
    g_pre0, g_pre1 = row(wts["norm_pre"][0]), row(wts["norm_pre"][1])
    g_post0, g_post1 = row(wts["norm_post"][0]), row(wts["norm_post"][1])
    w_ab = _ab_reorder(wts["ab_w_in"], wid, heads)
    w_out, w_c, w_cout = wts["ab_w_out"], wts["c_w_in"], wts["c_w_out"]
    wp0, wp1 = wts["ple_w_proj"][0], wts["ple_w_proj"][1]
    wg0, wg1 = wts["ple_w_gate"][0], wts["ple_w_gate"][1]
    fbias = jnp.pad(row(wts["fox_f_bias"]), ((0, 0), (0, LANE - heads)))
    mu = row(wts["rwkv_mu"])
    prm = dict(mu_r=mu[:, 0:wid], mu_k=mu[:, wid:2 * wid], mu_v=mu[:, 2 * wid:3 * wid], mu_wa=mu[:, 3 * wid:],
               w0=row(wts["rwkv_w0"]), w2=wts["rwkv_w2"], a0=row(wts["rwkv_a0"]), a2=wts["rwkv_a2"],
               k_k=row(wts["rwkv_k_k"]), k_a=row(wts["rwkv_k_a"]), r_k=row(wts["rwkv_r_k"]),
               ln_g=row(wts["rwkv_ln_g"]), ln_b=row(wts["rwkv_ln_b"]))
    c_ln_g, c_ln_b = row(wts["c_ln_g"]), row(wts["c_ln_b"])
    w_s = wts["c_w_s"]
    b_bc = jnp.broadcast_to(wts["c_b_s"][:, :, None], w_s.shape[:2] + (LANE,))

    xn0 = rms_fwd(x2, g_pre0, name="rms_pre0")
    proj0 = matmul(xn0, w_ab, name="ab_in")
    c = fox_gate_fwd(proj0, fbias, bsz, seq, nw)
    crow = jnp.pad(c.reshape(bsz, seq, LANE)[:, :, :heads].transpose(0, 2, 1).reshape(bsz, nw, 2, seq),
                   ((0, 0), (0, 0), (0, 6), (0, 0)))
    oa, ya = fox_attn_fwd(proj0, c, crow, bsz, seq, nw)
    sr, sw, sk, sv, skk, ska = rwkv_pre_fwd(proj0, prm, bsz, seq, nw)
    ysc, states = rwkv_scan_fwd(sr, sw, sk, sv, skk, ska, bsz, seq, nw)
    yb = rwkv_post_fwd(ysc, sr, sk, sv, proj0, prm, nw)
    ycat = jnp.concatenate([ya, yb], axis=1)
    y0 = matmul(ycat, w_out, name="ab_out")
    h1, h1b = rms_res_fwd(y0, x2, g_post0, name="rms_post0")
    a0 = matmul(p0, wp0, name="ple_proj0")
    z0 = matmul(h1b, wg0, name="ple_gate0")
    h1p = ple_fwd(h1, a0, z0, name="ple_fwd0")
    xn1 = rms_fwd(h1p, g_pre1, name="rms_pre1")
    proj1 = matmul(xn1, w_c, name="c_in")
    yin = sgu_fwd(proj1, c_ln_g, c_ln_b, w_s, b_bc)
    y1 = matmul(yin, w_cout, name="c_out")
    h2, h2b = rms_res_fwd(y1, h1p, g_post1, name="rms_post1")
    a1 = matmul(p1, wp1, name="ple_proj1")
    z1 = matmul(h2b, wg1, name="ple_gate1")
    h2p = ple_fwd(h2, a1, z1, name="ple_fwd1")
    dh, loss = loss_head(h2p, tgt, name="loss_head")

    g = {}
    da1, dz1 = ple_bwd(dh, a1, z1, name="ple_bwd1")
    g_wp1 = matmul(p1, da1, ta=True, name="d_ple_proj1")
    g_wg1 = matmul(h2b, dz1, ta=True, name="d_ple_gate1")
    dh2 = matmul(dz1, wg1, tb=True, add=dh, name="dx_ple_gate1")
    dy1, g_post1_g = rms_bwd(y1, dh2, g_post1, out_dtype=BF16, name="rms_post1_bwd")
    g["c_w_out"] = matmul(yin, dy1, ta=True, name="d_c_out")
    dyin = matmul(dy1, w_cout, tb=True, out_dtype=BF16, name="dx_c_out")
    dproj1, g["c_w_s"], g_bs, g["c_ln_g"], g["c_ln_b"] = sgu_bwd(proj1, c_ln_g, c_ln_b, w_s, b_bc, dyin)
    g["c_b_s"] = g_bs.reshape(w_s.shape[:2])
    g["c_w_in"] = matmul(xn1, dproj1, ta=True, name="d_c_in")
    dxn1 = matmul(dproj1, w_c, tb=True, out_dtype=BF16, name="dx_c_in")
    dh1p, g_pre1_g = rms_bwd(h1p, dxn1, g_pre1, res=dh2, name="rms_pre1_bwd")
    da0, dz0 = ple_bwd(dh1p, a0, z0, name="ple_bwd0")
    g_wp0 = matmul(p0, da0, ta=True, name="d_ple_proj0")
    g_wg0 = matmul(h1b, dz0, ta=True, name="d_ple_gate0")
    dh1 = matmul(dz0, wg0, tb=True, add=dh1p, name="dx_ple_gate0")
    dy0, g_post0_g = rms_bwd(y0, dh1, g_post0, out_dtype=BF16, name="rms_post0_bwd")
    g["ab_w_out"] = matmul(ycat, dy0, ta=True, name="d_ab_out")
    dycat = matmul(dy0, w_out, tb=True, out_dtype=BF16, name="dx_ab_out")
    dysc, dr_b, dk_b, dv_b, dgb, g["rwkv_ln_g"], g["rwkv_ln_b"], g_rk = rwkv_post_bwd(ysc, sr, sk, sv, proj0, prm, dycat, nw)
    g["rwkv_r_k"] = g_rk.reshape(wts["rwkv_r_k"].shape)
    grads_scan = rwkv_scan_bwd(sr, sw, sk, sv, skk, ska, states, dysc, bsz, seq, nw)
    (dxr, dxk, dxv, dxwa, dmur, dmuk, dmuv, dmuwa, g["rwkv_w0"], g["rwkv_w2"], g["rwkv_a0"], g["rwkv_a2"],
     g["rwkv_k_k"], g["rwkv_k_a"]) = rwkv_pre_bwd(proj0, prm, grads_scan, (dr_b, dk_b, dv_b), bsz, seq, nw)
    g["rwkv_mu"] = jnp.concatenate([dmur, dmuk, dmuv, dmuwa], axis=1)
    dq, dk, dv, dga, dcrow = fox_attn_bwd(proj0, c, crow, oa, dycat, bsz, seq, nw)
    dc = jnp.pad(dcrow[:, :, :2, :].reshape(bsz, heads, seq).transpose(0, 2, 1), ((0, 0), (0, 0), (0, LANE - heads)))
    dfa, g_fb = fox_gate_bwd(proj0, fbias, dc.reshape(n, LANE), bsz, seq, nw)
    g["fox_f_bias"] = g_fb[:, :heads]
    dproj0 = jnp.concatenate([dq, dk, dv, dga, dxr, dxk, dxv, dgb, dxwa, dfa], axis=1)
    g["ab_w_in"] = _ab_restore(matmul(xn0, dproj0, ta=True, name="d_ab_in"), wid, heads)
    dxn0 = matmul(dproj0, w_ab, tb=True, out_dtype=BF16, name="dx_ab_in")
    dx, g_pre0_g = rms_bwd(x2, dxn0, g_pre0, res=dh1, name="rms_pre0_bwd")

    g["norm_pre"] = jnp.concatenate([g_pre0_g, g_pre1_g], axis=0)
    g["norm_post"] = jnp.concatenate([g_post0_g, g_post1_g], axis=0)
    g["ple_w_proj"] = jnp.stack([g_wp0, g_wp1])
    g["ple_w_gate"] = jnp.stack([g_wg0, g_wg1])
    return loss, dx.reshape(bsz, seq, d), g


MESH = pl.DeviceIdType.MESH
ANY = pl.BlockSpec(memory_space=pl.ANY)
PACK_COLS = 1024
PACK_ROWS = 16


def _mesh_place():
    xi, yi, ci = lax.axis_index("x"), lax.axis_index("y"), lax.axis_index("c")
    return xi, yi, ci, 4 * xi + 2 * yi + ci


def _peer(xi, yi, ci, m):
    px = 1 - xi if m & 4 else xi
    py = 1 - yi if m & 2 else yi
    pc = 1 - ci if m & 1 else ci
    return (px, py, pc), 4 * px + 2 * py + pc


def all_gather(x, *, name):
    def body(x_ref, out_ref, send_sems, recv_sems, local_sem):
        xi, yi, ci, me = _mesh_place()
        mine = pltpu.make_async_copy(x_ref, out_ref.at[me], local_sem)
        mine.start()
        sends = []
        for m in range(1, N_DEV):
            peer, _ = _peer(xi, yi, ci, m)
            cp = pltpu.make_async_remote_copy(src_ref=x_ref, dst_ref=out_ref.at[me], send_sem=send_sems.at[m - 1],
                                              recv_sem=recv_sems.at[m - 1], device_id=peer, device_id_type=MESH)
            cp.start()
            sends.append(cp)
        for m in range(1, N_DEV):
            peer, pidx = _peer(xi, yi, ci, m)
            pltpu.make_async_remote_copy(src_ref=x_ref, dst_ref=out_ref.at[pidx], send_sem=send_sems.at[m - 1],
                                         recv_sem=recv_sems.at[m - 1], device_id=peer, device_id_type=MESH).wait_recv()
        for cp in sends:
            cp.wait_send()
        mine.wait()

    return pl.pallas_call(
        body, name=name, in_specs=[ANY], out_specs=ANY, out_shape=jax.ShapeDtypeStruct((N_DEV,) + x.shape, x.dtype),
        scratch_shapes=[pltpu.SemaphoreType.DMA((N_DEV - 1,)), pltpu.SemaphoreType.DMA((N_DEV - 1,)), pltpu.SemaphoreType.DMA],
    )(x)


def all_to_all(x, *, name):
    def body(x_ref, out_ref, send_sems, recv_sems, local_sem):
        xi, yi, ci, me = _mesh_place()
        mine = pltpu.make_async_copy(x_ref.at[me], out_ref.at[me], local_sem)
        mine.start()
        sends = []
        for m in range(1, N_DEV):
            peer, pidx = _peer(xi, yi, ci, m)
            cp = pltpu.make_async_remote_copy(src_ref=x_ref.at[pidx], dst_ref=out_ref.at[me], send_sem=send_sems.at[m - 1],
                                              recv_sem=recv_sems.at[m - 1], device_id=peer, device_id_type=MESH)
            cp.start()
            sends.append(cp)
        for m in range(1, N_DEV):
            peer, pidx = _peer(xi, yi, ci, m)
            pltpu.make_async_remote_copy(src_ref=x_ref.at[pidx], dst_ref=out_ref.at[pidx], send_sem=send_sems.at[m - 1],
                                         recv_sem=recv_sems.at[m - 1], device_id=peer, device_id_type=MESH).wait_recv()
        for cp in sends:
            cp.wait_send()
        mine.wait()

    return pl.pallas_call(
        body, name=name, in_specs=[ANY], out_specs=ANY, out_shape=jax.ShapeDtypeStruct(x.shape, x.dtype),
        scratch_shapes=[pltpu.SemaphoreType.DMA((N_DEV - 1,)), pltpu.SemaphoreType.DMA((N_DEV - 1,)), pltpu.SemaphoreType.DMA],
    )(x)


def sum_blocks(x, *, name):
    _, rows, cols = x.shape
    tr = _tile(rows, 256, PACK_ROWS)

    def body(x_ref, o_ref):
        acc = x_ref[0].astype(F32)
        for s in range(1, N_DEV):
            acc = acc + x_ref[s].astype(F32)
        o_ref[...] = acc

    return pl.pallas_call(
        body, name=name, grid=(rows // tr,), in_specs=[pl.BlockSpec((N_DEV, tr, cols), lambda i: (0, i, 0))],
        out_specs=pl.BlockSpec((tr, cols), lambda i: (i, 0)), out_shape=jax.ShapeDtypeStruct((rows, cols), F32),
        compiler_params=_cparams(("parallel",)),
    )(x)


def adamw(w, g, m, v, *, name):
    rows, cols = w.shape
    tr = _tile(rows, 256, 8)
    c1 = 1.0 / (1.0 - ADAM_B1 ** ADAM_STEP)
    c2 = 1.0 / (1.0 - ADAM_B2 ** ADAM_STEP)

    def body(w_ref, g_ref, m_ref, v_ref, d_ref, mo_ref, vo_ref):
        gv = g_ref[...]
        mn = ADAM_B1 * m_ref[...] + (1.0 - ADAM_B1) * gv
        vn = ADAM_B2 * v_ref[...] + (1.0 - ADAM_B2) * (gv * gv)
        d_ref[...] = -ADAM_LR * ((mn * c1) / (jnp.sqrt(vn * c2) + ADAM_EPS) + ADAM_WD * w_ref[...])
        mo_ref[...] = mn
        vo_ref[...] = vn

    spec = pl.BlockSpec((tr, cols), lambda i: (i, 0))
    return pl.pallas_call(
        body, name=name, grid=(rows // tr,), in_specs=[spec] * 4, out_specs=[spec] * 3,
        out_shape=[jax.ShapeDtypeStruct((rows, cols), F32)] * 3, compiler_params=_cparams(("parallel",)),
    )(w, g, m, v)


def _pack(arrays, dtype):
    flat = jnp.concatenate([a.astype(dtype).reshape(-1) for a in arrays])
    unit = PACK_COLS * PACK_ROWS
    total = -(-flat.shape[0] // unit) * unit
    return jnp.pad(flat, (0, total - flat.shape[0])).reshape(total // PACK_COLS, PACK_COLS)


def _unpack(flat2d, shapes, lead=()):
    flat = flat2d.reshape(lead + (-1,))
    out, off = [], 0
    for shp in shapes:
        size = 1
        for s in shp:
            size *= s
        out.append(flat[..., off:off + size].reshape(lead + tuple(shp)))
        off += size
    return out


def _join_shards(sh, axis):
    moved = jnp.moveaxis(sh, 0, axis)
    shape = moved.shape
    return moved.reshape(shape[:axis] + (shape[axis] * shape[axis + 1],) + shape[axis + 2:])


def _split_shards(full, axis):
    shape = full.shape
    cut = full.reshape(shape[:axis] + (N_DEV, shape[axis] // N_DEV) + shape[axis + 1:])
    return jnp.moveaxis(cut, axis, 0)


BIG = (("ab_w_in", 1), ("c_w_in", 1), ("ab_w_out", 0), ("c_w_out", 0), ("ple_w_gate", 1), ("ple_w_proj", 2))
SMALL_SHARDED = (("rwkv_w2", 1), ("rwkv_a2", 1), ("c_ln_g", 0), ("c_ln_b", 0))
REPLICATED = ("norm_pre", "norm_post", "fox_f_bias", "rwkv_mu", "rwkv_w0", "rwkv_a0", "rwkv_k_k", "rwkv_k_a", "rwkv_r_k",
              "rwkv_ln_g", "rwkv_ln_b", "c_w_s", "c_b_s")
WEIGHTS = ("norm_pre", "norm_post", "ab_w_in", "fox_f_bias", "rwkv_mu", "rwkv_w0", "rwkv_w2", "rwkv_a0", "rwkv_a2", "rwkv_k_k",
           "rwkv_k_a", "rwkv_r_k", "rwkv_ln_g", "rwkv_ln_b", "ab_w_out", "c_w_in", "c_ln_g", "c_ln_b", "c_w_s", "c_b_s",
           "c_w_out", "ple_w_proj", "ple_w_gate")
SQUEEZED = ("norm_pre", "norm_post", "ple_w_proj", "ple_w_gate")


def _step(x, p, loss_target, w, mom, vel):
    sq = {k: (a if k in SQUEEZED else a[0]) for k, a in w.items()}
    _, _, _, me = _mesh_place()

    big_shapes = [sq[k].shape for k, _ in BIG]
    gathered = all_gather(_pack([sq[k] for k, _ in BIG], BF16), name="gather_weights")
    full = {k: _join_shards(a, ax) for (k, ax), a in zip(BIG, _unpack(gathered, big_shapes, (N_DEV,)))}
    small_shapes = [sq[k].shape for k, _ in SMALL_SHARDED]
    gathered_s = all_gather(_pack([sq[k] for k, _ in SMALL_SHARDED], F32), name="gather_small_weights")
    full.update({k: _join_shards(a, ax) for (k, ax), a in zip(SMALL_SHARDED, _unpack(gathered_s, small_shapes, (N_DEV,)))})
    for k in REPLICATED:
        full[k] = sq[k]

    loss, grad_x, g = local_step(x, p, loss_target, full)

    parts = all_to_all(jnp.stack([_pack([_split_shards(g[k].reshape(full[k].shape), ax)[d] for k, ax in BIG], BF16)
                                  for d in range(N_DEV)]), name="exchange_grads")
    big_grads = _unpack(sum_blocks(parts, name="sum_grads"), big_shapes)
    grads = {k: a for (k, _), a in zip(BIG, big_grads)}

    small_names = [k for k, _ in SMALL_SHARDED] + list(REPLICATED)
    small_full_shapes = [full[k].shape for k in small_names] + [(1, 1)]
    partial = all_gather(_pack([g[k].reshape(full[k].shape) for k in small_names] + [loss], F32), name="gather_small_grads")
    summed = _unpack(sum_blocks(partial, name="sum_small_grads"), small_full_shapes)
    for k, a in zip(small_names, summed[:-1]):
        grads[k] = a
    for k, ax in SMALL_SHARDED:
        width = sq[k].shape[ax]
        grads[k] = lax.dynamic_slice_in_dim(grads[k], me * width, width, axis=ax)
    loss_all = summed[-1][0, 0]

    outs_g, outs_d, outs_m, outs_v = [], [], [], []
    for k in WEIGHTS:
        shape = w[k].shape
        two_d = lambda a: a.reshape(-1, shape[-1])
        d_w, n_m, n_v = adamw(two_d(w[k]), two_d(grads[k]), two_d(mom[k]), two_d(vel[k]), name="adamw_" + k)
        outs_g.append(grads[k].reshape(shape))
        outs_d.append(d_w.reshape(shape))
        outs_m.append(n_m.reshape(shape))
        outs_v.append(n_v.reshape(shape))
    return (loss_all, grad_x, *outs_g, *outs_d, *outs_m, *outs_v)


def kernel(x, p, norm_pre, norm_post, ab_w_in, fox_f_bias, rwkv_mu, rwkv_w0, rwkv_w2, rwkv_a0, rwkv_a2, rwkv_k_k, rwkv_k_a, rwkv_r_k, rwkv_ln_g, rwkv_ln_b, ab_w_out, c_w_in, c_ln_g, c_ln_b, c_w_s, c_b_s, c_w_out, ple_w_proj, ple_w_gate, loss_target, m_norm_pre, m_norm_post, m_ab_w_in, m_fox_f_bias, m_rwkv_mu, m_rwkv_w0, m_rwkv_w2, m_rwkv_a0, m_rwkv_a2, m_rwkv_k_k, m_rwkv_k_a, m_rwkv_r_k, m_rwkv_ln_g, m_rwkv_ln_b, m_ab_w_out, m_c_w_in, m_c_ln_g, m_c_ln_b, m_c_w_s, m_c_b_s, m_c_w_out, m_ple_w_proj, m_ple_w_gate, v_norm_pre, v_norm_post, v_ab_w_in, v_fox_f_bias, v_rwkv_mu, v_rwkv_w0, v_rwkv_w2, v_rwkv_a0, v_rwkv_a2, v_rwkv_k_k, v_rwkv_k_a, v_rwkv_r_k, v_rwkv_ln_g, v_rwkv_ln_b, v_ab_w_out, v_c_w_in, v_c_ln_g, v_c_ln_b, v_c_w_s, v_c_b_s, v_c_w_out, v_ple_w_proj, v_ple_w_gate):
    w = dict(norm_pre=norm_pre, norm_post=norm_post, ab_w_in=ab_w_in, fox_f_bias=fox_f_bias, rwkv_mu=rwkv_mu, rwkv_w0=rwkv_w0, rwkv_w2=rwkv_w2, rwkv_a0=rwkv_a0, rwkv_a2=rwkv_a2, rwkv_k_k=rwkv_k_k, rwkv_k_a=rwkv_k_a, rwkv_r_k=rwkv_r_k, rwkv_ln_g=rwkv_ln_g, rwkv_ln_b=rwkv_ln_b, ab_w_out=ab_w_out, c_w_in=c_w_in, c_ln_g=c_ln_g, c_ln_b=c_ln_b, c_w_s=c_w_s, c_b_s=c_b_s, c_w_out=c_w_out, ple_w_proj=ple_w_proj, ple_w_gate=ple_w_gate)
    mom = dict(norm_pre=m_norm_pre, norm_post=m_norm_post, ab_w_in=m_ab_w_in, fox_f_bias=m_fox_f_bias, rwkv_mu=m_rwkv_mu, rwkv_w0=m_rwkv_w0, rwkv_w2=m_rwkv_w2, rwkv_a0=m_rwkv_a0, rwkv_a2=m_rwkv_a2, rwkv_k_k=m_rwkv_k_k, rwkv_k_a=m_rwkv_k_a, rwkv_r_k=m_rwkv_r_k, rwkv_ln_g=m_rwkv_ln_g, rwkv_ln_b=m_rwkv_ln_b, ab_w_out=m_ab_w_out, c_w_in=m_c_w_in, c_ln_g=m_c_ln_g, c_ln_b=m_c_ln_b, c_w_s=m_c_w_s, c_b_s=m_c_b_s, c_w_out=m_c_w_out, ple_w_proj=m_ple_w_proj, ple_w_gate=m_ple_w_gate)
    vel = dict(norm_pre=v_norm_pre, norm_post=v_norm_post, ab_w_in=v_ab_w_in, fox_f_bias=v_fox_f_bias, rwkv_mu=v_rwkv_mu, rwkv_w0=v_rwkv_w0, rwkv_w2=v_rwkv_w2, rwkv_a0=v_rwkv_a0, rwkv_a2=v_rwkv_a2, rwkv_k_k=v_rwkv_k_k, rwkv_k_a=v_rwkv_k_a, rwkv_r_k=v_rwkv_r_k, rwkv_ln_g=v_rwkv_ln_g, rwkv_ln_b=v_rwkv_ln_b, ab_w_out=v_ab_w_out, c_w_in=v_c_w_in, c_ln_g=v_c_ln_g, c_ln_b=v_c_ln_b, c_w_s=v_c_w_s, c_b_s=v_c_b_s, c_w_out=v_c_w_out, ple_w_proj=v_ple_w_proj, ple_w_gate=v_ple_w_gate)
    return _step(x, p, loss_target, w, mom, vel)
```

```python
import functools

import jax
import jax.numpy as jnp
from jax import lax
from jax.experimental import pallas as pl
from jax.experimental.pallas import tpu as pltpu

F32 = jnp.float32
BF16 = jnp.bfloat16
HI = lax.Precision.HIGHEST

HEAD = 64
LANE = 128
LORA = 64
GMLP_CHUNK = 128
RMS_EPS = 1e-6
LN_EPS = 1e-5
GN_EPS = 64e-5
VMEM_LIMIT = 56 * 1024 * 1024
N_DEV = 8

ADAM_LR = 0.001
ADAM_B1 = 0.9
ADAM_B2 = 0.999
ADAM_EPS = 1e-08
ADAM_WD = 0.01
ADAM_STEP = 10

NT_DIMS = (((1,), (1,)), ((), ()))
TN_DIMS = (((0,), (0,)), ((), ()))


def _cparams(sem):
    return pltpu.CompilerParams(dimension_semantics=sem, vmem_limit_bytes=VMEM_LIMIT)


def _tile(n, target, q=LANE):
    if n <= target:
        return n
    best = None
    for d in range(q, target + 1, q):
        if n % d == 0:
            best = d
    assert best is not None, (n, target, q)
    return best


def _sigmoid(x):
    return 1.0 / (1.0 + jnp.exp(-x))


def _silu_and_grad(x):
    s = _sigmoid(x)
    return x * s, s * (1.0 + x * (1.0 - s))


def _gelu_and_grad(x):
    cdf = 0.5 * (1.0 + lax.erf(x * 0.7071067811865476))
    pdf = jnp.exp(-0.5 * x * x) * 0.3989422804014327
    return x * cdf, cdf + x * pdf


def _iota(shape, dim):
    return lax.broadcasted_iota(jnp.int32, shape, dim)


def _head_ones():
    return (_iota((LANE, LANE), 0) // HEAD == _iota((LANE, LANE), 1) // HEAD).astype(F32)


def _head_eye():
    return (_iota((HEAD, LANE), 0) == _iota((HEAD, LANE), 1) % HEAD).astype(F32)


def _hsum(x, ones):
    return jnp.dot(x, ones, precision=HI, preferred_element_type=F32)


def _colsum(x):
    return jnp.sum(x, axis=0, keepdims=True)


def matmul(a, b, *, name, ta=False, tb=False, add=None, out_dtype=F32, tm=512, tn=1024, tk=1024):
    m, k = (a.shape[1], a.shape[0]) if ta else a.shape
    n = b.shape[0] if tb else b.shape[1]
    assert (b.shape[1] if tb else b.shape[0]) == k
    tm, tn, tk = _tile(m, tm), _tile(n, tn), _tile(k, tk)
    nk = k // tk

    def body(*refs):
        if add is None:
            a_ref, b_ref, o_ref, acc_ref = refs
        else:
            a_ref, b_ref, add_ref, o_ref, acc_ref = refs
        kk = pl.program_id(2)

        @pl.when(kk == 0)
        def _():
            acc_ref[...] = jnp.zeros_like(acc_ref)

        dims = (((0 if ta else 1,), (1 if tb else 0,)), ((), ()))
        acc_ref[...] += lax.dot_general(a_ref[...].astype(BF16), b_ref[...].astype(BF16), dims,
                                        preferred_element_type=F32)

        @pl.when(kk == nk - 1)
        def _():
            r = acc_ref[...]
            if add is not None:
                r = r + add_ref[...].astype(F32)
            o_ref[...] = r.astype(out_dtype)

    a_spec = pl.BlockSpec((tk, tm), lambda i, j, kk: (kk, i)) if ta else pl.BlockSpec((tm, tk), lambda i, j, kk: (i, kk))
    b_spec = pl.BlockSpec((tn, tk), lambda i, j, kk: (j, kk)) if tb else pl.BlockSpec((tk, tn), lambda i, j, kk: (kk, j))
    o_spec = pl.BlockSpec((tm, tn), lambda i, j, kk: (i, j))
    in_specs = [a_spec, b_spec] + ([o_spec] if add is not None else [])
    args = (a, b) + ((add,) if add is not None else ())
    return pl.pallas_call(
        body, name=name, grid=(m // tm, n // tn, nk), in_specs=in_specs, out_specs=o_spec,
        out_shape=jax.ShapeDtypeStruct((m, n), out_dtype), scratch_shapes=[pltpu.VMEM((tm, tn), F32)],
        compiler_params=_cparams(("parallel", "parallel", "arbitrary")),
    )(*args)


def _row_spec(tm, width, cb=0):
    return pl.BlockSpec((tm, width), lambda i: (i, cb))


def _full_spec(shape):
    return pl.BlockSpec(shape, lambda i: (0,) * len(shape))


def rms_fwd(x, g, *, name):
    n, d = x.shape
    tm = _tile(n, 256, 8)

    def body(x_ref, g_ref, o_ref):
        xv = x_ref[...]
        r = lax.rsqrt(jnp.mean(xv * xv, axis=-1, keepdims=True) + RMS_EPS)
        o_ref[...] = (xv * r * g_ref[...]).astype(BF16)

    return pl.pallas_call(
        body, name=name, grid=(n // tm,), in_specs=[_row_spec(tm, d), _full_spec((1, d))],
        out_specs=_row_spec(tm, d), out_shape=jax.ShapeDtypeStruct((n, d), BF16),
        compiler_params=_cparams(("parallel",)),
    )(x, g)


def rms_res_fwd(y, res, g, *, name):
    n, d = y.shape
    tm = _tile(n, 256, 8)

    def body(y_ref, res_ref, g_ref, h_ref, hb_ref):
        yv = y_ref[...]
        r = lax.rsqrt(jnp.mean(yv * yv, axis=-1, keepdims=True) + RMS_EPS)
        h = res_ref[...] + yv * r * g_ref[...]
        h_ref[...] = h
        hb_ref[...] = h.astype(BF16)

    return pl.pallas_call(
        body, name=name, grid=(n // tm,), in_specs=[_row_spec(tm, d), _row_spec(tm, d), _full_spec((1, d))],
        out_specs=[_row_spec(tm, d), _row_spec(tm, d)],
        out_shape=[jax.ShapeDtypeStruct((n, d), F32), jax.ShapeDtypeStruct((n, d), BF16)],
        compiler_params=_cparams(("parallel",)),
    )(y, res, g)


def rms_bwd(x, dy, g, *, name, res=None, out_dtype=F32):
    n, d = x.shape
    tm = _tile(n, 256, 8)

    def body(*refs):
        if res is None:
            x_ref, dy_ref, g_ref, dx_ref, dg_ref = refs
        else:
            x_ref, dy_ref, g_ref, res_ref, dx_ref, dg_ref = refs

        @pl.when(pl.program_id(0) == 0)
        def _():
            dg_ref[...] = jnp.zeros_like(dg_ref)

        xv = x_ref[...]
        dyv = dy_ref[...].astype(F32)
        r = lax.rsqrt(jnp.mean(xv * xv, axis=-1, keepdims=True) + RMS_EPS)
        xhat = xv * r
        dxh = dyv * g_ref[...]
        dx = r * (dxh - xhat * jnp.mean(dxh * xhat, axis=-1, keepdims=True))
        if res is not None:
            dx = dx + res_ref[...]
        dx_ref[...] = dx.astype(out_dtype)
        dg_ref[...] += _colsum(dyv * xhat)

    in_specs = [_row_spec(tm, d), _row_spec(tm, d), _full_spec((1, d))] + ([_row_spec(tm, d)] if res is not None else [])
    args = (x, dy, g) + ((res,) if res is not None else ())
    return pl.pallas_call(
        body, name=name, grid=(n // tm,), in_specs=in_specs, out_specs=[_row_spec(tm, d), _full_spec((1, d))],
        out_shape=[jax.ShapeDtypeStruct((n, d), out_dtype), jax.ShapeDtypeStruct((1, d), F32)],
        compiler_params=_cparams(("arbitrary",)),
    )(*args)


def ple_fwd(h, a, z, *, name):
    n, d = h.shape
    tm = _tile(n, 256, 8)

    def body(h_ref, a_ref, z_ref, o_ref):
        o_ref[...] = h_ref[...] + a_ref[...] * _sigmoid(z_ref[...])

    return pl.pallas_call(
        body, name=name, grid=(n // tm,), in_specs=[_row_spec(tm, d)] * 3, out_specs=_row_spec(tm, d),
        out_shape=jax.ShapeDtypeStruct((n, d), F32), compiler_params=_cparams(("parallel",)),
    )(h, a, z)


def ple_bwd(dh, a, z, *, name):
    n, d = dh.shape
    tm = _tile(n, 256, 8)

    def body(dh_ref, a_ref, z_ref, da_ref, dz_ref):
        s = _sigmoid(z_ref[...])
        dhv = dh_ref[...]
        da_ref[...] = (dhv * s).astype(BF16)
        dz_ref[...] = (dhv * a_ref[...] * s * (1.0 - s)).astype(BF16)

    return pl.pallas_call(
        body, name=name, grid=(n // tm,), in_specs=[_row_spec(tm, d)] * 3, out_specs=[_row_spec(tm, d)] * 2,
        out_shape=[jax.ShapeDtypeStruct((n, d), BF16)] * 2, compiler_params=_cparams(("parallel",)),
    )(dh, a, z)


def loss_head(h, target, *, name):
    n, d = h.shape
    tm = _tile(n, 256, 8)

    def body(h_ref, t_ref, dh_ref, loss_ref):
        @pl.when(pl.program_id(0) == 0)
        def _():
            loss_ref[...] = jnp.zeros_like(loss_ref)

        e = h_ref[...] - t_ref[...]
        dh_ref[...] = e * (1.0 / d)
        loss_ref[...] += 0.5 * jnp.sum(jnp.sum(e * e, axis=-1, keepdims=True) * (1.0 / d), axis=0, keepdims=True)

    return pl.pallas_call(
        body, name=name, grid=(n // tm,), in_specs=[_row_spec(tm, d)] * 2,
        out_specs=[_row_spec(tm, d), _full_spec((1, 1))],
        out_shape=[jax.ShapeDtypeStruct((n, d), F32), jax.ShapeDtypeStruct((1, 1), F32)],
        compiler_params=_cparams(("arbitrary",)),
    )(h, target)


def _log_sigmoid(z):
    return jnp.minimum(z, 0.0) - jnp.log1p(jnp.exp(-jnp.abs(z)))


def fox_gate_fwd(proj, fbias, bsz, seq, nw):
    cc = _tile(seq, 256, 8)

    def body(f_ref, b_ref, c_ref):
        low = (_iota((cc, cc), 0) >= _iota((cc, cc), 1)).astype(F32)
        carry = jnp.zeros((1, LANE), F32)
        for ci in range(seq // cc):
            rows = slice(ci * cc, (ci + 1) * cc)
            lf = _log_sigmoid(f_ref[rows, :] + b_ref[...])
            c_ref[rows, :] = jnp.dot(low, lf, precision=HI, preferred_element_type=F32) + carry
            carry = carry + _colsum(lf)

    return pl.pallas_call(
        body, name="fox_gate_fwd", grid=(bsz,),
        in_specs=[pl.BlockSpec((seq, LANE), lambda b: (b, 8 * nw + 1)), _full_spec((1, LANE))],
        out_specs=pl.BlockSpec((seq, LANE), lambda b: (b, 0)),
        out_shape=jax.ShapeDtypeStruct((bsz * seq, LANE), F32), compiler_params=_cparams(("parallel",)),
    )(proj, fbias)


def fox_gate_bwd(proj, fbias, dc, bsz, seq, nw):
    cc = _tile(seq, 256, 8)
    nc = seq // cc

    def body(f_ref, b_ref, dc_ref, df_ref, db_ref):
        @pl.when(pl.program_id(0) == 0)
        def _():
            db_ref[...] = jnp.zeros_like(db_ref)

        upp = (_iota((cc, cc), 0) <= _iota((cc, cc), 1)).astype(F32)
        carry = jnp.zeros((1, LANE), F32)
        dbias = jnp.zeros((1, LANE), F32)
        for ci in reversed(range(nc)):
            rows = slice(ci * cc, (ci + 1) * cc)
            blk = dc_ref[rows, :]
            dlf = jnp.dot(upp, blk, precision=HI, preferred_element_type=F32) + carry
            carry = carry + _colsum(blk)
            df = dlf * _sigmoid(-(f_ref[rows, :] + b_ref[...]))
            df_ref[rows, :] = df.astype(BF16)
            dbias = dbias + _colsum(df)
        db_ref[...] += dbias

    return pl.pallas_call(
        body, name="fox_gate_bwd", grid=(bsz,),
        in_specs=[pl.BlockSpec((seq, LANE), lambda b: (b, 8 * nw + 1)), _full_spec((1, LANE)),
                  pl.BlockSpec((seq, LANE), lambda b: (b, 0))],
        out_specs=[pl.BlockSpec((seq, LANE), lambda b: (b, 0)), _full_spec((1, LANE))],
        out_shape=[jax.ShapeDtypeStruct((bsz * seq, LANE), BF16), jax.ShapeDtypeStruct((1, LANE), F32)],
        compiler_params=_cparams(("arbitrary",)),
    )(proj, fbias, dc)


def _attn_scores(qs, kh, ccol, crow, r0, kend, tq):
    s = lax.dot_general(qs[r0:kend], kh[:kend], NT_DIMS, preferred_element_type=F32)
    s = s + ccol[r0:kend] - crow[:, :kend]
    causal = _iota((tq, kend), 1) <= r0 + _iota((tq, kend), 0)
    return jnp.where(causal, s, -jnp.inf)


def _head_column(c_blk, h):
    return jnp.sum(jnp.where(_iota(c_blk.shape, 1) == h, c_blk, 0.0), axis=1, keepdims=True)


def fox_attn_fwd(proj, c, crow, bsz, seq, nw):
    tq = _tile(seq, 256, 8)
    scale = HEAD ** -0.5

    def body(q_ref, k_ref, v_ref, g_ref, c_ref, crow_ref, o_ref, y_ref):
        hp = pl.program_id(1)
        c_blk = c_ref[...]
        for hh in range(2):
            sl = slice(hh * HEAD, (hh + 1) * HEAD)
            ccol = _head_column(c_blk, 2 * hp + hh)
            crow_h = crow_ref[0, 0, hh:hh + 1, :]
            qs = (q_ref[:, sl] * scale).astype(BF16)
            kh = k_ref[:, sl].astype(BF16)
            vh = v_ref[:, sl].astype(BF16)
            for qi in range(seq // tq):
                r0, kend = qi * tq, (qi + 1) * tq
                s = _attn_scores(qs, kh, ccol, crow_h, r0, kend, tq)
                p = jnp.exp(s - jnp.max(s, axis=-1, keepdims=True))
                o = jnp.dot(p.astype(BF16), vh[:kend], preferred_element_type=F32) / jnp.sum(p, axis=-1, keepdims=True)
                o_ref[r0:kend, sl] = o
                y_ref[r0:kend, sl] = (o * _silu_and_grad(g_ref[r0:kend, sl])[0]).astype(BF16)

    blk = lambda cb: pl.BlockSpec((seq, LANE), lambda b, j: (b, cb * nw + j))
    return pl.pallas_call(
        body, name="fox_attn_fwd", grid=(bsz, nw),
        in_specs=[blk(0), blk(1), blk(2), blk(3), pl.BlockSpec((seq, LANE), lambda b, j: (b, 0)),
                  pl.BlockSpec((1, 1, 8, seq), lambda b, j: (b, j, 0, 0))],
        out_specs=[pl.BlockSpec((seq, LANE), lambda b, j: (b, j))] * 2,
        out_shape=[jax.ShapeDtypeStruct((bsz * seq, nw * LANE), F32), jax.ShapeDtypeStruct((bsz * seq, nw * LANE), BF16)],
        compiler_params=_cparams(("parallel", "parallel")),
    )(proj, proj, proj, proj, c, crow)


def fox_attn_bwd(proj, c, crow, oa, dycat, bsz, seq, nw):
    tq = _tile(seq, 256, 8)
    scale = HEAD ** -0.5

    def body(q_ref, k_ref, v_ref, g_ref, c_ref, crow_ref, o_ref, dy_ref,
             dq_ref, dk_ref, dv_ref, dg_ref, dc_ref, dk_acc, dv_acc, dc_acc):
        hp = pl.program_id(1)
        c_blk = c_ref[...]
        dc_ref[...] = jnp.zeros_like(dc_ref)
        for hh in range(2):
            sl = slice(hh * HEAD, (hh + 1) * HEAD)
            ccol = _head_column(c_blk, 2 * hp + hh)
            crow_h = crow_ref[0, 0, hh:hh + 1, :]
            qs = (q_ref[:, sl] * scale).astype(BF16)
            kh = k_ref[:, sl].astype(BF16)
            vh = v_ref[:, sl].astype(BF16)
            oh = o_ref[:, sl]
            dyh = dy_ref[:, sl].astype(F32)
            silu, dsilu = _silu_and_grad(g_ref[:, sl])
            dg_ref[:, sl] = (dyh * oh * dsilu).astype(BF16)
            do = dyh * silu
            dvec = jnp.sum(do * oh, axis=-1, keepdims=True)
            dob = do.astype(BF16)
            dk_acc[...] = jnp.zeros_like(dk_acc)
            dv_acc[...] = jnp.zeros_like(dv_acc)
            dc_acc[...] = jnp.zeros_like(dc_acc)
            for qi in range(seq // tq):
                r0, kend = qi * tq, (qi + 1) * tq
                s = _attn_scores(qs, kh, ccol, crow_h, r0, kend, tq)
                p = jnp.exp(s - jnp.max(s, axis=-1, keepdims=True))
                p = p / jnp.sum(p, axis=-1, keepdims=True)
                dp = lax.dot_general(dob[r0:kend], vh[:kend], NT_DIMS, preferred_element_type=F32)
                ds = p * (dp - dvec[r0:kend])
                dsb = ds.astype(BF16)
                dq_ref[r0:kend, sl] = (jnp.dot(dsb, kh[:kend], preferred_element_type=F32) * scale).astype(BF16)
                dk_acc[0:kend, :] += lax.dot_general(dsb, qs[r0:kend], TN_DIMS, preferred_element_type=F32)
                dv_acc[0:kend, :] += lax.dot_general(p.astype(BF16), dob[r0:kend], TN_DIMS, preferred_element_type=F32)
                dc_acc[:, 0:kend] += -_colsum(ds)
                rowsum = lax.dot_general(jnp.ones((8, kend), F32), ds, NT_DIMS, precision=HI, preferred_element_type=F32)
                dc_acc[:, r0:kend] += rowsum[0:1, :]
            dk_ref[:, sl] = dk_acc[...].astype(BF16)
            dv_ref[:, sl] = dv_acc[...].astype(BF16)
            dc_ref[0, 0, hh:hh + 1, :] = dc_acc[...]

    blk = lambda cb: pl.BlockSpec((seq, LANE), lambda b, j: (b, cb * nw + j))
    own = pl.BlockSpec((seq, LANE), lambda b, j: (b, j))
    n = bsz * seq
    return pl.pallas_call(
        body, name="fox_attn_bwd", grid=(bsz, nw),
        in_specs=[blk(0), blk(1), blk(2), blk(3), pl.BlockSpec((seq, LANE), lambda b, j: (b, 0)),
                  pl.BlockSpec((1, 1, 8, seq), lambda b, j: (b, j, 0, 0)), own, own],
        out_specs=[own] * 4 + [pl.BlockSpec((1, 1, 8, seq), lambda b, j: (b, j, 0, 0))],
        out_shape=[jax.ShapeDtypeStruct((n, nw * LANE), BF16)] * 4 + [jax.ShapeDtypeStruct((bsz, nw, 8, seq), F32)],
        scratch_shapes=[pltpu.VMEM((seq, HEAD), F32), pltpu.VMEM((seq, HEAD), F32), pltpu.VMEM((1, seq), F32)],
        compiler_params=_cparams(("parallel", "parallel")),
    )(proj, proj, proj, proj, c, crow, oa, dycat)


def _shift_mix(x, mu):
    prev = jnp.where(_iota(x.shape, 0) == 0, 0.0, pltpu.roll(x, 1, 0))
    return x + (prev - x) * mu, prev


def _rwkv_pre_math(r_in, k_in, v_in, wa_in, mu_r, mu_k, mu_v, mu_wa, w0, w2, a0, a2, k_k, k_a, ones):
    r, prev_r = _shift_mix(r_in, mu_r)
    k, prev_k = _shift_mix(k_in, mu_k)
    v, prev_v = _shift_mix(v_in, mu_v)
    wa, prev_wa = _shift_mix(wa_in, mu_wa)
    tw = jnp.tanh(wa[:, :LORA])
    alo = wa[:, LORA:]
    sg = _sigmoid(w0 + jnp.dot(tw, w2, precision=HI, preferred_element_type=F32))
    e = sg * 0.6065306597126334
    w = jnp.exp(-e)
    a = _sigmoid(a0 + jnp.dot(alo, a2, precision=HI, preferred_element_type=F32))
    kkraw = k * k_k
    nrm = jnp.sqrt(_hsum(kkraw * kkraw, ones))
    den = jnp.maximum(nrm, 1e-12)
    kk = kkraw / den
    k2 = k * (1.0 + (a - 1.0) * k_a)
    return dict(r=r, k=k, v=v, prev_r=prev_r, prev_k=prev_k, prev_v=prev_v, prev_wa=prev_wa, tw=tw, alo=alo,
                sg=sg, e=e, w=w, a=a, nrm=nrm, den=den, kk=kk, k2=k2)


def _rwkv_pre_specs(seq, nw):
    tok = lambda cb: pl.BlockSpec((seq, LANE), lambda b, j: (b, cb * nw + j))
    par = pl.BlockSpec((1, LANE), lambda b, j: (0, j))
    lora = pl.BlockSpec((LORA, LANE), lambda b, j: (0, j))
    return [tok(4), tok(5), tok(6), pl.BlockSpec((seq, LANE), lambda b, j: (b, 8 * nw)),
            par, par, par, pl.BlockSpec((1, LANE), lambda b, j: (0, 0)), par, lora, par, lora, par, par]


def rwkv_pre_fwd(proj, prm, bsz, seq, nw):
    def body(r_ref, k_ref, v_ref, wa_ref, mur, muk, muv, muwa, w0, w2, a0, a2, kk_, ka_,
             ro, wo, ko, vo, kko, kao):
        m = _rwkv_pre_math(r_ref[...], k_ref[...], v_ref[...], wa_ref[...], mur[...], muk[...], muv[...], muwa[...],
                           w0[...], w2[...], a0[...], a2[...], kk_[...], ka_[...], _head_ones())
        ro[...] = m["r"]
        wo[...] = m["w"]
        ko[...] = m["k2"]
        vo[...] = m["v"]
        kko[...] = m["kk"]
        kao[...] = m["kk"] * m["a"]

    n = bsz * seq
    own = pl.BlockSpec((seq, LANE), lambda b, j: (b, j))
    return pl.pallas_call(
        body, name="rwkv_pre_fwd", grid=(bsz, nw), in_specs=_rwkv_pre_specs(seq, nw), out_specs=[own] * 6,
        out_shape=[jax.ShapeDtypeStruct((n, nw * LANE), F32)] * 6, compiler_params=_cparams(("parallel", "parallel")),
    )(proj, proj, proj, proj, prm["mu_r"], prm["mu_k"], prm["mu_v"], prm["mu_wa"], prm["w0"], prm["w2"],
      prm["a0"], prm["a2"], prm["k_k"], prm["k_a"])


def rwkv_pre_bwd(proj, prm, grads_scan, grads_bonus, bsz, seq, nw):
    wid = nw * LANE

    def body(r_ref, k_ref, v_ref, wa_ref, mur, muk, muv, muwa, w0, w2, a0, a2, kk_, ka_,
             dr_s, dw_s, dk_s, dv_s, dkk_s, dka_s, dr_b, dk_b, dv_b,
             dxr, dxk, dxv, dxwa, dmur, dmuk, dmuv, dmuwa, dw0, dw2, da0, da2, dkk_p, dka_p, dwa_acc):
        b, j = pl.program_id(0), pl.program_id(1)
        cols = pl.ds(pl.multiple_of(j * LANE, LANE), LANE)
        acc_refs = (dmur, dmuk, dmuv, dmuwa, dw0, dw2, da0, da2, dkk_p, dka_p)

        @pl.when(jnp.logical_and(b == 0, j == 0))
        def _():
            for ref in acc_refs:
                ref[...] = jnp.zeros_like(ref)

        @pl.when(j == 0)
        def _():
            dwa_acc[...] = jnp.zeros_like(dwa_acc)

        ones = _head_ones()
        r_in, k_in, v_in, wa_in = r_ref[...], k_ref[...], v_ref[...], wa_ref[...]
        m = _rwkv_pre_math(r_in, k_in, v_in, wa_in, mur[...], muk[...], muv[...], muwa[...],
                           w0[...], w2[...], a0[...], a2[...], kk_[...], ka_[...], ones)
        k, a, kk = m["k"], m["a"], m["kk"]
        dr = dr_s[...] + dr_b[...]
        dk2 = dk_s[...] + dk_b[...]
        dv = dv_s[...] + dv_b[...]
        dka = dka_s[...]
        da = dka * kk + dk2 * k * ka_[...]
        dkk = dkk_s[...] + dka * a
        dkkraw = jnp.where(m["nrm"] > 1e-12, dkk - kk * _hsum(dkk * kk, ones), dkk) / m["den"]
        dk = dkkraw * kk_[...] + dk2 * (1.0 + (a - 1.0) * ka_[...])
        dkk_p[:, cols] += _colsum(dkkraw * k)
        dka_p[:, cols] += _colsum(dk2 * k * (a - 1.0))
        dza = da * a * (1.0 - a)
        da0[:, cols] += _colsum(dza)
        da2[:, cols] += lax.dot_general(m["alo"], dza, TN_DIMS, precision=HI, preferred_element_type=F32)
        dalo = lax.dot_general(dza, a2[...], NT_DIMS, precision=HI, preferred_element_type=F32)
        dzw = -dw_s[...] * m["w"] * m["e"] * (1.0 - m["sg"])
        dw0[:, cols] += _colsum(dzw)
        dw2[:, cols] += lax.dot_general(m["tw"], dzw, TN_DIMS, precision=HI, preferred_element_type=F32)
        dtw = lax.dot_general(dzw, w2[...], NT_DIMS, precision=HI, preferred_element_type=F32)
        dwa_acc[:, 0:LORA] += dtw * (1.0 - m["tw"] * m["tw"])
        dwa_acc[:, LORA:LANE] += dalo

        def shift_bwd(dxs, x, prev, mu):
            g = dxs * mu
            nxt = jnp.where(_iota(g.shape, 0) == seq - 1, 0.0, pltpu.roll(g, seq - 1, 0))
            return dxs * (1.0 - mu) + nxt, _colsum(dxs * (prev - x))

        for dxs, x, prev, mu, out, dmu in ((dr, r_in, m["prev_r"], mur, dxr, dmur), (dk, k_in, m["prev_k"], muk, dxk, dmuk),
                                           (dv, v_in, m["prev_v"], muv, dxv, dmuv)):
            dx, dm = shift_bwd(dxs, x, prev, mu[...])
            out[...] = dx.astype(BF16)
            dmu[:, cols] += dm

        @pl.when(j == nw - 1)
        def _():
            dx, dm = shift_bwd(dwa_acc[...], wa_in, m["prev_wa"], muwa[...])
            dxwa[...] = dx.astype(BF16)
            dmuwa[...] += dm

    n = bsz * seq
    own = pl.BlockSpec((seq, LANE), lambda b, j: (b, j))
    whole = lambda shape: pl.BlockSpec(shape, lambda b, j: (0,) * len(shape))
    row = jax.ShapeDtypeStruct((1, wid), F32)
    return pl.pallas_call(
        body, name="rwkv_pre_bwd", grid=(bsz, nw), in_specs=_rwkv_pre_specs(seq, nw) + [own] * 9,
        out_specs=[own] * 3 + [pl.BlockSpec((seq, LANE), lambda b, j: (b, 0))]
        + [whole((1, wid))] * 3 + [whole((1, LANE)), whole((1, wid)), whole((LORA, wid)), whole((1, wid)),
                                   whole((LORA, wid)), whole((1, wid)), whole((1, wid))],
        out_shape=[jax.ShapeDtypeStruct((n, wid), BF16)] * 3 + [jax.ShapeDtypeStruct((n, LANE), BF16)]
        + [row] * 3 + [jax.ShapeDtypeStruct((1, LANE), F32), row, jax.ShapeDtypeStruct((LORA, wid), F32), row,
                       jax.ShapeDtypeStruct((LORA, wid), F32), row, row],
        scratch_shapes=[pltpu.VMEM((seq, LANE), F32)],
        compiler_params=_cparams(("arbitrary", "arbitrary")),
    )(proj, proj, proj, proj, prm["mu_r"], prm["mu_k"], prm["mu_v"], prm["mu_wa"], prm["w0"], prm["w2"],
      prm["a0"], prm["a2"], prm["k_k"], prm["k_a"], *grads_scan, *grads_bonus)


def _scan_group(nw):
    return _tile(nw, 4, 1)


ROWS = 8


def _scan_consts():
    ones = _head_ones().astype(BF16)
    return _head_eye(), ones, jnp.concatenate([ones, ones], axis=0)


def _cols8(tile, eye, ones, ones2, two_parts):
    hi = tile.astype(BF16).astype(F32)
    place = lambda part, q: (part[q:q + 1, :] * eye).astype(BF16)
    if two_parts:
        lo = tile - hi
        lhs = jnp.concatenate([jnp.concatenate([place(hi, q), place(lo, q)], axis=1) for q in range(ROWS)], axis=0)
        big = jnp.dot(lhs, ones2, preferred_element_type=F32)
    else:
        big = jnp.dot(jnp.concatenate([place(hi, q) for q in range(ROWS)], axis=0), ones, preferred_element_type=F32)
    return [big[q * HEAD:(q + 1) * HEAD, :] for q in range(ROWS)]


def _rows8(mats, eye, ones):
    big = jnp.dot(jnp.concatenate([m.astype(BF16) for m in mats], axis=0), ones, preferred_element_type=F32)
    return _stack_rows([_colsum(big[q * HEAD:(q + 1) * HEAD, :] * eye) for q in range(ROWS)])


def _stack_rows(rows):
    rid = _iota((ROWS, LANE), 0)
    out = jnp.zeros((ROWS, LANE), F32)
    for q in range(ROWS):
        out = jnp.where(rid == q, rows[q], out)
    return out


def rwkv_scan_fwd(r, w, k, v, kk, ka, bsz, seq, nw):
    grp = _scan_group(nw)
    tc = _tile(seq, 32, 8)
    nt = seq // tc

    def body(r_ref, w_ref, k_ref, v_ref, kk_ref, ka_ref, y_ref, st_ref, s_ref):
        @pl.when(pl.program_id(2) == 0)
        def _():
            s_ref[...] = jnp.zeros_like(s_ref)

        eye, ones, ones2 = _scan_consts()

        def step(i8, carry):
            base = pl.multiple_of(i8 * ROWS, ROWS)
            for g in range(grp):
                cs = slice(g * LANE, (g + 1) * LANE)
                cols = lambda ref, two: _cols8(ref[pl.ds(base, ROWS), cs], eye, ones, ones2, two)
                rm, wm, km, kkm, kam = cols(r_ref, False), cols(w_ref, True), cols(k_ref, False), cols(kk_ref, True), cols(ka_ref, True)
                vt = v_ref[pl.ds(base, ROWS), cs]
                s = s_ref[g]
                ys = []
                for q in range(ROWS):
                    st_ref[base + q, g] = s
                    sa = _colsum(s * kkm[q])
                    s = s * wm[q] - kam[q] * sa + km[q] * vt[q:q + 1, :]
                    ys.append(_colsum(s * rm[q]))
                y_ref[pl.ds(base, ROWS), cs] = _stack_rows(ys)
                s_ref[g] = s
            return carry

        lax.fori_loop(0, tc // ROWS, step, 0)

    tok = pl.BlockSpec((tc, grp * LANE), lambda b, g, t: (b * nt + t, g))
    n = bsz * seq
    return pl.pallas_call(
        body, name="rwkv_scan_fwd", grid=(bsz, nw // grp, nt), in_specs=[tok] * 6,
        out_specs=[tok, pl.BlockSpec((tc, grp, HEAD, LANE), lambda b, g, t: (b * nt + t, g, 0, 0))],
        out_shape=[jax.ShapeDtypeStruct((n, nw * LANE), F32), jax.ShapeDtypeStruct((n, nw, HEAD, LANE), F32)],
        scratch_shapes=[pltpu.VMEM((grp, HEAD, LANE), F32)],
        compiler_params=_cparams(("parallel", "parallel", "arbitrary")),
    )(r, w, k, v, kk, ka)


def rwkv_scan_bwd(r, w, k, v, kk, ka, states, dy, bsz, seq, nw):
    grp = _scan_group(nw)
    tc = _tile(seq, 32, 8)
    nt = seq // tc
    nblk = tc // ROWS

    def body(r_ref, w_ref, k_ref, v_ref, kk_ref, ka_ref, st_ref, dy_ref,
             dr_ref, dw_ref, dk_ref, dv_ref, dkk_ref, dka_ref, ds_ref):
        @pl.when(pl.program_id(2) == 0)
        def _():
            ds_ref[...] = jnp.zeros_like(ds_ref)

        eye, ones, ones2 = _scan_consts()

        def step(ii, carry):
            base = pl.multiple_of((nblk - 1 - ii) * ROWS, ROWS)
            rows = pl.ds(base, ROWS)
            for g in range(grp):
                cs = slice(g * LANE, (g + 1) * LANE)
                cols = lambda ref, two: _cols8(ref[rows, cs], eye, ones, ones2, two)
                rm, wm, km, kkm, kam = cols(r_ref, False), cols(w_ref, True), cols(k_ref, False), cols(kk_ref, True), cols(ka_ref, True)
                vt, dyt = v_ref[rows, cs], dy_ref[rows, cs]
                ds = ds_ref[g]
                dvs, p_dr, p_dk, p_dka, p_dw, p_dkk = ([None] * ROWS for _ in range(6))
                for q in reversed(range(ROWS)):
                    sp = st_ref[base + q, g]
                    vrow, dyrow = vt[q:q + 1, :], dyt[q:q + 1, :]
                    sa = _colsum(sp * kkm[q])
                    st = sp * wm[q] - kam[q] * sa + km[q] * vrow
                    ds = ds + rm[q] * dyrow
                    dvs[q] = _colsum(ds * km[q])
                    dsa = -_colsum(ds * kam[q])
                    p_dr[q] = st * dyrow
                    p_dk[q] = ds * vrow
                    p_dka[q] = -(ds * sa)
                    p_dw[q] = ds * sp
                    p_dkk[q] = sp * dsa
                    ds = ds * wm[q] + kkm[q] * dsa
                ds_ref[g] = ds
                dv_ref[rows, cs] = _stack_rows(dvs)
                dr_ref[rows, cs] = _rows8(p_dr, eye, ones)
                dk_ref[rows, cs] = _rows8(p_dk, eye, ones)
                dka_ref[rows, cs] = _rows8(p_dka, eye, ones)
                dw_ref[rows, cs] = _rows8(p_dw, eye, ones)
                dkk_ref[rows, cs] = _rows8(p_dkk, eye, ones)
            return carry

        lax.fori_loop(0, nblk, step, 0)

    tok = pl.BlockSpec((tc, grp * LANE), lambda b, g, t: (b * nt + nt - 1 - t, g))
    n = bsz * seq
    return pl.pallas_call(
        body, name="rwkv_scan_bwd", grid=(bsz, nw // grp, nt),
        in_specs=[tok] * 6 + [pl.BlockSpec((tc, grp, HEAD, LANE), lambda b, g, t: (b * nt + nt - 1 - t, g, 0, 0)), tok],
        out_specs=[tok] * 6, out_shape=[jax.ShapeDtypeStruct((n, nw * LANE), F32)] * 6,
        scratch_shapes=[pltpu.VMEM((grp, HEAD, LANE), F32)],
        compiler_params=_cparams(("parallel", "parallel", "arbitrary")),
    )(r, w, k, v, kk, ka, states, dy)


def _rwkv_post_math(y, r, k2, v, ln_g, ln_b, r_k, ones):
    d = y - _hsum(y, ones) * (1.0 / HEAD)
    rstd = lax.rsqrt(_hsum(d * d, ones) * (1.0 / HEAD) + GN_EPS)
    yn = d * rstd
    s = _hsum(r * k2 * r_k, ones)
    return yn, rstd, s, yn * ln_g + ln_b + s * v


def _rwkv_post_specs(tm, nw):
    tok = pl.BlockSpec((tm, LANE), lambda i, j: (i, j))
    par = pl.BlockSpec((1, LANE), lambda i, j: (0, j))
    return [tok] * 4 + [pl.BlockSpec((tm, LANE), lambda i, j: (i, 7 * nw + j)), par, par, par]


def rwkv_post_fwd(y, r, k2, v, proj, prm, nw):
    n = y.shape[0]
    tm = _tile(n, 512, 8)

    def body(y_ref, r_ref, k_ref, v_ref, g_ref, lg, lb, rk, o_ref):
        ob = _rwkv_post_math(y_ref[...], r_ref[...], k_ref[...], v_ref[...], lg[...], lb[...], rk[...], _head_ones())[3]
        o_ref[...] = (ob * _silu_and_grad(g_ref[...])[0]).astype(BF16)

    return pl.pallas_call(
        body, name="rwkv_post_fwd", grid=(n // tm, nw), in_specs=_rwkv_post_specs(tm, nw),
        out_specs=pl.BlockSpec((tm, LANE), lambda i, j: (i, j)), out_shape=jax.ShapeDtypeStruct((n, nw * LANE), BF16),
        compiler_params=_cparams(("parallel", "parallel")),
    )(y, r, k2, v, proj, prm["ln_g"], prm["ln_b"], prm["r_k"])


def rwkv_post_bwd(y, r, k2, v, proj, prm, dycat, nw):
    n = y.shape[0]
    tm = _tile(n, 512, 8)
    wid = nw * LANE

    def body(y_ref, r_ref, k_ref, v_ref, g_ref, lg, lb, rk, dyc_ref,
             dy_ref, dr_ref, dk_ref, dv_ref, dg_ref, dlg, dlb, drk):
        i, j = pl.program_id(0), pl.program_id(1)
        cols = pl.ds(pl.multiple_of(j * LANE, LANE), LANE)

        @pl.when(jnp.logical_and(i == 0, j == 0))
        def _():
            for ref in (dlg, dlb, drk):
                ref[...] = jnp.zeros_like(ref)

        ones = _head_ones()
        rr, kr, vr = r_ref[...], k_ref[...], v_ref[...]
        yn, rstd, s, ob = _rwkv_post_math(y_ref[...], rr, kr, vr, lg[...], lb[...], rk[...], ones)
        silu, dsilu = _silu_and_grad(g_ref[...])
        dyc = dyc_ref[...].astype(F32)
        dg_ref[...] = (dyc * ob * dsilu).astype(BF16)
        dob = dyc * silu
        dlg[:, cols] += _colsum(dob * yn)
        dlb[:, cols] += _colsum(dob)
        dyn = dob * lg[...]
        dy_ref[...] = rstd * (dyn - _hsum(dyn, ones) * (1.0 / HEAD) - yn * _hsum(dyn * yn, ones) * (1.0 / HEAD))
        dv_ref[...] = dob * s
        dsum = _hsum(dob * vr, ones)
        dr_ref[...] = dsum * kr * rk[...]
        dk_ref[...] = dsum * rr * rk[...]
        drk[:, cols] += _colsum(dsum * rr * kr)

    tok = pl.BlockSpec((tm, LANE), lambda i, j: (i, j))
    whole = pl.BlockSpec((1, wid), lambda i, j: (0, 0))
    return pl.pallas_call(
        body, name="rwkv_post_bwd", grid=(n // tm, nw),
        in_specs=_rwkv_post_specs(tm, nw) + [pl.BlockSpec((tm, LANE), lambda i, j: (i, nw + j))],
        out_specs=[tok] * 5 + [whole] * 3,
        out_shape=[jax.ShapeDtypeStruct((n, wid), F32)] * 4 + [jax.ShapeDtypeStruct((n, wid), BF16)]
        + [jax.ShapeDtypeStruct((1, wid), F32)] * 3,
        compiler_params=_cparams(("arbitrary", "arbitrary")),
    )(y, r, k2, v, proj, prm["ln_g"], prm["ln_b"], prm["r_k"], dycat)


def _sgu_math(blk, ln_g, ln_b, ws_ref, bb_ref, mixed_ref, d, ngr):
    u, v, g = blk[:, 0:d], blk[:, d:2 * d], blk[:, 2 * d:3 * d]
    gu, dgu = _gelu_and_grad(u)
    gv, dgv = _gelu_and_grad(v)
    cen = gv - jnp.mean(gv, axis=-1, keepdims=True)
    rstd = lax.rsqrt(jnp.mean(cen * cen, axis=-1, keepdims=True) + LN_EPS)
    vhat = cen * rstd
    vn = vhat * ln_g + ln_b
    tril = _iota((GMLP_CHUNK, GMLP_CHUNK), 0) >= _iota((GMLP_CHUNK, GMLP_CHUNK), 1)
    for gi in range(ngr):
        cs = slice(gi * LANE, (gi + 1) * LANE)
        wm = jnp.where(tril, ws_ref[gi], 0.0)
        mixed_ref[:, cs] = jnp.dot(wm, vn[:, cs], precision=HI, preferred_element_type=F32) + bb_ref[gi]
    return dict(g=g, gu=gu, dgu=dgu, dgv=dgv, rstd=rstd, vhat=vhat, vn=vn, tril=tril)


def sgu_fwd(proj, ln_g, ln_b, w_s, b_bc):
    n, d3 = proj.shape
    d = d3 // 3
    ngr = d // LANE

    def body(p_ref, lg, lb, ws_ref, bb_ref, o_ref, mixed_ref):
        m = _sgu_math(p_ref[...], lg[...], lb[...], ws_ref, bb_ref, mixed_ref, d, ngr)
        o_ref[...] = (m["gu"] * mixed_ref[...] * _silu_and_grad(m["g"])[0]).astype(BF16)

    return pl.pallas_call(
        body, name="sgu_fwd", grid=(n // GMLP_CHUNK,),
        in_specs=[_row_spec(GMLP_CHUNK, d3), _full_spec((1, d)), _full_spec((1, d)),
                  _full_spec((ngr, GMLP_CHUNK, GMLP_CHUNK)), _full_spec((ngr, GMLP_CHUNK, LANE))],
        out_specs=_row_spec(GMLP_CHUNK, d), out_shape=jax.ShapeDtypeStruct((n, d), BF16),
        scratch_shapes=[pltpu.VMEM((GMLP_CHUNK, d), F32)], compiler_params=_cparams(("parallel",)),
    )(proj, ln_g, ln_b, w_s, b_bc)


def sgu_bwd(proj, ln_g, ln_b, w_s, b_bc, dyin):
    n, d3 = proj.shape
    d = d3 // 3
    ngr = d // LANE
    nsteps = n // GMLP_CHUNK

    def body(p_ref, lg, lb, ws_ref, bb_ref, dy_ref, dp_ref, dws_ref, dbs_ref, dlg_ref, dlb_ref,
             mixed_ref, dvn_ref, dbacc_ref):
        step = pl.program_id(0)

        @pl.when(step == 0)
        def _():
            for ref in (dws_ref, dlg_ref, dlb_ref, dbacc_ref):
                ref[...] = jnp.zeros_like(ref)

        m = _sgu_math(p_ref[...], lg[...], lb[...], ws_ref, bb_ref, mixed_ref, d, ngr)
        silu, dsilu = _silu_and_grad(m["g"])
        dyv = dy_ref[...].astype(F32)
        mixed = mixed_ref[...]
        dp_ref[:, 2 * d:3 * d] = (dyv * m["gu"] * mixed * dsilu).astype(BF16)
        doc = dyv * silu
        dp_ref[:, 0:d] = (doc * mixed * m["dgu"]).astype(BF16)
        dmixed = doc * m["gu"]
        for gi in range(ngr):
            cs = slice(gi * LANE, (gi + 1) * LANE)
            dm = dmixed[:, cs]
            wm = jnp.where(m["tril"], ws_ref[gi], 0.0)
            dws_ref[gi] += jnp.where(m["tril"], lax.dot_general(dm, m["vn"][:, cs], NT_DIMS, precision=HI,
                                                                preferred_element_type=F32), 0.0)
            dbacc_ref[gi] += dm
            dvn_ref[:, cs] = lax.dot_general(wm, dm, TN_DIMS, precision=HI, preferred_element_type=F32)
        dvn = dvn_ref[...]
        dlg_ref[...] += _colsum(dvn * m["vhat"])
        dlb_ref[...] += _colsum(dvn)
        dvh = dvn * lg[...]
        dgv = m["rstd"] * (dvh - jnp.mean(dvh, axis=-1, keepdims=True)
                           - m["vhat"] * jnp.mean(dvh * m["vhat"], axis=-1, keepdims=True))
        dp_ref[:, d:2 * d] = (dgv * m["dgv"]).astype(BF16)

        @pl.when(step == nsteps - 1)
        def _():
            dbs_ref[...] = jnp.sum(dbacc_ref[...], axis=-1, keepdims=True)

    return pl.pallas_call(
        body, name="sgu_bwd", grid=(nsteps,),
        in_specs=[_row_spec(GMLP_CHUNK, d3), _full_spec((1, d)), _full_spec((1, d)),
                  _full_spec((ngr, GMLP_CHUNK, GMLP_CHUNK)), _full_spec((ngr, GMLP_CHUNK, LANE)), _row_spec(GMLP_CHUNK, d)],
        out_specs=[_row_spec(GMLP_CHUNK, d3), _full_spec((ngr, GMLP_CHUNK, GMLP_CHUNK)), _full_spec((ngr, GMLP_CHUNK, 1)),
                   _full_spec((1, d)), _full_spec((1, d))],
        out_shape=[jax.ShapeDtypeStruct((n, d3), BF16), jax.ShapeDtypeStruct((ngr, GMLP_CHUNK, GMLP_CHUNK), F32),
                   jax.ShapeDtypeStruct((ngr, GMLP_CHUNK, 1), F32), jax.ShapeDtypeStruct((1, d), F32),
                   jax.ShapeDtypeStruct((1, d), F32)],
        scratch_shapes=[pltpu.VMEM((GMLP_CHUNK, d), F32), pltpu.VMEM((GMLP_CHUNK, d), F32),
                        pltpu.VMEM((ngr, GMLP_CHUNK, LANE), F32)],
        compiler_params=_cparams(("arbitrary",)),
    )(proj, ln_g, ln_b, w_s, b_bc, dyin)


def _ab_segments(wid, heads):
    return ((0, 3 * wid, 0),
            (3 * wid, 3 * wid + heads, 8 * wid + LANE),
            (3 * wid + heads, 4 * wid + heads, 3 * wid),
            (4 * wid + heads, 7 * wid + heads, 4 * wid),
            (7 * wid + heads, 7 * wid + heads + 2 * LORA, 8 * wid),
            (7 * wid + heads + 2 * LORA, 8 * wid + heads + 2 * LORA, 7 * wid))


def _ab_reorder(shards, wid, heads):
    per = shards[0].shape[1]
    pieces = []
    for o0, o1, _ in sorted(_ab_segments(wid, heads), key=lambda s: s[2]):
        for d, sh in enumerate(shards):
            lo, hi = max(o0, d * per), min(o1, (d + 1) * per)
            if lo < hi:
                pieces.append(sh[:, lo - d * per:hi - d * per])
    pieces.append(jnp.zeros((shards[0].shape[0], LANE - heads), shards[0].dtype))
    return jnp.concatenate(pieces, axis=1)


def _ab_restore(g, wid, heads, n_shards):
    per = (8 * wid + heads + 2 * LORA) // n_shards
    out = []
    for d in range(n_shards):
        pieces = []
        for o0, o1, r0 in _ab_segments(wid, heads):
            lo, hi = max(o0, d * per), min(o1, (d + 1) * per)
            if lo < hi:
                pieces.append(g[:, r0 + lo - o0:r0 + hi - o0])
        out.append(jnp.concatenate(pieces, axis=1))
    return out


def local_step(x, p, target, wts):
    bsz, seq, d = x.shape
    n = bsz * seq
    wid = d // 2
    nw = wid // LANE
    heads = wid // HEAD
    x2 = x.reshape(n, d)
    tgt = target.reshape(n, d)
    p0, p1 = p[0].reshape(n, -1), p[1].reshape(n, -1)
    row = lambda a: a.reshape(1, -1)
    g_pre0, g_pre1 = row(wts["norm_pre"][0]), row(wts["norm_pre"][1])
    g_post0, g_post1 = row(wts["norm_post"][0]), row(wts["norm_post"][1])
    w_ab = wts["ab_w_in_r"]
    w_out, w_c, w_cout = wts["ab_w_out"], wts["c_w_in"], wts["c_w_out"]
    wp0, wp1 = wts["ple_w_proj"][0], wts["ple_w_proj"][1]
    wg0, wg1 = wts["ple_w_gate"][0], wts["ple_w_gate"][1]
    fbias = jnp.pad(row(wts["fox_f_bias"]), ((0, 0), (0, LANE - heads)))
    mu = row(wts["rwkv_mu"])
    prm = dict(mu_r=mu[:, 0:wid], mu_k=mu[:, wid:2 * wid], mu_v=mu[:, 2 * wid:3 * wid], mu_wa=mu[:, 3 * wid:],
               w0=row(wts["rwkv_w0"]), w2=wts["rwkv_w2"], a0=row(wts["rwkv_a0"]), a2=wts["rwkv_a2"],
               k_k=row(wts["rwkv_k_k"]), k_a=row(wts["rwkv_k_a"]), r_k=row(wts["rwkv_r_k"]),
               ln_g=row(wts["rwkv_ln_g"]), ln_b=row(wts["rwkv_ln_b"]))
    c_ln_g, c_ln_b = row(wts["c_ln_g"]), row(wts["c_ln_b"])
    w_s = wts["c_w_s"]
    b_bc = jnp.broadcast_to(wts["c_b_s"][:, :, None], w_s.shape[:2] + (LANE,))

    xn0 = rms_fwd(x2, g_pre0, name="rms_pre0")
    proj0 = matmul(xn0, w_ab, name="ab_in")
    c = fox_gate_fwd(proj0, fbias, bsz, seq, nw)
    crow = jnp.pad(c.reshape(bsz, seq, LANE)[:, :, :heads].transpose(0, 2, 1).reshape(bsz, nw, 2, seq),
                   ((0, 0), (0, 0), (0, 6), (0, 0)))
    oa, ya = fox_attn_fwd(proj0, c, crow, bsz, seq, nw)
    sr, sw, sk, sv, skk, ska = rwkv_pre_fwd(proj0, prm, bsz, seq, nw)
    ysc, states = rwkv_scan_fwd(sr, sw, sk, sv, skk, ska, bsz, seq, nw)
    yb = rwkv_post_fwd(ysc, sr, sk, sv, proj0, prm, nw)
    ycat = jnp.concatenate([ya, yb], axis=1)
    y0 = matmul(ycat, w_out, name="ab_out")
    h1, h1b = rms_res_fwd(y0, x2, g_post0, name="rms_post0")
    a0 = matmul(p0, wp0, name="ple_proj0")
    z0 = matmul(h1b, wg0, name="ple_gate0")
    h1p = ple_fwd(h1, a0, z0, name="ple_fwd0")
    xn1 = rms_fwd(h1p, g_pre1, name="rms_pre1")
    proj1 = matmul(xn1, w_c, name="c_in")
    yin = sgu_fwd(proj1, c_ln_g, c_ln_b, w_s, b_bc)
    y1 = matmul(yin, w_cout, name="c_out")
    h2, h2b = rms_res_fwd(y1, h1p, g_post1, name="rms_post1")
    a1 = matmul(p1, wp1, name="ple_proj1")
    z1 = matmul(h2b, wg1, name="ple_gate1")
    h2p = ple_fwd(h2, a1, z1, name="ple_fwd1")
    dh, loss = loss_head(h2p, tgt, name="loss_head")

    g = {}
    da1, dz1 = ple_bwd(dh, a1, z1, name="ple_bwd1")
    g_wp1 = matmul(p1, da1, ta=True, name="d_ple_proj1")
    g_wg1 = matmul(h2b, dz1, ta=True, name="d_ple_gate1")
    dh2 = matmul(dz1, wg1, tb=True, add=dh, name="dx_ple_gate1")
    dy1, g_post1_g = rms_bwd(y1, dh2, g_post1, out_dtype=BF16, name="rms_post1_bwd")
    g["c_w_out"] = matmul(yin, dy1, ta=True, name="d_c_out")
    dyin = matmul(dy1, w_cout, tb=True, out_dtype=BF16, name="dx_c_out")
    dproj1, g["c_w_s"], g_bs, g["c_ln_g"], g["c_ln_b"] = sgu_bwd(proj1, c_ln_g, c_ln_b, w_s, b_bc, dyin)
    g["c_b_s"] = g_bs.reshape(w_s.shape[:2])
    g["c_w_in"] = matmul(xn1, dproj1, ta=True, name="d_c_in")
    dxn1 = matmul(dproj1, w_c, tb=True, out_dtype=BF16, name="dx_c_in")
    dh1p, g_pre1_g = rms_bwd(h1p, dxn1, g_pre1, res=dh2, name="rms_pre1_bwd")
    da0, dz0 = ple_bwd(dh1p, a0, z0, name="ple_bwd0")
    g_wp0 = matmul(p0, da0, ta=True, name="d_ple_proj0")
    g_wg0 = matmul(h1b, dz0, ta=True, name="d_ple_gate0")
    dh1 = matmul(dz0, wg0, tb=True, add=dh1p, name="dx_ple_gate0")
    dy0, g_post0_g = rms_bwd(y0, dh1, g_post0, out_dtype=BF16, name="rms_post0_bwd")
    g["ab_w_out"] = matmul(ycat, dy0, ta=True, name="d_ab_out")
    dycat = matmul(dy0, w_out, tb=True, out_dtype=BF16, name="dx_ab_out")
    dysc, dr_b, dk_b, dv_b, dgb, g["rwkv_ln_g"], g["rwkv_ln_b"], g_rk = rwkv_post_bwd(ysc, sr, sk, sv, proj0, prm, dycat, nw)
    g["rwkv_r_k"] = g_rk.reshape(wts["rwkv_r_k"].shape)
    grads_scan = rwkv_scan_bwd(sr, sw, sk, sv, skk, ska, states, dysc, bsz, seq, nw)
    (dxr, dxk, dxv, dxwa, dmur, dmuk, dmuv, dmuwa, g["rwkv_w0"], g["rwkv_w2"], g["rwkv_a0"], g["rwkv_a2"],
     g["rwkv_k_k"], g["rwkv_k_a"]) = rwkv_pre_bwd(proj0, prm, grads_scan, (dr_b, dk_b, dv_b), bsz, seq, nw)
    g["rwkv_mu"] = jnp.concatenate([dmur, dmuk, dmuv, dmuwa], axis=1)
    dq, dk, dv, dga, dcrow = fox_attn_bwd(proj0, c, crow, oa, dycat, bsz, seq, nw)
    dc = jnp.pad(dcrow[:, :, :2, :].reshape(bsz, heads, seq).transpose(0, 2, 1), ((0, 0), (0, 0), (0, LANE - heads)))
    dfa, g_fb = fox_gate_bwd(proj0, fbias, dc.reshape(n, LANE), bsz, seq, nw)
    g["fox_f_bias"] = g_fb[:, :heads]
    dproj0 = jnp.concatenate([dq, dk, dv, dga, dxr, dxk, dxv, dgb, dxwa, dfa], axis=1)
    g["ab_w_in_r"] = matmul(xn0, dproj0, ta=True, name="d_ab_in")
    dxn0 = matmul(dproj0, w_ab, tb=True, out_dtype=BF16, name="dx_ab_in")
    dx, g_pre0_g = rms_bwd(x2, dxn0, g_pre0, res=dh1, name="rms_pre0_bwd")

    g["norm_pre"] = jnp.concatenate([g_pre0_g, g_pre1_g], axis=0)
    g["norm_post"] = jnp.concatenate([g_post0_g, g_post1_g], axis=0)
    g["ple_w_proj"] = jnp.stack([g_wp0, g_wp1])
    g["ple_w_gate"] = jnp.stack([g_wg0, g_wg1])
    return loss, dx.reshape(bsz, seq, d), g


MESH = pl.DeviceIdType.MESH
ANY = pl.BlockSpec(memory_space=pl.ANY)
PACK_COLS = 1024
PACK_ROWS = 16


def _mesh_place():
    xi, yi, ci = lax.axis_index("x"), lax.axis_index("y"), lax.axis_index("c")
    return xi, yi, ci, 4 * xi + 2 * yi + ci


def _peer(xi, yi, ci, m):
    px = 1 - xi if m & 4 else xi
    py = 1 - yi if m & 2 else yi
    pc = 1 - ci if m & 1 else ci
    return (px, py, pc), 4 * px + 2 * py + pc


def all_gather(x, *, name):
    def body(x_ref, out_ref, send_sems, recv_sems, local_sem):
        xi, yi, ci, me = _mesh_place()
        mine = pltpu.make_async_copy(x_ref, out_ref.at[me], local_sem)
        mine.start()
        sends = []
        for m in range(1, N_DEV):
            peer, _ = _peer(xi, yi, ci, m)
            cp = pltpu.make_async_remote_copy(src_ref=x_ref, dst_ref=out_ref.at[me], send_sem=send_sems.at[m - 1],
                                              recv_sem=recv_sems.at[m - 1], device_id=peer, device_id_type=MESH)
            cp.start()
            sends.append(cp)
        for m in range(1, N_DEV):
            peer, pidx = _peer(xi, yi, ci, m)
            pltpu.make_async_remote_copy(src_ref=x_ref, dst_ref=out_ref.at[pidx], send_sem=send_sems.at[m - 1],
                                         recv_sem=recv_sems.at[m - 1], device_id=peer, device_id_type=MESH).wait_recv()
        for cp in sends:
            cp.wait_send()
        mine.wait()

    return pl.pallas_call(
        body, name=name, in_specs=[ANY], out_specs=ANY, out_shape=jax.ShapeDtypeStruct((N_DEV,) + x.shape, x.dtype),
        scratch_shapes=[pltpu.SemaphoreType.DMA((N_DEV - 1,)), pltpu.SemaphoreType.DMA((N_DEV - 1,)), pltpu.SemaphoreType.DMA],
    )(x)


def all_to_all(x, *, name):
    def body(x_ref, out_ref, send_sems, recv_sems, local_sem):
        xi, yi, ci, me = _mesh_place()
        mine = pltpu.make_async_copy(x_ref.at[me], out_ref.at[me], local_sem)
        mine.start()
        sends = []
        for m in range(1, N_DEV):
            peer, pidx = _peer(xi, yi, ci, m)
            cp = pltpu.make_async_remote_copy(src_ref=x_ref.at[pidx], dst_ref=out_ref.at[me], send_sem=send_sems.at[m - 1],
                                              recv_sem=recv_sems.at[m - 1], device_id=peer, device_id_type=MESH)
            cp.start()
            sends.append(cp)
        for m in range(1, N_DEV):
            peer, pidx = _peer(xi, yi, ci, m)
            pltpu.make_async_remote_copy(src_ref=x_ref.at[pidx], dst_ref=out_ref.at[pidx], send_sem=send_sems.at[m - 1],
                                         recv_sem=recv_sems.at[m - 1], device_id=peer, device_id_type=MESH).wait_recv()
        for cp in sends:
            cp.wait_send()
        mine.wait()

    return pl.pallas_call(
        body, name=name, in_specs=[ANY], out_specs=ANY, out_shape=jax.ShapeDtypeStruct(x.shape, x.dtype),
        scratch_shapes=[pltpu.SemaphoreType.DMA((N_DEV - 1,)), pltpu.SemaphoreType.DMA((N_DEV - 1,)), pltpu.SemaphoreType.DMA],
    )(x)


def sum_blocks(x, *, name):
    _, rows, cols = x.shape
    tr = _tile(rows, 256, PACK_ROWS)

    def body(x_ref, o_ref):
        acc = x_ref[0].astype(F32)
        for s in range(1, N_DEV):
            acc = acc + x_ref[s].astype(F32)
        o_ref[...] = acc

    return pl.pallas_call(
        body, name=name, grid=(rows // tr,), in_specs=[pl.BlockSpec((N_DEV, tr, cols), lambda i: (0, i, 0))],
        out_specs=pl.BlockSpec((tr, cols), lambda i: (i, 0)), out_shape=jax.ShapeDtypeStruct((rows, cols), F32),
        compiler_params=_cparams(("parallel",)),
    )(x)


def adamw(w, g, m, v, *, name):
    rows, cols = w.shape
    tr = _tile(rows, 256, 8)
    c1 = 1.0 / (1.0 - ADAM_B1 ** ADAM_STEP)
    c2 = 1.0 / (1.0 - ADAM_B2 ** ADAM_STEP)

    def body(w_ref, g_ref, m_ref, v_ref, d_ref, mo_ref, vo_ref):
        gv = g_ref[...]
        mn = ADAM_B1 * m_ref[...] + (1.0 - ADAM_B1) * gv
        vn = ADAM_B2 * v_ref[...] + (1.0 - ADAM_B2) * (gv * gv)
        d_ref[...] = -ADAM_LR * ((mn * c1) / (jnp.sqrt(vn * c2) + ADAM_EPS) + ADAM_WD * w_ref[...])
        mo_ref[...] = mn
        vo_ref[...] = vn

    spec = pl.BlockSpec((tr, cols), lambda i: (i, 0))
    return pl.pallas_call(
        body, name=name, grid=(rows // tr,), in_specs=[spec] * 4, out_specs=[spec] * 3,
        out_shape=[jax.ShapeDtypeStruct((rows, cols), F32)] * 3, compiler_params=_cparams(("parallel",)),
    )(w, g, m, v)


def _pack(arrays, dtype):
    flat = jnp.concatenate([a.astype(dtype).reshape(-1) for a in arrays])
    unit = PACK_COLS * PACK_ROWS
    total = -(-flat.shape[0] // unit) * unit
    return jnp.pad(flat, (0, total - flat.shape[0])).reshape(total // PACK_COLS, PACK_COLS)


def _unpack(flat2d, shapes, lead=()):
    flat = flat2d.reshape(lead + (-1,))
    out, off = [], 0
    for shp in shapes:
        size = 1
        for s in shp:
            size *= s
        out.append(flat[..., off:off + size].reshape(lead + tuple(shp)))
        off += size
    return out


def _join_shards(sh, axis):
    return jnp.concatenate([sh[d] for d in range(N_DEV)], axis=axis)


def _split_shards(full, axis):
    per = full.shape[axis] // N_DEV
    return [lax.slice_in_dim(full, d * per, (d + 1) * per, axis=axis) for d in range(N_DEV)]


def _pack_shards(per_weight, dtype):
    flat = jnp.concatenate([jnp.stack([s.reshape(-1) for s in shards]).astype(dtype) for shards in per_weight], axis=1)
    unit = PACK_COLS * PACK_ROWS
    total = -(-flat.shape[1] // unit) * unit
    return jnp.pad(flat, ((0, 0), (0, total - flat.shape[1]))).reshape(N_DEV, total // PACK_COLS, PACK_COLS)


BIG = (("ab_w_in", 1), ("c_w_in", 1), ("ab_w_out", 0), ("c_w_out", 0), ("ple_w_gate", 1), ("ple_w_proj", 2))
SMALL_SHARDED = (("rwkv_w2", 1), ("rwkv_a2", 1), ("c_ln_g", 0), ("c_ln_b", 0))
REPLICATED = ("norm_pre", "norm_post", "fox_f_bias", "rwkv_mu", "rwkv_w0", "rwkv_a0", "rwkv_k_k", "rwkv_k_a", "rwkv_r_k",
              "rwkv_ln_g", "rwkv_ln_b", "c_w_s", "c_b_s")
WEIGHTS = ("norm_pre", "norm_post", "ab_w_in", "fox_f_bias", "rwkv_mu", "rwkv_w0", "rwkv_w2", "rwkv_a0", "rwkv_a2", "rwkv_k_k",
           "rwkv_k_a", "rwkv_r_k", "rwkv_ln_g", "rwkv_ln_b", "ab_w_out", "c_w_in", "c_ln_g", "c_ln_b", "c_w_s", "c_b_s",
           "c_w_out", "ple_w_proj", "ple_w_gate")
SQUEEZED = ("norm_pre", "norm_post", "ple_w_proj", "ple_w_gate")


def _step(x, p, loss_target, w, mom, vel):
    sq = {k: (a if k in SQUEEZED else a[0]) for k, a in w.items()}
    _, _, _, me = _mesh_place()

    big_shapes = [sq[k].shape for k, _ in BIG]
    gathered = all_gather(_pack([sq[k] for k, _ in BIG], BF16), name="gather_weights")
    full = {k: _join_shards(a, ax) for (k, ax), a in zip(BIG, _unpack(gathered, big_shapes, (N_DEV,))) if k != "ab_w_in"}
    wid, heads = x.shape[-1] // 2, x.shape[-1] // 2 // HEAD
    ab_shards = _unpack(gathered, big_shapes, (N_DEV,))[0]
    full["ab_w_in_r"] = _ab_reorder([ab_shards[d] for d in range(N_DEV)], wid, heads)
    small_shapes = [sq[k].shape for k, _ in SMALL_SHARDED]
    gathered_s = all_gather(_pack([sq[k] for k, _ in SMALL_SHARDED], F32), name="gather_small_weights")
    full.update({k: _join_shards(a, ax) for (k, ax), a in zip(SMALL_SHARDED, _unpack(gathered_s, small_shapes, (N_DEV,)))})
    for k in REPLICATED:
        full[k] = sq[k]

    loss, grad_x, g = local_step(x, p, loss_target, full)

    grad_shards = [_ab_restore(g["ab_w_in_r"], wid, heads, N_DEV) if k == "ab_w_in" else _split_shards(g[k], ax) for k, ax in BIG]
    parts = all_to_all(_pack_shards(grad_shards, BF16), name="exchange_grads")
    big_grads = _unpack(sum_blocks(parts, name="sum_grads"), big_shapes)
    grads = {k: a for (k, _), a in zip(BIG, big_grads)}

    small_names = [k for k, _ in SMALL_SHARDED] + list(REPLICATED)
    small_full_shapes = [full[k].shape for k in small_names] + [(1, 1)]
    partial = all_gather(_pack([g[k].reshape(full[k].shape) for k in small_names] + [loss], F32), name="gather_small_grads")
    summed = _unpack(sum_blocks(partial, name="sum_small_grads"), small_full_shapes)
    for k, a in zip(small_names, summed[:-1]):
        grads[k] = a
    for k, ax in SMALL_SHARDED:
        width = sq[k].shape[ax]
        grads[k] = lax.dynamic_slice_in_dim(grads[k], me * width, width, axis=ax)
    loss_all = summed[-1][0, 0]

    outs_g, outs_d, outs_m, outs_v = [], [], [], []
    for k in WEIGHTS:
        shape = w[k].shape
        two_d = lambda a: a.reshape(-1, shape[-1])
        d_w, n_m, n_v = adamw(two_d(w[k]), two_d(grads[k]), two_d(mom[k]), two_d(vel[k]), name="adamw_" + k)
        outs_g.append(grads[k].reshape(shape))
        outs_d.append(d_w.reshape(shape))
        outs_m.append(n_m.reshape(shape))
        outs_v.append(n_v.reshape(shape))
    return (loss_all, grad_x, *outs_g, *outs_d, *outs_m, *outs_v)


def kernel(x, p, norm_pre, norm_post, ab_w_in, fox_f_bias, rwkv_mu, rwkv_w0, rwkv_w2, rwkv_a0, rwkv_a2, rwkv_k_k, rwkv_k_a, rwkv_r_k, rwkv_ln_g, rwkv_ln_b, ab_w_out, c_w_in, c_ln_g, c_ln_b, c_w_s, c_b_s, c_w_out, ple_w_proj, ple_w_gate, loss_target, m_norm_pre, m_norm_post, m_ab_w_in, m_fox_f_bias, m_rwkv_mu, m_rwkv_w0, m_rwkv_w2, m_rwkv_a0, m_rwkv_a2, m_rwkv_k_k, m_rwkv_k_a, m_rwkv_r_k, m_rwkv_ln_g, m_rwkv_ln_b, m_ab_w_out, m_c_w_in, m_c_ln_g, m_c_ln_b, m_c_w_s, m_c_b_s, m_c_w_out, m_ple_w_proj, m_ple_w_gate, v_norm_pre, v_norm_post, v_ab_w_in, v_fox_f_bias, v_rwkv_mu, v_rwkv_w0, v_rwkv_w2, v_rwkv_a0, v_rwkv_a2, v_rwkv_k_k, v_rwkv_k_a, v_rwkv_r_k, v_rwkv_ln_g, v_rwkv_ln_b, v_ab_w_out, v_c_w_in, v_c_ln_g, v_c_ln_b, v_c_w_s, v_c_b_s, v_c_w_out, v_ple_w_proj, v_ple_w_gate):
    w = dict(norm_pre=norm_pre, norm_post=norm_post, ab_w_in=ab_w_in, fox_f_bias=fox_f_bias, rwkv_mu=rwkv_mu, rwkv_w0=rwkv_w0, rwkv_w2=rwkv_w2, rwkv_a0=rwkv_a0, rwkv_a2=rwkv_a2, rwkv_k_k=rwkv_k_k, rwkv_k_a=rwkv_k_a, rwkv_r_k=rwkv_r_k, rwkv_ln_g=rwkv_ln_g, rwkv_ln_b=rwkv_ln_b, ab_w_out=ab_w_out, c_w_in=c_w_in, c_ln_g=c_ln_g, c_ln_b=c_ln_b, c_w_s=c_w_s, c_b_s=c_b_s, c_w_out=c_w_out, ple_w_proj=ple_w_proj, ple_w_gate=ple_w_gate)
    mom = dict(norm_pre=m_norm_pre, norm_post=m_norm_post, ab_w_in=m_ab_w_in, fox_f_bias=m_fox_f_bias, rwkv_mu=m_rwkv_mu, rwkv_w0=m_rwkv_w0, rwkv_w2=m_rwkv_w2, rwkv_a0=m_rwkv_a0, rwkv_a2=m_rwkv_a2, rwkv_k_k=m_rwkv_k_k, rwkv_k_a=m_rwkv_k_a, rwkv_r_k=m_rwkv_r_k, rwkv_ln_g=m_rwkv_ln_g, rwkv_ln_b=m_rwkv_ln_b, ab_w_out=m_ab_w_out, c_w_in=m_c_w_in, c_ln_g=m_c_ln_g, c_ln_b=m_c_ln_b, c_w_s=m_c_w_s, c_b_s=m_c_b_s, c_w_out=m_c_w_out, ple_w_proj=m_ple_w_proj, ple_w_gate=m_ple_w_gate)
    vel = dict(norm_pre=v_norm_pre, norm_post=v_norm_post, ab_w_in=v_ab_w_in, fox_f_bias=v_fox_f_bias, rwkv_mu=v_rwkv_mu, rwkv_w0=v_rwkv_w0, rwkv_w2=v_rwkv_w2, rwkv_a0=v_rwkv_a0, rwkv_a2=v_rwkv_a2, rwkv_k_k=v_rwkv_k_k, rwkv_k_a=v_rwkv_k_a, rwkv_r_k=v_rwkv_r_k, rwkv_ln_g=v_rwkv_ln_g, rwkv_ln_b=v_rwkv_ln_b, ab_w_out=v_ab_w_out, c_w_in=v_c_w_in, c_ln_g=v_c_ln_g, c_ln_b=v_c_ln_b, c_w_s=v_c_w_s, c_b_s=v_c_b_s, c_w_out=v_c_w_out, ple_w_proj=v_ple_w_proj, ple_w_gate=v_ple_w_gate)
    return _step(x, p, loss_target, w, mom, vel)
```

```python
import functools

import jax
import jax.numpy as jnp
from jax import lax
from jax.experimental import pallas as pl
from jax.experimental.pallas import tpu as pltpu

F32 = jnp.float32
BF16 = jnp.bfloat16
HI = lax.Precision.HIGHEST

HEAD = 64
LANE = 128
LORA = 64
GMLP_CHUNK = 128
RMS_EPS = 1e-6
LN_EPS = 1e-5
GN_EPS = 64e-5
VMEM_LIMIT = 56 * 1024 * 1024
N_DEV = 8

ADAM_LR = 0.001
ADAM_B1 = 0.9
ADAM_B2 = 0.999
ADAM_EPS = 1e-08
ADAM_WD = 0.01
ADAM_STEP = 10

NT_DIMS = (((1,), (1,)), ((), ()))
TN_DIMS = (((0,), (0,)), ((), ()))


def _cparams(sem):
    return pltpu.CompilerParams(dimension_semantics=sem, vmem_limit_bytes=VMEM_LIMIT)


def _tile(n, target, q=LANE):
    if n <= target:
        return n
    best = None
    for d in range(q, target + 1, q):
        if n % d == 0:
            best = d
    assert best is not None, (n, target, q)
    return best


def _sigmoid(x):
    return 1.0 / (1.0 + jnp.exp(-x))


def _silu_and_grad(x):
    s = _sigmoid(x)
    return x * s, s * (1.0 + x * (1.0 - s))


def _gelu_and_grad(x):
    cdf = 0.5 * (1.0 + lax.erf(x * 0.7071067811865476))
    pdf = jnp.exp(-0.5 * x * x) * 0.3989422804014327
    return x * cdf, cdf + x * pdf


def _iota(shape, dim):
    return lax.broadcasted_iota(jnp.int32, shape, dim)


def _head_ones():
    return (_iota((LANE, LANE), 0) // HEAD == _iota((LANE, LANE), 1) // HEAD).astype(F32)


def _head_eye():
    return (_iota((HEAD, LANE), 0) == _iota((HEAD, LANE), 1) % HEAD).astype(F32)


def _hsum(x, ones):
    return jnp.dot(x, ones, precision=HI, preferred_element_type=F32)


def _colsum(x):
    return jnp.sum(x, axis=0, keepdims=True)


def matmul(a, b, *, name, ta=False, tb=False, add=None, out_dtype=F32, tm=512, tn=1024, tk=1024, col_blocks=1):
    m, k = (a.shape[1], a.shape[0]) if ta else a.shape
    n = b.shape[0] if tb else b.shape[1]
    assert (b.shape[1] if tb else b.shape[0]) == k
    tm, tn, tk = _tile(m, tm), _tile(n // col_blocks, tn), _tile(k, tk)
    nk = k // tk
    per = n // col_blocks // tn

    def body(*refs):
        if add is None:
            a_ref, b_ref, o_ref, acc_ref = refs
        else:
            a_ref, b_ref, add_ref, o_ref, acc_ref = refs
        kk = pl.program_id(2)

        @pl.when(kk == 0)
        def _():
            acc_ref[...] = jnp.zeros_like(acc_ref)

        dims = (((0 if ta else 1,), (1 if tb else 0,)), ((), ()))
        acc_ref[...] += lax.dot_general(a_ref[...].astype(BF16), b_ref[...].astype(BF16), dims,
                                        preferred_element_type=F32)

        @pl.when(kk == nk - 1)
        def _():
            r = acc_ref[...]
            if add is not None:
                r = r + add_ref[...].astype(F32)
            o_ref[...] = r.astype(out_dtype)

    a_spec = pl.BlockSpec((tk, tm), lambda i, j, kk: (kk, i)) if ta else pl.BlockSpec((tm, tk), lambda i, j, kk: (i, kk))
    b_spec = pl.BlockSpec((tn, tk), lambda i, j, kk: (j, kk)) if tb else pl.BlockSpec((tk, tn), lambda i, j, kk: (kk, j))
    o_spec = pl.BlockSpec((tm, tn), lambda i, j, kk: (i, j))
    in_specs = [a_spec, b_spec] + ([o_spec] if add is not None else [])
    args = (a, b) + ((add,) if add is not None else ())
    out_shape = (m, n)
    if col_blocks > 1:
        assert add is None
        o_spec = pl.BlockSpec((None, tm, tn), lambda i, j, kk: (j // per, i, j % per))
        out_shape = (col_blocks, m, n // col_blocks)
    return pl.pallas_call(
        body, name=name, grid=(m // tm, n // tn, nk), in_specs=in_specs, out_specs=o_spec,
        out_shape=jax.ShapeDtypeStruct(out_shape, out_dtype), scratch_shapes=[pltpu.VMEM((tm, tn), F32)],
        compiler_params=_cparams(("parallel", "parallel", "arbitrary")),
    )(*args)


def _row_spec(tm, width, cb=0):
    return pl.BlockSpec((tm, width), lambda i: (i, cb))


def _full_spec(shape):
    return pl.BlockSpec(shape, lambda i: (0,) * len(shape))


def rms_fwd(x, g, *, name):
    n, d = x.shape
    tm = _tile(n, 256, 8)

    def body(x_ref, g_ref, o_ref):
        xv = x_ref[...]
        r = lax.rsqrt(jnp.mean(xv * xv, axis=-1, keepdims=True) + RMS_EPS)
        o_ref[...] = (xv * r * g_ref[...]).astype(BF16)

    return pl.pallas_call(
        body, name=name, grid=(n // tm,), in_specs=[_row_spec(tm, d), _full_spec((1, d))],
        out_specs=_row_spec(tm, d), out_shape=jax.ShapeDtypeStruct((n, d), BF16),
        compiler_params=_cparams(("parallel",)),
    )(x, g)


def rms_res_fwd(y, res, g, *, name):
    n, d = y.shape
    tm = _tile(n, 256, 8)

    def body(y_ref, res_ref, g_ref, h_ref, hb_ref):
        yv = y_ref[...]
        r = lax.rsqrt(jnp.mean(yv * yv, axis=-1, keepdims=True) + RMS_EPS)
        h = res_ref[...] + yv * r * g_ref[...]
        h_ref[...] = h
        hb_ref[...] = h.astype(BF16)

    return pl.pallas_call(
        body, name=name, grid=(n // tm,), in_specs=[_row_spec(tm, d), _row_spec(tm, d), _full_spec((1, d))],
        out_specs=[_row_spec(tm, d), _row_spec(tm, d)],
        out_shape=[jax.ShapeDtypeStruct((n, d), F32), jax.ShapeDtypeStruct((n, d), BF16)],
        compiler_params=_cparams(("parallel",)),
    )(y, res, g)


def rms_bwd(x, dy, g, *, name, res=None, out_dtype=F32):
    n, d = x.shape
    tm = _tile(n, 256, 8)

    def body(*refs):
        if res is None:
            x_ref, dy_ref, g_ref, dx_ref, dg_ref = refs
        else:
            x_ref, dy_ref, g_ref, res_ref, dx_ref, dg_ref = refs

        @pl.when(pl.program_id(0) == 0)
        def _():
            dg_ref[...] = jnp.zeros_like(dg_ref)

        xv = x_ref[...]
        dyv = dy_ref[...].astype(F32)
        r = lax.rsqrt(jnp.mean(xv * xv, axis=-1, keepdims=True) + RMS_EPS)
        xhat = xv * r
        dxh = dyv * g_ref[...]
        dx = r * (dxh - xhat * jnp.mean(dxh * xhat, axis=-1, keepdims=True))
        if res is not None:
            dx = dx + res_ref[...]
        dx_ref[...] = dx.astype(out_dtype)
        dg_ref[...] += _colsum(dyv * xhat)

    in_specs = [_row_spec(tm, d), _row_spec(tm, d), _full_spec((1, d))] + ([_row_spec(tm, d)] if res is not None else [])
    args = (x, dy, g) + ((res,) if res is not None else ())
    return pl.pallas_call(
        body, name=name, grid=(n // tm,), in_specs=in_specs, out_specs=[_row_spec(tm, d), _full_spec((1, d))],
        out_shape=[jax.ShapeDtypeStruct((n, d), out_dtype), jax.ShapeDtypeStruct((1, d), F32)],
        compiler_params=_cparams(("arbitrary",)),
    )(*args)


def ple_fwd(h, a, z, *, name):
    n, d = h.shape
    tm = _tile(n, 256, 8)

    def body(h_ref, a_ref, z_ref, o_ref):
        o_ref[...] = h_ref[...] + a_ref[...] * _sigmoid(z_ref[...])

    return pl.pallas_call(
        body, name=name, grid=(n // tm,), in_specs=[_row_spec(tm, d)] * 3, out_specs=_row_spec(tm, d),
        out_shape=jax.ShapeDtypeStruct((n, d), F32), compiler_params=_cparams(("parallel",)),
    )(h, a, z)


def ple_bwd(dh, a, z, *, name):
    n, d = dh.shape
    tm = _tile(n, 256, 8)

    def body(dh_ref, a_ref, z_ref, da_ref, dz_ref):
        s = _sigmoid(z_ref[...])
        dhv = dh_ref[...]
        da_ref[...] = (dhv * s).astype(BF16)
        dz_ref[...] = (dhv * a_ref[...] * s * (1.0 - s)).astype(BF16)

    return pl.pallas_call(
        body, name=name, grid=(n // tm,), in_specs=[_row_spec(tm, d)] * 3, out_specs=[_row_spec(tm, d)] * 2,
        out_shape=[jax.ShapeDtypeStruct((n, d), BF16)] * 2, compiler_params=_cparams(("parallel",)),
    )(dh, a, z)


def loss_head(h, target, *, name):
    n, d = h.shape
    tm = _tile(n, 256, 8)

    def body(h_ref, t_ref, dh_ref, loss_ref):
        @pl.when(pl.program_id(0) == 0)
        def _():
            loss_ref[...] = jnp.zeros_like(loss_ref)

        e = h_ref[...] - t_ref[...]
        dh_ref[...] = e * (1.0 / d)
        loss_ref[...] += 0.5 * jnp.sum(jnp.sum(e * e, axis=-1, keepdims=True) * (1.0 / d), axis=0, keepdims=True)

    return pl.pallas_call(
        body, name=name, grid=(n // tm,), in_specs=[_row_spec(tm, d)] * 2,
        out_specs=[_row_spec(tm, d), _full_spec((1, 1))],
        out_shape=[jax.ShapeDtypeStruct((n, d), F32), jax.ShapeDtypeStruct((1, 1), F32)],
        compiler_params=_cparams(("arbitrary",)),
    )(h, target)


def _log_sigmoid(z):
    return jnp.minimum(z, 0.0) - jnp.log1p(jnp.exp(-jnp.abs(z)))


def fox_gate_fwd(proj, fbias, bsz, seq, nw):
    cc = _tile(seq, 256, 8)

    def body(f_ref, b_ref, c_ref):
        low = (_iota((cc, cc), 0) >= _iota((cc, cc), 1)).astype(F32)
        carry = jnp.zeros((1, LANE), F32)
        for ci in range(seq // cc):
            rows = slice(ci * cc, (ci + 1) * cc)
            lf = _log_sigmoid(f_ref[rows, :] + b_ref[...])
            c_ref[rows, :] = jnp.dot(low, lf, precision=HI, preferred_element_type=F32) + carry
            carry = carry + _colsum(lf)

    return pl.pallas_call(
        body, name="fox_gate_fwd", grid=(bsz,),
        in_specs=[pl.BlockSpec((seq, LANE), lambda b: (b, 8 * nw + 1)), _full_spec((1, LANE))],
        out_specs=pl.BlockSpec((seq, LANE), lambda b: (b, 0)),
        out_shape=jax.ShapeDtypeStruct((bsz * seq, LANE), F32), compiler_params=_cparams(("parallel",)),
    )(proj, fbias)


def fox_gate_bwd(proj, fbias, dc, bsz, seq, nw):
    cc = _tile(seq, 256, 8)
    nc = seq // cc

    def body(f_ref, b_ref, dc_ref, df_ref, db_ref):
        @pl.when(pl.program_id(0) == 0)
        def _():
            db_ref[...] = jnp.zeros_like(db_ref)

        upp = (_iota((cc, cc), 0) <= _iota((cc, cc), 1)).astype(F32)
        carry = jnp.zeros((1, LANE), F32)
        dbias = jnp.zeros((1, LANE), F32)
        for ci in reversed(range(nc)):
            rows = slice(ci * cc, (ci + 1) * cc)
            blk = dc_ref[rows, :]
            dlf = jnp.dot(upp, blk, precision=HI, preferred_element_type=F32) + carry
            carry = carry + _colsum(blk)
            df = dlf * _sigmoid(-(f_ref[rows, :] + b_ref[...]))
            df_ref[rows, :] = df.astype(BF16)
            dbias = dbias + _colsum(df)
        db_ref[...] += dbias

    return pl.pallas_call(
        body, name="fox_gate_bwd", grid=(bsz,),
        in_specs=[pl.BlockSpec((seq, LANE), lambda b: (b, 8 * nw + 1)), _full_spec((1, LANE)),
                  pl.BlockSpec((seq, LANE), lambda b: (b, 0))],
        out_specs=[pl.BlockSpec((seq, LANE), lambda b: (b, 0)), _full_spec((1, LANE))],
        out_shape=[jax.ShapeDtypeStruct((bsz * seq, LANE), BF16), jax.ShapeDtypeStruct((1, LANE), F32)],
        compiler_params=_cparams(("arbitrary",)),
    )(proj, fbias, dc)


def _attn_scores(qs, kh, ccol, crow, r0, kend, tq):
    s = lax.dot_general(qs[r0:kend], kh[:kend], NT_DIMS, preferred_element_type=F32)
    s = s + ccol[r0:kend] - crow[:, :kend]
    causal = _iota((tq, kend), 1) <= r0 + _iota((tq, kend), 0)
    return jnp.where(causal, s, -jnp.inf)


def _head_column(c_blk, h):
    return jnp.sum(jnp.where(_iota(c_blk.shape, 1) == h, c_blk, 0.0), axis=1, keepdims=True)


def fox_attn_fwd(proj, c, crow, bsz, seq, nw):
    tq = _tile(seq, 256, 8)
    scale = HEAD ** -0.5

    def body(q_ref, k_ref, v_ref, g_ref, c_ref, crow_ref, o_ref, y_ref):
        hp = pl.program_id(1)
        c_blk = c_ref[...]
        for hh in range(2):
            sl = slice(hh * HEAD, (hh + 1) * HEAD)
            ccol = _head_column(c_blk, 2 * hp + hh)
            crow_h = crow_ref[0, 0, hh:hh + 1, :]
            qs = (q_ref[:, sl] * scale).astype(BF16)
            kh = k_ref[:, sl].astype(BF16)
            vh = v_ref[:, sl].astype(BF16)
            for qi in range(seq // tq):
                r0, kend = qi * tq, (qi + 1) * tq
                s = _attn_scores(qs, kh, ccol, crow_h, r0, kend, tq)
                p = jnp.exp(s - jnp.max(s, axis=-1, keepdims=True))
                o = jnp.dot(p.astype(BF16), vh[:kend], preferred_element_type=F32) / jnp.sum(p, axis=-1, keepdims=True)
                o_ref[r0:kend, sl] = o
                y_ref[r0:kend, sl] = (o * _silu_and_grad(g_ref[r0:kend, sl])[0]).astype(BF16)

    blk = lambda cb: pl.BlockSpec((seq, LANE), lambda b, j: (b, cb * nw + j))
    return pl.pallas_call(
        body, name="fox_attn_fwd", grid=(bsz, nw),
        in_specs=[blk(0), blk(1), blk(2), blk(3), pl.BlockSpec((seq, LANE), lambda b, j: (b, 0)),
                  pl.BlockSpec((1, 1, 8, seq), lambda b, j: (b, j, 0, 0))],
        out_specs=[pl.BlockSpec((seq, LANE), lambda b, j: (b, j))] * 2,
        out_shape=[jax.ShapeDtypeStruct((bsz * seq, nw * LANE), F32), jax.ShapeDtypeStruct((bsz * seq, nw * LANE), BF16)],
        compiler_params=_cparams(("parallel", "parallel")),
    )(proj, proj, proj, proj, c, crow)


def fox_attn_bwd(proj, c, crow, oa, dycat, bsz, seq, nw):
    tq = _tile(seq, 256, 8)
    scale = HEAD ** -0.5

    def body(q_ref, k_ref, v_ref, g_ref, c_ref, crow_ref, o_ref, dy_ref,
             dq_ref, dk_ref, dv_ref, dg_ref, dc_ref, dk_acc, dv_acc, dc_acc):
        hp = pl.program_id(1)
        c_blk = c_ref[...]
        dc_ref[...] = jnp.zeros_like(dc_ref)
        for hh in range(2):
            sl = slice(hh * HEAD, (hh + 1) * HEAD)
            ccol = _head_column(c_blk, 2 * hp + hh)
            crow_h = crow_ref[0, 0, hh:hh + 1, :]
            qs = (q_ref[:, sl] * scale).astype(BF16)
            kh = k_ref[:, sl].astype(BF16)
            vh = v_ref[:, sl].astype(BF16)
            oh = o_ref[:, sl]
            dyh = dy_ref[:, sl].astype(F32)
            silu, dsilu = _silu_and_grad(g_ref[:, sl])
            dg_ref[:, sl] = (dyh * oh * dsilu).astype(BF16)
            do = dyh * silu
            dvec = jnp.sum(do * oh, axis=-1, keepdims=True)
            dob = do.astype(BF16)
            dk_acc[...] = jnp.zeros_like(dk_acc)
            dv_acc[...] = jnp.zeros_like(dv_acc)
            dc_acc[...] = jnp.zeros_like(dc_acc)
            for qi in range(seq // tq):
                r0, kend = qi * tq, (qi + 1) * tq
                s = _attn_scores(qs, kh, ccol, crow_h, r0, kend, tq)
                p = jnp.exp(s - jnp.max(s, axis=-1, keepdims=True))
                p = p / jnp.sum(p, axis=-1, keepdims=True)
                dp = lax.dot_general(dob[r0:kend], vh[:kend], NT_DIMS, preferred_element_type=F32)
                ds = p * (dp - dvec[r0:kend])
                dsb = ds.astype(BF16)
                dq_ref[r0:kend, sl] = (jnp.dot(dsb, kh[:kend], preferred_element_type=F32) * scale).astype(BF16)
                dk_acc[0:kend, :] += lax.dot_general(dsb, qs[r0:kend], TN_DIMS, preferred_element_type=F32)
                dv_acc[0:kend, :] += lax.dot_general(p.astype(BF16), dob[r0:kend], TN_DIMS, preferred_element_type=F32)
                dc_acc[:, 0:kend] += -_colsum(ds)
                rowsum = lax.dot_general(jnp.ones((8, kend), F32), ds, NT_DIMS, precision=HI, preferred_element_type=F32)
                dc_acc[:, r0:kend] += rowsum[0:1, :]
            dk_ref[:, sl] = dk_acc[...].astype(BF16)
            dv_ref[:, sl] = dv_acc[...].astype(BF16)
            dc_ref[0, 0, hh:hh + 1, :] = dc_acc[...]

    blk = lambda cb: pl.BlockSpec((seq, LANE), lambda b, j: (b, cb * nw + j))
    own = pl.BlockSpec((seq, LANE), lambda b, j: (b, j))
    n = bsz * seq
    return pl.pallas_call(
        body, name="fox_attn_bwd", grid=(bsz, nw),
        in_specs=[blk(0), blk(1), blk(2), blk(3), pl.BlockSpec((seq, LANE), lambda b, j: (b, 0)),
                  pl.BlockSpec((1, 1, 8, seq), lambda b, j: (b, j, 0, 0)), own, own],
        out_specs=[own] * 4 + [pl.BlockSpec((1, 1, 8, seq), lambda b, j: (b, j, 0, 0))],
        out_shape=[jax.ShapeDtypeStruct((n, nw * LANE), BF16)] * 4 + [jax.ShapeDtypeStruct((bsz, nw, 8, seq), F32)],
        scratch_shapes=[pltpu.VMEM((seq, HEAD), F32), pltpu.VMEM((seq, HEAD), F32), pltpu.VMEM((1, seq), F32)],
        compiler_params=_cparams(("parallel", "parallel")),
    )(proj, proj, proj, proj, c, crow, oa, dycat)


def _shift_mix(x, mu):
    prev = jnp.where(_iota(x.shape, 0) == 0, 0.0, pltpu.roll(x, 1, 0))
    return x + (prev - x) * mu, prev


def _rwkv_pre_math(r_in, k_in, v_in, wa_in, mu_r, mu_k, mu_v, mu_wa, w0, w2, a0, a2, k_k, k_a, ones):
    r, prev_r = _shift_mix(r_in, mu_r)
    k, prev_k = _shift_mix(k_in, mu_k)
    v, prev_v = _shift_mix(v_in, mu_v)
    wa, prev_wa = _shift_mix(wa_in, mu_wa)
    tw = jnp.tanh(wa[:, :LORA])
    alo = wa[:, LORA:]
    sg = _sigmoid(w0 + jnp.dot(tw, w2, precision=HI, preferred_element_type=F32))
    e = sg * 0.6065306597126334
    w = jnp.exp(-e)
    a = _sigmoid(a0 + jnp.dot(alo, a2, precision=HI, preferred_element_type=F32))
    kkraw = k * k_k
    nrm = jnp.sqrt(_hsum(kkraw * kkraw, ones))
    den = jnp.maximum(nrm, 1e-12)
    kk = kkraw / den
    k2 = k * (1.0 + (a - 1.0) * k_a)
    return dict(r=r, k=k, v=v, prev_r=prev_r, prev_k=prev_k, prev_v=prev_v, prev_wa=prev_wa, tw=tw, alo=alo,
                sg=sg, e=e, w=w, a=a, nrm=nrm, den=den, kk=kk, k2=k2)


def _rwkv_pre_specs(seq, nw):
    tok = lambda cb: pl.BlockSpec((seq, LANE), lambda b, j: (b, cb * nw + j))
    par = pl.BlockSpec((1, LANE), lambda b, j: (0, j))
    lora = pl.BlockSpec((LORA, LANE), lambda b, j: (0, j))
    return [tok(4), tok(5), tok(6), pl.BlockSpec((seq, LANE), lambda b, j: (b, 8 * nw)),
            par, par, par, pl.BlockSpec((1, LANE), lambda b, j: (0, 0)), par, lora, par, lora, par, par]


def rwkv_pre_fwd(proj, prm, bsz, seq, nw):
    def body(r_ref, k_ref, v_ref, wa_ref, mur, muk, muv, muwa, w0, w2, a0, a2, kk_, ka_,
             ro, wo, ko, vo, kko, kao):
        m = _rwkv_pre_math(r_ref[...], k_ref[...], v_ref[...], wa_ref[...], mur[...], muk[...], muv[...], muwa[...],
                           w0[...], w2[...], a0[...], a2[...], kk_[...], ka_[...], _head_ones())
        ro[...] = m["r"]
        wo[...] = m["w"]
        ko[...] = m["k2"]
        vo[...] = m["v"]
        kko[...] = m["kk"]
        kao[...] = m["kk"] * m["a"]

    n = bsz * seq
    own = pl.BlockSpec((seq, LANE), lambda b, j: (b, j))
    return pl.pallas_call(
        body, name="rwkv_pre_fwd", grid=(bsz, nw), in_specs=_rwkv_pre_specs(seq, nw), out_specs=[own] * 6,
        out_shape=[jax.ShapeDtypeStruct((n, nw * LANE), F32)] * 6, compiler_params=_cparams(("parallel", "parallel")),
    )(proj, proj, proj, proj, prm["mu_r"], prm["mu_k"], prm["mu_v"], prm["mu_wa"], prm["w0"], prm["w2"],
      prm["a0"], prm["a2"], prm["k_k"], prm["k_a"])


def rwkv_pre_bwd(proj, prm, grads_scan, grads_bonus, bsz, seq, nw):
    wid = nw * LANE

    def body(r_ref, k_ref, v_ref, wa_ref, mur, muk, muv, muwa, w0, w2, a0, a2, kk_, ka_,
             dr_s, dw_s, dk_s, dv_s, dkk_s, dka_s, dr_b, dk_b, dv_b,
             dxr, dxk, dxv, dxwa, dmur, dmuk, dmuv, dmuwa, dw0, dw2, da0, da2, dkk_p, dka_p, dwa_acc):
        b, j = pl.program_id(0), pl.program_id(1)
        cols = pl.ds(pl.multiple_of(j * LANE, LANE), LANE)
        acc_refs = (dmur, dmuk, dmuv, dmuwa, dw0, dw2, da0, da2, dkk_p, dka_p)

        @pl.when(jnp.logical_and(b == 0, j == 0))
        def _():
            for ref in acc_refs:
                ref[...] = jnp.zeros_like(ref)

        @pl.when(j == 0)
        def _():
            dwa_acc[...] = jnp.zeros_like(dwa_acc)

        ones = _head_ones()
        r_in, k_in, v_in, wa_in = r_ref[...], k_ref[...], v_ref[...], wa_ref[...]
        m = _rwkv_pre_math(r_in, k_in, v_in, wa_in, mur[...], muk[...], muv[...], muwa[...],
                           w0[...], w2[...], a0[...], a2[...], kk_[...], ka_[...], ones)
        k, a, kk = m["k"], m["a"], m["kk"]
        dr = dr_s[...] + dr_b[...]
        dk2 = dk_s[...] + dk_b[...]
        dv = dv_s[...] + dv_b[...]
        dka = dka_s[...]
        da = dka * kk + dk2 * k * ka_[...]
        dkk = dkk_s[...] + dka * a
        dkkraw = jnp.where(m["nrm"] > 1e-12, dkk - kk * _hsum(dkk * kk, ones), dkk) / m["den"]
        dk = dkkraw * kk_[...] + dk2 * (1.0 + (a - 1.0) * ka_[...])
        dkk_p[:, cols] += _colsum(dkkraw * k)
        dka_p[:, cols] += _colsum(dk2 * k * (a - 1.0))
        dza = da * a * (1.0 - a)
        da0[:, cols] += _colsum(dza)
        da2[:, cols] += lax.dot_general(m["alo"], dza, TN_DIMS, precision=HI, preferred_element_type=F32)
        dalo = lax.dot_general(dza, a2[...], NT_DIMS, precision=HI, preferred_element_type=F32)
        dzw = -dw_s[...] * m["w"] * m["e"] * (1.0 - m["sg"])
        dw0[:, cols] += _colsum(dzw)
        dw2[:, cols] += lax.dot_general(m["tw"], dzw, TN_DIMS, precision=HI, preferred_element_type=F32)
        dtw = lax.dot_general(dzw, w2[...], NT_DIMS, precision=HI, preferred_element_type=F32)
        dwa_acc[:, 0:LORA] += dtw * (1.0 - m["tw"] * m["tw"])
        dwa_acc[:, LORA:LANE] += dalo

        def shift_bwd(dxs, x, prev, mu):
            g = dxs * mu
            nxt = jnp.where(_iota(g.shape, 0) == seq - 1, 0.0, pltpu.roll(g, seq - 1, 0))
            return dxs * (1.0 - mu) + nxt, _colsum(dxs * (prev - x))

        for dxs, x, prev, mu, out, dmu in ((dr, r_in, m["prev_r"], mur, dxr, dmur), (dk, k_in, m["prev_k"], muk, dxk, dmuk),
                                           (dv, v_in, m["prev_v"], muv, dxv, dmuv)):
            dx, dm = shift_bwd(dxs, x, prev, mu[...])
            out[...] = dx.astype(BF16)
            dmu[:, cols] += dm

        @pl.when(j == nw - 1)
        def _():
            dx, dm = shift_bwd(dwa_acc[...], wa_in, m["prev_wa"], muwa[...])
            dxwa[...] = dx.astype(BF16)
            dmuwa[...] += dm

    n = bsz * seq
    own = pl.BlockSpec((seq, LANE), lambda b, j: (b, j))
    whole = lambda shape: pl.BlockSpec(shape, lambda b, j: (0,) * len(shape))
    row = jax.ShapeDtypeStruct((1, wid), F32)
    return pl.pallas_call(
        body, name="rwkv_pre_bwd", grid=(bsz, nw), in_specs=_rwkv_pre_specs(seq, nw) + [own] * 9,
        out_specs=[own] * 3 + [pl.BlockSpec((seq, LANE), lambda b, j: (b, 0))]
        + [whole((1, wid))] * 3 + [whole((1, LANE)), whole((1, wid)), whole((LORA, wid)), whole((1, wid)),
                                   whole((LORA, wid)), whole((1, wid)), whole((1, wid))],
        out_shape=[jax.ShapeDtypeStruct((n, wid), BF16)] * 3 + [jax.ShapeDtypeStruct((n, LANE), BF16)]
        + [row] * 3 + [jax.ShapeDtypeStruct((1, LANE), F32), row, jax.ShapeDtypeStruct((LORA, wid), F32), row,
                       jax.ShapeDtypeStruct((LORA, wid), F32), row, row],
        scratch_shapes=[pltpu.VMEM((seq, LANE), F32)],
        compiler_params=_cparams(("arbitrary", "arbitrary")),
    )(proj, proj, proj, proj, prm["mu_r"], prm["mu_k"], prm["mu_v"], prm["mu_wa"], prm["w0"], prm["w2"],
      prm["a0"], prm["a2"], prm["k_k"], prm["k_a"], *grads_scan, *grads_bonus)


def _scan_group(nw):
    return _tile(nw, 4, 1)


ROWS = 8


def _scan_consts():
    ones = _head_ones().astype(BF16)
    return _head_eye(), ones, jnp.concatenate([ones, ones], axis=0)


def _cols8(tile, eye, ones, ones2, two_parts):
    hi = tile.astype(BF16).astype(F32)
    place = lambda part, q: (part[q:q + 1, :] * eye).astype(BF16)
    if two_parts:
        lo = tile - hi
        lhs = jnp.concatenate([jnp.concatenate([place(hi, q), place(lo, q)], axis=1) for q in range(ROWS)], axis=0)
        big = jnp.dot(lhs, ones2, preferred_element_type=F32)
    else:
        big = jnp.dot(jnp.concatenate([place(hi, q) for q in range(ROWS)], axis=0), ones, preferred_element_type=F32)
    return [big[q * HEAD:(q + 1) * HEAD, :] for q in range(ROWS)]


def _rows8(mats, eye, ones):
    big = jnp.dot(jnp.concatenate([m.astype(BF16) for m in mats], axis=0), ones, preferred_element_type=F32)
    return _stack_rows([_colsum(big[q * HEAD:(q + 1) * HEAD, :] * eye) for q in range(ROWS)])


def _stack_rows(rows):
    rid = _iota((ROWS, LANE), 0)
    out = jnp.zeros((ROWS, LANE), F32)
    for q in range(ROWS):
        out = jnp.where(rid == q, rows[q], out)
    return out


def rwkv_scan_fwd(r, w, k, v, kk, ka, bsz, seq, nw):
    grp = _scan_group(nw)
    tc = _tile(seq, 32, 8)
    nt = seq // tc

    def body(r_ref, w_ref, k_ref, v_ref, kk_ref, ka_ref, y_ref, st_ref, s_ref):
        @pl.when(pl.program_id(2) == 0)
        def _():
            s_ref[...] = jnp.zeros_like(s_ref)

        eye, ones, ones2 = _scan_consts()

        def step(i8, carry):
            base = pl.multiple_of(i8 * ROWS, ROWS)
            for g in range(grp):
                cs = slice(g * LANE, (g + 1) * LANE)
                cols = lambda ref, two: _cols8(ref[pl.ds(base, ROWS), cs], eye, ones, ones2, two)
                rm, wm, km, kkm, kam = cols(r_ref, False), cols(w_ref, True), cols(k_ref, False), cols(kk_ref, True), cols(ka_ref, True)
                vt = v_ref[pl.ds(base, ROWS), cs]
                s = s_ref[g]
                ys = []
                for q in range(ROWS):
                    st_ref[base + q, g] = s
                    sa = _colsum(s * kkm[q])
                    s = s * wm[q] - kam[q] * sa + km[q] * vt[q:q + 1, :]
                    ys.append(_colsum(s * rm[q]))
                y_ref[pl.ds(base, ROWS), cs] = _stack_rows(ys)
                s_ref[g] = s
            return carry

        lax.fori_loop(0, tc // ROWS, step, 0)

    tok = pl.BlockSpec((tc, grp * LANE), lambda b, g, t: (b * nt + t, g))
    n = bsz * seq
    return pl.pallas_call(
        body, name="rwkv_scan_fwd", grid=(bsz, nw // grp, nt), in_specs=[tok] * 6,
        out_specs=[tok, pl.BlockSpec((tc, grp, HEAD, LANE), lambda b, g, t: (b * nt + t, g, 0, 0))],
        out_shape=[jax.ShapeDtypeStruct((n, nw * LANE), F32), jax.ShapeDtypeStruct((n, nw, HEAD, LANE), F32)],
        scratch_shapes=[pltpu.VMEM((grp, HEAD, LANE), F32)],
        compiler_params=_cparams(("parallel", "parallel", "arbitrary")),
    )(r, w, k, v, kk, ka)


def rwkv_scan_bwd(r, w, k, v, kk, ka, states, dy, bsz, seq, nw):
    grp = _scan_group(nw)
    tc = _tile(seq, 32, 8)
    nt = seq // tc
    nblk = tc // ROWS

    def body(r_ref, w_ref, k_ref, v_ref, kk_ref, ka_ref, st_ref, dy_ref,
             dr_ref, dw_ref, dk_ref, dv_ref, dkk_ref, dka_ref, ds_ref):
        @pl.when(pl.program_id(2) == 0)
        def _():
            ds_ref[...] = jnp.zeros_like(ds_ref)

        eye, ones, ones2 = _scan_consts()

        def step(ii, carry):
            base = pl.multiple_of((nblk - 1 - ii) * ROWS, ROWS)
            rows = pl.ds(base, ROWS)
            for g in range(grp):
                cs = slice(g * LANE, (g + 1) * LANE)
                cols = lambda ref, two: _cols8(ref[rows, cs], eye, ones, ones2, two)
                rm, wm, km, kkm, kam = cols(r_ref, False), cols(w_ref, True), cols(k_ref, False), cols(kk_ref, True), cols(ka_ref, True)
                vt, dyt = v_ref[rows, cs], dy_ref[rows, cs]
                ds = ds_ref[g]
                dvs, p_dr, p_dk, p_dka, p_dw, p_dkk = ([None] * ROWS for _ in range(6))
                for q in reversed(range(ROWS)):
                    sp = st_ref[base + q, g]
                    vrow, dyrow = vt[q:q + 1, :], dyt[q:q + 1, :]
                    sa = _colsum(sp * kkm[q])
                    st = sp * wm[q] - kam[q] * sa + km[q] * vrow
                    ds = ds + rm[q] * dyrow
                    dvs[q] = _colsum(ds * km[q])
                    dsa = -_colsum(ds * kam[q])
                    p_dr[q] = st * dyrow
                    p_dk[q] = ds * vrow
                    p_dka[q] = -(ds * sa)
                    p_dw[q] = ds * sp
                    p_dkk[q] = sp * dsa
                    ds = ds * wm[q] + kkm[q] * dsa
                ds_ref[g] = ds
                dv_ref[rows, cs] = _stack_rows(dvs)
                dr_ref[rows, cs] = _rows8(p_dr, eye, ones)
                dk_ref[rows, cs] = _rows8(p_dk, eye, ones)
                dka_ref[rows, cs] = _rows8(p_dka, eye, ones)
                dw_ref[rows, cs] = _rows8(p_dw, eye, ones)
                dkk_ref[rows, cs] = _rows8(p_dkk, eye, ones)
            return carry

        lax.fori_loop(0, nblk, step, 0)

    tok = pl.BlockSpec((tc, grp * LANE), lambda b, g, t: (b * nt + nt - 1 - t, g))
    n = bsz * seq
    return pl.pallas_call(
        body, name="rwkv_scan_bwd", grid=(bsz, nw // grp, nt),
        in_specs=[tok] * 6 + [pl.BlockSpec((tc, grp, HEAD, LANE), lambda b, g, t: (b * nt + nt - 1 - t, g, 0, 0)), tok],
        out_specs=[tok] * 6, out_shape=[jax.ShapeDtypeStruct((n, nw * LANE), F32)] * 6,
        scratch_shapes=[pltpu.VMEM((grp, HEAD, LANE), F32)],
        compiler_params=_cparams(("parallel", "parallel", "arbitrary")),
    )(r, w, k, v, kk, ka, states, dy)


def _rwkv_post_math(y, r, k2, v, ln_g, ln_b, r_k, ones):
    d = y - _hsum(y, ones) * (1.0 / HEAD)
    rstd = lax.rsqrt(_hsum(d * d, ones) * (1.0 / HEAD) + GN_EPS)
    yn = d * rstd
    s = _hsum(r * k2 * r_k, ones)
    return yn, rstd, s, yn * ln_g + ln_b + s * v


def _rwkv_post_specs(tm, nw):
    tok = pl.BlockSpec((tm, LANE), lambda i, j: (i, j))
    par = pl.BlockSpec((1, LANE), lambda i, j: (0, j))
    return [tok] * 4 + [pl.BlockSpec((tm, LANE), lambda i, j: (i, 7 * nw + j)), par, par, par]


def rwkv_post_fwd(y, r, k2, v, proj, prm, nw):
    n = y.shape[0]
    tm = _tile(n, 512, 8)

    def body(y_ref, r_ref, k_ref, v_ref, g_ref, lg, lb, rk, o_ref):
        ob = _rwkv_post_math(y_ref[...], r_ref[...], k_ref[...], v_ref[...], lg[...], lb[...], rk[...], _head_ones())[3]
        o_ref[...] = (ob * _silu_and_grad(g_ref[...])[0]).astype(BF16)

    return pl.pallas_call(
        body, name="rwkv_post_fwd", grid=(n // tm, nw), in_specs=_rwkv_post_specs(tm, nw),
        out_specs=pl.BlockSpec((tm, LANE), lambda i, j: (i, j)), out_shape=jax.ShapeDtypeStruct((n, nw * LANE), BF16),
        compiler_params=_cparams(("parallel", "parallel")),
    )(y, r, k2, v, proj, prm["ln_g"], prm["ln_b"], prm["r_k"])


def rwkv_post_bwd(y, r, k2, v, proj, prm, dycat, nw):
    n = y.shape[0]
    tm = _tile(n, 512, 8)
    wid = nw * LANE

    def body(y_ref, r_ref, k_ref, v_ref, g_ref, lg, lb, rk, dyc_ref,
             dy_ref, dr_ref, dk_ref, dv_ref, dg_ref, dlg, dlb, drk):
        i, j = pl.program_id(0), pl.program_id(1)
        cols = pl.ds(pl.multiple_of(j * LANE, LANE), LANE)

        @pl.when(jnp.logical_and(i == 0, j == 0))
        def _():
            for ref in (dlg, dlb, drk):
                ref[...] = jnp.zeros_like(ref)

        ones = _head_ones()
        rr, kr, vr = r_ref[...], k_ref[...], v_ref[...]
        yn, rstd, s, ob = _rwkv_post_math(y_ref[...], rr, kr, vr, lg[...], lb[...], rk[...], ones)
        silu, dsilu = _silu_and_grad(g_ref[...])
        dyc = dyc_ref[...].astype(F32)
        dg_ref[...] = (dyc * ob * dsilu).astype(BF16)
        dob = dyc * silu
        dlg[:, cols] += _colsum(dob * yn)
        dlb[:, cols] += _colsum(dob)
        dyn = dob * lg[...]
        dy_ref[...] = rstd * (dyn - _hsum(dyn, ones) * (1.0 / HEAD) - yn * _hsum(dyn * yn, ones) * (1.0 / HEAD))
        dv_ref[...] = dob * s
        dsum = _hsum(dob * vr, ones)
        dr_ref[...] = dsum * kr * rk[...]
        dk_ref[...] = dsum * rr * rk[...]
        drk[:, cols] += _colsum(dsum * rr * kr)

    tok = pl.BlockSpec((tm, LANE), lambda i, j: (i, j))
    whole = pl.BlockSpec((1, wid), lambda i, j: (0, 0))
    return pl.pallas_call(
        body, name="rwkv_post_bwd", grid=(n // tm, nw),
        in_specs=_rwkv_post_specs(tm, nw) + [pl.BlockSpec((tm, LANE), lambda i, j: (i, nw + j))],
        out_specs=[tok] * 5 + [whole] * 3,
        out_shape=[jax.ShapeDtypeStruct((n, wid), F32)] * 4 + [jax.ShapeDtypeStruct((n, wid), BF16)]
        + [jax.ShapeDtypeStruct((1, wid), F32)] * 3,
        compiler_params=_cparams(("arbitrary", "arbitrary")),
    )(y, r, k2, v, proj, prm["ln_g"], prm["ln_b"], prm["r_k"], dycat)


def _sgu_math(blk, ln_g, ln_b, ws_ref, bb_ref, mixed_ref, d, ngr):
    u, v, g = blk[:, 0:d], blk[:, d:2 * d], blk[:, 2 * d:3 * d]
    gu, dgu = _gelu_and_grad(u)
    gv, dgv = _gelu_and_grad(v)
    cen = gv - jnp.mean(gv, axis=-1, keepdims=True)
    rstd = lax.rsqrt(jnp.mean(cen * cen, axis=-1, keepdims=True) + LN_EPS)
    vhat = cen * rstd
    vn = vhat * ln_g + ln_b
    tril = _iota((GMLP_CHUNK, GMLP_CHUNK), 0) >= _iota((GMLP_CHUNK, GMLP_CHUNK), 1)
    for gi in range(ngr):
        cs = slice(gi * LANE, (gi + 1) * LANE)
        wm = jnp.where(tril, ws_ref[gi], 0.0)
        mixed_ref[:, cs] = jnp.dot(wm, vn[:, cs], precision=HI, preferred_element_type=F32) + bb_ref[gi]
    return dict(g=g, gu=gu, dgu=dgu, dgv=dgv, rstd=rstd, vhat=vhat, vn=vn, tril=tril)


def sgu_fwd(proj, ln_g, ln_b, w_s, b_bc):
    n, d3 = proj.shape
    d = d3 // 3
    ngr = d // LANE

    def body(p_ref, lg, lb, ws_ref, bb_ref, o_ref, mixed_ref):
        m = _sgu_math(p_ref[...], lg[...], lb[...], ws_ref, bb_ref, mixed_ref, d, ngr)
        o_ref[...] = (m["gu"] * mixed_ref[...] * _silu_and_grad(m["g"])[0]).astype(BF16)

    return pl.pallas_call(
        body, name="sgu_fwd", grid=(n // GMLP_CHUNK,),
        in_specs=[_row_spec(GMLP_CHUNK, d3), _full_spec((1, d)), _full_spec((1, d)),
                  _full_spec((ngr, GMLP_CHUNK, GMLP_CHUNK)), _full_spec((ngr, GMLP_CHUNK, LANE))],
        out_specs=_row_spec(GMLP_CHUNK, d), out_shape=jax.ShapeDtypeStruct((n, d), BF16),
        scratch_shapes=[pltpu.VMEM((GMLP_CHUNK, d), F32)], compiler_params=_cparams(("parallel",)),
    )(proj, ln_g, ln_b, w_s, b_bc)


def sgu_bwd(proj, ln_g, ln_b, w_s, b_bc, dyin):
    n, d3 = proj.shape
    d = d3 // 3
    ngr = d // LANE
    nsteps = n // GMLP_CHUNK

    def body(p_ref, lg, lb, ws_ref, bb_ref, dy_ref, dp_ref, dws_ref, dbs_ref, dlg_ref, dlb_ref,
             mixed_ref, dvn_ref, dbacc_ref):
        step = pl.program_id(0)

        @pl.when(step == 0)
        def _():
            for ref in (dws_ref, dlg_ref, dlb_ref, dbacc_ref):
                ref[...] = jnp.zeros_like(ref)

        m = _sgu_math(p_ref[...], lg[...], lb[...], ws_ref, bb_ref, mixed_ref, d, ngr)
        silu, dsilu = _silu_and_grad(m["g"])
        dyv = dy_ref[...].astype(F32)
        mixed = mixed_ref[...]
        dp_ref[:, 2 * d:3 * d] = (dyv * m["gu"] * mixed * dsilu).astype(BF16)
        doc = dyv * silu
        dp_ref[:, 0:d] = (doc * mixed * m["dgu"]).astype(BF16)
        dmixed = doc * m["gu"]
        for gi in range(ngr):
            cs = slice(gi * LANE, (gi + 1) * LANE)
            dm = dmixed[:, cs]
            wm = jnp.where(m["tril"], ws_ref[gi], 0.0)
            dws_ref[gi] += jnp.where(m["tril"], lax.dot_general(dm, m["vn"][:, cs], NT_DIMS, precision=HI,
                                                                preferred_element_type=F32), 0.0)
            dbacc_ref[gi] += dm
            dvn_ref[:, cs] = lax.dot_general(wm, dm, TN_DIMS, precision=HI, preferred_element_type=F32)
        dvn = dvn_ref[...]
        dlg_ref[...] += _colsum(dvn * m["vhat"])
        dlb_ref[...] += _colsum(dvn)
        dvh = dvn * lg[...]
        dgv = m["rstd"] * (dvh - jnp.mean(dvh, axis=-1, keepdims=True)
                           - m["vhat"] * jnp.mean(dvh * m["vhat"], axis=-1, keepdims=True))
        dp_ref[:, d:2 * d] = (dgv * m["dgv"]).astype(BF16)

        @pl.when(step == nsteps - 1)
        def _():
            dbs_ref[...] = jnp.sum(dbacc_ref[...], axis=-1, keepdims=True)

    return pl.pallas_call(
        body, name="sgu_bwd", grid=(nsteps,),
        in_specs=[_row_spec(GMLP_CHUNK, d3), _full_spec((1, d)), _full_spec((1, d)),
                  _full_spec((ngr, GMLP_CHUNK, GMLP_CHUNK)), _full_spec((ngr, GMLP_CHUNK, LANE)), _row_spec(GMLP_CHUNK, d)],
        out_specs=[_row_spec(GMLP_CHUNK, d3), _full_spec((ngr, GMLP_CHUNK, GMLP_CHUNK)), _full_spec((ngr, GMLP_CHUNK, 1)),
                   _full_spec((1, d)), _full_spec((1, d))],
        out_shape=[jax.ShapeDtypeStruct((n, d3), BF16), jax.ShapeDtypeStruct((ngr, GMLP_CHUNK, GMLP_CHUNK), F32),
                   jax.ShapeDtypeStruct((ngr, GMLP_CHUNK, 1), F32), jax.ShapeDtypeStruct((1, d), F32),
                   jax.ShapeDtypeStruct((1, d), F32)],
        scratch_shapes=[pltpu.VMEM((GMLP_CHUNK, d), F32), pltpu.VMEM((GMLP_CHUNK, d), F32),
                        pltpu.VMEM((ngr, GMLP_CHUNK, LANE), F32)],
        compiler_params=_cparams(("arbitrary",)),
    )(proj, ln_g, ln_b, w_s, b_bc, dyin)


def _ab_segments(wid, heads):
    return ((0, 3 * wid, 0),
            (3 * wid, 3 * wid + heads, 8 * wid + LANE),
            (3 * wid + heads, 4 * wid + heads, 3 * wid),
            (4 * wid + heads, 7 * wid + heads, 4 * wid),
            (7 * wid + heads, 7 * wid + heads + 2 * LORA, 8 * wid),
            (7 * wid + heads + 2 * LORA, 8 * wid + heads + 2 * LORA, 7 * wid))


def _ab_reorder(shards, wid, heads):
    per = shards[0].shape[1]
    pieces = []
    for o0, o1, _ in sorted(_ab_segments(wid, heads), key=lambda s: s[2]):
        for d, sh in enumerate(shards):
            lo, hi = max(o0, d * per), min(o1, (d + 1) * per)
            if lo < hi:
                pieces.append(sh[:, lo - d * per:hi - d * per])
    pieces.append(jnp.zeros((shards[0].shape[0], LANE - heads), shards[0].dtype))
    return jnp.concatenate(pieces, axis=1)


def _ab_restore(g, wid, heads, n_shards):
    per = (8 * wid + heads + 2 * LORA) // n_shards
    out = []
    for d in range(n_shards):
        pieces = []
        for o0, o1, r0 in _ab_segments(wid, heads):
            lo, hi = max(o0, d * per), min(o1, (d + 1) * per)
            if lo < hi:
                pieces.append(g[:, r0 + lo - o0:r0 + hi - o0])
        out.append(jnp.concatenate(pieces, axis=1))
    return out


def local_step(x, p, target, wts, n_shards):
    bsz, seq, d = x.shape
    n = bsz * seq
    wid = d // 2
    nw = wid // LANE
    heads = wid // HEAD
    x2 = x.reshape(n, d)
    tgt = target.reshape(n, d)
    p0, p1 = p[0].reshape(n, -1), p[1].reshape(n, -1)
    row = lambda a: a.reshape(1, -1)
    g_pre0, g_pre1 = row(wts["norm_pre"][0]), row(wts["norm_pre"][1])
    g_post0, g_post1 = row(wts["norm_post"][0]), row(wts["norm_post"][1])
    w_ab = wts["ab_w_in_r"]
    w_out, w_c, w_cout = wts["ab_w_out"], wts["c_w_in"], wts["c_w_out"]
    wp0, wp1 = wts["ple_w_proj0"], wts["ple_w_proj1"]
    wg0, wg1 = wts["ple_w_gate0"], wts["ple_w_gate1"]
    rows_cut = lambda a: a.reshape((n_shards, a.shape[0] // n_shards) + a.shape[1:])
    fbias = jnp.pad(row(wts["fox_f_bias"]), ((0, 0), (0, LANE - heads)))
    mu = row(wts["rwkv_mu"])
    prm = dict(mu_r=mu[:, 0:wid], mu_k=mu[:, wid:2 * wid], mu_v=mu[:, 2 * wid:3 * wid], mu_wa=mu[:, 3 * wid:],
               w0=row(wts["rwkv_w0"]), w2=wts["rwkv_w2"], a0=row(wts["rwkv_a0"]), a2=wts["rwkv_a2"],
               k_k=row(wts["rwkv_k_k"]), k_a=row(wts["rwkv_k_a"]), r_k=row(wts["rwkv_r_k"]),
               ln_g=row(wts["rwkv_ln_g"]), ln_b=row(wts["rwkv_ln_b"]))
    c_ln_g, c_ln_b = row(wts["c_ln_g"]), row(wts["c_ln_b"])
    w_s = wts["c_w_s"]
    b_bc = jnp.broadcast_to(wts["c_b_s"][:, :, None], w_s.shape[:2] + (LANE,))

    xn0 = rms_fwd(x2, g_pre0, name="rms_pre0")
    proj0 = matmul(xn0, w_ab, name="ab_in")
    c = fox_gate_fwd(proj0, fbias, bsz, seq, nw)
    crow = jnp.pad(c.reshape(bsz, seq, LANE)[:, :, :heads].transpose(0, 2, 1).reshape(bsz, nw, 2, seq),
                   ((0, 0), (0, 0), (0, 6), (0, 0)))
    oa, ya = fox_attn_fwd(proj0, c, crow, bsz, seq, nw)
    sr, sw, sk, sv, skk, ska = rwkv_pre_fwd(proj0, prm, bsz, seq, nw)
    ysc, states = rwkv_scan_fwd(sr, sw, sk, sv, skk, ska, bsz, seq, nw)
    yb = rwkv_post_fwd(ysc, sr, sk, sv, proj0, prm, nw)
    ycat = jnp.concatenate([ya, yb], axis=1)
    y0 = matmul(ycat, w_out, name="ab_out")
    h1, h1b = rms_res_fwd(y0, x2, g_post0, name="rms_post0")
    a0 = matmul(p0, wp0, name="ple_proj0")
    z0 = matmul(h1b, wg0, name="ple_gate0")
    h1p = ple_fwd(h1, a0, z0, name="ple_fwd0")
    xn1 = rms_fwd(h1p, g_pre1, name="rms_pre1")
    proj1 = matmul(xn1, w_c, name="c_in")
    yin = sgu_fwd(proj1, c_ln_g, c_ln_b, w_s, b_bc)
    y1 = matmul(yin, w_cout, name="c_out")
    h2, h2b = rms_res_fwd(y1, h1p, g_post1, name="rms_post1")
    a1 = matmul(p1, wp1, name="ple_proj1")
    z1 = matmul(h2b, wg1, name="ple_gate1")
    h2p = ple_fwd(h2, a1, z1, name="ple_fwd1")
    dh, loss = loss_head(h2p, tgt, name="loss_head")

    g = {}
    da1, dz1 = ple_bwd(dh, a1, z1, name="ple_bwd1")
    g["ple_w_proj1"] = matmul(p1, da1, ta=True, out_dtype=BF16, col_blocks=n_shards, name="d_ple_proj1")
    g["ple_w_gate1"] = rows_cut(matmul(h2b, dz1, ta=True, out_dtype=BF16, name="d_ple_gate1"))
    dh2 = matmul(dz1, wg1, tb=True, add=dh, name="dx_ple_gate1")
    dy1, g_post1_g = rms_bwd(y1, dh2, g_post1, out_dtype=BF16, name="rms_post1_bwd")
    g["c_w_out"] = rows_cut(matmul(yin, dy1, ta=True, out_dtype=BF16, name="d_c_out"))
    dyin = matmul(dy1, w_cout, tb=True, out_dtype=BF16, name="dx_c_out")
    dproj1, g["c_w_s"], g_bs, g["c_ln_g"], g["c_ln_b"] = sgu_bwd(proj1, c_ln_g, c_ln_b, w_s, b_bc, dyin)
    g["c_b_s"] = g_bs.reshape(w_s.shape[:2])
    g["c_w_in"] = matmul(xn1, dproj1, ta=True, out_dtype=BF16, col_blocks=n_shards, name="d_c_in")
    dxn1 = matmul(dproj1, w_c, tb=True, out_dtype=BF16, name="dx_c_in")
    dh1p, g_pre1_g = rms_bwd(h1p, dxn1, g_pre1, res=dh2, name="rms_pre1_bwd")
    da0, dz0 = ple_bwd(dh1p, a0, z0, name="ple_bwd0")
    g["ple_w_proj0"] = matmul(p0, da0, ta=True, out_dtype=BF16, col_blocks=n_shards, name="d_ple_proj0")
    g["ple_w_gate0"] = rows_cut(matmul(h1b, dz0, ta=True, out_dtype=BF16, name="d_ple_gate0"))
    dh1 = matmul(dz0, wg0, tb=True, add=dh1p, name="dx_ple_gate0")
    dy0, g_post0_g = rms_bwd(y0, dh1, g_post0, out_dtype=BF16, name="rms_post0_bwd")
    g["ab_w_out"] = rows_cut(matmul(ycat, dy0, ta=True, out_dtype=BF16, name="d_ab_out"))
    dycat = matmul(dy0, w_out, tb=True, out_dtype=BF16, name="dx_ab_out")
    dysc, dr_b, dk_b, dv_b, dgb, g["rwkv_ln_g"], g["rwkv_ln_b"], g_rk = rwkv_post_bwd(ysc, sr, sk, sv, proj0, prm, dycat, nw)
    g["rwkv_r_k"] = g_rk.reshape(wts["rwkv_r_k"].shape)
    grads_scan = rwkv_scan_bwd(sr, sw, sk, sv, skk, ska, states, dysc, bsz, seq, nw)
    (dxr, dxk, dxv, dxwa, dmur, dmuk, dmuv, dmuwa, g["rwkv_w0"], g["rwkv_w2"], g["rwkv_a0"], g["rwkv_a2"],
     g["rwkv_k_k"], g["rwkv_k_a"]) = rwkv_pre_bwd(proj0, prm, grads_scan, (dr_b, dk_b, dv_b), bsz, seq, nw)
    g["rwkv_mu"] = jnp.concatenate([dmur, dmuk, dmuv, dmuwa], axis=1)
    dq, dk, dv, dga, dcrow = fox_attn_bwd(proj0, c, crow, oa, dycat, bsz, seq, nw)
    dc = jnp.pad(dcrow[:, :, :2, :].reshape(bsz, heads, seq).transpose(0, 2, 1), ((0, 0), (0, 0), (0, LANE - heads)))
    dfa, g_fb = fox_gate_bwd(proj0, fbias, dc.reshape(n, LANE), bsz, seq, nw)
    g["fox_f_bias"] = g_fb[:, :heads]
    dproj0 = jnp.concatenate([dq, dk, dv, dga, dxr, dxk, dxv, dgb, dxwa, dfa], axis=1)
    g["ab_w_in"] = jnp.stack(_ab_restore(matmul(xn0, dproj0, ta=True, out_dtype=BF16, name="d_ab_in"), wid, heads, n_shards))
    dxn0 = matmul(dproj0, w_ab, tb=True, out_dtype=BF16, name="dx_ab_in")
    dx, g_pre0_g = rms_bwd(x2, dxn0, g_pre0, res=dh1, name="rms_pre0_bwd")

    g["norm_pre"] = jnp.concatenate([g_pre0_g, g_pre1_g], axis=0)
    g["norm_post"] = jnp.concatenate([g_post0_g, g_post1_g], axis=0)
    return loss, dx.reshape(bsz, seq, d), g


MESH = pl.DeviceIdType.MESH
ANY = pl.BlockSpec(memory_space=pl.ANY)
PACK_COLS = 1024
PACK_ROWS = 16


def _mesh_place():
    xi, yi, ci = lax.axis_index("x"), lax.axis_index("y"), lax.axis_index("c")
    return xi, yi, ci, 4 * xi + 2 * yi + ci


def _peer(xi, yi, ci, m):
    px = 1 - xi if m & 4 else xi
    py = 1 - yi if m & 2 else yi
    pc = 1 - ci if m & 1 else ci
    return (px, py, pc), 4 * px + 2 * py + pc


def _exchange(arrays, scatter, *, name):
    n = len(arrays)

    def body(*refs):
        ins, outs = refs[:n], refs[n:2 * n]
        send_sems, recv_sems, local_sems = refs[2 * n:]
        xi, yi, ci, me = _mesh_place()
        src = lambda a, idx: ins[a].at[idx] if scatter else ins[a]
        started = []
        for a in range(n):
            cp = pltpu.make_async_copy(src(a, me), outs[a].at[me], local_sems.at[a])
            cp.start()
            started.append(cp)
        sends = []
        for m in range(1, N_DEV):
            peer, pidx = _peer(xi, yi, ci, m)
            for a in range(n):
                cp = pltpu.make_async_remote_copy(src_ref=src(a, pidx), dst_ref=outs[a].at[me], send_sem=send_sems.at[m - 1, a],
                                                  recv_sem=recv_sems.at[m - 1, a], device_id=peer, device_id_type=MESH)
                cp.start()
                sends.append(cp)
        for m in range(1, N_DEV):
            peer, pidx = _peer(xi, yi, ci, m)
            for a in range(n):
                pltpu.make_async_remote_copy(src_ref=src(a, pidx), dst_ref=outs[a].at[pidx], send_sem=send_sems.at[m - 1, a],
                                             recv_sem=recv_sems.at[m - 1, a], device_id=peer, device_id_type=MESH).wait_recv()
        for cp in sends:
            cp.wait_send()
        for cp in started:
            cp.wait()

    out_shape = [jax.ShapeDtypeStruct(a.shape if scatter else (N_DEV,) + a.shape, a.dtype) for a in arrays]
    return pl.pallas_call(
        body, name=name, in_specs=[ANY] * n, out_specs=[ANY] * n, out_shape=out_shape,
        scratch_shapes=[pltpu.SemaphoreType.DMA((N_DEV - 1, n)), pltpu.SemaphoreType.DMA((N_DEV - 1, n)),
                        pltpu.SemaphoreType.DMA((n,))],
    )(*arrays)


def sum_blocks(x, *, name):
    _, rows, cols = x.shape
    tr = _tile(rows, 256, PACK_ROWS)

    def body(x_ref, o_ref):
        acc = x_ref[0].astype(F32)
        for s in range(1, N_DEV):
            acc = acc + x_ref[s].astype(F32)
        o_ref[...] = acc

    return pl.pallas_call(
        body, name=name, grid=(rows // tr,), in_specs=[pl.BlockSpec((N_DEV, tr, cols), lambda i: (0, i, 0))],
        out_specs=pl.BlockSpec((tr, cols), lambda i: (i, 0)), out_shape=jax.ShapeDtypeStruct((rows, cols), F32),
        compiler_params=_cparams(("parallel",)),
    )(x)


def adamw(w, parts, m, v, *, name):
    rows, cols = w.shape
    nparts = parts.shape[0]
    tr = _tile(rows, 256, 16)
    c1 = 1.0 / (1.0 - ADAM_B1 ** ADAM_STEP)
    c2 = 1.0 / (1.0 - ADAM_B2 ** ADAM_STEP)

    def body(w_ref, p_ref, m_ref, v_ref, g_ref, d_ref, mo_ref, vo_ref):
        gv = p_ref[0].astype(F32)
        for s in range(1, nparts):
            gv = gv + p_ref[s].astype(F32)
        mn = ADAM_B1 * m_ref[...] + (1.0 - ADAM_B1) * gv
        vn = ADAM_B2 * v_ref[...] + (1.0 - ADAM_B2) * (gv * gv)
        g_ref[...] = gv
        d_ref[...] = -ADAM_LR * ((mn * c1) / (jnp.sqrt(vn * c2) + ADAM_EPS) + ADAM_WD * w_ref[...])
        mo_ref[...] = mn
        vo_ref[...] = vn

    spec = pl.BlockSpec((tr, cols), lambda i: (i, 0))
    return pl.pallas_call(
        body, name=name, grid=(rows // tr,), in_specs=[spec, pl.BlockSpec((nparts, tr, cols), lambda i: (0, i, 0)), spec, spec],
        out_specs=[spec] * 4, out_shape=[jax.ShapeDtypeStruct((rows, cols), F32)] * 4, compiler_params=_cparams(("parallel",)),
    )(w, parts, m, v)


def _pack(arrays, dtype):
    flat = jnp.concatenate([a.astype(dtype).reshape(-1) for a in arrays])
    unit = PACK_COLS * PACK_ROWS
    total = -(-flat.shape[0] // unit) * unit
    return jnp.pad(flat, (0, total - flat.shape[0])).reshape(total // PACK_COLS, PACK_COLS)


def _unpack(flat2d, shapes, lead=()):
    flat = flat2d.reshape(lead + (-1,))
    out, off = [], 0
    for shp in shapes:
        size = 1
        for s in shp:
            size *= s
        out.append(flat[..., off:off + size].reshape(lead + tuple(shp)))
        off += size
    return out


def _join_shards(sh, axis):
    return jnp.concatenate([sh[d] for d in range(N_DEV)], axis=axis)


BIG = (("ab_w_in", "ab_w_in", None, False), ("c_w_in", "c_w_in", None, False), ("ab_w_out", "ab_w_out", None, True),
       ("c_w_out", "c_w_out", None, True), ("ple_w_gate0", "ple_w_gate", 0, True), ("ple_w_gate1", "ple_w_gate", 1, True),
       ("ple_w_proj0", "ple_w_proj", 0, False), ("ple_w_proj1", "ple_w_proj", 1, False))
SMALL_SHARDED = (("rwkv_w2", 1), ("rwkv_a2", 1), ("c_ln_g", 0), ("c_ln_b", 0))
REPLICATED = ("norm_pre", "norm_post", "fox_f_bias", "rwkv_mu", "rwkv_w0", "rwkv_a0", "rwkv_k_k", "rwkv_k_a", "rwkv_r_k",
              "rwkv_ln_g", "rwkv_ln_b", "c_w_s", "c_b_s")
WEIGHTS = ("norm_pre", "norm_post", "ab_w_in", "fox_f_bias", "rwkv_mu", "rwkv_w0", "rwkv_w2", "rwkv_a0", "rwkv_a2", "rwkv_k_k",
           "rwkv_k_a", "rwkv_r_k", "rwkv_ln_g", "rwkv_ln_b", "ab_w_out", "c_w_in", "c_ln_g", "c_ln_b", "c_w_s", "c_b_s",
           "c_w_out", "ple_w_proj", "ple_w_gate")
SQUEEZED = ("norm_pre", "norm_post", "ple_w_proj", "ple_w_gate")


def _step(x, p, loss_target, w, mom, vel):
    sq = {k: (a if k in SQUEEZED else a[0]) for k, a in w.items()}
    _, _, _, me = _mesh_place()

    wid, heads = x.shape[-1] // 2, x.shape[-1] // 2 // HEAD
    layer = lambda a, l: a if l is None else a[l]

    sends = [layer(sq[wname], l).astype(BF16) for _, wname, l, _ in BIG]
    sends += [sq[k] if sq[k].ndim == 2 else sq[k].reshape(1, -1) for k, _ in SMALL_SHARDED]
    gathered = _exchange(sends, False, name="gather_weights")
    full = {}
    for (piece, _, _, by_rows), got in zip(BIG, gathered):
        if piece == "ab_w_in":
            full["ab_w_in_r"] = _ab_reorder([got[d] for d in range(N_DEV)], wid, heads)
        elif by_rows:
            full[piece] = got.reshape((-1,) + got.shape[2:])
        else:
            full[piece] = _join_shards(got, 1)
    for (k, ax), got in zip(SMALL_SHARDED, gathered[len(BIG):]):
        full[k] = _join_shards(got, 1).reshape(-1) if sq[k].ndim == 1 else _join_shards(got, 1)
    for k in REPLICATED:
        full[k] = sq[k]

    loss, grad_x, g = local_step(x, p, loss_target, full, N_DEV)

    landed = _exchange([g[piece] for piece, _, _, _ in BIG], True, name="exchange_grads")
    parts = {piece: a for (piece, _, _, _), a in zip(BIG, landed)}

    small_names = [k for k, _ in SMALL_SHARDED] + list(REPLICATED)
    small_full_shapes = [full[k].shape for k in small_names] + [(1, 1)]
    partial = _exchange([_pack([g[k].reshape(full[k].shape) for k in small_names] + [loss], F32)], False, name="gather_small_grads")[0]
    summed = _unpack(sum_blocks(partial, name="sum_small_grads"), small_full_shapes)
    small_grads = dict(zip(small_names, summed[:-1]))
    for k, ax in SMALL_SHARDED:
        width = sq[k].shape[ax]
        small_grads[k] = lax.dynamic_slice_in_dim(small_grads[k], me * width, width, axis=ax)
    loss_all = summed[-1][0, 0]

    outs_g, outs_d, outs_m, outs_v = [], [], [], []
    for k in WEIGHTS:
        shape = w[k].shape
        two_d = lambda a: a.reshape(-1, a.shape[-1])
        pieces = [(piece, l) for piece, wname, l, _ in BIG if wname == k]
        if not pieces:
            res = adamw(two_d(w[k]), two_d(small_grads[k].reshape(shape))[None], two_d(mom[k]), two_d(vel[k]), name="adamw_" + k)
        elif pieces[0][1] is None:
            res = adamw(two_d(w[k]), parts[k], two_d(mom[k]), two_d(vel[k]), name="adamw_" + k)
        else:
            per_layer = [adamw(w[k][l], parts[piece], mom[k][l], vel[k][l], name="adamw_" + piece) for piece, l in pieces]
            res = [jnp.stack(r) for r in zip(*per_layer)]
        for out, r in zip((outs_g, outs_d, outs_m, outs_v), res):
            out.append(r.reshape(shape))
    return (loss_all, grad_x, *outs_g, *outs_d, *outs_m, *outs_v)


def kernel(x, p, norm_pre, norm_post, ab_w_in, fox_f_bias, rwkv_mu, rwkv_w0, rwkv_w2, rwkv_a0, rwkv_a2, rwkv_k_k, rwkv_k_a, rwkv_r_k, rwkv_ln_g, rwkv_ln_b, ab_w_out, c_w_in, c_ln_g, c_ln_b, c_w_s, c_b_s, c_w_out, ple_w_proj, ple_w_gate, loss_target, m_norm_pre, m_norm_post, m_ab_w_in, m_fox_f_bias, m_rwkv_mu, m_rwkv_w0, m_rwkv_w2, m_rwkv_a0, m_rwkv_a2, m_rwkv_k_k, m_rwkv_k_a, m_rwkv_r_k, m_rwkv_ln_g, m_rwkv_ln_b, m_ab_w_out, m_c_w_in, m_c_ln_g, m_c_ln_b, m_c_w_s, m_c_b_s, m_c_w_out, m_ple_w_proj, m_ple_w_gate, v_norm_pre, v_norm_post, v_ab_w_in, v_fox_f_bias, v_rwkv_mu, v_rwkv_w0, v_rwkv_w2, v_rwkv_a0, v_rwkv_a2, v_rwkv_k_k, v_rwkv_k_a, v_rwkv_r_k, v_rwkv_ln_g, v_rwkv_ln_b, v_ab_w_out, v_c_w_in, v_c_ln_g, v_c_ln_b, v_c_w_s, v_c_b_s, v_c_w_out, v_ple_w_proj, v_ple_w_gate):
    w = dict(norm_pre=norm_pre, norm_post=norm_post, ab_w_in=ab_w_in, fox_f_bias=fox_f_bias, rwkv_mu=rwkv_mu, rwkv_w0=rwkv_w0, rwkv_w2=rwkv_w2, rwkv_a0=rwkv_a0, rwkv_a2=rwkv_a2, rwkv_k_k=rwkv_k_k, rwkv_k_a=rwkv_k_a, rwkv_r_k=rwkv_r_k, rwkv_ln_g=rwkv_ln_g, rwkv_ln_b=rwkv_ln_b, ab_w_out=ab_w_out, c_w_in=c_w_in, c_ln_g=c_ln_g, c_ln_b=c_ln_b, c_w_s=c_w_s, c_b_s=c_b_s, c_w_out=c_w_out, ple_w_proj=ple_w_proj, ple_w_gate=ple_w_gate)
    mom = dict(norm_pre=m_norm_pre, norm_post=m_norm_post, ab_w_in=m_ab_w_in, fox_f_bias=m_fox_f_bias, rwkv_mu=m_rwkv_mu, rwkv_w0=m_rwkv_w0, rwkv_w2=m_rwkv_w2, rwkv_a0=m_rwkv_a0, rwkv_a2=m_rwkv_a2, rwkv_k_k=m_rwkv_k_k, rwkv_k_a=m_rwkv_k_a, rwkv_r_k=m_rwkv_r_k, rwkv_ln_g=m_rwkv_ln_g, rwkv_ln_b=m_rwkv_ln_b, ab_w_out=m_ab_w_out, c_w_in=m_c_w_in, c_ln_g=m_c_ln_g, c_ln_b=m_c_ln_b, c_w_s=m_c_w_s, c_b_s=m_c_b_s, c_w_out=m_c_w_out, ple_w_proj=m_ple_w_proj, ple_w_gate=m_ple_w_gate)
    vel = dict(norm_pre=v_norm_pre, norm_post=v_norm_post, ab_w_in=v_ab_w_in, fox_f_bias=v_fox_f_bias, rwkv_mu=v_rwkv_mu, rwkv_w0=v_rwkv_w0, rwkv_w2=v_rwkv_w2, rwkv_a0=v_rwkv_a0, rwkv_a2=v_rwkv_a2, rwkv_k_k=v_rwkv_k_k, rwkv_k_a=v_rwkv_k_a, rwkv_r_k=v_rwkv_r_k, rwkv_ln_g=v_rwkv_ln_g, rwkv_ln_b=v_rwkv_ln_b, ab_w_out=v_ab_w_out, c_w_in=v_c_w_in, c_ln_g=v_c_ln_g, c_ln_b=v_c_ln_b, c_w_s=v_c_w_s, c_b_s=v_c_b_s, c_w_out=v_c_w_out, ple_w_proj=v_ple_w_proj, ple_w_gate=v_ple_w_gate)
    return _step(x, p, loss_target, w, mom, vel)
```

```python
import functools

import jax
import jax.numpy as jnp
from jax import lax
from jax.experimental import pallas as pl
from jax.experimental.pallas import tpu as pltpu

F32 = jnp.float32
BF16 = jnp.bfloat16
HI = lax.Precision.HIGHEST

HEAD = 64
LANE = 128
LORA = 64
GMLP_CHUNK = 128
RMS_EPS = 1e-6
LN_EPS = 1e-5
GN_EPS = 64e-5
VMEM_LIMIT = 56 * 1024 * 1024
N_DEV = 8

ADAM_LR = 0.001
ADAM_B1 = 0.9
ADAM_B2 = 0.999
ADAM_EPS = 1e-08
ADAM_WD = 0.01
ADAM_STEP = 10

NT_DIMS = (((1,), (1,)), ((), ()))
TN_DIMS = (((0,), (0,)), ((), ()))


def _cparams(sem):
    return pltpu.CompilerParams(dimension_semantics=sem, vmem_limit_bytes=VMEM_LIMIT)


def _tile(n, target, q=LANE):
    if n <= target:
        return n
    best = None
    for d in range(q, target + 1, q):
        if n % d == 0:
            best = d
    assert best is not None, (n, target, q)
    return best


def _sigmoid(x):
    return 1.0 / (1.0 + jnp.exp(-x))


def _silu_and_grad(x):
    s = _sigmoid(x)
    return x * s, s * (1.0 + x * (1.0 - s))


def _gelu_and_grad(x):
    cdf = 0.5 * (1.0 + lax.erf(x * 0.7071067811865476))
    pdf = jnp.exp(-0.5 * x * x) * 0.3989422804014327
    return x * cdf, cdf + x * pdf


def _iota(shape, dim):
    return lax.broadcasted_iota(jnp.int32, shape, dim)


def _head_ones():
    return (_iota((LANE, LANE), 0) // HEAD == _iota((LANE, LANE), 1) // HEAD).astype(F32)


def _head_eye():
    return (_iota((HEAD, LANE), 0) == _iota((HEAD, LANE), 1) % HEAD).astype(F32)


def _hsum(x, ones):
    return jnp.dot(x, ones, precision=HI, preferred_element_type=F32)


def _colsum(x):
    return jnp.sum(x, axis=0, keepdims=True)


def matmul(a, b, *, name, ta=False, tb=False, add=None, out_dtype=F32, tm=1024, tn=1024, tk=2048, col_blocks=1):
    m, k = (a.shape[1], a.shape[0]) if ta else a.shape
    n = b.shape[0] if tb else b.shape[1]
    assert (b.shape[1] if tb else b.shape[0]) == k
    tm, tn, tk = _tile(m, tm), _tile(n // col_blocks, tn), _tile(k, tk)
    nk = k // tk
    per = n // col_blocks // tn

    def body(*refs):
        if add is None:
            a_ref, b_ref, o_ref, acc_ref = refs
        else:
            a_ref, b_ref, add_ref, o_ref, acc_ref = refs
        kk = pl.program_id(2)

        @pl.when(kk == 0)
        def _():
            acc_ref[...] = jnp.zeros_like(acc_ref)

        dims = (((0 if ta else 1,), (1 if tb else 0,)), ((), ()))
        acc_ref[...] += lax.dot_general(a_ref[...].astype(BF16), b_ref[...].astype(BF16), dims,
                                        preferred_element_type=F32)

        @pl.when(kk == nk - 1)
        def _():
            r = acc_ref[...]
            if add is not None:
                r = r + add_ref[...].astype(F32)
            o_ref[...] = r.astype(out_dtype)

    a_spec = pl.BlockSpec((tk, tm), lambda i, j, kk: (kk, i)) if ta else pl.BlockSpec((tm, tk), lambda i, j, kk: (i, kk))
    b_spec = pl.BlockSpec((tn, tk), lambda i, j, kk: (j, kk)) if tb else pl.BlockSpec((tk, tn), lambda i, j, kk: (kk, j))
    o_spec = pl.BlockSpec((tm, tn), lambda i, j, kk: (i, j))
    in_specs = [a_spec, b_spec] + ([o_spec] if add is not None else [])
    args = (a, b) + ((add,) if add is not None else ())
    out_shape = (m, n)
    if col_blocks > 1:
        assert add is None
        o_spec = pl.BlockSpec((None, tm, tn), lambda i, j, kk: (j // per, i, j % per))
        out_shape = (col_blocks, m, n // col_blocks)
    return pl.pallas_call(
        body, name=name, grid=(m // tm, n // tn, nk), in_specs=in_specs, out_specs=o_spec,
        out_shape=jax.ShapeDtypeStruct(out_shape, out_dtype), scratch_shapes=[pltpu.VMEM((tm, tn), F32)],
        compiler_params=_cparams(("parallel", "parallel", "arbitrary")),
    )(*args)


def _row_spec(tm, width, cb=0):
    return pl.BlockSpec((tm, width), lambda i: (i, cb))


def _full_spec(shape):
    return pl.BlockSpec(shape, lambda i: (0,) * len(shape))


def rms_fwd(x, g, *, name):
    n, d = x.shape
    tm = _tile(n, 256, 8)

    def body(x_ref, g_ref, o_ref):
        xv = x_ref[...]
        r = lax.rsqrt(jnp.mean(xv * xv, axis=-1, keepdims=True) + RMS_EPS)
        o_ref[...] = (xv * r * g_ref[...]).astype(BF16)

    return pl.pallas_call(
        body, name=name, grid=(n // tm,), in_specs=[_row_spec(tm, d), _full_spec((1, d))],
        out_specs=_row_spec(tm, d), out_shape=jax.ShapeDtypeStruct((n, d), BF16),
        compiler_params=_cparams(("parallel",)),
    )(x, g)


def rms_res_fwd(y, res, g, *, name):
    n, d = y.shape
    tm = _tile(n, 256, 8)

    def body(y_ref, res_ref, g_ref, h_ref, hb_ref):
        yv = y_ref[...]
        r = lax.rsqrt(jnp.mean(yv * yv, axis=-1, keepdims=True) + RMS_EPS)
        h = res_ref[...] + yv * r * g_ref[...]
        h_ref[...] = h
        hb_ref[...] = h.astype(BF16)

    return pl.pallas_call(
        body, name=name, grid=(n // tm,), in_specs=[_row_spec(tm, d), _row_spec(tm, d), _full_spec((1, d))],
        out_specs=[_row_spec(tm, d), _row_spec(tm, d)],
        out_shape=[jax.ShapeDtypeStruct((n, d), F32), jax.ShapeDtypeStruct((n, d), BF16)],
        compiler_params=_cparams(("parallel",)),
    )(y, res, g)


def rms_bwd(x, dy, g, *, name, res=None, out_dtype=F32):
    n, d = x.shape
    tm = _tile(n, 256, 8)

    def body(*refs):
        if res is None:
            x_ref, dy_ref, g_ref, dx_ref, dg_ref = refs
        else:
            x_ref, dy_ref, g_ref, res_ref, dx_ref, dg_ref = refs

        @pl.when(pl.program_id(0) == 0)
        def _():
            dg_ref[...] = jnp.zeros_like(dg_ref)

        xv = x_ref[...]
        dyv = dy_ref[...].astype(F32)
        r = lax.rsqrt(jnp.mean(xv * xv, axis=-1, keepdims=True) + RMS_EPS)
        xhat = xv * r
        dxh = dyv * g_ref[...]
        dx = r * (dxh - xhat * jnp.mean(dxh * xhat, axis=-1, keepdims=True))
        if res is not None:
            dx = dx + res_ref[...]
        dx_ref[...] = dx.astype(out_dtype)
        dg_ref[...] += _colsum(dyv * xhat)

    in_specs = [_row_spec(tm, d), _row_spec(tm, d), _full_spec((1, d))] + ([_row_spec(tm, d)] if res is not None else [])
    args = (x, dy, g) + ((res,) if res is not None else ())
    return pl.pallas_call(
        body, name=name, grid=(n // tm,), in_specs=in_specs, out_specs=[_row_spec(tm, d), _full_spec((1, d))],
        out_shape=[jax.ShapeDtypeStruct((n, d), out_dtype), jax.ShapeDtypeStruct((1, d), F32)],
        compiler_params=_cparams(("arbitrary",)),
    )(*args)


def ple_fwd(h, a, z, *, name):
    n, d = h.shape
    tm = _tile(n, 256, 8)

    def body(h_ref, a_ref, z_ref, o_ref):
        o_ref[...] = h_ref[...] + a_ref[...] * _sigmoid(z_ref[...])

    return pl.pallas_call(
        body, name=name, grid=(n // tm,), in_specs=[_row_spec(tm, d)] * 3, out_specs=_row_spec(tm, d),
        out_shape=jax.ShapeDtypeStruct((n, d), F32), compiler_params=_cparams(("parallel",)),
    )(h, a, z)


def ple_bwd(dh, a, z, *, name):
    n, d = dh.shape
    tm = _tile(n, 256, 8)

    def body(dh_ref, a_ref, z_ref, da_ref, dz_ref):
        s = _sigmoid(z_ref[...])
        dhv = dh_ref[...]
        da_ref[...] = (dhv * s).astype(BF16)
        dz_ref[...] = (dhv * a_ref[...] * s * (1.0 - s)).astype(BF16)

    return pl.pallas_call(
        body, name=name, grid=(n // tm,), in_specs=[_row_spec(tm, d)] * 3, out_specs=[_row_spec(tm, d)] * 2,
        out_shape=[jax.ShapeDtypeStruct((n, d), BF16)] * 2, compiler_params=_cparams(("parallel",)),
    )(dh, a, z)


def loss_head(h, target, *, name):
    n, d = h.shape
    tm = _tile(n, 256, 8)

    def body(h_ref, t_ref, dh_ref, loss_ref):
        @pl.when(pl.program_id(0) == 0)
        def _():
            loss_ref[...] = jnp.zeros_like(loss_ref)

        e = h_ref[...] - t_ref[...]
        dh_ref[...] = e * (1.0 / d)
        loss_ref[...] += 0.5 * jnp.sum(jnp.sum(e * e, axis=-1, keepdims=True) * (1.0 / d), axis=0, keepdims=True)

    return pl.pallas_call(
        body, name=name, grid=(n // tm,), in_specs=[_row_spec(tm, d)] * 2,
        out_specs=[_row_spec(tm, d), _full_spec((1, 1))],
        out_shape=[jax.ShapeDtypeStruct((n, d), F32), jax.ShapeDtypeStruct((1, 1), F32)],
        compiler_params=_cparams(("arbitrary",)),
    )(h, target)


def _log_sigmoid(z):
    return jnp.minimum(z, 0.0) - jnp.log1p(jnp.exp(-jnp.abs(z)))


def fox_gate_fwd(proj, fbias, bsz, seq, nw):
    cc = _tile(seq, 256, 8)

    def body(f_ref, b_ref, c_ref):
        low = (_iota((cc, cc), 0) >= _iota((cc, cc), 1)).astype(F32)
        carry = jnp.zeros((1, LANE), F32)
        for ci in range(seq // cc):
            rows = slice(ci * cc, (ci + 1) * cc)
            lf = _log_sigmoid(f_ref[rows, :] + b_ref[...])
            c_ref[rows, :] = jnp.dot(low, lf, precision=HI, preferred_element_type=F32) + carry
            carry = carry + _colsum(lf)

    return pl.pallas_call(
        body, name="fox_gate_fwd", grid=(bsz,),
        in_specs=[pl.BlockSpec((seq, LANE), lambda b: (b, 8 * nw + 1)), _full_spec((1, LANE))],
        out_specs=pl.BlockSpec((seq, LANE), lambda b: (b, 0)),
        out_shape=jax.ShapeDtypeStruct((bsz * seq, LANE), F32), compiler_params=_cparams(("parallel",)),
    )(proj, fbias)


def fox_gate_bwd(proj, fbias, dc, bsz, seq, nw):
    cc = _tile(seq, 256, 8)
    nc = seq // cc

    def body(f_ref, b_ref, dc_ref, df_ref, db_ref):
        @pl.when(pl.program_id(0) == 0)
        def _():
            db_ref[...] = jnp.zeros_like(db_ref)

        upp = (_iota((cc, cc), 0) <= _iota((cc, cc), 1)).astype(F32)
        carry = jnp.zeros((1, LANE), F32)
        dbias = jnp.zeros((1, LANE), F32)
        for ci in reversed(range(nc)):
            rows = slice(ci * cc, (ci + 1) * cc)
            blk = dc_ref[rows, :]
            dlf = jnp.dot(upp, blk, precision=HI, preferred_element_type=F32) + carry
            carry = carry + _colsum(blk)
            df = dlf * _sigmoid(-(f_ref[rows, :] + b_ref[...]))
            df_ref[rows, :] = df.astype(BF16)
            dbias = dbias + _colsum(df)
        db_ref[...] += dbias

    return pl.pallas_call(
        body, name="fox_gate_bwd", grid=(bsz,),
        in_specs=[pl.BlockSpec((seq, LANE), lambda b: (b, 8 * nw + 1)), _full_spec((1, LANE)),
                  pl.BlockSpec((seq, LANE), lambda b: (b, 0))],
        out_specs=[pl.BlockSpec((seq, LANE), lambda b: (b, 0)), _full_spec((1, LANE))],
        out_shape=[jax.ShapeDtypeStruct((bsz * seq, LANE), BF16), jax.ShapeDtypeStruct((1, LANE), F32)],
        compiler_params=_cparams(("arbitrary",)),
    )(proj, fbias, dc)


def _attn_scores(qs, kh, ccol, crow, r0, kend, tq):
    s = lax.dot_general(qs[r0:kend], kh[:kend], NT_DIMS, preferred_element_type=F32)
    s = s + ccol[r0:kend] - crow[:, :kend]
    causal = _iota((tq, kend), 1) <= r0 + _iota((tq, kend), 0)
    return jnp.where(causal, s, -jnp.inf)


def _head_column(c_blk, h):
    return jnp.sum(jnp.where(_iota(c_blk.shape, 1) == h, c_blk, 0.0), axis=1, keepdims=True)


def fox_attn_fwd(proj, c, crow, bsz, seq, nw):
    tq = _tile(seq, 256, 8)
    scale = HEAD ** -0.5

    def body(q_ref, k_ref, v_ref, g_ref, c_ref, crow_ref, o_ref, y_ref):
        hp = pl.program_id(1)
        c_blk = c_ref[...]
        for hh in range(2):
            sl = slice(hh * HEAD, (hh + 1) * HEAD)
            ccol = _head_column(c_blk, 2 * hp + hh)
            crow_h = crow_ref[0, 0, hh:hh + 1, :]
            qs = (q_ref[:, sl] * scale).astype(BF16)
            kh = k_ref[:, sl].astype(BF16)
            vh = v_ref[:, sl].astype(BF16)
            for qi in range(seq // tq):
                r0, kend = qi * tq, (qi + 1) * tq
                s = _attn_scores(qs, kh, ccol, crow_h, r0, kend, tq)
                p = jnp.exp(s - jnp.max(s, axis=-1, keepdims=True))
                o = jnp.dot(p.astype(BF16), vh[:kend], preferred_element_type=F32) / jnp.sum(p, axis=-1, keepdims=True)
                o_ref[r0:kend, sl] = o
                y_ref[r0:kend, sl] = (o * _silu_and_grad(g_ref[r0:kend, sl])[0]).astype(BF16)

    blk = lambda cb: pl.BlockSpec((seq, LANE), lambda b, j: (b, cb * nw + j))
    return pl.pallas_call(
        body, name="fox_attn_fwd", grid=(bsz, nw),
        in_specs=[blk(0), blk(1), blk(2), blk(3), pl.BlockSpec((seq, LANE), lambda b, j: (b, 0)),
                  pl.BlockSpec((1, 1, 8, seq), lambda b, j: (b, j, 0, 0))],
        out_specs=[pl.BlockSpec((seq, LANE), lambda b, j: (b, j))] * 2,
        out_shape=[jax.ShapeDtypeStruct((bsz * seq, nw * LANE), F32), jax.ShapeDtypeStruct((bsz * seq, nw * LANE), BF16)],
        compiler_params=_cparams(("parallel", "parallel")),
    )(proj, proj, proj, proj, c, crow)


def fox_attn_bwd(proj, c, crow, oa, dycat, bsz, seq, nw):
    tq = _tile(seq, 256, 8)
    scale = HEAD ** -0.5

    def body(q_ref, k_ref, v_ref, g_ref, c_ref, crow_ref, o_ref, dy_ref,
             dq_ref, dk_ref, dv_ref, dg_ref, dc_ref, dk_acc, dv_acc, dc_acc):
        hp = pl.program_id(1)
        c_blk = c_ref[...]
        dc_ref[...] = jnp.zeros_like(dc_ref)
        for hh in range(2):
            sl = slice(hh * HEAD, (hh + 1) * HEAD)
            ccol = _head_column(c_blk, 2 * hp + hh)
            crow_h = crow_ref[0, 0, hh:hh + 1, :]
            qs = (q_ref[:, sl] * scale).astype(BF16)
            kh = k_ref[:, sl].astype(BF16)
            vh = v_ref[:, sl].astype(BF16)
            oh = o_ref[:, sl]
            dyh = dy_ref[:, sl].astype(F32)
            silu, dsilu = _silu_and_grad(g_ref[:, sl])
            dg_ref[:, sl] = (dyh * oh * dsilu).astype(BF16)
            do = dyh * silu
            dvec = jnp.sum(do * oh, axis=-1, keepdims=True)
            dob = do.astype(BF16)
            dk_acc[...] = jnp.zeros_like(dk_acc)
            dv_acc[...] = jnp.zeros_like(dv_acc)
            dc_acc[...] = jnp.zeros_like(dc_acc)
            for qi in range(seq // tq):
                r0, kend = qi * tq, (qi + 1) * tq
                s = _attn_scores(qs, kh, ccol, crow_h, r0, kend, tq)
                p = jnp.exp(s - jnp.max(s, axis=-1, keepdims=True))
                p = p / jnp.sum(p, axis=-1, keepdims=True)
                dp = lax.dot_general(dob[r0:kend], vh[:kend], NT_DIMS, preferred_element_type=F32)
                ds = p * (dp - dvec[r0:kend])
                dsb = ds.astype(BF16)
                dq_ref[r0:kend, sl] = (jnp.dot(dsb, kh[:kend], preferred_element_type=F32) * scale).astype(BF16)
                dk_acc[0:kend, :] += lax.dot_general(dsb, qs[r0:kend], TN_DIMS, preferred_element_type=F32)
                dv_acc[0:kend, :] += lax.dot_general(p.astype(BF16), dob[r0:kend], TN_DIMS, preferred_element_type=F32)
                dc_acc[:, 0:kend] += -_colsum(ds)
                rowsum = lax.dot_general(jnp.ones((8, kend), F32), ds, NT_DIMS, precision=HI, preferred_element_type=F32)
                dc_acc[:, r0:kend] += rowsum[0:1, :]
            dk_ref[:, sl] = dk_acc[...].astype(BF16)
            dv_ref[:, sl] = dv_acc[...].astype(BF16)
            dc_ref[0, 0, hh:hh + 1, :] = dc_acc[...]

    blk = lambda cb: pl.BlockSpec((seq, LANE), lambda b, j: (b, cb * nw + j))
    own = pl.BlockSpec((seq, LANE), lambda b, j: (b, j))
    n = bsz * seq
    return pl.pallas_call(
        body, name="fox_attn_bwd", grid=(bsz, nw),
        in_specs=[blk(0), blk(1), blk(2), blk(3), pl.BlockSpec((seq, LANE), lambda b, j: (b, 0)),
                  pl.BlockSpec((1, 1, 8, seq), lambda b, j: (b, j, 0, 0)), own, own],
        out_specs=[own] * 4 + [pl.BlockSpec((1, 1, 8, seq), lambda b, j: (b, j, 0, 0))],
        out_shape=[jax.ShapeDtypeStruct((n, nw * LANE), BF16)] * 4 + [jax.ShapeDtypeStruct((bsz, nw, 8, seq), F32)],
        scratch_shapes=[pltpu.VMEM((seq, HEAD), F32), pltpu.VMEM((seq, HEAD), F32), pltpu.VMEM((1, seq), F32)],
        compiler_params=_cparams(("parallel", "parallel")),
    )(proj, proj, proj, proj, c, crow, oa, dycat)


def _shift_mix(x, mu):
    prev = jnp.where(_iota(x.shape, 0) == 0, 0.0, pltpu.roll(x, 1, 0))
    return x + (prev - x) * mu, prev


def _rwkv_pre_math(r_in, k_in, v_in, wa_in, mu_r, mu_k, mu_v, mu_wa, w0, w2, a0, a2, k_k, k_a, ones):
    r, prev_r = _shift_mix(r_in, mu_r)
    k, prev_k = _shift_mix(k_in, mu_k)
    v, prev_v = _shift_mix(v_in, mu_v)
    wa, prev_wa = _shift_mix(wa_in, mu_wa)
    tw = jnp.tanh(wa[:, :LORA])
    alo = wa[:, LORA:]
    sg = _sigmoid(w0 + jnp.dot(tw, w2, precision=HI, preferred_element_type=F32))
    e = sg * 0.6065306597126334
    w = jnp.exp(-e)
    a = _sigmoid(a0 + jnp.dot(alo, a2, precision=HI, preferred_element_type=F32))
    kkraw = k * k_k
    nrm = jnp.sqrt(_hsum(kkraw * kkraw, ones))
    den = jnp.maximum(nrm, 1e-12)
    kk = kkraw / den
    k2 = k * (1.0 + (a - 1.0) * k_a)
    return dict(r=r, k=k, v=v, prev_r=prev_r, prev_k=prev_k, prev_v=prev_v, prev_wa=prev_wa, tw=tw, alo=alo,
                sg=sg, e=e, w=w, a=a, nrm=nrm, den=den, kk=kk, k2=k2)


def _rwkv_pre_specs(seq, nw):
    tok = lambda cb: pl.BlockSpec((seq, LANE), lambda b, j: (b, cb * nw + j))
    par = pl.BlockSpec((1, LANE), lambda b, j: (0, j))
    lora = pl.BlockSpec((LORA, LANE), lambda b, j: (0, j))
    return [tok(4), tok(5), tok(6), pl.BlockSpec((seq, LANE), lambda b, j: (b, 8 * nw)),
            par, par, par, pl.BlockSpec((1, LANE), lambda b, j: (0, 0)), par, lora, par, lora, par, par]


def rwkv_pre_fwd(proj, prm, bsz, seq, nw):
    def body(r_ref, k_ref, v_ref, wa_ref, mur, muk, muv, muwa, w0, w2, a0, a2, kk_, ka_,
             ro, wo, ko, vo, kko, kao):
        m = _rwkv_pre_math(r_ref[...], k_ref[...], v_ref[...], wa_ref[...], mur[...], muk[...], muv[...], muwa[...],
                           w0[...], w2[...], a0[...], a2[...], kk_[...], ka_[...], _head_ones())
        ro[...] = m["r"]
        wo[...] = m["w"]
        ko[...] = m["k2"]
        vo[...] = m["v"]
        kko[...] = m["kk"]
        kao[...] = m["kk"] * m["a"]

    n = bsz * seq
    own = pl.BlockSpec((seq, LANE), lambda b, j: (b, j))
    return pl.pallas_call(
        body, name="rwkv_pre_fwd", grid=(bsz, nw), in_specs=_rwkv_pre_specs(seq, nw), out_specs=[own] * 6,
        out_shape=[jax.ShapeDtypeStruct((n, nw * LANE), F32)] * 6, compiler_params=_cparams(("parallel", "parallel")),
    )(proj, proj, proj, proj, prm["mu_r"], prm["mu_k"], prm["mu_v"], prm["mu_wa"], prm["w0"], prm["w2"],
      prm["a0"], prm["a2"], prm["k_k"], prm["k_a"])


def rwkv_pre_bwd(proj, prm, grads_scan, grads_bonus, bsz, seq, nw):
    wid = nw * LANE

    def body(r_ref, k_ref, v_ref, wa_ref, mur, muk, muv, muwa, w0, w2, a0, a2, kk_, ka_,
             dr_s, dw_s, dk_s, dv_s, dkk_s, dka_s, dr_b, dk_b, dv_b,
             dxr, dxk, dxv, dxwa, dmur, dmuk, dmuv, dmuwa, dw0, dw2, da0, da2, dkk_p, dka_p, dwa_acc):
        b, j = pl.program_id(0), pl.program_id(1)
        cols = pl.ds(pl.multiple_of(j * LANE, LANE), LANE)
        acc_refs = (dmur, dmuk, dmuv, dmuwa, dw0, dw2, da0, da2, dkk_p, dka_p)

        @pl.when(jnp.logical_and(b == 0, j == 0))
        def _():
            for ref in acc_refs:
                ref[...] = jnp.zeros_like(ref)

        @pl.when(j == 0)
        def _():
            dwa_acc[...] = jnp.zeros_like(dwa_acc)

        ones = _head_ones()
        r_in, k_in, v_in, wa_in = r_ref[...], k_ref[...], v_ref[...], wa_ref[...]
        m = _rwkv_pre_math(r_in, k_in, v_in, wa_in, mur[...], muk[...], muv[...], muwa[...],
                           w0[...], w2[...], a0[...], a2[...], kk_[...], ka_[...], ones)
        k, a, kk = m["k"], m["a"], m["kk"]
        dr = dr_s[...] + dr_b[...]
        dk2 = dk_s[...] + dk_b[...]
        dv = dv_s[...] + dv_b[...]
        dka = dka_s[...]
        da = dka * kk + dk2 * k * ka_[...]
        dkk = dkk_s[...] + dka * a
        dkkraw = jnp.where(m["nrm"] > 1e-12, dkk - kk * _hsum(dkk * kk, ones), dkk) / m["den"]
        dk = dkkraw * kk_[...] + dk2 * (1.0 + (a - 1.0) * ka_[...])
        dkk_p[:, cols] += _colsum(dkkraw * k)
        dka_p[:, cols] += _colsum(dk2 * k * (a - 1.0))
        dza = da * a * (1.0 - a)
        da0[:, cols] += _colsum(dza)
        da2[:, cols] += lax.dot_general(m["alo"], dza, TN_DIMS, precision=HI, preferred_element_type=F32)
        dalo = lax.dot_general(dza, a2[...], NT_DIMS, precision=HI, preferred_element_type=F32)
        dzw = -dw_s[...] * m["w"] * m["e"] * (1.0 - m["sg"])
        dw0[:, cols] += _colsum(dzw)
        dw2[:, cols] += lax.dot_general(m["tw"], dzw, TN_DIMS, precision=HI, preferred_element_type=F32)
        dtw = lax.dot_general(dzw, w2[...], NT_DIMS, precision=HI, preferred_element_type=F32)
        dwa_acc[:, 0:LORA] += dtw * (1.0 - m["tw"] * m["tw"])
        dwa_acc[:, LORA:LANE] += dalo

        def shift_bwd(dxs, x, prev, mu):
            g = dxs * mu
            nxt = jnp.where(_iota(g.shape, 0) == seq - 1, 0.0, pltpu.roll(g, seq - 1, 0))
            return dxs * (1.0 - mu) + nxt, _colsum(dxs * (prev - x))

        for dxs, x, prev, mu, out, dmu in ((dr, r_in, m["prev_r"], mur, dxr, dmur), (dk, k_in, m["prev_k"], muk, dxk, dmuk),
                                           (dv, v_in, m["prev_v"], muv, dxv, dmuv)):
            dx, dm = shift_bwd(dxs, x, prev, mu[...])
            out[...] = dx.astype(BF16)
            dmu[:, cols] += dm

        @pl.when(j == nw - 1)
        def _():
            dx, dm = shift_bwd(dwa_acc[...], wa_in, m["prev_wa"], muwa[...])
            dxwa[...] = dx.astype(BF16)
            dmuwa[...] += dm

    n = bsz * seq
    own = pl.BlockSpec((seq, LANE), lambda b, j: (b, j))
    whole = lambda shape: pl.BlockSpec(shape, lambda b, j: (0,) * len(shape))
    row = jax.ShapeDtypeStruct((1, wid), F32)
    return pl.pallas_call(
        body, name="rwkv_pre_bwd", grid=(bsz, nw), in_specs=_rwkv_pre_specs(seq, nw) + [own] * 9,
        out_specs=[own] * 3 + [pl.BlockSpec((seq, LANE), lambda b, j: (b, 0))]
        + [whole((1, wid))] * 3 + [whole((1, LANE)), whole((1, wid)), whole((LORA, wid)), whole((1, wid)),
                                   whole((LORA, wid)), whole((1, wid)), whole((1, wid))],
        out_shape=[jax.ShapeDtypeStruct((n, wid), BF16)] * 3 + [jax.ShapeDtypeStruct((n, LANE), BF16)]
        + [row] * 3 + [jax.ShapeDtypeStruct((1, LANE), F32), row, jax.ShapeDtypeStruct((LORA, wid), F32), row,
                       jax.ShapeDtypeStruct((LORA, wid), F32), row, row],
        scratch_shapes=[pltpu.VMEM((seq, LANE), F32)],
        compiler_params=_cparams(("arbitrary", "arbitrary")),
    )(proj, proj, proj, proj, prm["mu_r"], prm["mu_k"], prm["mu_v"], prm["mu_wa"], prm["w0"], prm["w2"],
      prm["a0"], prm["a2"], prm["k_k"], prm["k_a"], *grads_scan, *grads_bonus)


def _scan_group(nw):
    return _tile(nw, 4, 1)


ROWS = 8


def _scan_consts():
    ones = _head_ones().astype(BF16)
    return _head_eye(), ones, jnp.concatenate([ones, ones], axis=0)


def _cols8(tile, eye, ones, ones2, two_parts):
    hi = tile.astype(BF16).astype(F32)
    place = lambda part, q: (part[q:q + 1, :] * eye).astype(BF16)
    if two_parts:
        lo = tile - hi
        lhs = jnp.concatenate([jnp.concatenate([place(hi, q), place(lo, q)], axis=1) for q in range(ROWS)], axis=0)
        big = jnp.dot(lhs, ones2, preferred_element_type=F32)
    else:
        big = jnp.dot(jnp.concatenate([place(hi, q) for q in range(ROWS)], axis=0), ones, preferred_element_type=F32)
    return [big[q * HEAD:(q + 1) * HEAD, :] for q in range(ROWS)]


def _rows8(mats, eye, ones):
    big = jnp.dot(jnp.concatenate([m.astype(BF16) for m in mats], axis=0), ones, preferred_element_type=F32)
    return _stack_rows([_colsum(big[q * HEAD:(q + 1) * HEAD, :] * eye) for q in range(ROWS)])


def _stack_rows(rows):
    rid = _iota((ROWS, LANE), 0)
    out = jnp.zeros((ROWS, LANE), F32)
    for q in range(ROWS):
        out = jnp.where(rid == q, rows[q], out)
    return out


def rwkv_scan_fwd(r, w, k, v, kk, ka, bsz, seq, nw):
    grp = _scan_group(nw)
    tc = _tile(seq, 32, 8)
    nt = seq // tc

    def body(r_ref, w_ref, k_ref, v_ref, kk_ref, ka_ref, y_ref, st_ref, s_ref):
        @pl.when(pl.program_id(2) == 0)
        def _():
            s_ref[...] = jnp.zeros_like(s_ref)

        eye, ones, ones2 = _scan_consts()

        def step(i8, carry):
            base = pl.multiple_of(i8 * ROWS, ROWS)
            for g in range(grp):
                cs = slice(g * LANE, (g + 1) * LANE)
                cols = lambda ref, two: _cols8(ref[pl.ds(base, ROWS), cs], eye, ones, ones2, two)
                rm, wm, km, kkm, kam = cols(r_ref, False), cols(w_ref, True), cols(k_ref, False), cols(kk_ref, False), cols(ka_ref, False)
                vt = v_ref[pl.ds(base, ROWS), cs]
                s = s_ref[g]
                ys = []
                for q in range(ROWS):
                    st_ref[base + q, g] = s
                    sa = _colsum(s * kkm[q])
                    s = s * wm[q] - kam[q] * sa + km[q] * vt[q:q + 1, :]
                    ys.append(_colsum(s * rm[q]))
                y_ref[pl.ds(base, ROWS), cs] = _stack_rows(ys)
                s_ref[g] = s
            return carry

        lax.fori_loop(0, tc // ROWS, step, 0)

    tok = pl.BlockSpec((tc, grp * LANE), lambda b, g, t: (b * nt + t, g))
    n = bsz * seq
    return pl.pallas_call(
        body, name="rwkv_scan_fwd", grid=(bsz, nw // grp, nt), in_specs=[tok] * 6,
        out_specs=[tok, pl.BlockSpec((tc, grp, HEAD, LANE), lambda b, g, t: (b * nt + t, g, 0, 0))],
        out_shape=[jax.ShapeDtypeStruct((n, nw * LANE), F32), jax.ShapeDtypeStruct((n, nw, HEAD, LANE), F32)],
        scratch_shapes=[pltpu.VMEM((grp, HEAD, LANE), F32)],
        compiler_params=_cparams(("parallel", "parallel", "arbitrary")),
    )(r, w, k, v, kk, ka)


def rwkv_scan_bwd(r, w, k, v, kk, ka, states, dy, bsz, seq, nw):
    grp = _scan_group(nw)
    tc = _tile(seq, 32, 8)
    nt = seq // tc
    nblk = tc // ROWS

    def body(r_ref, w_ref, k_ref, v_ref, kk_ref, ka_ref, st_ref, dy_ref,
             dr_ref, dw_ref, dk_ref, dv_ref, dkk_ref, dka_ref, ds_ref):
        @pl.when(pl.program_id(2) == 0)
        def _():
            ds_ref[...] = jnp.zeros_like(ds_ref)

        eye, ones, ones2 = _scan_consts()

        def step(ii, carry):
            base = pl.multiple_of((nblk - 1 - ii) * ROWS, ROWS)
            rows = pl.ds(base, ROWS)
            for g in range(grp):
                cs = slice(g * LANE, (g + 1) * LANE)
                cols = lambda ref, two: _cols8(ref[rows, cs], eye, ones, ones2, two)
                rm, wm, km, kkm, kam = cols(r_ref, False), cols(w_ref, True), cols(k_ref, False), cols(kk_ref, False), cols(ka_ref, False)
                vt, dyt = v_ref[rows, cs], dy_ref[rows, cs]
                ds = ds_ref[g]
                dvs, p_dr, p_dk, p_dka, p_dw, p_dkk = ([None] * ROWS for _ in range(6))
                for q in reversed(range(ROWS)):
                    sp = st_ref[base + q, g]
                    vrow, dyrow = vt[q:q + 1, :], dyt[q:q + 1, :]
                    sa = _colsum(sp * kkm[q])
                    st = sp * wm[q] - kam[q] * sa + km[q] * vrow
                    ds = ds + rm[q] * dyrow
                    dvs[q] = _colsum(ds * km[q])
                    dsa = -_colsum(ds * kam[q])
                    p_dr[q] = st * dyrow
                    p_dk[q] = ds * vrow
                    p_dka[q] = -(ds * sa)
                    p_dw[q] = ds * sp
                    p_dkk[q] = sp * dsa
                    ds = ds * wm[q] + kkm[q] * dsa
                ds_ref[g] = ds
                dv_ref[rows, cs] = _stack_rows(dvs)
                dr_ref[rows, cs] = _rows8(p_dr, eye, ones)
                dk_ref[rows, cs] = _rows8(p_dk, eye, ones)
                dka_ref[rows, cs] = _rows8(p_dka, eye, ones)
                dw_ref[rows, cs] = _rows8(p_dw, eye, ones)
                dkk_ref[rows, cs] = _rows8(p_dkk, eye, ones)
            return carry

        lax.fori_loop(0, nblk, step, 0)

    tok = pl.BlockSpec((tc, grp * LANE), lambda b, g, t: (b * nt + nt - 1 - t, g))
    n = bsz * seq
    return pl.pallas_call(
        body, name="rwkv_scan_bwd", grid=(bsz, nw // grp, nt),
        in_specs=[tok] * 6 + [pl.BlockSpec((tc, grp, HEAD, LANE), lambda b, g, t: (b * nt + nt - 1 - t, g, 0, 0)), tok],
        out_specs=[tok] * 6, out_shape=[jax.ShapeDtypeStruct((n, nw * LANE), F32)] * 6,
        scratch_shapes=[pltpu.VMEM((grp, HEAD, LANE), F32)],
        compiler_params=_cparams(("parallel", "parallel", "arbitrary")),
    )(r, w, k, v, kk, ka, states, dy)


def _rwkv_post_math(y, r, k2, v, ln_g, ln_b, r_k, ones):
    d = y - _hsum(y, ones) * (1.0 / HEAD)
    rstd = lax.rsqrt(_hsum(d * d, ones) * (1.0 / HEAD) + GN_EPS)
    yn = d * rstd
    s = _hsum(r * k2 * r_k, ones)
    return yn, rstd, s, yn * ln_g + ln_b + s * v


def _rwkv_post_specs(tm, nw):
    tok = pl.BlockSpec((tm, LANE), lambda i, j: (i, j))
    par = pl.BlockSpec((1, LANE), lambda i, j: (0, j))
    return [tok] * 4 + [pl.BlockSpec((tm, LANE), lambda i, j: (i, 7 * nw + j)), par, par, par]


def rwkv_post_fwd(y, r, k2, v, proj, prm, nw):
    n = y.shape[0]
    tm = _tile(n, 512, 8)

    def body(y_ref, r_ref, k_ref, v_ref, g_ref, lg, lb, rk, o_ref):
        ob = _rwkv_post_math(y_ref[...], r_ref[...], k_ref[...], v_ref[...], lg[...], lb[...], rk[...], _head_ones())[3]
        o_ref[...] = (ob * _silu_and_grad(g_ref[...])[0]).astype(BF16)

    return pl.pallas_call(
        body, name="rwkv_post_fwd", grid=(n // tm, nw), in_specs=_rwkv_post_specs(tm, nw),
        out_specs=pl.BlockSpec((tm, LANE), lambda i, j: (i, j)), out_shape=jax.ShapeDtypeStruct((n, nw * LANE), BF16),
        compiler_params=_cparams(("parallel", "parallel")),
    )(y, r, k2, v, proj, prm["ln_g"], prm["ln_b"], prm["r_k"])


def rwkv_post_bwd(y, r, k2, v, proj, prm, dycat, nw):
    n = y.shape[0]
    tm = _tile(n, 512, 8)
    wid = nw * LANE

    def body(y_ref, r_ref, k_ref, v_ref, g_ref, lg, lb, rk, dyc_ref,
             dy_ref, dr_ref, dk_ref, dv_ref, dg_ref, dlg, dlb, drk):
        i, j = pl.program_id(0), pl.program_id(1)
        cols = pl.ds(pl.multiple_of(j * LANE, LANE), LANE)

        @pl.when(jnp.logical_and(i == 0, j == 0))
        def _():
            for ref in (dlg, dlb, drk):
                ref[...] = jnp.zeros_like(ref)

        ones = _head_ones()
        rr, kr, vr = r_ref[...], k_ref[...], v_ref[...]
        yn, rstd, s, ob = _rwkv_post_math(y_ref[...], rr, kr, vr, lg[...], lb[...], rk[...], ones)
        silu, dsilu = _silu_and_grad(g_ref[...])
        dyc = dyc_ref[...].astype(F32)
        dg_ref[...] = (dyc * ob * dsilu).astype(BF16)
        dob = dyc * silu
        dlg[:, cols] += _colsum(dob * yn)
        dlb[:, cols] += _colsum(dob)
        dyn = dob * lg[...]
        dy_ref[...] = rstd * (dyn - _hsum(dyn, ones) * (1.0 / HEAD) - yn * _hsum(dyn * yn, ones) * (1.0 / HEAD))
        dv_ref[...] = dob * s
        dsum = _hsum(dob * vr, ones)
        dr_ref[...] = dsum * kr * rk[...]
        dk_ref[...] = dsum * rr * rk[...]
        drk[:, cols] += _colsum(dsum * rr * kr)

    tok = pl.BlockSpec((tm, LANE), lambda i, j: (i, j))
    whole = pl.BlockSpec((1, wid), lambda i, j: (0, 0))
    return pl.pallas_call(
        body, name="rwkv_post_bwd", grid=(n // tm, nw),
        in_specs=_rwkv_post_specs(tm, nw) + [pl.BlockSpec((tm, LANE), lambda i, j: (i, nw + j))],
        out_specs=[tok] * 5 + [whole] * 3,
        out_shape=[jax.ShapeDtypeStruct((n, wid), F32)] * 4 + [jax.ShapeDtypeStruct((n, wid), BF16)]
        + [jax.ShapeDtypeStruct((1, wid), F32)] * 3,
        compiler_params=_cparams(("arbitrary", "arbitrary")),
    )(y, r, k2, v, proj, prm["ln_g"], prm["ln_b"], prm["r_k"], dycat)


def _sgu_math(blk, ln_g, ln_b, ws_ref, bb_ref, mixed_ref, d, ngr):
    u, v, g = blk[:, 0:d], blk[:, d:2 * d], blk[:, 2 * d:3 * d]
    gu, dgu = _gelu_and_grad(u)
    gv, dgv = _gelu_and_grad(v)
    cen = gv - jnp.mean(gv, axis=-1, keepdims=True)
    rstd = lax.rsqrt(jnp.mean(cen * cen, axis=-1, keepdims=True) + LN_EPS)
    vhat = cen * rstd
    vn = vhat * ln_g + ln_b
    tril = _iota((GMLP_CHUNK, GMLP_CHUNK), 0) >= _iota((GMLP_CHUNK, GMLP_CHUNK), 1)
    for gi in range(ngr):
        cs = slice(gi * LANE, (gi + 1) * LANE)
        wm = jnp.where(tril, ws_ref[gi], 0.0)
        mixed_ref[:, cs] = jnp.dot(wm, vn[:, cs], precision=HI, preferred_element_type=F32) + bb_ref[gi]
    return dict(g=g, gu=gu, dgu=dgu, dgv=dgv, rstd=rstd, vhat=vhat, vn=vn, tril=tril)


def sgu_fwd(proj, ln_g, ln_b, w_s, b_bc):
    n, d3 = proj.shape
    d = d3 // 3
    ngr = d // LANE

    def body(p_ref, lg, lb, ws_ref, bb_ref, o_ref, mixed_ref):
        m = _sgu_math(p_ref[...], lg[...], lb[...], ws_ref, bb_ref, mixed_ref, d, ngr)
        o_ref[...] = (m["gu"] * mixed_ref[...] * _silu_and_grad(m["g"])[0]).astype(BF16)

    return pl.pallas_call(
        body, name="sgu_fwd", grid=(n // GMLP_CHUNK,),
        in_specs=[_row_spec(GMLP_CHUNK, d3), _full_spec((1, d)), _full_spec((1, d)),
                  _full_spec((ngr, GMLP_CHUNK, GMLP_CHUNK)), _full_spec((ngr, GMLP_CHUNK, LANE))],
        out_specs=_row_spec(GMLP_CHUNK, d), out_shape=jax.ShapeDtypeStruct((n, d), BF16),
        scratch_shapes=[pltpu.VMEM((GMLP_CHUNK, d), F32)], compiler_params=_cparams(("parallel",)),
    )(proj, ln_g, ln_b, w_s, b_bc)


def sgu_bwd(proj, ln_g, ln_b, w_s, b_bc, dyin):
    n, d3 = proj.shape
    d = d3 // 3
    ngr = d // LANE
    nsteps = n // GMLP_CHUNK

    def body(p_ref, lg, lb, ws_ref, bb_ref, dy_ref, dp_ref, dws_ref, dbs_ref, dlg_ref, dlb_ref,
             mixed_ref, dvn_ref, dbacc_ref):
        step = pl.program_id(0)

        @pl.when(step == 0)
        def _():
            for ref in (dws_ref, dlg_ref, dlb_ref, dbacc_ref):
                ref[...] = jnp.zeros_like(ref)

        m = _sgu_math(p_ref[...], lg[...], lb[...], ws_ref, bb_ref, mixed_ref, d, ngr)
        silu, dsilu = _silu_and_grad(m["g"])
        dyv = dy_ref[...].astype(F32)
        mixed = mixed_ref[...]
        dp_ref[:, 2 * d:3 * d] = (dyv * m["gu"] * mixed * dsilu).astype(BF16)
        doc = dyv * silu
        dp_ref[:, 0:d] = (doc * mixed * m["dgu"]).astype(BF16)
        dmixed = doc * m["gu"]
        for gi in range(ngr):
            cs = slice(gi * LANE, (gi + 1) * LANE)
            dm = dmixed[:, cs]
            wm = jnp.where(m["tril"], ws_ref[gi], 0.0)
            dws_ref[gi] += jnp.where(m["tril"], lax.dot_general(dm, m["vn"][:, cs], NT_DIMS, precision=HI,
                                                                preferred_element_type=F32), 0.0)
            dbacc_ref[gi] += dm
            dvn_ref[:, cs] = lax.dot_general(wm, dm, TN_DIMS, precision=HI, preferred_element_type=F32)
        dvn = dvn_ref[...]
        dlg_ref[...] += _colsum(dvn * m["vhat"])
        dlb_ref[...] += _colsum(dvn)
        dvh = dvn * lg[...]
        dgv = m["rstd"] * (dvh - jnp.mean(dvh, axis=-1, keepdims=True)
                           - m["vhat"] * jnp.mean(dvh * m["vhat"], axis=-1, keepdims=True))
        dp_ref[:, d:2 * d] = (dgv * m["dgv"]).astype(BF16)

        @pl.when(step == nsteps - 1)
        def _():
            dbs_ref[...] = jnp.sum(dbacc_ref[...], axis=-1, keepdims=True)

    return pl.pallas_call(
        body, name="sgu_bwd", grid=(nsteps,),
        in_specs=[_row_spec(GMLP_CHUNK, d3), _full_spec((1, d)), _full_spec((1, d)),
                  _full_spec((ngr, GMLP_CHUNK, GMLP_CHUNK)), _full_spec((ngr, GMLP_CHUNK, LANE)), _row_spec(GMLP_CHUNK, d)],
        out_specs=[_row_spec(GMLP_CHUNK, d3), _full_spec((ngr, GMLP_CHUNK, GMLP_CHUNK)), _full_spec((ngr, GMLP_CHUNK, 1)),
                   _full_spec((1, d)), _full_spec((1, d))],
        out_shape=[jax.ShapeDtypeStruct((n, d3), BF16), jax.ShapeDtypeStruct((ngr, GMLP_CHUNK, GMLP_CHUNK), F32),
                   jax.ShapeDtypeStruct((ngr, GMLP_CHUNK, 1), F32), jax.ShapeDtypeStruct((1, d), F32),
                   jax.ShapeDtypeStruct((1, d), F32)],
        scratch_shapes=[pltpu.VMEM((GMLP_CHUNK, d), F32), pltpu.VMEM((GMLP_CHUNK, d), F32),
                        pltpu.VMEM((ngr, GMLP_CHUNK, LANE), F32)],
        compiler_params=_cparams(("arbitrary",)),
    )(proj, ln_g, ln_b, w_s, b_bc, dyin)


def _ab_segments(wid, heads):
    return ((0, 3 * wid, 0),
            (3 * wid, 3 * wid + heads, 8 * wid + LANE),
            (3 * wid + heads, 4 * wid + heads, 3 * wid),
            (4 * wid + heads, 7 * wid + heads, 4 * wid),
            (7 * wid + heads, 7 * wid + heads + 2 * LORA, 8 * wid),
            (7 * wid + heads + 2 * LORA, 8 * wid + heads + 2 * LORA, 7 * wid))


def _ab_reorder(shards, wid, heads):
    per = shards[0].shape[1]
    pieces = []
    for o0, o1, _ in sorted(_ab_segments(wid, heads), key=lambda s: s[2]):
        for d, sh in enumerate(shards):
            lo, hi = max(o0, d * per), min(o1, (d + 1) * per)
            if lo < hi:
                pieces.append(sh[:, lo - d * per:hi - d * per])
    pieces.append(jnp.zeros((shards[0].shape[0], LANE - heads), shards[0].dtype))
    return jnp.concatenate(pieces, axis=1)


def _ab_restore(g, wid, heads, n_shards):
    per = (8 * wid + heads + 2 * LORA) // n_shards
    out = []
    for d in range(n_shards):
        pieces = []
        for o0, o1, r0 in _ab_segments(wid, heads):
            lo, hi = max(o0, d * per), min(o1, (d + 1) * per)
            if lo < hi:
                pieces.append(g[:, r0 + lo - o0:r0 + hi - o0])
        out.append(jnp.concatenate(pieces, axis=1))
    return out


def local_step(x, p, target, wts, n_shards):
    bsz, seq, d = x.shape
    n = bsz * seq
    wid = d // 2
    nw = wid // LANE
    heads = wid // HEAD
    x2 = x.reshape(n, d)
    tgt = target.reshape(n, d)
    p0, p1 = p[0].reshape(n, -1), p[1].reshape(n, -1)
    row = lambda a: a.reshape(1, -1)
    g_pre0, g_pre1 = row(wts["norm_pre"][0]), row(wts["norm_pre"][1])
    g_post0, g_post1 = row(wts["norm_post"][0]), row(wts["norm_post"][1])
    w_ab = wts["ab_w_in_r"]
    w_out, w_c, w_cout = wts["ab_w_out"], wts["c_w_in"], wts["c_w_out"]
    wp0, wp1 = wts["ple_w_proj0"], wts["ple_w_proj1"]
    wg0, wg1 = wts["ple_w_gate0"], wts["ple_w_gate1"]
    rows_cut = lambda a: a.reshape((n_shards, a.shape[0] // n_shards) + a.shape[1:])
    fbias = jnp.pad(row(wts["fox_f_bias"]), ((0, 0), (0, LANE - heads)))
    mu = row(wts["rwkv_mu"])
    prm = dict(mu_r=mu[:, 0:wid], mu_k=mu[:, wid:2 * wid], mu_v=mu[:, 2 * wid:3 * wid], mu_wa=mu[:, 3 * wid:],
               w0=row(wts["rwkv_w0"]), w2=wts["rwkv_w2"], a0=row(wts["rwkv_a0"]), a2=wts["rwkv_a2"],
               k_k=row(wts["rwkv_k_k"]), k_a=row(wts["rwkv_k_a"]), r_k=row(wts["rwkv_r_k"]),
               ln_g=row(wts["rwkv_ln_g"]), ln_b=row(wts["rwkv_ln_b"]))
    c_ln_g, c_ln_b = row(wts["c_ln_g"]), row(wts["c_ln_b"])
    w_s = wts["c_w_s"]
    b_bc = jnp.broadcast_to(wts["c_b_s"][:, :, None], w_s.shape[:2] + (LANE,))

    xn0 = rms_fwd(x2, g_pre0, name="rms_pre0")
    proj0 = matmul(xn0, w_ab, name="ab_in")
    c = fox_gate_fwd(proj0, fbias, bsz, seq, nw)
    crow = jnp.pad(c.reshape(bsz, seq, LANE)[:, :, :heads].transpose(0, 2, 1).reshape(bsz, nw, 2, seq),
                   ((0, 0), (0, 0), (0, 6), (0, 0)))
    oa, ya = fox_attn_fwd(proj0, c, crow, bsz, seq, nw)
    sr, sw, sk, sv, skk, ska = rwkv_pre_fwd(proj0, prm, bsz, seq, nw)
    ysc, states = rwkv_scan_fwd(sr, sw, sk, sv, skk, ska, bsz, seq, nw)
    yb = rwkv_post_fwd(ysc, sr, sk, sv, proj0, prm, nw)
    ycat = jnp.concatenate([ya, yb], axis=1)
    y0 = matmul(ycat, w_out, name="ab_out")
    h1, h1b = rms_res_fwd(y0, x2, g_post0, name="rms_post0")
    a0 = matmul(p0, wp0, name="ple_proj0")
    z0 = matmul(h1b, wg0, name="ple_gate0")
    h1p = ple_fwd(h1, a0, z0, name="ple_fwd0")
    xn1 = rms_fwd(h1p, g_pre1, name="rms_pre1")
    proj1 = matmul(xn1, w_c, name="c_in")
    yin = sgu_fwd(proj1, c_ln_g, c_ln_b, w_s, b_bc)
    y1 = matmul(yin, w_cout, name="c_out")
    h2, h2b = rms_res_fwd(y1, h1p, g_post1, name="rms_post1")
    a1 = matmul(p1, wp1, name="ple_proj1")
    z1 = matmul(h2b, wg1, name="ple_gate1")
    h2p = ple_fwd(h2, a1, z1, name="ple_fwd1")
    dh, loss = loss_head(h2p, tgt, name="loss_head")

    g = {}
    da1, dz1 = ple_bwd(dh, a1, z1, name="ple_bwd1")
    g["ple_w_proj1"] = matmul(p1, da1, ta=True, out_dtype=BF16, col_blocks=n_shards, name="d_ple_proj1")
    g["ple_w_gate1"] = rows_cut(matmul(h2b, dz1, ta=True, out_dtype=BF16, name="d_ple_gate1"))
    dh2 = matmul(dz1, wg1, tb=True, add=dh, name="dx_ple_gate1")
    dy1, g_post1_g = rms_bwd(y1, dh2, g_post1, out_dtype=BF16, name="rms_post1_bwd")
    g["c_w_out"] = rows_cut(matmul(yin, dy1, ta=True, out_dtype=BF16, name="d_c_out"))
    dyin = matmul(dy1, w_cout, tb=True, out_dtype=BF16, name="dx_c_out")
    dproj1, g["c_w_s"], g_bs, g["c_ln_g"], g["c_ln_b"] = sgu_bwd(proj1, c_ln_g, c_ln_b, w_s, b_bc, dyin)
    g["c_b_s"] = g_bs.reshape(w_s.shape[:2])
    g["c_w_in"] = matmul(xn1, dproj1, ta=True, out_dtype=BF16, col_blocks=n_shards, name="d_c_in")
    dxn1 = matmul(dproj1, w_c, tb=True, out_dtype=BF16, name="dx_c_in")
    dh1p, g_pre1_g = rms_bwd(h1p, dxn1, g_pre1, res=dh2, name="rms_pre1_bwd")
    da0, dz0 = ple_bwd(dh1p, a0, z0, name="ple_bwd0")
    g["ple_w_proj0"] = matmul(p0, da0, ta=True, out_dtype=BF16, col_blocks=n_shards, name="d_ple_proj0")
    g["ple_w_gate0"] = rows_cut(matmul(h1b, dz0, ta=True, out_dtype=BF16, name="d_ple_gate0"))
    dh1 = matmul(dz0, wg0, tb=True, add=dh1p, name="dx_ple_gate0")
    dy0, g_post0_g = rms_bwd(y0, dh1, g_post0, out_dtype=BF16, name="rms_post0_bwd")
    g["ab_w_out"] = rows_cut(matmul(ycat, dy0, ta=True, out_dtype=BF16, name="d_ab_out"))
    dycat = matmul(dy0, w_out, tb=True, out_dtype=BF16, name="dx_ab_out")
    dysc, dr_b, dk_b, dv_b, dgb, g["rwkv_ln_g"], g["rwkv_ln_b"], g_rk = rwkv_post_bwd(ysc, sr, sk, sv, proj0, prm, dycat, nw)
    g["rwkv_r_k"] = g_rk.reshape(wts["rwkv_r_k"].shape)
    grads_scan = rwkv_scan_bwd(sr, sw, sk, sv, skk, ska, states, dysc, bsz, seq, nw)
    (dxr, dxk, dxv, dxwa, dmur, dmuk, dmuv, dmuwa, g["rwkv_w0"], g["rwkv_w2"], g["rwkv_a0"], g["rwkv_a2"],
     g["rwkv_k_k"], g["rwkv_k_a"]) = rwkv_pre_bwd(proj0, prm, grads_scan, (dr_b, dk_b, dv_b), bsz, seq, nw)
    g["rwkv_mu"] = jnp.concatenate([dmur, dmuk, dmuv, dmuwa], axis=1)
    dq, dk, dv, dga, dcrow = fox_attn_bwd(proj0, c, crow, oa, dycat, bsz, seq, nw)
    dc = jnp.pad(dcrow[:, :, :2, :].reshape(bsz, heads, seq).transpose(0, 2, 1), ((0, 0), (0, 0), (0, LANE - heads)))
    dfa, g_fb = fox_gate_bwd(proj0, fbias, dc.reshape(n, LANE), bsz, seq, nw)
    g["fox_f_bias"] = g_fb[:, :heads]
    dproj0 = jnp.concatenate([dq, dk, dv, dga, dxr, dxk, dxv, dgb, dxwa, dfa], axis=1)
    g["ab_w_in"] = jnp.stack(_ab_restore(matmul(xn0, dproj0, ta=True, out_dtype=BF16, name="d_ab_in"), wid, heads, n_shards))
    dxn0 = matmul(dproj0, w_ab, tb=True, out_dtype=BF16, name="dx_ab_in")
    dx, g_pre0_g = rms_bwd(x2, dxn0, g_pre0, res=dh1, name="rms_pre0_bwd")

    g["norm_pre"] = jnp.concatenate([g_pre0_g, g_pre1_g], axis=0)
    g["norm_post"] = jnp.concatenate([g_post0_g, g_post1_g], axis=0)
    return loss, dx.reshape(bsz, seq, d), g


MESH = pl.DeviceIdType.MESH
ANY = pl.BlockSpec(memory_space=pl.ANY)
PACK_COLS = 1024
PACK_ROWS = 16


def _mesh_place():
    xi, yi, ci = lax.axis_index("x"), lax.axis_index("y"), lax.axis_index("c")
    return xi, yi, ci, 4 * xi + 2 * yi + ci


def _peer(xi, yi, ci, m):
    px = 1 - xi if m & 4 else xi
    py = 1 - yi if m & 2 else yi
    pc = 1 - ci if m & 1 else ci
    return (px, py, pc), 4 * px + 2 * py + pc


def _gather(arrays, *, name):
    n = len(arrays)

    def body(*refs):
        ins, outs = refs[:n], refs[n:2 * n]
        send_sems, recv_sems, local_sems = refs[2 * n:]
        xi, yi, ci, me = _mesh_place()
        sibling = (xi, yi, 1 - ci)
        chips = [(1 - xi, yi), (xi, 1 - yi), (1 - xi, 1 - yi)]
        block = lambda px, py, pc: 4 * px + 2 * py + pc

        def copy(a, k, blk, to, src=None):
            return pltpu.make_async_remote_copy(src_ref=outs[a].at[blk] if src is None else src, dst_ref=outs[a].at[blk],
                                                send_sem=send_sems.at[k, a], recv_sem=recv_sems.at[k, a],
                                                device_id=to, device_id_type=MESH)

        started = []
        for a in range(n):
            cp = pltpu.make_async_copy(ins[a], outs[a].at[me], local_sems.at[a])
            cp.start()
            started.append(cp)
        sends = []
        for a in range(n):
            sends.append(copy(a, 0, me, sibling, src=ins[a]))
            sends += [copy(a, 1 + j, me, (*chip, ci), src=ins[a]) for j, chip in enumerate(chips)]
        for cp in sends:
            cp.start()
        for j, chip in enumerate(chips):
            for a in range(n):
                copy(a, 1 + j, block(*chip, ci), (xi, yi, ci)).wait_recv()
                fwd = copy(a, 4 + j, block(*chip, ci), sibling)
                fwd.start()
                sends.append(fwd)
        for a in range(n):
            copy(a, 0, block(xi, yi, 1 - ci), (xi, yi, ci)).wait_recv()
            for j, chip in enumerate(chips):
                copy(a, 4 + j, block(*chip, 1 - ci), (xi, yi, ci)).wait_recv()
        for cp in sends:
            cp.wait_send()
        for cp in started:
            cp.wait()

    return pl.pallas_call(
        body, name=name, in_specs=[ANY] * n, out_specs=[ANY] * n,
        out_shape=[jax.ShapeDtypeStruct((N_DEV,) + a.shape, a.dtype) for a in arrays],
        scratch_shapes=[pltpu.SemaphoreType.DMA((N_DEV - 1, n)), pltpu.SemaphoreType.DMA((N_DEV - 1, n)),
                        pltpu.SemaphoreType.DMA((n,))],
    )(*arrays)


def _scatter(arrays, *, name):
    n = len(arrays)

    def body(*refs):
        ins, outs = refs[:n], refs[n:2 * n]
        send_sems, recv_sems, local_sems = refs[2 * n:]
        xi, yi, ci, me = _mesh_place()
        src = lambda a, idx: ins[a].at[idx]
        started = []
        for a in range(n):
            cp = pltpu.make_async_copy(src(a, me), outs[a].at[me], local_sems.at[a])
            cp.start()
            started.append(cp)
        sends = []
        for m in range(1, N_DEV):
            peer, pidx = _peer(xi, yi, ci, m)
            for a in range(n):
                cp = pltpu.make_async_remote_copy(src_ref=src(a, pidx), dst_ref=outs[a].at[me], send_sem=send_sems.at[m - 1, a],
                                                  recv_sem=recv_sems.at[m - 1, a], device_id=peer, device_id_type=MESH)
                cp.start()
                sends.append(cp)
        for m in range(1, N_DEV):
            peer, pidx = _peer(xi, yi, ci, m)
            for a in range(n):
                pltpu.make_async_remote_copy(src_ref=src(a, pidx), dst_ref=outs[a].at[pidx], send_sem=send_sems.at[m - 1, a],
                                             recv_sem=recv_sems.at[m - 1, a], device_id=peer, device_id_type=MESH).wait_recv()
        for cp in sends:
            cp.wait_send()
        for cp in started:
            cp.wait()

    return pl.pallas_call(
        body, name=name, in_specs=[ANY] * n, out_specs=[ANY] * n, out_shape=[jax.ShapeDtypeStruct(a.shape, a.dtype) for a in arrays],
        scratch_shapes=[pltpu.SemaphoreType.DMA((N_DEV - 1, n)), pltpu.SemaphoreType.DMA((N_DEV - 1, n)),
                        pltpu.SemaphoreType.DMA((n,))],
    )(*arrays)


def sum_blocks(x, *, name):
    _, rows, cols = x.shape
    tr = _tile(rows, 256, PACK_ROWS)

    def body(x_ref, o_ref):
        acc = x_ref[0].astype(F32)
        for s in range(1, N_DEV):
            acc = acc + x_ref[s].astype(F32)
        o_ref[...] = acc

    return pl.pallas_call(
        body, name=name, grid=(rows // tr,), in_specs=[pl.BlockSpec((N_DEV, tr, cols), lambda i: (0, i, 0))],
        out_specs=pl.BlockSpec((tr, cols), lambda i: (i, 0)), out_shape=jax.ShapeDtypeStruct((rows, cols), F32),
        compiler_params=_cparams(("parallel",)),
    )(x)


def adamw(w, parts, m, v, *, name):
    rows, cols = w.shape
    nparts = parts.shape[0]
    tr = _tile(rows, 256, 16)
    c1 = 1.0 / (1.0 - ADAM_B1 ** ADAM_STEP)
    c2 = 1.0 / (1.0 - ADAM_B2 ** ADAM_STEP)

    def body(w_ref, p_ref, m_ref, v_ref, g_ref, d_ref, mo_ref, vo_ref):
        gv = p_ref[0].astype(F32)
        for s in range(1, nparts):
            gv = gv + p_ref[s].astype(F32)
        mn = ADAM_B1 * m_ref[...] + (1.0 - ADAM_B1) * gv
        vn = ADAM_B2 * v_ref[...] + (1.0 - ADAM_B2) * (gv * gv)
        g_ref[...] = gv
        d_ref[...] = -ADAM_LR * ((mn * c1) / (jnp.sqrt(vn * c2) + ADAM_EPS) + ADAM_WD * w_ref[...])
        mo_ref[...] = mn
        vo_ref[...] = vn

    spec = pl.BlockSpec((tr, cols), lambda i: (i, 0))
    return pl.pallas_call(
        body, name=name, grid=(rows // tr,), in_specs=[spec, pl.BlockSpec((nparts, tr, cols), lambda i: (0, i, 0)), spec, spec],
        out_specs=[spec] * 4, out_shape=[jax.ShapeDtypeStruct((rows, cols), F32)] * 4, compiler_params=_cparams(("parallel",)),
    )(w, parts, m, v)


def _pack(arrays, dtype):
    flat = jnp.concatenate([a.astype(dtype).reshape(-1) for a in arrays])
    unit = PACK_COLS * PACK_ROWS
    total = -(-flat.shape[0] // unit) * unit
    return jnp.pad(flat, (0, total - flat.shape[0])).reshape(total // PACK_COLS, PACK_COLS)


def _unpack(flat2d, shapes, lead=()):
    flat = flat2d.reshape(lead + (-1,))
    out, off = [], 0
    for shp in shapes:
        size = 1
        for s in shp:
            size *= s
        out.append(flat[..., off:off + size].reshape(lead + tuple(shp)))
        off += size
    return out


def _join_shards(sh, axis):
    return jnp.concatenate([sh[d] for d in range(N_DEV)], axis=axis)


BIG = (("ab_w_in", "ab_w_in", None, False), ("c_w_in", "c_w_in", None, False), ("ab_w_out", "ab_w_out", None, True),
       ("c_w_out", "c_w_out", None, True), ("ple_w_gate0", "ple_w_gate", 0, True), ("ple_w_gate1", "ple_w_gate", 1, True),
       ("ple_w_proj0", "ple_w_proj", 0, False), ("ple_w_proj1", "ple_w_proj", 1, False))
SMALL_SHARDED = (("rwkv_w2", 1), ("rwkv_a2", 1), ("c_ln_g", 0), ("c_ln_b", 0))
REPLICATED = ("norm_pre", "norm_post", "fox_f_bias", "rwkv_mu", "rwkv_w0", "rwkv_a0", "rwkv_k_k", "rwkv_k_a", "rwkv_r_k",
              "rwkv_ln_g", "rwkv_ln_b", "c_w_s", "c_b_s")
WEIGHTS = ("norm_pre", "norm_post", "ab_w_in", "fox_f_bias", "rwkv_mu", "rwkv_w0", "rwkv_w2", "rwkv_a0", "rwkv_a2", "rwkv_k_k",
           "rwkv_k_a", "rwkv_r_k", "rwkv_ln_g", "rwkv_ln_b", "ab_w_out", "c_w_in", "c_ln_g", "c_ln_b", "c_w_s", "c_b_s",
           "c_w_out", "ple_w_proj", "ple_w_gate")
SQUEEZED = ("norm_pre", "norm_post", "ple_w_proj", "ple_w_gate")


def _step(x, p, loss_target, w, mom, vel):
    sq = {k: (a if k in SQUEEZED else a[0]) for k, a in w.items()}
    _, _, _, me = _mesh_place()

    wid, heads = x.shape[-1] // 2, x.shape[-1] // 2 // HEAD
    layer = lambda a, l: a if l is None else a[l]

    sends = [layer(sq[wname], l).astype(BF16) for _, wname, l, _ in BIG]
    sends += [sq[k] if sq[k].ndim == 2 else sq[k].reshape(1, -1) for k, _ in SMALL_SHARDED]
    gathered = _gather(sends, name="gather_weights")
    full = {}
    for (piece, _, _, by_rows), got in zip(BIG, gathered):
        if piece == "ab_w_in":
            full["ab_w_in_r"] = _ab_reorder([got[d] for d in range(N_DEV)], wid, heads)
        elif by_rows:
            full[piece] = got.reshape((-1,) + got.shape[2:])
        else:
            full[piece] = _join_shards(got, 1)
    for (k, ax), got in zip(SMALL_SHARDED, gathered[len(BIG):]):
        full[k] = _join_shards(got, 1).reshape(-1) if sq[k].ndim == 1 else _join_shards(got, 1)
    for k in REPLICATED:
        full[k] = sq[k]

    loss, grad_x, g = local_step(x, p, loss_target, full, N_DEV)

    landed = _scatter([g[piece] for piece, _, _, _ in BIG], name="exchange_grads")
    parts = {piece: a for (piece, _, _, _), a in zip(BIG, landed)}

    small_names = [k for k, _ in SMALL_SHARDED] + list(REPLICATED)
    small_full_shapes = [full[k].shape for k in small_names] + [(1, 1)]
    partial = _gather([_pack([g[k].reshape(full[k].shape) for k in small_names] + [loss], F32)], name="gather_small_grads")[0]
    summed = _unpack(sum_blocks(partial, name="sum_small_grads"), small_full_shapes)
    small_grads = dict(zip(small_names, summed[:-1]))
    for k, ax in SMALL_SHARDED:
        width = sq[k].shape[ax]
        small_grads[k] = lax.dynamic_slice_in_dim(small_grads[k], me * width, width, axis=ax)
    loss_all = summed[-1][0, 0]

    outs_g, outs_d, outs_m, outs_v = [], [], [], []
    for k in WEIGHTS:
        shape = w[k].shape
        two_d = lambda a: a.reshape(-1, a.shape[-1])
        pieces = [(piece, l) for piece, wname, l, _ in BIG if wname == k]
        if not pieces:
            res = adamw(two_d(w[k]), two_d(small_grads[k].reshape(shape))[None], two_d(mom[k]), two_d(vel[k]), name="adamw_" + k)
        elif pieces[0][1] is None:
            res = adamw(two_d(w[k]), parts[k], two_d(mom[k]), two_d(vel[k]), name="adamw_" + k)
        else:
            per_layer = [adamw(w[k][l], parts[piece], mom[k][l], vel[k][l], name="adamw_" + piece) for piece, l in pieces]
            res = [jnp.stack(r) for r in zip(*per_layer)]
        for out, r in zip((outs_g, outs_d, outs_m, outs_v), res):
            out.append(r.reshape(shape))
    return (loss_all, grad_x, *outs_g, *outs_d, *outs_m, *outs_v)


def kernel(x, p, norm_pre, norm_post, ab_w_in, fox_f_bias, rwkv_mu, rwkv_w0, rwkv_w2, rwkv_a0, rwkv_a2, rwkv_k_k, rwkv_k_a, rwkv_r_k, rwkv_ln_g, rwkv_ln_b, ab_w_out, c_w_in, c_ln_g, c_ln_b, c_w_s, c_b_s, c_w_out, ple_w_proj, ple_w_gate, loss_target, m_norm_pre, m_norm_post, m_ab_w_in, m_fox_f_bias, m_rwkv_mu, m_rwkv_w0, m_rwkv_w2, m_rwkv_a0, m_rwkv_a2, m_rwkv_k_k, m_rwkv_k_a, m_rwkv_r_k, m_rwkv_ln_g, m_rwkv_ln_b, m_ab_w_out, m_c_w_in, m_c_ln_g, m_c_ln_b, m_c_w_s, m_c_b_s, m_c_w_out, m_ple_w_proj, m_ple_w_gate, v_norm_pre, v_norm_post, v_ab_w_in, v_fox_f_bias, v_rwkv_mu, v_rwkv_w0, v_rwkv_w2, v_rwkv_a0, v_rwkv_a2, v_rwkv_k_k, v_rwkv_k_a, v_rwkv_r_k, v_rwkv_ln_g, v_rwkv_ln_b, v_ab_w_out, v_c_w_in, v_c_ln_g, v_c_ln_b, v_c_w_s, v_c_b_s, v_c_w_out, v_ple_w_proj, v_ple_w_gate):
    w = dict(norm_pre=norm_pre, norm_post=norm_post, ab_w_in=ab_w_in, fox_f_bias=fox_f_bias, rwkv_mu=rwkv_mu, rwkv_w0=rwkv_w0, rwkv_w2=rwkv_w2, rwkv_a0=rwkv_a0, rwkv_a2=rwkv_a2, rwkv_k_k=rwkv_k_k, rwkv_k_a=rwkv_k_a, rwkv_r_k=rwkv_r_k, rwkv_ln_g=rwkv_ln_g, rwkv_ln_b=rwkv_ln_b, ab_w_out=ab_w_out, c_w_in=c_w_in, c_ln_g=c_ln_g, c_ln_b=c_ln_b, c_w_s=c_w_s, c_b_s=c_b_s, c_w_out=c_w_out, ple_w_proj=ple_w_proj, ple_w_gate=ple_w_gate)
    mom = dict(norm_pre=m_norm_pre, norm_post=m_norm_post, ab_w_in=m_ab_w_in, fox_f_bias=m_fox_f_bias, rwkv_mu=m_rwkv_mu, rwkv_w0=m_rwkv_w0, rwkv_w2=m_rwkv_w2, rwkv_a0=m_rwkv_a0, rwkv_a2=m_rwkv_a2, rwkv_k_k=m_rwkv_k_k, rwkv_k_a=m_rwkv_k_a, rwkv_r_k=m_rwkv_r_k, rwkv_ln_g=m_rwkv_ln_g, rwkv_ln_b=m_rwkv_ln_b, ab_w_out=m_ab_w_out, c_w_in=m_c_w_in, c_ln_g=m_c_ln_g, c_ln_b=m_c_ln_b, c_w_s=m_c_w_s, c_b_s=m_c_b_s, c_w_out=m_c_w_out, ple_w_proj=m_ple_w_proj, ple_w_gate=m_ple_w_gate)
    vel = dict(norm_pre=v_norm_pre, norm_post=v_norm_post, ab_w_in=v_ab_w_in, fox_f_bias=v_fox_f_bias, rwkv_mu=v_rwkv_mu, rwkv_w0=v_rwkv_w0, rwkv_w2=v_rwkv_w2, rwkv_a0=v_rwkv_a0, rwkv_a2=v_rwkv_a2, rwkv_k_k=v_rwkv_k_k, rwkv_k_a=v_rwkv_k_a, rwkv_r_k=v_rwkv_r_k, rwkv_ln_g=v_rwkv_ln_g, rwkv_ln_b=v_rwkv_ln_b, ab_w_out=v_ab_w_out, c_w_in=v_c_w_in, c_ln_g=v_c_ln_g, c_ln_b=v_c_ln_b, c_w_s=v_c_w_s, c_b_s=v_c_b_s, c_w_out=v_c_w_out, ple_w_proj=v_ple_w_proj, ple_w_gate=v_ple_w_gate)
    return _step(x, p, loss_target, w, mom, vel)
```

```python
import functools

import jax
import jax.numpy as jnp
from jax import lax
from jax.experimental import pallas as pl
from jax.experimental.pallas import tpu as pltpu

F32 = jnp.float32
BF16 = jnp.bfloat16
HI = lax.Precision.HIGHEST

HEAD = 64
LANE = 128
LORA = 64
GMLP_CHUNK = 128
RMS_EPS = 1e-6
LN_EPS = 1e-5
GN_EPS = 64e-5
VMEM_LIMIT = 56 * 1024 * 1024
N_DEV = 8

ADAM_LR = 0.001
ADAM_B1 = 0.9
ADAM_B2 = 0.999
ADAM_EPS = 1e-08
ADAM_WD = 0.01
ADAM_STEP = 10

NT_DIMS = (((1,), (1,)), ((), ()))
TN_DIMS = (((0,), (0,)), ((), ()))


def _cparams(sem):
    return pltpu.CompilerParams(dimension_semantics=sem, vmem_limit_bytes=VMEM_LIMIT)


def _tile(n, target, q=LANE):
    if n <= target:
        return n
    best = None
    for d in range(q, target + 1, q):
        if n % d == 0:
            best = d
    assert best is not None, (n, target, q)
    return best


def _sigmoid(x):
    return 1.0 / (1.0 + jnp.exp(-x))


def _silu_and_grad(x):
    s = _sigmoid(x)
    return x * s, s * (1.0 + x * (1.0 - s))


def _gelu_and_grad(x):
    cdf = 0.5 * (1.0 + lax.erf(x * 0.7071067811865476))
    pdf = jnp.exp(-0.5 * x * x) * 0.3989422804014327
    return x * cdf, cdf + x * pdf


def _iota(shape, dim):
    return lax.broadcasted_iota(jnp.int32, shape, dim)


def _head_ones():
    return (_iota((LANE, LANE), 0) // HEAD == _iota((LANE, LANE), 1) // HEAD).astype(F32)


def _head_eye():
    return (_iota((HEAD, LANE), 0) == _iota((HEAD, LANE), 1) % HEAD).astype(F32)


def _hsum(x, ones):
    return jnp.dot(x, ones, precision=HI, preferred_element_type=F32)


def _colsum(x):
    return jnp.sum(x, axis=0, keepdims=True)


def matmul(a, b, *, name, ta=False, tb=False, add=None, out_dtype=F32, tm=1024, tn=1024, tk=2048, col_blocks=1):
    m, k = (a.shape[1], a.shape[0]) if ta else a.shape
    n = b.shape[0] if tb else b.shape[1]
    assert (b.shape[1] if tb else b.shape[0]) == k
    tm, tn, tk = _tile(m, tm), _tile(n // col_blocks, tn), _tile(k, tk)
    nk = k // tk
    per = n // col_blocks // tn

    def body(*refs):
        if add is None:
            a_ref, b_ref, o_ref, acc_ref = refs
        else:
            a_ref, b_ref, add_ref, o_ref, acc_ref = refs
        kk = pl.program_id(2)

        @pl.when(kk == 0)
        def _():
            acc_ref[...] = jnp.zeros_like(acc_ref)

        dims = (((0 if ta else 1,), (1 if tb else 0,)), ((), ()))
        acc_ref[...] += lax.dot_general(a_ref[...].astype(BF16), b_ref[...].astype(BF16), dims,
                                        preferred_element_type=F32)

        @pl.when(kk == nk - 1)
        def _():
            r = acc_ref[...]
            if add is not None:
                r = r + add_ref[...].astype(F32)
            o_ref[...] = r.astype(out_dtype)

    a_spec = pl.BlockSpec((tk, tm), lambda i, j, kk: (kk, i)) if ta else pl.BlockSpec((tm, tk), lambda i, j, kk: (i, kk))
    b_spec = pl.BlockSpec((tn, tk), lambda i, j, kk: (j, kk)) if tb else pl.BlockSpec((tk, tn), lambda i, j, kk: (kk, j))
    o_spec = pl.BlockSpec((tm, tn), lambda i, j, kk: (i, j))
    in_specs = [a_spec, b_spec] + ([o_spec] if add is not None else [])
    args = (a, b) + ((add,) if add is not None else ())
    out_shape = (m, n)
    if col_blocks > 1:
        assert add is None
        o_spec = pl.BlockSpec((None, tm, tn), lambda i, j, kk: (j // per, i, j % per))
        out_shape = (col_blocks, m, n // col_blocks)
    return pl.pallas_call(
        body, name=name, grid=(m // tm, n // tn, nk), in_specs=in_specs, out_specs=o_spec,
        out_shape=jax.ShapeDtypeStruct(out_shape, out_dtype), scratch_shapes=[pltpu.VMEM((tm, tn), F32)],
        compiler_params=_cparams(("parallel", "parallel", "arbitrary")),
    )(*args)


def _row_spec(tm, width, cb=0):
    return pl.BlockSpec((tm, width), lambda i: (i, cb))


def _full_spec(shape):
    return pl.BlockSpec(shape, lambda i: (0,) * len(shape))


def rms_fwd(x, g, *, name):
    n, d = x.shape
    tm = _tile(n, 256, 8)

    def body(x_ref, g_ref, o_ref):
        xv = x_ref[...]
        r = lax.rsqrt(jnp.mean(xv * xv, axis=-1, keepdims=True) + RMS_EPS)
        o_ref[...] = (xv * r * g_ref[...]).astype(BF16)

    return pl.pallas_call(
        body, name=name, grid=(n // tm,), in_specs=[_row_spec(tm, d), _full_spec((1, d))],
        out_specs=_row_spec(tm, d), out_shape=jax.ShapeDtypeStruct((n, d), BF16),
        compiler_params=_cparams(("parallel",)),
    )(x, g)


def rms_res_fwd(y, res, g, *, name):
    n, d = y.shape
    tm = _tile(n, 256, 8)

    def body(y_ref, res_ref, g_ref, h_ref, hb_ref):
        yv = y_ref[...]
        r = lax.rsqrt(jnp.mean(yv * yv, axis=-1, keepdims=True) + RMS_EPS)
        h = res_ref[...] + yv * r * g_ref[...]
        h_ref[...] = h
        hb_ref[...] = h.astype(BF16)

    return pl.pallas_call(
        body, name=name, grid=(n // tm,), in_specs=[_row_spec(tm, d), _row_spec(tm, d), _full_spec((1, d))],
        out_specs=[_row_spec(tm, d), _row_spec(tm, d)],
        out_shape=[jax.ShapeDtypeStruct((n, d), F32), jax.ShapeDtypeStruct((n, d), BF16)],
        compiler_params=_cparams(("parallel",)),
    )(y, res, g)


def rms_bwd(x, dy, g, *, name, res=None, out_dtype=F32):
    n, d = x.shape
    tm = _tile(n, 256, 8)

    def body(*refs):
        if res is None:
            x_ref, dy_ref, g_ref, dx_ref, dg_ref = refs
        else:
            x_ref, dy_ref, g_ref, res_ref, dx_ref, dg_ref = refs

        @pl.when(pl.program_id(0) == 0)
        def _():
            dg_ref[...] = jnp.zeros_like(dg_ref)

        xv = x_ref[...]
        dyv = dy_ref[...].astype(F32)
        r = lax.rsqrt(jnp.mean(xv * xv, axis=-1, keepdims=True) + RMS_EPS)
        xhat = xv * r
        dxh = dyv * g_ref[...]
        dx = r * (dxh - xhat * jnp.mean(dxh * xhat, axis=-1, keepdims=True))
        if res is not None:
            dx = dx + res_ref[...]
        dx_ref[...] = dx.astype(out_dtype)
        dg_ref[...] += _colsum(dyv * xhat)

    in_specs = [_row_spec(tm, d), _row_spec(tm, d), _full_spec((1, d))] + ([_row_spec(tm, d)] if res is not None else [])
    args = (x, dy, g) + ((res,) if res is not None else ())
    return pl.pallas_call(
        body, name=name, grid=(n // tm,), in_specs=in_specs, out_specs=[_row_spec(tm, d), _full_spec((1, d))],
        out_shape=[jax.ShapeDtypeStruct((n, d), out_dtype), jax.ShapeDtypeStruct((1, d), F32)],
        compiler_params=_cparams(("arbitrary",)),
    )(*args)


def ple_fwd(h, a, z, *, name):
    n, d = h.shape
    tm = _tile(n, 256, 8)

    def body(h_ref, a_ref, z_ref, o_ref):
        o_ref[...] = h_ref[...] + a_ref[...] * _sigmoid(z_ref[...])

    return pl.pallas_call(
        body, name=name, grid=(n // tm,), in_specs=[_row_spec(tm, d)] * 3, out_specs=_row_spec(tm, d),
        out_shape=jax.ShapeDtypeStruct((n, d), F32), compiler_params=_cparams(("parallel",)),
    )(h, a, z)


def ple_bwd(dh, a, z, *, name):
    n, d = dh.shape
    tm = _tile(n, 256, 8)

    def body(dh_ref, a_ref, z_ref, da_ref, dz_ref):
        s = _sigmoid(z_ref[...])
        dhv = dh_ref[...]
        da_ref[...] = (dhv * s).astype(BF16)
        dz_ref[...] = (dhv * a_ref[...] * s * (1.0 - s)).astype(BF16)

    return pl.pallas_call(
        body, name=name, grid=(n // tm,), in_specs=[_row_spec(tm, d)] * 3, out_specs=[_row_spec(tm, d)] * 2,
        out_shape=[jax.ShapeDtypeStruct((n, d), BF16)] * 2, compiler_params=_cparams(("parallel",)),
    )(dh, a, z)


def loss_head(h, target, *, name):
    n, d = h.shape
    tm = _tile(n, 256, 8)

    def body(h_ref, t_ref, dh_ref, loss_ref):
        @pl.when(pl.program_id(0) == 0)
        def _():
            loss_ref[...] = jnp.zeros_like(loss_ref)

        e = h_ref[...] - t_ref[...]
        dh_ref[...] = e * (1.0 / d)
        loss_ref[...] += 0.5 * jnp.sum(jnp.sum(e * e, axis=-1, keepdims=True) * (1.0 / d), axis=0, keepdims=True)

    return pl.pallas_call(
        body, name=name, grid=(n // tm,), in_specs=[_row_spec(tm, d)] * 2,
        out_specs=[_row_spec(tm, d), _full_spec((1, 1))],
        out_shape=[jax.ShapeDtypeStruct((n, d), F32), jax.ShapeDtypeStruct((1, 1), F32)],
        compiler_params=_cparams(("arbitrary",)),
    )(h, target)


def _log_sigmoid(z):
    return jnp.minimum(z, 0.0) - jnp.log1p(jnp.exp(-jnp.abs(z)))


def fox_gate_fwd(proj, fbias, bsz, seq, nw):
    cc = _tile(seq, 256, 8)

    def body(f_ref, b_ref, c_ref):
        low = (_iota((cc, cc), 0) >= _iota((cc, cc), 1)).astype(F32)
        carry = jnp.zeros((1, LANE), F32)
        for ci in range(seq // cc):
            rows = slice(ci * cc, (ci + 1) * cc)
            lf = _log_sigmoid(f_ref[rows, :] + b_ref[...])
            c_ref[rows, :] = jnp.dot(low, lf, precision=HI, preferred_element_type=F32) + carry
            carry = carry + _colsum(lf)

    return pl.pallas_call(
        body, name="fox_gate_fwd", grid=(bsz,),
        in_specs=[pl.BlockSpec((seq, LANE), lambda b: (b, 8 * nw + 1)), _full_spec((1, LANE))],
        out_specs=pl.BlockSpec((seq, LANE), lambda b: (b, 0)),
        out_shape=jax.ShapeDtypeStruct((bsz * seq, LANE), F32), compiler_params=_cparams(("parallel",)),
    )(proj, fbias)


def fox_gate_bwd(proj, fbias, dc, bsz, seq, nw):
    cc = _tile(seq, 256, 8)
    nc = seq // cc

    def body(f_ref, b_ref, dc_ref, df_ref, db_ref):
        @pl.when(pl.program_id(0) == 0)
        def _():
            db_ref[...] = jnp.zeros_like(db_ref)

        upp = (_iota((cc, cc), 0) <= _iota((cc, cc), 1)).astype(F32)
        carry = jnp.zeros((1, LANE), F32)
        dbias = jnp.zeros((1, LANE), F32)
        for ci in reversed(range(nc)):
            rows = slice(ci * cc, (ci + 1) * cc)
            blk = dc_ref[rows, :]
            dlf = jnp.dot(upp, blk, precision=HI, preferred_element_type=F32) + carry
            carry = carry + _colsum(blk)
            df = dlf * _sigmoid(-(f_ref[rows, :] + b_ref[...]))
            df_ref[rows, :] = df.astype(BF16)
            dbias = dbias + _colsum(df)
        db_ref[...] += dbias

    return pl.pallas_call(
        body, name="fox_gate_bwd", grid=(bsz,),
        in_specs=[pl.BlockSpec((seq, LANE), lambda b: (b, 8 * nw + 1)), _full_spec((1, LANE)),
                  pl.BlockSpec((seq, LANE), lambda b: (b, 0))],
        out_specs=[pl.BlockSpec((seq, LANE), lambda b: (b, 0)), _full_spec((1, LANE))],
        out_shape=[jax.ShapeDtypeStruct((bsz * seq, LANE), BF16), jax.ShapeDtypeStruct((1, LANE), F32)],
        compiler_params=_cparams(("arbitrary",)),
    )(proj, fbias, dc)


def _attn_scores(qs, kh, ccol, crow, r0, kend, tq):
    s = lax.dot_general(qs[r0:kend], kh[:kend], NT_DIMS, preferred_element_type=F32)
    s = s + ccol[r0:kend] - crow[:, :kend]
    causal = _iota((tq, kend), 1) <= r0 + _iota((tq, kend), 0)
    return jnp.where(causal, s, -jnp.inf)


def _head_column(c_blk, h):
    return jnp.sum(jnp.where(_iota(c_blk.shape, 1) == h, c_blk, 0.0), axis=1, keepdims=True)


def fox_attn_fwd(proj, c, crow, bsz, seq, nw):
    tq = _tile(seq, 256, 8)
    scale = HEAD ** -0.5

    def body(q_ref, k_ref, v_ref, g_ref, c_ref, crow_ref, o_ref, y_ref):
        hp = pl.program_id(1)
        c_blk = c_ref[...]
        for hh in range(2):
            sl = slice(hh * HEAD, (hh + 1) * HEAD)
            ccol = _head_column(c_blk, 2 * hp + hh)
            crow_h = crow_ref[0, 0, hh:hh + 1, :]
            qs = (q_ref[:, sl] * scale).astype(BF16)
            kh = k_ref[:, sl].astype(BF16)
            vh = v_ref[:, sl].astype(BF16)
            for qi in range(seq // tq):
                r0, kend = qi * tq, (qi + 1) * tq
                s = _attn_scores(qs, kh, ccol, crow_h, r0, kend, tq)
                p = jnp.exp(s - jnp.max(s, axis=-1, keepdims=True))
                o = jnp.dot(p.astype(BF16), vh[:kend], preferred_element_type=F32) / jnp.sum(p, axis=-1, keepdims=True)
                o_ref[r0:kend, sl] = o
                y_ref[r0:kend, sl] = (o * _silu_and_grad(g_ref[r0:kend, sl])[0]).astype(BF16)

    blk = lambda cb: pl.BlockSpec((seq, LANE), lambda b, j: (b, cb * nw + j))
    return pl.pallas_call(
        body, name="fox_attn_fwd", grid=(bsz, nw),
        in_specs=[blk(0), blk(1), blk(2), blk(3), pl.BlockSpec((seq, LANE), lambda b, j: (b, 0)),
                  pl.BlockSpec((1, 1, 8, seq), lambda b, j: (b, j, 0, 0))],
        out_specs=[pl.BlockSpec((seq, LANE), lambda b, j: (b, j))] * 2,
        out_shape=[jax.ShapeDtypeStruct((bsz * seq, nw * LANE), F32), jax.ShapeDtypeStruct((bsz * seq, nw * LANE), BF16)],
        compiler_params=_cparams(("parallel", "parallel")),
    )(proj, proj, proj, proj, c, crow)


def fox_attn_bwd(proj, c, crow, oa, dycat, bsz, seq, nw):
    tq = _tile(seq, 256, 8)
    scale = HEAD ** -0.5

    def body(q_ref, k_ref, v_ref, g_ref, c_ref, crow_ref, o_ref, dy_ref,
             dq_ref, dk_ref, dv_ref, dg_ref, dc_ref, dk_acc, dv_acc, dc_acc):
        hp = pl.program_id(1)
        c_blk = c_ref[...]
        dc_ref[...] = jnp.zeros_like(dc_ref)
        for hh in range(2):
            sl = slice(hh * HEAD, (hh + 1) * HEAD)
            ccol = _head_column(c_blk, 2 * hp + hh)
            crow_h = crow_ref[0, 0, hh:hh + 1, :]
            qs = (q_ref[:, sl] * scale).astype(BF16)
            kh = k_ref[:, sl].astype(BF16)
            vh = v_ref[:, sl].astype(BF16)
            oh = o_ref[:, sl]
            dyh = dy_ref[:, sl].astype(F32)
            silu, dsilu = _silu_and_grad(g_ref[:, sl])
            dg_ref[:, sl] = (dyh * oh * dsilu).astype(BF16)
            do = dyh * silu
            dvec = jnp.sum(do * oh, axis=-1, keepdims=True)
            dob = do.astype(BF16)
            dk_acc[...] = jnp.zeros_like(dk_acc)
            dv_acc[...] = jnp.zeros_like(dv_acc)
            dc_acc[...] = jnp.zeros_like(dc_acc)
            for qi in range(seq // tq):
                r0, kend = qi * tq, (qi + 1) * tq
                s = _attn_scores(qs, kh, ccol, crow_h, r0, kend, tq)
                p = jnp.exp(s - jnp.max(s, axis=-1, keepdims=True))
                p = p / jnp.sum(p, axis=-1, keepdims=True)
                dp = lax.dot_general(dob[r0:kend], vh[:kend], NT_DIMS, preferred_element_type=F32)
                ds = p * (dp - dvec[r0:kend])
                dsb = ds.astype(BF16)
                dq_ref[r0:kend, sl] = (jnp.dot(dsb, kh[:kend], preferred_element_type=F32) * scale).astype(BF16)
                dk_acc[0:kend, :] += lax.dot_general(dsb, qs[r0:kend], TN_DIMS, preferred_element_type=F32)
                dv_acc[0:kend, :] += lax.dot_general(p.astype(BF16), dob[r0:kend], TN_DIMS, preferred_element_type=F32)
                dc_acc[:, 0:kend] += -_colsum(ds)
                rowsum = lax.dot_general(jnp.ones((8, kend), F32), ds, NT_DIMS, precision=HI, preferred_element_type=F32)
                dc_acc[:, r0:kend] += rowsum[0:1, :]
            dk_ref[:, sl] = dk_acc[...].astype(BF16)
            dv_ref[:, sl] = dv_acc[...].astype(BF16)
            dc_ref[0, 0, hh:hh + 1, :] = dc_acc[...]

    blk = lambda cb: pl.BlockSpec((seq, LANE), lambda b, j: (b, cb * nw + j))
    own = pl.BlockSpec((seq, LANE), lambda b, j: (b, j))
    n = bsz * seq
    return pl.pallas_call(
        body, name="fox_attn_bwd", grid=(bsz, nw),
        in_specs=[blk(0), blk(1), blk(2), blk(3), pl.BlockSpec((seq, LANE), lambda b, j: (b, 0)),
                  pl.BlockSpec((1, 1, 8, seq), lambda b, j: (b, j, 0, 0)), own, own],
        out_specs=[own] * 4 + [pl.BlockSpec((1, 1, 8, seq), lambda b, j: (b, j, 0, 0))],
        out_shape=[jax.ShapeDtypeStruct((n, nw * LANE), BF16)] * 4 + [jax.ShapeDtypeStruct((bsz, nw, 8, seq), F32)],
        scratch_shapes=[pltpu.VMEM((seq, HEAD), F32), pltpu.VMEM((seq, HEAD), F32), pltpu.VMEM((1, seq), F32)],
        compiler_params=_cparams(("parallel", "parallel")),
    )(proj, proj, proj, proj, c, crow, oa, dycat)


def _shift_mix(x, mu):
    prev = jnp.where(_iota(x.shape, 0) == 0, 0.0, pltpu.roll(x, 1, 0))
    return x + (prev - x) * mu, prev


def _rwkv_pre_math(r_in, k_in, v_in, wa_in, mu_r, mu_k, mu_v, mu_wa, w0, w2, a0, a2, k_k, k_a, ones):
    r, prev_r = _shift_mix(r_in, mu_r)
    k, prev_k = _shift_mix(k_in, mu_k)
    v, prev_v = _shift_mix(v_in, mu_v)
    wa, prev_wa = _shift_mix(wa_in, mu_wa)
    tw = jnp.tanh(wa[:, :LORA])
    alo = wa[:, LORA:]
    sg = _sigmoid(w0 + jnp.dot(tw, w2, precision=HI, preferred_element_type=F32))
    e = sg * 0.6065306597126334
    w = jnp.exp(-e)
    a = _sigmoid(a0 + jnp.dot(alo, a2, precision=HI, preferred_element_type=F32))
    kkraw = k * k_k
    nrm = jnp.sqrt(_hsum(kkraw * kkraw, ones))
    den = jnp.maximum(nrm, 1e-12)
    kk = kkraw / den
    k2 = k * (1.0 + (a - 1.0) * k_a)
    return dict(r=r, k=k, v=v, prev_r=prev_r, prev_k=prev_k, prev_v=prev_v, prev_wa=prev_wa, tw=tw, alo=alo,
                sg=sg, e=e, w=w, a=a, nrm=nrm, den=den, kk=kk, k2=k2)


def _rwkv_pre_specs(seq, nw):
    tok = lambda cb: pl.BlockSpec((seq, LANE), lambda b, j: (b, cb * nw + j))
    par = pl.BlockSpec((1, LANE), lambda b, j: (0, j))
    lora = pl.BlockSpec((LORA, LANE), lambda b, j: (0, j))
    return [tok(4), tok(5), tok(6), pl.BlockSpec((seq, LANE), lambda b, j: (b, 8 * nw)),
            par, par, par, pl.BlockSpec((1, LANE), lambda b, j: (0, 0)), par, lora, par, lora, par, par]


def rwkv_pre_fwd(proj, prm, bsz, seq, nw):
    def body(r_ref, k_ref, v_ref, wa_ref, mur, muk, muv, muwa, w0, w2, a0, a2, kk_, ka_,
             ro, wo, ko, vo, kko, kao):
        m = _rwkv_pre_math(r_ref[...], k_ref[...], v_ref[...], wa_ref[...], mur[...], muk[...], muv[...], muwa[...],
                           w0[...], w2[...], a0[...], a2[...], kk_[...], ka_[...], _head_ones())
        ro[...] = m["r"]
        wo[...] = m["w"]
        ko[...] = m["k2"]
        vo[...] = m["v"]
        kko[...] = m["kk"]
        kao[...] = m["kk"] * m["a"]

    n = bsz * seq
    own = pl.BlockSpec((seq, LANE), lambda b, j: (b, j))
    return pl.pallas_call(
        body, name="rwkv_pre_fwd", grid=(bsz, nw), in_specs=_rwkv_pre_specs(seq, nw), out_specs=[own] * 6,
        out_shape=[jax.ShapeDtypeStruct((n, nw * LANE), F32)] * 6, compiler_params=_cparams(("parallel", "parallel")),
    )(proj, proj, proj, proj, prm["mu_r"], prm["mu_k"], prm["mu_v"], prm["mu_wa"], prm["w0"], prm["w2"],
      prm["a0"], prm["a2"], prm["k_k"], prm["k_a"])


def rwkv_pre_bwd(proj, prm, grads_scan, grads_bonus, bsz, seq, nw):
    wid = nw * LANE

    def body(r_ref, k_ref, v_ref, wa_ref, mur, muk, muv, muwa, w0, w2, a0, a2, kk_, ka_,
             dr_s, dw_s, dk_s, dv_s, dkk_s, dka_s, dr_b, dk_b, dv_b,
             dxr, dxk, dxv, dxwa, dmur, dmuk, dmuv, dmuwa, dw0, dw2, da0, da2, dkk_p, dka_p, dwa_acc):
        b, j = pl.program_id(0), pl.program_id(1)
        cols = pl.ds(pl.multiple_of(j * LANE, LANE), LANE)
        acc_refs = (dmur, dmuk, dmuv, dmuwa, dw0, dw2, da0, da2, dkk_p, dka_p)

        @pl.when(jnp.logical_and(b == 0, j == 0))
        def _():
            for ref in acc_refs:
                ref[...] = jnp.zeros_like(ref)

        @pl.when(j == 0)
        def _():
            dwa_acc[...] = jnp.zeros_like(dwa_acc)

        ones = _head_ones()
        r_in, k_in, v_in, wa_in = r_ref[...], k_ref[...], v_ref[...], wa_ref[...]
        m = _rwkv_pre_math(r_in, k_in, v_in, wa_in, mur[...], muk[...], muv[...], muwa[...],
                           w0[...], w2[...], a0[...], a2[...], kk_[...], ka_[...], ones)
        k, a, kk = m["k"], m["a"], m["kk"]
        dr = dr_s[...] + dr_b[...]
        dk2 = dk_s[...] + dk_b[...]
        dv = dv_s[...] + dv_b[...]
        dka = dka_s[...]
        da = dka * kk + dk2 * k * ka_[...]
        dkk = dkk_s[...] + dka * a
        dkkraw = jnp.where(m["nrm"] > 1e-12, dkk - kk * _hsum(dkk * kk, ones), dkk) / m["den"]
        dk = dkkraw * kk_[...] + dk2 * (1.0 + (a - 1.0) * ka_[...])
        dkk_p[:, cols] += _colsum(dkkraw * k)
        dka_p[:, cols] += _colsum(dk2 * k * (a - 1.0))
        dza = da * a * (1.0 - a)
        da0[:, cols] += _colsum(dza)
        da2[:, cols] += lax.dot_general(m["alo"], dza, TN_DIMS, precision=HI, preferred_element_type=F32)
        dalo = lax.dot_general(dza, a2[...], NT_DIMS, precision=HI, preferred_element_type=F32)
        dzw = -dw_s[...] * m["w"] * m["e"] * (1.0 - m["sg"])
        dw0[:, cols] += _colsum(dzw)
        dw2[:, cols] += lax.dot_general(m["tw"], dzw, TN_DIMS, precision=HI, preferred_element_type=F32)
        dtw = lax.dot_general(dzw, w2[...], NT_DIMS, precision=HI, preferred_element_type=F32)
        dwa_acc[:, 0:LORA] += dtw * (1.0 - m["tw"] * m["tw"])
        dwa_acc[:, LORA:LANE] += dalo

        def shift_bwd(dxs, x, prev, mu):
            g = dxs * mu
            nxt = jnp.where(_iota(g.shape, 0) == seq - 1, 0.0, pltpu.roll(g, seq - 1, 0))
            return dxs * (1.0 - mu) + nxt, _colsum(dxs * (prev - x))

        for dxs, x, prev, mu, out, dmu in ((dr, r_in, m["prev_r"], mur, dxr, dmur), (dk, k_in, m["prev_k"], muk, dxk, dmuk),
                                           (dv, v_in, m["prev_v"], muv, dxv, dmuv)):
            dx, dm = shift_bwd(dxs, x, prev, mu[...])
            out[...] = dx.astype(BF16)
            dmu[:, cols] += dm

        @pl.when(j == nw - 1)
        def _():
            dx, dm = shift_bwd(dwa_acc[...], wa_in, m["prev_wa"], muwa[...])
            dxwa[...] = dx.astype(BF16)
            dmuwa[...] += dm

    n = bsz * seq
    own = pl.BlockSpec((seq, LANE), lambda b, j: (b, j))
    whole = lambda shape: pl.BlockSpec(shape, lambda b, j: (0,) * len(shape))
    row = jax.ShapeDtypeStruct((1, wid), F32)
    return pl.pallas_call(
        body, name="rwkv_pre_bwd", grid=(bsz, nw), in_specs=_rwkv_pre_specs(seq, nw) + [own] * 9,
        out_specs=[own] * 3 + [pl.BlockSpec((seq, LANE), lambda b, j: (b, 0))]
        + [whole((1, wid))] * 3 + [whole((1, LANE)), whole((1, wid)), whole((LORA, wid)), whole((1, wid)),
                                   whole((LORA, wid)), whole((1, wid)), whole((1, wid))],
        out_shape=[jax.ShapeDtypeStruct((n, wid), BF16)] * 3 + [jax.ShapeDtypeStruct((n, LANE), BF16)]
        + [row] * 3 + [jax.ShapeDtypeStruct((1, LANE), F32), row, jax.ShapeDtypeStruct((LORA, wid), F32), row,
                       jax.ShapeDtypeStruct((LORA, wid), F32), row, row],
        scratch_shapes=[pltpu.VMEM((seq, LANE), F32)],
        compiler_params=_cparams(("arbitrary", "arbitrary")),
    )(proj, proj, proj, proj, prm["mu_r"], prm["mu_k"], prm["mu_v"], prm["mu_wa"], prm["w0"], prm["w2"],
      prm["a0"], prm["a2"], prm["k_k"], prm["k_a"], *grads_scan, *grads_bonus)


def _scan_group(nw):
    return _tile(nw, 4, 1)


ROWS = 8


def _scan_consts():
    ones = _head_ones().astype(BF16)
    return _head_eye(), ones, jnp.concatenate([ones, ones], axis=0)


def _cols8(tile, eye, ones, ones2, two_parts):
    hi = tile.astype(BF16).astype(F32)
    place = lambda part, q: (part[q:q + 1, :] * eye).astype(BF16)
    if two_parts:
        lo = tile - hi
        lhs = jnp.concatenate([jnp.concatenate([place(hi, q), place(lo, q)], axis=1) for q in range(ROWS)], axis=0)
        big = jnp.dot(lhs, ones2, preferred_element_type=F32)
    else:
        big = jnp.dot(jnp.concatenate([place(hi, q) for q in range(ROWS)], axis=0), ones, preferred_element_type=F32)
    return [big[q * HEAD:(q + 1) * HEAD, :] for q in range(ROWS)]


def _rows8(mats, eye, ones):
    big = jnp.dot(jnp.concatenate([m.astype(BF16) for m in mats], axis=0), ones, preferred_element_type=F32)
    return _stack_rows([_colsum(big[q * HEAD:(q + 1) * HEAD, :] * eye) for q in range(ROWS)])


def _stack_rows(rows):
    rid = _iota((ROWS, LANE), 0)
    out = jnp.zeros((ROWS, LANE), F32)
    for q in range(ROWS):
        out = jnp.where(rid == q, rows[q], out)
    return out


def _grid_ends(grid):
    ids = [pl.program_id(i) for i in range(len(grid))]
    first = functools.reduce(jnp.logical_and, [i == 0 for i in ids])
    last = functools.reduce(jnp.logical_and, [i == g - 1 for i, g in zip(ids, grid)])
    return first, last


def rwkv_scan_fwd(r, w, k, v, kk, ka, bsz, seq, nw, gather):
    grp = _scan_group(nw)
    tc = _tile(seq, 32, 8)
    nt = seq // tc
    grid = (bsz, nw // grp, nt)
    ng = len(gather)

    def body(*refs):
        r_ref, w_ref, k_ref, v_ref, kk_ref, ka_ref = refs[:6]
        y_ref, st_ref = refs[6 + ng:8 + ng]
        s_ref = refs[8 + 2 * ng]
        start, wait = _direct_exchange(refs[6:6 + ng], refs[8 + ng:8 + 2 * ng], *refs[9 + 2 * ng:], scatter=False)
        first, last = _grid_ends(grid)
        pl.when(first)(start)

        @pl.when(pl.program_id(2) == 0)
        def _():
            s_ref[...] = jnp.zeros_like(s_ref)

        eye, ones, ones2 = _scan_consts()

        def step(i8, carry):
            base = pl.multiple_of(i8 * ROWS, ROWS)
            for g in range(grp):
                cs = slice(g * LANE, (g + 1) * LANE)
                cols = lambda ref, two: _cols8(ref[pl.ds(base, ROWS), cs], eye, ones, ones2, two)
                rm, wm, km, kkm, kam = cols(r_ref, False), cols(w_ref, True), cols(k_ref, False), cols(kk_ref, False), cols(ka_ref, False)
                vt = v_ref[pl.ds(base, ROWS), cs]
                s = s_ref[g]
                ys = []
                for q in range(ROWS):
                    st_ref[base + q, g] = s
                    sa = _colsum(s * kkm[q])
                    s = s * wm[q] - kam[q] * sa + km[q] * vt[q:q + 1, :]
                    ys.append(_colsum(s * rm[q]))
                y_ref[pl.ds(base, ROWS), cs] = _stack_rows(ys)
                s_ref[g] = s
            return carry

        lax.fori_loop(0, tc // ROWS, step, 0)
        pl.when(last)(wait)

    tok = pl.BlockSpec((tc, grp * LANE), lambda b, g, t: (b * nt + t, g))
    n = bsz * seq
    out = pl.pallas_call(
        body, name="rwkv_scan_fwd", grid=grid, in_specs=[tok] * 6 + [ANY] * ng,
        out_specs=[tok, pl.BlockSpec((tc, grp, HEAD, LANE), lambda b, g, t: (b * nt + t, g, 0, 0))] + [ANY] * ng,
        out_shape=[jax.ShapeDtypeStruct((n, nw * LANE), F32), jax.ShapeDtypeStruct((n, nw, HEAD, LANE), F32)]
        + [jax.ShapeDtypeStruct((N_DEV,) + a.shape, a.dtype) for a in gather],
        scratch_shapes=[pltpu.VMEM((grp, HEAD, LANE), F32)] + _exchange_sems(ng),
        compiler_params=_cparams(("arbitrary", "arbitrary", "arbitrary")),
    )(r, w, k, v, kk, ka, *gather)
    return out[0], out[1], out[2:]


def rwkv_scan_bwd(r, w, k, v, kk, ka, states, dy, bsz, seq, nw, scatter):
    grp = _scan_group(nw)
    tc = _tile(seq, 32, 8)
    nt = seq // tc
    nblk = tc // ROWS
    grid = (bsz, nw // grp, nt)
    ns = len(scatter)

    def body(*refs):
        r_ref, w_ref, k_ref, v_ref, kk_ref, ka_ref, st_ref, dy_ref = refs[:8]
        dr_ref, dw_ref, dk_ref, dv_ref, dkk_ref, dka_ref = refs[8 + ns:14 + ns]
        ds_ref = refs[14 + 2 * ns]
        start, wait = _direct_exchange(refs[8:8 + ns], refs[14 + ns:14 + 2 * ns], *refs[15 + 2 * ns:], scatter=True)
        first, last = _grid_ends(grid)
        pl.when(first)(start)

        @pl.when(pl.program_id(2) == 0)
        def _():
            ds_ref[...] = jnp.zeros_like(ds_ref)

        eye, ones, ones2 = _scan_consts()

        def step(ii, carry):
            base = pl.multiple_of((nblk - 1 - ii) * ROWS, ROWS)
            rows = pl.ds(base, ROWS)
            for g in range(grp):
                cs = slice(g * LANE, (g + 1) * LANE)
                cols = lambda ref, two: _cols8(ref[rows, cs], eye, ones, ones2, two)
                rm, wm, km, kkm, kam = cols(r_ref, False), cols(w_ref, True), cols(k_ref, False), cols(kk_ref, False), cols(ka_ref, False)
                vt, dyt = v_ref[rows, cs], dy_ref[rows, cs]
                ds = ds_ref[g]
                dvs, p_dr, p_dk, p_dka, p_dw, p_dkk = ([None] * ROWS for _ in range(6))
                for q in reversed(range(ROWS)):
                    sp = st_ref[base + q, g]
                    vrow, dyrow = vt[q:q + 1, :], dyt[q:q + 1, :]
                    sa = _colsum(sp * kkm[q])
                    st = sp * wm[q] - kam[q] * sa + km[q] * vrow
                    ds = ds + rm[q] * dyrow
                    dvs[q] = _colsum(ds * km[q])
                    dsa = -_colsum(ds * kam[q])
                    p_dr[q] = st * dyrow
                    p_dk[q] = ds * vrow
                    p_dka[q] = -(ds * sa)
                    p_dw[q] = ds * sp
                    p_dkk[q] = sp * dsa
                    ds = ds * wm[q] + kkm[q] * dsa
                ds_ref[g] = ds
                dv_ref[rows, cs] = _stack_rows(dvs)
                dr_ref[rows, cs] = _rows8(p_dr, eye, ones)
                dk_ref[rows, cs] = _rows8(p_dk, eye, ones)
                dka_ref[rows, cs] = _rows8(p_dka, eye, ones)
                dw_ref[rows, cs] = _rows8(p_dw, eye, ones)
                dkk_ref[rows, cs] = _rows8(p_dkk, eye, ones)
            return carry

        lax.fori_loop(0, nblk, step, 0)
        pl.when(last)(wait)

    tok = pl.BlockSpec((tc, grp * LANE), lambda b, g, t: (b * nt + nt - 1 - t, g))
    n = bsz * seq
    out = pl.pallas_call(
        body, name="rwkv_scan_bwd", grid=grid,
        in_specs=[tok] * 6 + [pl.BlockSpec((tc, grp, HEAD, LANE), lambda b, g, t: (b * nt + nt - 1 - t, g, 0, 0)), tok] + [ANY] * ns,
        out_specs=[tok] * 6 + [ANY] * ns,
        out_shape=[jax.ShapeDtypeStruct((n, nw * LANE), F32)] * 6 + [jax.ShapeDtypeStruct(a.shape, a.dtype) for a in scatter],
        scratch_shapes=[pltpu.VMEM((grp, HEAD, LANE), F32)] + _exchange_sems(ns),
        compiler_params=_cparams(("arbitrary", "arbitrary", "arbitrary")),
    )(r, w, k, v, kk, ka, states, dy, *scatter)
    return out[:6], out[6:]


def _rwkv_post_math(y, r, k2, v, ln_g, ln_b, r_k, ones):
    d = y - _hsum(y, ones) * (1.0 / HEAD)
    rstd = lax.rsqrt(_hsum(d * d, ones) * (1.0 / HEAD) + GN_EPS)
    yn = d * rstd
    s = _hsum(r * k2 * r_k, ones)
    return yn, rstd, s, yn * ln_g + ln_b + s * v


def _rwkv_post_specs(tm, nw):
    tok = pl.BlockSpec((tm, LANE), lambda i, j: (i, j))
    par = pl.BlockSpec((1, LANE), lambda i, j: (0, j))
    return [tok] * 4 + [pl.BlockSpec((tm, LANE), lambda i, j: (i, 7 * nw + j)), par, par, par]


def rwkv_post_fwd(y, r, k2, v, proj, prm, nw):
    n = y.shape[0]
    tm = _tile(n, 512, 8)

    def body(y_ref, r_ref, k_ref, v_ref, g_ref, lg, lb, rk, o_ref):
        ob = _rwkv_post_math(y_ref[...], r_ref[...], k_ref[...], v_ref[...], lg[...], lb[...], rk[...], _head_ones())[3]
        o_ref[...] = (ob * _silu_and_grad(g_ref[...])[0]).astype(BF16)

    return pl.pallas_call(
        body, name="rwkv_post_fwd", grid=(n // tm, nw), in_specs=_rwkv_post_specs(tm, nw),
        out_specs=pl.BlockSpec((tm, LANE), lambda i, j: (i, j)), out_shape=jax.ShapeDtypeStruct((n, nw * LANE), BF16),
        compiler_params=_cparams(("parallel", "parallel")),
    )(y, r, k2, v, proj, prm["ln_g"], prm["ln_b"], prm["r_k"])


def rwkv_post_bwd(y, r, k2, v, proj, prm, dycat, nw):
    n = y.shape[0]
    tm = _tile(n, 512, 8)
    wid = nw * LANE

    def body(y_ref, r_ref, k_ref, v_ref, g_ref, lg, lb, rk, dyc_ref,
             dy_ref, dr_ref, dk_ref, dv_ref, dg_ref, dlg, dlb, drk):
        i, j = pl.program_id(0), pl.program_id(1)
        cols = pl.ds(pl.multiple_of(j * LANE, LANE), LANE)

        @pl.when(jnp.logical_and(i == 0, j == 0))
        def _():
            for ref in (dlg, dlb, drk):
                ref[...] = jnp.zeros_like(ref)

        ones = _head_ones()
        rr, kr, vr = r_ref[...], k_ref[...], v_ref[...]
        yn, rstd, s, ob = _rwkv_post_math(y_ref[...], rr, kr, vr, lg[...], lb[...], rk[...], ones)
        silu, dsilu = _silu_and_grad(g_ref[...])
        dyc = dyc_ref[...].astype(F32)
        dg_ref[...] = (dyc * ob * dsilu).astype(BF16)
        dob = dyc * silu
        dlg[:, cols] += _colsum(dob * yn)
        dlb[:, cols] += _colsum(dob)
        dyn = dob * lg[...]
        dy_ref[...] = rstd * (dyn - _hsum(dyn, ones) * (1.0 / HEAD) - yn * _hsum(dyn * yn, ones) * (1.0 / HEAD))
        dv_ref[...] = dob * s
        dsum = _hsum(dob * vr, ones)
        dr_ref[...] = dsum * kr * rk[...]
        dk_ref[...] = dsum * rr * rk[...]
        drk[:, cols] += _colsum(dsum * rr * kr)

    tok = pl.BlockSpec((tm, LANE), lambda i, j: (i, j))
    whole = pl.BlockSpec((1, wid), lambda i, j: (0, 0))
    return pl.pallas_call(
        body, name="rwkv_post_bwd", grid=(n // tm, nw),
        in_specs=_rwkv_post_specs(tm, nw) + [pl.BlockSpec((tm, LANE), lambda i, j: (i, nw + j))],
        out_specs=[tok] * 5 + [whole] * 3,
        out_shape=[jax.ShapeDtypeStruct((n, wid), F32)] * 4 + [jax.ShapeDtypeStruct((n, wid), BF16)]
        + [jax.ShapeDtypeStruct((1, wid), F32)] * 3,
        compiler_params=_cparams(("arbitrary", "arbitrary")),
    )(y, r, k2, v, proj, prm["ln_g"], prm["ln_b"], prm["r_k"], dycat)


def _sgu_math(blk, ln_g, ln_b, ws_ref, bb_ref, mixed_ref, d, ngr):
    u, v, g = blk[:, 0:d], blk[:, d:2 * d], blk[:, 2 * d:3 * d]
    gu, dgu = _gelu_and_grad(u)
    gv, dgv = _gelu_and_grad(v)
    cen = gv - jnp.mean(gv, axis=-1, keepdims=True)
    rstd = lax.rsqrt(jnp.mean(cen * cen, axis=-1, keepdims=True) + LN_EPS)
    vhat = cen * rstd
    vn = vhat * ln_g + ln_b
    tril = _iota((GMLP_CHUNK, GMLP_CHUNK), 0) >= _iota((GMLP_CHUNK, GMLP_CHUNK), 1)
    for gi in range(ngr):
        cs = slice(gi * LANE, (gi + 1) * LANE)
        wm = jnp.where(tril, ws_ref[gi], 0.0)
        mixed_ref[:, cs] = jnp.dot(wm, vn[:, cs], precision=HI, preferred_element_type=F32) + bb_ref[gi]
    return dict(g=g, gu=gu, dgu=dgu, dgv=dgv, rstd=rstd, vhat=vhat, vn=vn, tril=tril)


def sgu_fwd(proj, ln_g, ln_b, w_s, b_bc):
    n, d3 = proj.shape
    d = d3 // 3
    ngr = d // LANE

    def body(p_ref, lg, lb, ws_ref, bb_ref, o_ref, mixed_ref):
        m = _sgu_math(p_ref[...], lg[...], lb[...], ws_ref, bb_ref, mixed_ref, d, ngr)
        o_ref[...] = (m["gu"] * mixed_ref[...] * _silu_and_grad(m["g"])[0]).astype(BF16)

    return pl.pallas_call(
        body, name="sgu_fwd", grid=(n // GMLP_CHUNK,),
        in_specs=[_row_spec(GMLP_CHUNK, d3), _full_spec((1, d)), _full_spec((1, d)),
                  _full_spec((ngr, GMLP_CHUNK, GMLP_CHUNK)), _full_spec((ngr, GMLP_CHUNK, LANE))],
        out_specs=_row_spec(GMLP_CHUNK, d), out_shape=jax.ShapeDtypeStruct((n, d), BF16),
        scratch_shapes=[pltpu.VMEM((GMLP_CHUNK, d), F32)], compiler_params=_cparams(("parallel",)),
    )(proj, ln_g, ln_b, w_s, b_bc)


def sgu_bwd(proj, ln_g, ln_b, w_s, b_bc, dyin):
    n, d3 = proj.shape
    d = d3 // 3
    ngr = d // LANE
    nsteps = n // GMLP_CHUNK

    def body(p_ref, lg, lb, ws_ref, bb_ref, dy_ref, dp_ref, dws_ref, dbs_ref, dlg_ref, dlb_ref,
             mixed_ref, dvn_ref, dbacc_ref):
        step = pl.program_id(0)

        @pl.when(step == 0)
        def _():
            for ref in (dws_ref, dlg_ref, dlb_ref, dbacc_ref):
                ref[...] = jnp.zeros_like(ref)

        m = _sgu_math(p_ref[...], lg[...], lb[...], ws_ref, bb_ref, mixed_ref, d, ngr)
        silu, dsilu = _silu_and_grad(m["g"])
        dyv = dy_ref[...].astype(F32)
        mixed = mixed_ref[...]
        dp_ref[:, 2 * d:3 * d] = (dyv * m["gu"] * mixed * dsilu).astype(BF16)
        doc = dyv * silu
        dp_ref[:, 0:d] = (doc * mixed * m["dgu"]).astype(BF16)
        dmixed = doc * m["gu"]
        for gi in range(ngr):
            cs = slice(gi * LANE, (gi + 1) * LANE)
            dm = dmixed[:, cs]
            wm = jnp.where(m["tril"], ws_ref[gi], 0.0)
            dws_ref[gi] += jnp.where(m["tril"], lax.dot_general(dm, m["vn"][:, cs], NT_DIMS, precision=HI,
                                                                preferred_element_type=F32), 0.0)
            dbacc_ref[gi] += dm
            dvn_ref[:, cs] = lax.dot_general(wm, dm, TN_DIMS, precision=HI, preferred_element_type=F32)
        dvn = dvn_ref[...]
        dlg_ref[...] += _colsum(dvn * m["vhat"])
        dlb_ref[...] += _colsum(dvn)
        dvh = dvn * lg[...]
        dgv = m["rstd"] * (dvh - jnp.mean(dvh, axis=-1, keepdims=True)
                           - m["vhat"] * jnp.mean(dvh * m["vhat"], axis=-1, keepdims=True))
        dp_ref[:, d:2 * d] = (dgv * m["dgv"]).astype(BF16)

        @pl.when(step == nsteps - 1)
        def _():
            dbs_ref[...] = jnp.sum(dbacc_ref[...], axis=-1, keepdims=True)

    return pl.pallas_call(
        body, name="sgu_bwd", grid=(nsteps,),
        in_specs=[_row_spec(GMLP_CHUNK, d3), _full_spec((1, d)), _full_spec((1, d)),
                  _full_spec((ngr, GMLP_CHUNK, GMLP_CHUNK)), _full_spec((ngr, GMLP_CHUNK, LANE)), _row_spec(GMLP_CHUNK, d)],
        out_specs=[_row_spec(GMLP_CHUNK, d3), _full_spec((ngr, GMLP_CHUNK, GMLP_CHUNK)), _full_spec((ngr, GMLP_CHUNK, 1)),
                   _full_spec((1, d)), _full_spec((1, d))],
        out_shape=[jax.ShapeDtypeStruct((n, d3), BF16), jax.ShapeDtypeStruct((ngr, GMLP_CHUNK, GMLP_CHUNK), F32),
                   jax.ShapeDtypeStruct((ngr, GMLP_CHUNK, 1), F32), jax.ShapeDtypeStruct((1, d), F32),
                   jax.ShapeDtypeStruct((1, d), F32)],
        scratch_shapes=[pltpu.VMEM((GMLP_CHUNK, d), F32), pltpu.VMEM((GMLP_CHUNK, d), F32),
                        pltpu.VMEM((ngr, GMLP_CHUNK, LANE), F32)],
        compiler_params=_cparams(("arbitrary",)),
    )(proj, ln_g, ln_b, w_s, b_bc, dyin)


def _ab_segments(wid, heads):
    return ((0, 3 * wid, 0),
            (3 * wid, 3 * wid + heads, 8 * wid + LANE),
            (3 * wid + heads, 4 * wid + heads, 3 * wid),
            (4 * wid + heads, 7 * wid + heads, 4 * wid),
            (7 * wid + heads, 7 * wid + heads + 2 * LORA, 8 * wid),
            (7 * wid + heads + 2 * LORA, 8 * wid + heads + 2 * LORA, 7 * wid))


def _ab_reorder(shards, wid, heads):
    per = shards[0].shape[1]
    pieces = []
    for o0, o1, _ in sorted(_ab_segments(wid, heads), key=lambda s: s[2]):
        for d, sh in enumerate(shards):
            lo, hi = max(o0, d * per), min(o1, (d + 1) * per)
            if lo < hi:
                pieces.append(sh[:, lo - d * per:hi - d * per])
    pieces.append(jnp.zeros((shards[0].shape[0], LANE - heads), shards[0].dtype))
    return jnp.concatenate(pieces, axis=1)


def _ab_restore(g, wid, heads, n_shards):
    per = (8 * wid + heads + 2 * LORA) // n_shards
    out = []
    for d in range(n_shards):
        pieces = []
        for o0, o1, r0 in _ab_segments(wid, heads):
            lo, hi = max(o0, d * per), min(o1, (d + 1) * per)
            if lo < hi:
                pieces.append(g[:, r0 + lo - o0:r0 + hi - o0])
        out.append(jnp.concatenate(pieces, axis=1))
    return out


ODD_PIECES = ("c_w_in", "c_w_out", "ple_w_gate1", "ple_w_proj1")


def local_step(x, p, target, wts, n_shards, odd_shards):
    bsz, seq, d = x.shape
    n = bsz * seq
    wid = d // 2
    nw = wid // LANE
    heads = wid // HEAD
    x2 = x.reshape(n, d)
    tgt = target.reshape(n, d)
    p0, p1 = p[0].reshape(n, -1), p[1].reshape(n, -1)
    row = lambda a: a.reshape(1, -1)
    g_pre0, g_pre1 = row(wts["norm_pre"][0]), row(wts["norm_pre"][1])
    g_post0, g_post1 = row(wts["norm_post"][0]), row(wts["norm_post"][1])
    w_ab = wts["ab_w_in_r"]
    w_out, wp0, wg0 = wts["ab_w_out"], wts["ple_w_proj0"], wts["ple_w_gate0"]
    rows_cut = lambda a: a.reshape((n_shards, a.shape[0] // n_shards) + a.shape[1:])
    fbias = jnp.pad(row(wts["fox_f_bias"]), ((0, 0), (0, LANE - heads)))
    mu = row(wts["rwkv_mu"])
    prm = dict(mu_r=mu[:, 0:wid], mu_k=mu[:, wid:2 * wid], mu_v=mu[:, 2 * wid:3 * wid], mu_wa=mu[:, 3 * wid:],
               w0=row(wts["rwkv_w0"]), w2=wts["rwkv_w2"], a0=row(wts["rwkv_a0"]), a2=wts["rwkv_a2"],
               k_k=row(wts["rwkv_k_k"]), k_a=row(wts["rwkv_k_a"]), r_k=row(wts["rwkv_r_k"]),
               ln_g=row(wts["rwkv_ln_g"]), ln_b=row(wts["rwkv_ln_b"]))
    c_ln_g, c_ln_b = row(wts["c_ln_g"]), row(wts["c_ln_b"])
    w_s = wts["c_w_s"]
    b_bc = jnp.broadcast_to(wts["c_b_s"][:, :, None], w_s.shape[:2] + (LANE,))

    xn0 = rms_fwd(x2, g_pre0, name="rms_pre0")
    proj0 = matmul(xn0, w_ab, name="ab_in")
    c = fox_gate_fwd(proj0, fbias, bsz, seq, nw)
    crow = jnp.pad(c.reshape(bsz, seq, LANE)[:, :, :heads].transpose(0, 2, 1).reshape(bsz, nw, 2, seq),
                   ((0, 0), (0, 0), (0, 6), (0, 0)))
    oa, ya = fox_attn_fwd(proj0, c, crow, bsz, seq, nw)
    sr, sw, sk, sv, skk, ska = rwkv_pre_fwd(proj0, prm, bsz, seq, nw)
    ysc, states, odd_w = rwkv_scan_fwd(sr, sw, sk, sv, skk, ska, bsz, seq, nw, [odd_shards[piece] for piece in ODD_PIECES])
    w_c, w_cout, wg1, wp1 = (_join_shards(a, 1 - by_rows) for a, by_rows in zip(odd_w, (0, 1, 1, 0)))
    yb = rwkv_post_fwd(ysc, sr, sk, sv, proj0, prm, nw)
    ycat = jnp.concatenate([ya, yb], axis=1)
    y0 = matmul(ycat, w_out, name="ab_out")
    h1, h1b = rms_res_fwd(y0, x2, g_post0, name="rms_post0")
    a0 = matmul(p0, wp0, name="ple_proj0")
    z0 = matmul(h1b, wg0, name="ple_gate0")
    h1p = ple_fwd(h1, a0, z0, name="ple_fwd0")
    xn1 = rms_fwd(h1p, g_pre1, name="rms_pre1")
    proj1 = matmul(xn1, w_c, name="c_in")
    yin = sgu_fwd(proj1, c_ln_g, c_ln_b, w_s, b_bc)
    y1 = matmul(yin, w_cout, name="c_out")
    h2, h2b = rms_res_fwd(y1, h1p, g_post1, name="rms_post1")
    a1 = matmul(p1, wp1, name="ple_proj1")
    z1 = matmul(h2b, wg1, name="ple_gate1")
    h2p = ple_fwd(h2, a1, z1, name="ple_fwd1")
    dh, loss = loss_head(h2p, tgt, name="loss_head")

    g = {}
    da1, dz1 = ple_bwd(dh, a1, z1, name="ple_bwd1")
    g["ple_w_proj1"] = matmul(p1, da1, ta=True, out_dtype=BF16, col_blocks=n_shards, name="d_ple_proj1")
    g["ple_w_gate1"] = rows_cut(matmul(h2b, dz1, ta=True, out_dtype=BF16, name="d_ple_gate1"))
    dh2 = matmul(dz1, wg1, tb=True, add=dh, name="dx_ple_gate1")
    dy1, g_post1_g = rms_bwd(y1, dh2, g_post1, out_dtype=BF16, name="rms_post1_bwd")
    g["c_w_out"] = rows_cut(matmul(yin, dy1, ta=True, out_dtype=BF16, name="d_c_out"))
    dyin = matmul(dy1, w_cout, tb=True, out_dtype=BF16, name="dx_c_out")
    dproj1, g["c_w_s"], g_bs, g["c_ln_g"], g["c_ln_b"] = sgu_bwd(proj1, c_ln_g, c_ln_b, w_s, b_bc, dyin)
    g["c_b_s"] = g_bs.reshape(w_s.shape[:2])
    g["c_w_in"] = matmul(xn1, dproj1, ta=True, out_dtype=BF16, col_blocks=n_shards, name="d_c_in")
    dxn1 = matmul(dproj1, w_c, tb=True, out_dtype=BF16, name="dx_c_in")
    dh1p, g_pre1_g = rms_bwd(h1p, dxn1, g_pre1, res=dh2, name="rms_pre1_bwd")
    da0, dz0 = ple_bwd(dh1p, a0, z0, name="ple_bwd0")
    g["ple_w_proj0"] = matmul(p0, da0, ta=True, out_dtype=BF16, col_blocks=n_shards, name="d_ple_proj0")
    g["ple_w_gate0"] = rows_cut(matmul(h1b, dz0, ta=True, out_dtype=BF16, name="d_ple_gate0"))
    dh1 = matmul(dz0, wg0, tb=True, add=dh1p, name="dx_ple_gate0")
    dy0, g_post0_g = rms_bwd(y0, dh1, g_post0, out_dtype=BF16, name="rms_post0_bwd")
    g["ab_w_out"] = rows_cut(matmul(ycat, dy0, ta=True, out_dtype=BF16, name="d_ab_out"))
    dycat = matmul(dy0, w_out, tb=True, out_dtype=BF16, name="dx_ab_out")
    dysc, dr_b, dk_b, dv_b, dgb, g["rwkv_ln_g"], g["rwkv_ln_b"], g_rk = rwkv_post_bwd(ysc, sr, sk, sv, proj0, prm, dycat, nw)
    g["rwkv_r_k"] = g_rk.reshape(wts["rwkv_r_k"].shape)
    grads_scan, odd_landed = rwkv_scan_bwd(sr, sw, sk, sv, skk, ska, states, dysc, bsz, seq, nw, [g.pop(piece) for piece in ODD_PIECES])
    (dxr, dxk, dxv, dxwa, dmur, dmuk, dmuv, dmuwa, g["rwkv_w0"], g["rwkv_w2"], g["rwkv_a0"], g["rwkv_a2"],
     g["rwkv_k_k"], g["rwkv_k_a"]) = rwkv_pre_bwd(proj0, prm, grads_scan, (dr_b, dk_b, dv_b), bsz, seq, nw)
    g["rwkv_mu"] = jnp.concatenate([dmur, dmuk, dmuv, dmuwa], axis=1)
    dq, dk, dv, dga, dcrow = fox_attn_bwd(proj0, c, crow, oa, dycat, bsz, seq, nw)
    dc = jnp.pad(dcrow[:, :, :2, :].reshape(bsz, heads, seq).transpose(0, 2, 1), ((0, 0), (0, 0), (0, LANE - heads)))
    dfa, g_fb = fox_gate_bwd(proj0, fbias, dc.reshape(n, LANE), bsz, seq, nw)
    g["fox_f_bias"] = g_fb[:, :heads]
    dproj0 = jnp.concatenate([dq, dk, dv, dga, dxr, dxk, dxv, dgb, dxwa, dfa], axis=1)
    g["ab_w_in"] = jnp.stack(_ab_restore(matmul(xn0, dproj0, ta=True, out_dtype=BF16, name="d_ab_in"), wid, heads, n_shards))
    dxn0 = matmul(dproj0, w_ab, tb=True, out_dtype=BF16, name="dx_ab_in")
    dx, g_pre0_g = rms_bwd(x2, dxn0, g_pre0, res=dh1, name="rms_pre0_bwd")

    g["norm_pre"] = jnp.concatenate([g_pre0_g, g_pre1_g], axis=0)
    g["norm_post"] = jnp.concatenate([g_post0_g, g_post1_g], axis=0)
    return loss, dx.reshape(bsz, seq, d), g, dict(zip(ODD_PIECES, odd_landed))


MESH = pl.DeviceIdType.MESH
ANY = pl.BlockSpec(memory_space=pl.ANY)
PACK_COLS = 1024
PACK_ROWS = 16


def _mesh_place():
    xi, yi, ci = lax.axis_index("x"), lax.axis_index("y"), lax.axis_index("c")
    return xi, yi, ci, 4 * xi + 2 * yi + ci


def _peer(xi, yi, ci, m):
    px = 1 - xi if m & 4 else xi
    py = 1 - yi if m & 2 else yi
    pc = 1 - ci if m & 1 else ci
    return (px, py, pc), 4 * px + 2 * py + pc


def _gather(arrays, *, name):
    n = len(arrays)

    def body(*refs):
        ins, outs = refs[:n], refs[n:2 * n]
        send_sems, recv_sems, local_sems = refs[2 * n:]
        xi, yi, ci, me = _mesh_place()
        sibling = (xi, yi, 1 - ci)
        chips = [(1 - xi, yi), (xi, 1 - yi), (1 - xi, 1 - yi)]
        block = lambda px, py, pc: 4 * px + 2 * py + pc

        def copy(a, k, blk, to, src=None):
            return pltpu.make_async_remote_copy(src_ref=outs[a].at[blk] if src is None else src, dst_ref=outs[a].at[blk],
                                                send_sem=send_sems.at[k, a], recv_sem=recv_sems.at[k, a],
                                                device_id=to, device_id_type=MESH)

        started = []
        for a in range(n):
            cp = pltpu.make_async_copy(ins[a], outs[a].at[me], local_sems.at[a])
            cp.start()
            started.append(cp)
        sends = []
        for a in range(n):
            sends.append(copy(a, 0, me, sibling, src=ins[a]))
            sends += [copy(a, 1 + j, me, (*chip, ci), src=ins[a]) for j, chip in enumerate(chips)]
        for cp in sends:
            cp.start()
        for j, chip in enumerate(chips):
            for a in range(n):
                copy(a, 1 + j, block(*chip, ci), (xi, yi, ci)).wait_recv()
                fwd = copy(a, 4 + j, block(*chip, ci), sibling)
                fwd.start()
                sends.append(fwd)
        for a in range(n):
            copy(a, 0, block(xi, yi, 1 - ci), (xi, yi, ci)).wait_recv()
            for j, chip in enumerate(chips):
                copy(a, 4 + j, block(*chip, 1 - ci), (xi, yi, ci)).wait_recv()
        for cp in sends:
            cp.wait_send()
        for cp in started:
            cp.wait()

    return pl.pallas_call(
        body, name=name, in_specs=[ANY] * n, out_specs=[ANY] * n,
        out_shape=[jax.ShapeDtypeStruct((N_DEV,) + a.shape, a.dtype) for a in arrays],
        scratch_shapes=[pltpu.SemaphoreType.DMA((N_DEV - 1, n)), pltpu.SemaphoreType.DMA((N_DEV - 1, n)),
                        pltpu.SemaphoreType.DMA((n,))],
    )(*arrays)


def _scatter(arrays, *, name):
    n = len(arrays)

    def body(*refs):
        start, wait = _direct_exchange(refs[:n], refs[n:2 * n], *refs[2 * n:], scatter=True)
        start()
        wait()

    return pl.pallas_call(
        body, name=name, in_specs=[ANY] * n, out_specs=[ANY] * n, out_shape=[jax.ShapeDtypeStruct(a.shape, a.dtype) for a in arrays],
        scratch_shapes=_exchange_sems(n),
    )(*arrays)


def _exchange_sems(n):
    if n == 0:
        return []
    return [pltpu.SemaphoreType.DMA((N_DEV - 1, n)), pltpu.SemaphoreType.DMA((N_DEV - 1, n)), pltpu.SemaphoreType.DMA((n,))]


def _direct_exchange(ins, outs, send_sems=None, recv_sems=None, local_sems=None, *, scatter):
    n = len(ins)

    def copies():
        xi, yi, ci, me = _mesh_place()
        src = lambda a, idx: ins[a].at[idx] if scatter else ins[a]
        local = [pltpu.make_async_copy(src(a, me), outs[a].at[me], local_sems.at[a]) for a in range(n)]
        sends, arrivals = [], []
        for m in range(1, N_DEV):
            peer, pidx = _peer(xi, yi, ci, m)
            for a in range(n):
                pair = dict(send_sem=send_sems.at[m - 1, a], recv_sem=recv_sems.at[m - 1, a], device_id=peer, device_id_type=MESH)
                sends.append(pltpu.make_async_remote_copy(src_ref=src(a, pidx), dst_ref=outs[a].at[me], **pair))
                arrivals.append(pltpu.make_async_remote_copy(src_ref=src(a, pidx), dst_ref=outs[a].at[pidx], **pair))
        return local, sends, arrivals

    def start():
        local, sends, _ = copies()
        for cp in local + sends:
            cp.start()

    def wait():
        local, sends, arrivals = copies()
        for cp in arrivals:
            cp.wait_recv()
        for cp in sends:
            cp.wait_send()
        for cp in local:
            cp.wait()

    return start, wait


def sum_blocks(x, *, name):
    _, rows, cols = x.shape
    tr = _tile(rows, 256, PACK_ROWS)

    def body(x_ref, o_ref):
        acc = x_ref[0].astype(F32)
        for s in range(1, N_DEV):
            acc = acc + x_ref[s].astype(F32)
        o_ref[...] = acc

    return pl.pallas_call(
        body, name=name, grid=(rows // tr,), in_specs=[pl.BlockSpec((N_DEV, tr, cols), lambda i: (0, i, 0))],
        out_specs=pl.BlockSpec((tr, cols), lambda i: (i, 0)), out_shape=jax.ShapeDtypeStruct((rows, cols), F32),
        compiler_params=_cparams(("parallel",)),
    )(x)


def adamw(w, parts, m, v, *, name):
    rows, cols = w.shape
    nparts = parts.shape[0]
    tr = _tile(rows, 256, 16)
    c1 = 1.0 / (1.0 - ADAM_B1 ** ADAM_STEP)
    c2 = 1.0 / (1.0 - ADAM_B2 ** ADAM_STEP)

    def body(w_ref, p_ref, m_ref, v_ref, g_ref, d_ref, mo_ref, vo_ref):
        gv = p_ref[0].astype(F32)
        for s in range(1, nparts):
            gv = gv + p_ref[s].astype(F32)
        mn = ADAM_B1 * m_ref[...] + (1.0 - ADAM_B1) * gv
        vn = ADAM_B2 * v_ref[...] + (1.0 - ADAM_B2) * (gv * gv)
        g_ref[...] = gv
        d_ref[...] = -ADAM_LR * ((mn * c1) / (jnp.sqrt(vn * c2) + ADAM_EPS) + ADAM_WD * w_ref[...])
        mo_ref[...] = mn
        vo_ref[...] = vn

    spec = pl.BlockSpec((tr, cols), lambda i: (i, 0))
    return pl.pallas_call(
        body, name=name, grid=(rows // tr,), in_specs=[spec, pl.BlockSpec((nparts, tr, cols), lambda i: (0, i, 0)), spec, spec],
        out_specs=[spec] * 4, out_shape=[jax.ShapeDtypeStruct((rows, cols), F32)] * 4, compiler_params=_cparams(("parallel",)),
    )(w, parts, m, v)


def _pack(arrays, dtype):
    flat = jnp.concatenate([a.astype(dtype).reshape(-1) for a in arrays])
    unit = PACK_COLS * PACK_ROWS
    total = -(-flat.shape[0] // unit) * unit
    return jnp.pad(flat, (0, total - flat.shape[0])).reshape(total // PACK_COLS, PACK_COLS)


def _unpack(flat2d, shapes, lead=()):
    flat = flat2d.reshape(lead + (-1,))
    out, off = [], 0
    for shp in shapes:
        size = 1
        for s in shp:
            size *= s
        out.append(flat[..., off:off + size].reshape(lead + tuple(shp)))
        off += size
    return out


def _join_shards(sh, axis):
    return jnp.concatenate([sh[d] for d in range(N_DEV)], axis=axis)


BIG = (("ab_w_in", "ab_w_in", None, False), ("c_w_in", "c_w_in", None, False), ("ab_w_out", "ab_w_out", None, True),
       ("c_w_out", "c_w_out", None, True), ("ple_w_gate0", "ple_w_gate", 0, True), ("ple_w_gate1", "ple_w_gate", 1, True),
       ("ple_w_proj0", "ple_w_proj", 0, False), ("ple_w_proj1", "ple_w_proj", 1, False))
SMALL_SHARDED = (("rwkv_w2", 1), ("rwkv_a2", 1), ("c_ln_g", 0), ("c_ln_b", 0))
REPLICATED = ("norm_pre", "norm_post", "fox_f_bias", "rwkv_mu", "rwkv_w0", "rwkv_a0", "rwkv_k_k", "rwkv_k_a", "rwkv_r_k",
              "rwkv_ln_g", "rwkv_ln_b", "c_w_s", "c_b_s")
WEIGHTS = ("norm_pre", "norm_post", "ab_w_in", "fox_f_bias", "rwkv_mu", "rwkv_w0", "rwkv_w2", "rwkv_a0", "rwkv_a2", "rwkv_k_k",
           "rwkv_k_a", "rwkv_r_k", "rwkv_ln_g", "rwkv_ln_b", "ab_w_out", "c_w_in", "c_ln_g", "c_ln_b", "c_w_s", "c_b_s",
           "c_w_out", "ple_w_proj", "ple_w_gate")
SQUEEZED = ("norm_pre", "norm_post", "ple_w_proj", "ple_w_gate")


def _step(x, p, loss_target, w, mom, vel):
    sq = {k: (a if k in SQUEEZED else a[0]) for k, a in w.items()}
    _, _, _, me = _mesh_place()

    wid, heads = x.shape[-1] // 2, x.shape[-1] // 2 // HEAD
    layer = lambda a, l: a if l is None else a[l]

    shards = {piece: layer(sq[wname], l).astype(BF16) for piece, wname, l, _ in BIG}
    even = [entry for entry in BIG if entry[0] not in ODD_PIECES]
    sends = [shards[piece] for piece, _, _, _ in even]
    sends += [sq[k] if sq[k].ndim == 2 else sq[k].reshape(1, -1) for k, _ in SMALL_SHARDED]
    gathered = _gather(sends, name="gather_weights")
    full = {}
    for (piece, _, _, by_rows), got in zip(even, gathered):
        if piece == "ab_w_in":
            full["ab_w_in_r"] = _ab_reorder([got[d] for d in range(N_DEV)], wid, heads)
        else:
            full[piece] = _join_shards(got, 1 - by_rows)
    for (k, ax), got in zip(SMALL_SHARDED, gathered[len(even):]):
        full[k] = _join_shards(got, 1).reshape(-1) if sq[k].ndim == 1 else _join_shards(got, 1)
    for k in REPLICATED:
        full[k] = sq[k]

    loss, grad_x, g, parts = local_step(x, p, loss_target, full, N_DEV, {piece: shards[piece] for piece in ODD_PIECES})

    landed = _scatter([g[piece] for piece, _, _, _ in even], name="exchange_grads")
    parts.update({piece: a for (piece, _, _, _), a in zip(even, landed)})

    small_names = [k for k, _ in SMALL_SHARDED] + list(REPLICATED)
    small_full_shapes = [full[k].shape for k in small_names] + [(1, 1)]
    partial = _gather([_pack([g[k].reshape(full[k].shape) for k in small_names] + [loss], F32)], name="gather_small_grads")[0]
    summed = _unpack(sum_blocks(partial, name="sum_small_grads"), small_full_shapes)
    small_grads = dict(zip(small_names, summed[:-1]))
    for k, ax in SMALL_SHARDED:
        width = sq[k].shape[ax]
        small_grads[k] = lax.dynamic_slice_in_dim(small_grads[k], me * width, width, axis=ax)
    loss_all = summed[-1][0, 0]

    outs_g, outs_d, outs_m, outs_v = [], [], [], []
    for k in WEIGHTS:
        shape = w[k].shape
        two_d = lambda a: a.reshape(-1, a.shape[-1])
        pieces = [(piece, l) for piece, wname, l, _ in BIG if wname == k]
        if not pieces:
            res = adamw(two_d(w[k]), two_d(small_grads[k].reshape(shape))[None], two_d(mom[k]), two_d(vel[k]), name="adamw_" + k)
        elif pieces[0][1] is None:
            res = adamw(two_d(w[k]), parts[k], two_d(mom[k]), two_d(vel[k]), name="adamw_" + k)
        else:
            per_layer = [adamw(w[k][l], parts[piece], mom[k][l], vel[k][l], name="adamw_" + piece) for piece, l in pieces]
            res = [jnp.stack(r) for r in zip(*per_layer)]
        for out, r in zip((outs_g, outs_d, outs_m, outs_v), res):
            out.append(r.reshape(shape))
    return (loss_all, grad_x, *outs_g, *outs_d, *outs_m, *outs_v)


def kernel(x, p, norm_pre, norm_post, ab_w_in, fox_f_bias, rwkv_mu, rwkv_w0, rwkv_w2, rwkv_a0, rwkv_a2, rwkv_k_k, rwkv_k_a, rwkv_r_k, rwkv_ln_g, rwkv_ln_b, ab_w_out, c_w_in, c_ln_g, c_ln_b, c_w_s, c_b_s, c_w_out, ple_w_proj, ple_w_gate, loss_target, m_norm_pre, m_norm_post, m_ab_w_in, m_fox_f_bias, m_rwkv_mu, m_rwkv_w0, m_rwkv_w2, m_rwkv_a0, m_rwkv_a2, m_rwkv_k_k, m_rwkv_k_a, m_rwkv_r_k, m_rwkv_ln_g, m_rwkv_ln_b, m_ab_w_out, m_c_w_in, m_c_ln_g, m_c_ln_b, m_c_w_s, m_c_b_s, m_c_w_out, m_ple_w_proj, m_ple_w_gate, v_norm_pre, v_norm_post, v_ab_w_in, v_fox_f_bias, v_rwkv_mu, v_rwkv_w0, v_rwkv_w2, v_rwkv_a0, v_rwkv_a2, v_rwkv_k_k, v_rwkv_k_a, v_rwkv_r_k, v_rwkv_ln_g, v_rwkv_ln_b, v_ab_w_out, v_c_w_in, v_c_ln_g, v_c_ln_b, v_c_w_s, v_c_b_s, v_c_w_out, v_ple_w_proj, v_ple_w_gate):
    w = dict(norm_pre=norm_pre, norm_post=norm_post, ab_w_in=ab_w_in, fox_f_bias=fox_f_bias, rwkv_mu=rwkv_mu, rwkv_w0=rwkv_w0, rwkv_w2=rwkv_w2, rwkv_a0=rwkv_a0, rwkv_a2=rwkv_a2, rwkv_k_k=rwkv_k_k, rwkv_k_a=rwkv_k_a, rwkv_r_k=rwkv_r_k, rwkv_ln_g=rwkv_ln_g, rwkv_ln_b=rwkv_ln_b, ab_w_out=ab_w_out, c_w_in=c_w_in, c_ln_g=c_ln_g, c_ln_b=c_ln_b, c_w_s=c_w_s, c_b_s=c_b_s, c_w_out=c_w_out, ple_w_proj=ple_w_proj, ple_w_gate=ple_w_gate)
    mom = dict(norm_pre=m_norm_pre, norm_post=m_norm_post, ab_w_in=m_ab_w_in, fox_f_bias=m_fox_f_bias, rwkv_mu=m_rwkv_mu, rwkv_w0=m_rwkv_w0, rwkv_w2=m_rwkv_w2, rwkv_a0=m_rwkv_a0, rwkv_a2=m_rwkv_a2, rwkv_k_k=m_rwkv_k_k, rwkv_k_a=m_rwkv_k_a, rwkv_r_k=m_rwkv_r_k, rwkv_ln_g=m_rwkv_ln_g, rwkv_ln_b=m_rwkv_ln_b, ab_w_out=m_ab_w_out, c_w_in=m_c_w_in, c_ln_g=m_c_ln_g, c_ln_b=m_c_ln_b, c_w_s=m_c_w_s, c_b_s=m_c_b_s, c_w_out=m_c_w_out, ple_w_proj=m_ple_w_proj, ple_w_gate=m_ple_w_gate)
    vel = dict(norm_pre=v_norm_pre, norm_post=v_norm_post, ab_w_in=v_ab_w_in, fox_f_bias=v_fox_f_bias, rwkv_mu=v_rwkv_mu, rwkv_w0=v_rwkv_w0, rwkv_w2=v_rwkv_w2, rwkv_a0=v_rwkv_a0, rwkv_a2=v_rwkv_a2, rwkv_k_k=v_rwkv_k_k, rwkv_k_a=v_rwkv_k_a, rwkv_r_k=v_rwkv_r_k, rwkv_ln_g=v_rwkv_ln_g, rwkv_ln_b=v_rwkv_ln_b, ab_w_out=v_ab_w_out, c_w_in=v_c_w_in, c_ln_g=v_c_ln_g, c_ln_b=v_c_ln_b, c_w_s=v_c_w_s, c_b_s=v_c_b_s, c_w_out=v_c_w_out, ple_w_proj=v_ple_w_proj, ple_w_gate=v_ple_w_gate)
    return _step(x, p, loss_target, w, mom, vel)
```

```python
import functools

import jax
import jax.numpy as jnp
from jax import lax
from jax.experimental import pallas as pl
from jax.experimental.pallas import tpu as pltpu

F32 = jnp.float32
BF16 = jnp.bfloat16
HI = lax.Precision.HIGHEST

HEAD = 64
LANE = 128
LORA = 64
GMLP_CHUNK = 128
RMS_EPS = 1e-6
LN_EPS = 1e-5
GN_EPS = 64e-5
VMEM_LIMIT = 56 * 1024 * 1024
N_DEV = 8

ADAM_LR = 0.001
ADAM_B1 = 0.9
ADAM_B2 = 0.999
ADAM_EPS = 1e-08
ADAM_WD = 0.01
ADAM_STEP = 10

NT_DIMS = (((1,), (1,)), ((), ()))
TN_DIMS = (((0,), (0,)), ((), ()))


def _cparams(sem):
    return pltpu.CompilerParams(dimension_semantics=sem, vmem_limit_bytes=VMEM_LIMIT)


def _tile(n, target, q=LANE):
    if n <= target:
        return n
    best = None
    for d in range(q, target + 1, q):
        if n % d == 0:
            best = d
    assert best is not None, (n, target, q)
    return best


def _sigmoid(x):
    return 1.0 / (1.0 + jnp.exp(-x))


def _silu_and_grad(x):
    s = _sigmoid(x)
    return x * s, s * (1.0 + x * (1.0 - s))


def _gelu_and_grad(x):
    cdf = 0.5 * (1.0 + lax.erf(x * 0.7071067811865476))
    pdf = jnp.exp(-0.5 * x * x) * 0.3989422804014327
    return x * cdf, cdf + x * pdf


def _iota(shape, dim):
    return lax.broadcasted_iota(jnp.int32, shape, dim)


def _head_ones():
    return (_iota((LANE, LANE), 0) // HEAD == _iota((LANE, LANE), 1) // HEAD).astype(F32)


def _head_eye():
    return (_iota((HEAD, LANE), 0) == _iota((HEAD, LANE), 1) % HEAD).astype(F32)


def _hsum(x, ones):
    return jnp.dot(x, ones, precision=HI, preferred_element_type=F32)


def _colsum(x):
    return jnp.sum(x, axis=0, keepdims=True)


def matmul(a, b, *, name, ta=False, tb=False, add=None, out_dtype=F32, tm=1024, tn=1024, tk=2048, col_blocks=1):
    m, k = (a.shape[1], a.shape[0]) if ta else a.shape
    n = b.shape[0] if tb else b.shape[1]
    assert (b.shape[1] if tb else b.shape[0]) == k
    tm, tn, tk = _tile(m, tm), _tile(n // col_blocks, tn), _tile(k, tk)
    nk = k // tk
    per = n // col_blocks // tn

    def body(*refs):
        if add is None:
            a_ref, b_ref, o_ref, acc_ref = refs
        else:
            a_ref, b_ref, add_ref, o_ref, acc_ref = refs
        kk = pl.program_id(2)

        @pl.when(kk == 0)
        def _():
            acc_ref[...] = jnp.zeros_like(acc_ref)

        dims = (((0 if ta else 1,), (1 if tb else 0,)), ((), ()))
        acc_ref[...] += lax.dot_general(a_ref[...].astype(BF16), b_ref[...].astype(BF16), dims,
                                        preferred_element_type=F32)

        @pl.when(kk == nk - 1)
        def _():
            r = acc_ref[...]
            if add is not None:
                r = r + add_ref[...].astype(F32)
            o_ref[...] = r.astype(out_dtype)

    a_spec = pl.BlockSpec((tk, tm), lambda i, j, kk: (kk, i)) if ta else pl.BlockSpec((tm, tk), lambda i, j, kk: (i, kk))
    b_spec = pl.BlockSpec((tn, tk), lambda i, j, kk: (j, kk)) if tb else pl.BlockSpec((tk, tn), lambda i, j, kk: (kk, j))
    o_spec = pl.BlockSpec((tm, tn), lambda i, j, kk: (i, j))
    in_specs = [a_spec, b_spec] + ([o_spec] if add is not None else [])
    args = (a, b) + ((add,) if add is not None else ())
    out_shape = (m, n)
    if col_blocks > 1:
        assert add is None
        o_spec = pl.BlockSpec((None, tm, tn), lambda i, j, kk: (j // per, i, j % per))
        out_shape = (col_blocks, m, n // col_blocks)
    return pl.pallas_call(
        body, name=name, grid=(m // tm, n // tn, nk), in_specs=in_specs, out_specs=o_spec,
        out_shape=jax.ShapeDtypeStruct(out_shape, out_dtype), scratch_shapes=[pltpu.VMEM((tm, tn), F32)],
        compiler_params=_cparams(("parallel", "parallel", "arbitrary")),
    )(*args)


def _row_spec(tm, width, cb=0):
    return pl.BlockSpec((tm, width), lambda i: (i, cb))


def _full_spec(shape):
    return pl.BlockSpec(shape, lambda i: (0,) * len(shape))


def rms_fwd(x, g, *, name):
    n, d = x.shape
    tm = _tile(n, 256, 8)

    def body(x_ref, g_ref, o_ref):
        xv = x_ref[...]
        r = lax.rsqrt(jnp.mean(xv * xv, axis=-1, keepdims=True) + RMS_EPS)
        o_ref[...] = (xv * r * g_ref[...]).astype(BF16)

    return pl.pallas_call(
        body, name=name, grid=(n // tm,), in_specs=[_row_spec(tm, d), _full_spec((1, d))],
        out_specs=_row_spec(tm, d), out_shape=jax.ShapeDtypeStruct((n, d), BF16),
        compiler_params=_cparams(("parallel",)),
    )(x, g)


def rms_res_fwd(y, res, g, *, name):
    n, d = y.shape
    tm = _tile(n, 256, 8)

    def body(y_ref, res_ref, g_ref, h_ref, hb_ref):
        yv = y_ref[...]
        r = lax.rsqrt(jnp.mean(yv * yv, axis=-1, keepdims=True) + RMS_EPS)
        h = res_ref[...] + yv * r * g_ref[...]
        h_ref[...] = h
        hb_ref[...] = h.astype(BF16)

    return pl.pallas_call(
        body, name=name, grid=(n // tm,), in_specs=[_row_spec(tm, d), _row_spec(tm, d), _full_spec((1, d))],
        out_specs=[_row_spec(tm, d), _row_spec(tm, d)],
        out_shape=[jax.ShapeDtypeStruct((n, d), F32), jax.ShapeDtypeStruct((n, d), BF16)],
        compiler_params=_cparams(("parallel",)),
    )(y, res, g)


def rms_bwd(x, dy, g, *, name, res=None, out_dtype=F32):
    n, d = x.shape
    tm = _tile(n, 256, 8)

    def body(*refs):
        if res is None:
            x_ref, dy_ref, g_ref, dx_ref, dg_ref = refs
        else:
            x_ref, dy_ref, g_ref, res_ref, dx_ref, dg_ref = refs

        @pl.when(pl.program_id(0) == 0)
        def _():
            dg_ref[...] = jnp.zeros_like(dg_ref)

        xv = x_ref[...]
        dyv = dy_ref[...].astype(F32)
        r = lax.rsqrt(jnp.mean(xv * xv, axis=-1, keepdims=True) + RMS_EPS)
        xhat = xv * r
        dxh = dyv * g_ref[...]
        dx = r * (dxh - xhat * jnp.mean(dxh * xhat, axis=-1, keepdims=True))
        if res is not None:
            dx = dx + res_ref[...]
        dx_ref[...] = dx.astype(out_dtype)
        dg_ref[...] += _colsum(dyv * xhat)

    in_specs = [_row_spec(tm, d), _row_spec(tm, d), _full_spec((1, d))] + ([_row_spec(tm, d)] if res is not None else [])
    args = (x, dy, g) + ((res,) if res is not None else ())
    return pl.pallas_call(
        body, name=name, grid=(n // tm,), in_specs=in_specs, out_specs=[_row_spec(tm, d), _full_spec((1, d))],
        out_shape=[jax.ShapeDtypeStruct((n, d), out_dtype), jax.ShapeDtypeStruct((1, d), F32)],
        compiler_params=_cparams(("arbitrary",)),
    )(*args)


def ple_fwd(h, a, z, *, name):
    n, d = h.shape
    tm = _tile(n, 256, 8)

    def body(h_ref, a_ref, z_ref, o_ref):
        o_ref[...] = h_ref[...] + a_ref[...] * _sigmoid(z_ref[...])

    return pl.pallas_call(
        body, name=name, grid=(n // tm,), in_specs=[_row_spec(tm, d)] * 3, out_specs=_row_spec(tm, d),
        out_shape=jax.ShapeDtypeStruct((n, d), F32), compiler_params=_cparams(("parallel",)),
    )(h, a, z)


def ple_bwd(dh, a, z, *, name):
    n, d = dh.shape
    tm = _tile(n, 256, 8)

    def body(dh_ref, a_ref, z_ref, da_ref, dz_ref):
        s = _sigmoid(z_ref[...])
        dhv = dh_ref[...]
        da_ref[...] = (dhv * s).astype(BF16)
        dz_ref[...] = (dhv * a_ref[...] * s * (1.0 - s)).astype(BF16)

    return pl.pallas_call(
        body, name=name, grid=(n // tm,), in_specs=[_row_spec(tm, d)] * 3, out_specs=[_row_spec(tm, d)] * 2,
        out_shape=[jax.ShapeDtypeStruct((n, d), BF16)] * 2, compiler_params=_cparams(("parallel",)),
    )(dh, a, z)


def loss_head(h, target, *, name):
    n, d = h.shape
    tm = _tile(n, 256, 8)

    def body(h_ref, t_ref, dh_ref, loss_ref):
        @pl.when(pl.program_id(0) == 0)
        def _():
            loss_ref[...] = jnp.zeros_like(loss_ref)

        e = h_ref[...] - t_ref[...]
        dh_ref[...] = e * (1.0 / d)
        loss_ref[...] += 0.5 * jnp.sum(jnp.sum(e * e, axis=-1, keepdims=True) * (1.0 / d), axis=0, keepdims=True)

    return pl.pallas_call(
        body, name=name, grid=(n // tm,), in_specs=[_row_spec(tm, d)] * 2,
        out_specs=[_row_spec(tm, d), _full_spec((1, 1))],
        out_shape=[jax.ShapeDtypeStruct((n, d), F32), jax.ShapeDtypeStruct((1, 1), F32)],
        compiler_params=_cparams(("arbitrary",)),
    )(h, target)


def _log_sigmoid(z):
    return jnp.minimum(z, 0.0) - jnp.log1p(jnp.exp(-jnp.abs(z)))


def fox_gate_fwd(proj, fbias, bsz, seq, nw):
    cc = _tile(seq, 256, 8)

    def body(f_ref, b_ref, c_ref):
        low = (_iota((cc, cc), 0) >= _iota((cc, cc), 1)).astype(F32)
        carry = jnp.zeros((1, LANE), F32)
        for ci in range(seq // cc):
            rows = slice(ci * cc, (ci + 1) * cc)
            lf = _log_sigmoid(f_ref[rows, :] + b_ref[...])
            c_ref[rows, :] = jnp.dot(low, lf, precision=HI, preferred_element_type=F32) + carry
            carry = carry + _colsum(lf)

    return pl.pallas_call(
        body, name="fox_gate_fwd", grid=(bsz,),
        in_specs=[pl.BlockSpec((seq, LANE), lambda b: (b, 8 * nw + 1)), _full_spec((1, LANE))],
        out_specs=pl.BlockSpec((seq, LANE), lambda b: (b, 0)),
        out_shape=jax.ShapeDtypeStruct((bsz * seq, LANE), F32), compiler_params=_cparams(("parallel",)),
    )(proj, fbias)


def fox_gate_bwd(proj, fbias, dc, bsz, seq, nw):
    cc = _tile(seq, 256, 8)
    nc = seq // cc

    def body(f_ref, b_ref, dc_ref, df_ref, db_ref):
        @pl.when(pl.program_id(0) == 0)
        def _():
            db_ref[...] = jnp.zeros_like(db_ref)

        upp = (_iota((cc, cc), 0) <= _iota((cc, cc), 1)).astype(F32)
        carry = jnp.zeros((1, LANE), F32)
        dbias = jnp.zeros((1, LANE), F32)
        for ci in reversed(range(nc)):
            rows = slice(ci * cc, (ci + 1) * cc)
            blk = dc_ref[rows, :]
            dlf = jnp.dot(upp, blk, precision=HI, preferred_element_type=F32) + carry
            carry = carry + _colsum(blk)
            df = dlf * _sigmoid(-(f_ref[rows, :] + b_ref[...]))
            df_ref[rows, :] = df.astype(BF16)
            dbias = dbias + _colsum(df)
        db_ref[...] += dbias

    return pl.pallas_call(
        body, name="fox_gate_bwd", grid=(bsz,),
        in_specs=[pl.BlockSpec((seq, LANE), lambda b: (b, 8 * nw + 1)), _full_spec((1, LANE)),
                  pl.BlockSpec((seq, LANE), lambda b: (b, 0))],
        out_specs=[pl.BlockSpec((seq, LANE), lambda b: (b, 0)), _full_spec((1, LANE))],
        out_shape=[jax.ShapeDtypeStruct((bsz * seq, LANE), BF16), jax.ShapeDtypeStruct((1, LANE), F32)],
        compiler_params=_cparams(("arbitrary",)),
    )(proj, fbias, dc)


def _attn_scores(qs, kh, ccol, crow, r0, kend, tq):
    s = lax.dot_general(qs[r0:kend], kh[:kend], NT_DIMS, preferred_element_type=F32)
    s = s + ccol[r0:kend] - crow[:, :kend]
    causal = _iota((tq, kend), 1) <= r0 + _iota((tq, kend), 0)
    return jnp.where(causal, s, -jnp.inf)


def _head_column(c_blk, h):
    return jnp.sum(jnp.where(_iota(c_blk.shape, 1) == h, c_blk, 0.0), axis=1, keepdims=True)


def fox_attn_fwd(proj, c, crow, bsz, seq, nw):
    tq = _tile(seq, 256, 8)
    scale = HEAD ** -0.5

    def body(q_ref, k_ref, v_ref, g_ref, c_ref, crow_ref, o_ref, y_ref):
        hp = pl.program_id(1)
        c_blk = c_ref[...]
        for hh in range(2):
            sl = slice(hh * HEAD, (hh + 1) * HEAD)
            ccol = _head_column(c_blk, 2 * hp + hh)
            crow_h = crow_ref[0, 0, hh:hh + 1, :]
            qs = (q_ref[:, sl] * scale).astype(BF16)
            kh = k_ref[:, sl].astype(BF16)
            vh = v_ref[:, sl].astype(BF16)
            for qi in range(seq // tq):
                r0, kend = qi * tq, (qi + 1) * tq
                s = _attn_scores(qs, kh, ccol, crow_h, r0, kend, tq)
                p = jnp.exp(s - jnp.max(s, axis=-1, keepdims=True))
                o = jnp.dot(p.astype(BF16), vh[:kend], preferred_element_type=F32) / jnp.sum(p, axis=-1, keepdims=True)
                o_ref[r0:kend, sl] = o
                y_ref[r0:kend, sl] = (o * _silu_and_grad(g_ref[r0:kend, sl])[0]).astype(BF16)

    blk = lambda cb: pl.BlockSpec((seq, LANE), lambda b, j: (b, cb * nw + j))
    return pl.pallas_call(
        body, name="fox_attn_fwd", grid=(bsz, nw),
        in_specs=[blk(0), blk(1), blk(2), blk(3), pl.BlockSpec((seq, LANE), lambda b, j: (b, 0)),
                  pl.BlockSpec((1, 1, 8, seq), lambda b, j: (b, j, 0, 0))],
        out_specs=[pl.BlockSpec((seq, LANE), lambda b, j: (b, j))] * 2,
        out_shape=[jax.ShapeDtypeStruct((bsz * seq, nw * LANE), F32), jax.ShapeDtypeStruct((bsz * seq, nw * LANE), BF16)],
        compiler_params=_cparams(("parallel", "parallel")),
    )(proj, proj, proj, proj, c, crow)


def fox_attn_bwd(proj, c, crow, oa, dycat, bsz, seq, nw):
    tq = _tile(seq, 256, 8)
    scale = HEAD ** -0.5

    def body(q_ref, k_ref, v_ref, g_ref, c_ref, crow_ref, o_ref, dy_ref,
             dq_ref, dk_ref, dv_ref, dg_ref, dc_ref, dk_acc, dv_acc, dc_acc):
        hp = pl.program_id(1)
        c_blk = c_ref[...]
        dc_ref[...] = jnp.zeros_like(dc_ref)
        for hh in range(2):
            sl = slice(hh * HEAD, (hh + 1) * HEAD)
            ccol = _head_column(c_blk, 2 * hp + hh)
            crow_h = crow_ref[0, 0, hh:hh + 1, :]
            qs = (q_ref[:, sl] * scale).astype(BF16)
            kh = k_ref[:, sl].astype(BF16)
            vh = v_ref[:, sl].astype(BF16)
            oh = o_ref[:, sl]
            dyh = dy_ref[:, sl].astype(F32)
            silu, dsilu = _silu_and_grad(g_ref[:, sl])
            dg_ref[:, sl] = (dyh * oh * dsilu).astype(BF16)
            do = dyh * silu
            dvec = jnp.sum(do * oh, axis=-1, keepdims=True)
            dob = do.astype(BF16)
            dk_acc[...] = jnp.zeros_like(dk_acc)
            dv_acc[...] = jnp.zeros_like(dv_acc)
            dc_acc[...] = jnp.zeros_like(dc_acc)
            for qi in range(seq // tq):
                r0, kend = qi * tq, (qi + 1) * tq
                s = _attn_scores(qs, kh, ccol, crow_h, r0, kend, tq)
                p = jnp.exp(s - jnp.max(s, axis=-1, keepdims=True))
                p = p / jnp.sum(p, axis=-1, keepdims=True)
                dp = lax.dot_general(dob[r0:kend], vh[:kend], NT_DIMS, preferred_element_type=F32)
                ds = p * (dp - dvec[r0:kend])
                dsb = ds.astype(BF16)
                dq_ref[r0:kend, sl] = (jnp.dot(dsb, kh[:kend], preferred_element_type=F32) * scale).astype(BF16)
                dk_acc[0:kend, :] += lax.dot_general(dsb, qs[r0:kend], TN_DIMS, preferred_element_type=F32)
                dv_acc[0:kend, :] += lax.dot_general(p.astype(BF16), dob[r0:kend], TN_DIMS, preferred_element_type=F32)
                dc_acc[:, 0:kend] += -_colsum(ds)
                rowsum = lax.dot_general(jnp.ones((8, kend), F32), ds, NT_DIMS, precision=HI, preferred_element_type=F32)
                dc_acc[:, r0:kend] += rowsum[0:1, :]
            dk_ref[:, sl] = dk_acc[...].astype(BF16)
            dv_ref[:, sl] = dv_acc[...].astype(BF16)
            dc_ref[0, 0, hh:hh + 1, :] = dc_acc[...]

    blk = lambda cb: pl.BlockSpec((seq, LANE), lambda b, j: (b, cb * nw + j))
    own = pl.BlockSpec((seq, LANE), lambda b, j: (b, j))
    n = bsz * seq
    return pl.pallas_call(
        body, name="fox_attn_bwd", grid=(bsz, nw),
        in_specs=[blk(0), blk(1), blk(2), blk(3), pl.BlockSpec((seq, LANE), lambda b, j: (b, 0)),
                  pl.BlockSpec((1, 1, 8, seq), lambda b, j: (b, j, 0, 0)), own, own],
        out_specs=[own] * 4 + [pl.BlockSpec((1, 1, 8, seq), lambda b, j: (b, j, 0, 0))],
        out_shape=[jax.ShapeDtypeStruct((n, nw * LANE), BF16)] * 4 + [jax.ShapeDtypeStruct((bsz, nw, 8, seq), F32)],
        scratch_shapes=[pltpu.VMEM((seq, HEAD), F32), pltpu.VMEM((seq, HEAD), F32), pltpu.VMEM((1, seq), F32)],
        compiler_params=_cparams(("parallel", "parallel")),
    )(proj, proj, proj, proj, c, crow, oa, dycat)


def _shift_mix(x, mu):
    prev = jnp.where(_iota(x.shape, 0) == 0, 0.0, pltpu.roll(x, 1, 0))
    return x + (prev - x) * mu, prev


def _rwkv_pre_math(r_in, k_in, v_in, wa_in, mu_r, mu_k, mu_v, mu_wa, w0, w2, a0, a2, k_k, k_a, ones):
    r, prev_r = _shift_mix(r_in, mu_r)
    k, prev_k = _shift_mix(k_in, mu_k)
    v, prev_v = _shift_mix(v_in, mu_v)
    wa, prev_wa = _shift_mix(wa_in, mu_wa)
    tw = jnp.tanh(wa[:, :LORA])
    alo = wa[:, LORA:]
    sg = _sigmoid(w0 + jnp.dot(tw, w2, precision=HI, preferred_element_type=F32))
    e = sg * 0.6065306597126334
    w = jnp.exp(-e)
    a = _sigmoid(a0 + jnp.dot(alo, a2, precision=HI, preferred_element_type=F32))
    kkraw = k * k_k
    nrm = jnp.sqrt(_hsum(kkraw * kkraw, ones))
    den = jnp.maximum(nrm, 1e-12)
    kk = kkraw / den
    k2 = k * (1.0 + (a - 1.0) * k_a)
    return dict(r=r, k=k, v=v, prev_r=prev_r, prev_k=prev_k, prev_v=prev_v, prev_wa=prev_wa, tw=tw, alo=alo,
                sg=sg, e=e, w=w, a=a, nrm=nrm, den=den, kk=kk, k2=k2)


def _rwkv_pre_specs(seq, nw):
    tok = lambda cb: pl.BlockSpec((seq, LANE), lambda b, j: (b, cb * nw + j))
    par = pl.BlockSpec((1, LANE), lambda b, j: (0, j))
    lora = pl.BlockSpec((LORA, LANE), lambda b, j: (0, j))
    return [tok(4), tok(5), tok(6), pl.BlockSpec((seq, LANE), lambda b, j: (b, 8 * nw)),
            par, par, par, pl.BlockSpec((1, LANE), lambda b, j: (0, 0)), par, lora, par, lora, par, par]


def rwkv_pre_fwd(proj, prm, bsz, seq, nw):
    def body(r_ref, k_ref, v_ref, wa_ref, mur, muk, muv, muwa, w0, w2, a0, a2, kk_, ka_,
             ro, wo, ko, vo, kko, kao):
        m = _rwkv_pre_math(r_ref[...], k_ref[...], v_ref[...], wa_ref[...], mur[...], muk[...], muv[...], muwa[...],
                           w0[...], w2[...], a0[...], a2[...], kk_[...], ka_[...], _head_ones())
        ro[...] = m["r"]
        wo[...] = m["w"]
        ko[...] = m["k2"]
        vo[...] = m["v"]
        kko[...] = m["kk"]
        kao[...] = m["kk"] * m["a"]

    n = bsz * seq
    own = pl.BlockSpec((seq, LANE), lambda b, j: (b, j))
    return pl.pallas_call(
        body, name="rwkv_pre_fwd", grid=(bsz, nw), in_specs=_rwkv_pre_specs(seq, nw), out_specs=[own] * 6,
        out_shape=[jax.ShapeDtypeStruct((n, nw * LANE), F32)] * 6, compiler_params=_cparams(("parallel", "parallel")),
    )(proj, proj, proj, proj, prm["mu_r"], prm["mu_k"], prm["mu_v"], prm["mu_wa"], prm["w0"], prm["w2"],
      prm["a0"], prm["a2"], prm["k_k"], prm["k_a"])


def rwkv_pre_bwd(proj, prm, grads_scan, grads_bonus, bsz, seq, nw):
    wid = nw * LANE

    def body(r_ref, k_ref, v_ref, wa_ref, mur, muk, muv, muwa, w0, w2, a0, a2, kk_, ka_,
             dr_s, dw_s, dk_s, dv_s, dkk_s, dka_s, dr_b, dk_b, dv_b,
             dxr, dxk, dxv, dxwa, dmur, dmuk, dmuv, dmuwa, dw0, dw2, da0, da2, dkk_p, dka_p, dwa_acc):
        b, j = pl.program_id(0), pl.program_id(1)
        cols = pl.ds(pl.multiple_of(j * LANE, LANE), LANE)
        acc_refs = (dmur, dmuk, dmuv, dmuwa, dw0, dw2, da0, da2, dkk_p, dka_p)

        @pl.when(jnp.logical_and(b == 0, j == 0))
        def _():
            for ref in acc_refs:
                ref[...] = jnp.zeros_like(ref)

        @pl.when(j == 0)
        def _():
            dwa_acc[...] = jnp.zeros_like(dwa_acc)

        ones = _head_ones()
        r_in, k_in, v_in, wa_in = r_ref[...], k_ref[...], v_ref[...], wa_ref[...]
        m = _rwkv_pre_math(r_in, k_in, v_in, wa_in, mur[...], muk[...], muv[...], muwa[...],
                           w0[...], w2[...], a0[...], a2[...], kk_[...], ka_[...], ones)
        k, a, kk = m["k"], m["a"], m["kk"]
        dr = dr_s[...] + dr_b[...]
        dk2 = dk_s[...] + dk_b[...]
        dv = dv_s[...] + dv_b[...]
        dka = dka_s[...]
        da = dka * kk + dk2 * k * ka_[...]
        dkk = dkk_s[...] + dka * a
        dkkraw = jnp.where(m["nrm"] > 1e-12, dkk - kk * _hsum(dkk * kk, ones), dkk) / m["den"]
        dk = dkkraw * kk_[...] + dk2 * (1.0 + (a - 1.0) * ka_[...])
        dkk_p[:, cols] += _colsum(dkkraw * k)
        dka_p[:, cols] += _colsum(dk2 * k * (a - 1.0))
        dza = da * a * (1.0 - a)
        da0[:, cols] += _colsum(dza)
        da2[:, cols] += lax.dot_general(m["alo"], dza, TN_DIMS, precision=HI, preferred_element_type=F32)
        dalo = lax.dot_general(dza, a2[...], NT_DIMS, precision=HI, preferred_element_type=F32)
        dzw = -dw_s[...] * m["w"] * m["e"] * (1.0 - m["sg"])
        dw0[:, cols] += _colsum(dzw)
        dw2[:, cols] += lax.dot_general(m["tw"], dzw, TN_DIMS, precision=HI, preferred_element_type=F32)
        dtw = lax.dot_general(dzw, w2[...], NT_DIMS, precision=HI, preferred_element_type=F32)
        dwa_acc[:, 0:LORA] += dtw * (1.0 - m["tw"] * m["tw"])
        dwa_acc[:, LORA:LANE] += dalo

        def shift_bwd(dxs, x, prev, mu):
            g = dxs * mu
            nxt = jnp.where(_iota(g.shape, 0) == seq - 1, 0.0, pltpu.roll(g, seq - 1, 0))
            return dxs * (1.0 - mu) + nxt, _colsum(dxs * (prev - x))

        for dxs, x, prev, mu, out, dmu in ((dr, r_in, m["prev_r"], mur, dxr, dmur), (dk, k_in, m["prev_k"], muk, dxk, dmuk),
                                           (dv, v_in, m["prev_v"], muv, dxv, dmuv)):
            dx, dm = shift_bwd(dxs, x, prev, mu[...])
            out[...] = dx.astype(BF16)
            dmu[:, cols] += dm

        @pl.when(j == nw - 1)
        def _():
            dx, dm = shift_bwd(dwa_acc[...], wa_in, m["prev_wa"], muwa[...])
            dxwa[...] = dx.astype(BF16)
            dmuwa[...] += dm

    n = bsz * seq
    own = pl.BlockSpec((seq, LANE), lambda b, j: (b, j))
    whole = lambda shape: pl.BlockSpec(shape, lambda b, j: (0,) * len(shape))
    row = jax.ShapeDtypeStruct((1, wid), F32)
    return pl.pallas_call(
        body, name="rwkv_pre_bwd", grid=(bsz, nw), in_specs=_rwkv_pre_specs(seq, nw) + [own] * 9,
        out_specs=[own] * 3 + [pl.BlockSpec((seq, LANE), lambda b, j: (b, 0))]
        + [whole((1, wid))] * 3 + [whole((1, LANE)), whole((1, wid)), whole((LORA, wid)), whole((1, wid)),
                                   whole((LORA, wid)), whole((1, wid)), whole((1, wid))],
        out_shape=[jax.ShapeDtypeStruct((n, wid), BF16)] * 3 + [jax.ShapeDtypeStruct((n, LANE), BF16)]
        + [row] * 3 + [jax.ShapeDtypeStruct((1, LANE), F32), row, jax.ShapeDtypeStruct((LORA, wid), F32), row,
                       jax.ShapeDtypeStruct((LORA, wid), F32), row, row],
        scratch_shapes=[pltpu.VMEM((seq, LANE), F32)],
        compiler_params=_cparams(("arbitrary", "arbitrary")),
    )(proj, proj, proj, proj, prm["mu_r"], prm["mu_k"], prm["mu_v"], prm["mu_wa"], prm["w0"], prm["w2"],
      prm["a0"], prm["a2"], prm["k_k"], prm["k_a"], *grads_scan, *grads_bonus)


def _scan_group(nw):
    return _tile(nw, 4, 1)


ROWS = 8


def _scan_consts():
    ones = _head_ones().astype(BF16)
    return _head_eye(), ones, jnp.concatenate([ones, ones], axis=0)


def _cols8(tile, eye, ones, ones2, two_parts):
    hi = tile.astype(BF16).astype(F32)
    place = lambda part, q: (part[q:q + 1, :] * eye).astype(BF16)
    if two_parts:
        lo = tile - hi
        lhs = jnp.concatenate([jnp.concatenate([place(hi, q), place(lo, q)], axis=1) for q in range(ROWS)], axis=0)
        big = jnp.dot(lhs, ones2, preferred_element_type=F32)
    else:
        big = jnp.dot(jnp.concatenate([place(hi, q) for q in range(ROWS)], axis=0), ones, preferred_element_type=F32)
    return [big[q * HEAD:(q + 1) * HEAD, :] for q in range(ROWS)]


def _rows8(mats, eye, ones):
    big = jnp.dot(jnp.concatenate([m.astype(BF16) for m in mats], axis=0), ones, preferred_element_type=F32)
    return _stack_rows([_colsum(big[q * HEAD:(q + 1) * HEAD, :] * eye) for q in range(ROWS)])


def _stack_rows(rows):
    rid = _iota((ROWS, LANE), 0)
    out = jnp.zeros((ROWS, LANE), F32)
    for q in range(ROWS):
        out = jnp.where(rid == q, rows[q], out)
    return out


def _grid_ends(grid):
    ids = [pl.program_id(i) for i in range(len(grid))]
    first = functools.reduce(jnp.logical_and, [i == 0 for i in ids])
    last = functools.reduce(jnp.logical_and, [i == g - 1 for i, g in zip(ids, grid)])
    return first, last


def rwkv_scan_fwd(r, w, k, v, kk, ka, bsz, seq, nw, gather):
    grp = _scan_group(nw)
    tc = _tile(seq, 32, 8)
    nt = seq // tc
    grid = (bsz, nw // grp, nt)
    ng = len(gather)

    def body(*refs):
        r_ref, w_ref, k_ref, v_ref, kk_ref, ka_ref = refs[:6]
        y_ref, st_ref = refs[6 + ng:8 + ng]
        s_ref = refs[8 + 2 * ng]
        start, wait = _direct_exchange(refs[6:6 + ng], refs[8 + ng:8 + 2 * ng], *refs[9 + 2 * ng:], scatter=False)
        first, last = _grid_ends(grid)
        pl.when(first)(start)

        @pl.when(pl.program_id(2) == 0)
        def _():
            s_ref[...] = jnp.zeros_like(s_ref)

        eye, ones, ones2 = _scan_consts()

        def step(i8, carry):
            base = pl.multiple_of(i8 * ROWS, ROWS)
            for g in range(grp):
                cs = slice(g * LANE, (g + 1) * LANE)
                cols = lambda ref, two: _cols8(ref[pl.ds(base, ROWS), cs], eye, ones, ones2, two)
                rm, wm, km, kkm, kam = cols(r_ref, False), cols(w_ref, True), cols(k_ref, False), cols(kk_ref, False), cols(ka_ref, False)
                vt = v_ref[pl.ds(base, ROWS), cs]
                s = s_ref[g]
                ys = []
                for q in range(ROWS):
                    st_ref[base + q, g] = s
                    sa = _colsum(s * kkm[q])
                    s = s * wm[q] - kam[q] * sa + km[q] * vt[q:q + 1, :]
                    ys.append(_colsum(s * rm[q]))
                y_ref[pl.ds(base, ROWS), cs] = _stack_rows(ys)
                s_ref[g] = s
            return carry

        lax.fori_loop(0, tc // ROWS, step, 0)
        pl.when(last)(wait)

    tok = pl.BlockSpec((tc, grp * LANE), lambda b, g, t: (b * nt + t, g))
    n = bsz * seq
    out = pl.pallas_call(
        body, name="rwkv_scan_fwd", grid=grid, in_specs=[tok] * 6 + [ANY] * ng,
        out_specs=[tok, pl.BlockSpec((tc, grp, HEAD, LANE), lambda b, g, t: (b * nt + t, g, 0, 0))] + [ANY] * ng,
        out_shape=[jax.ShapeDtypeStruct((n, nw * LANE), F32), jax.ShapeDtypeStruct((n, nw, HEAD, LANE), F32)]
        + [jax.ShapeDtypeStruct((N_DEV,) + a.shape, a.dtype) for a in gather],
        scratch_shapes=[pltpu.VMEM((grp, HEAD, LANE), F32)] + _exchange_sems(ng),
        compiler_params=_cparams(("arbitrary", "arbitrary", "arbitrary")),
    )(r, w, k, v, kk, ka, *gather)
    return out[0], out[1], out[2:]


def rwkv_scan_bwd(r, w, k, v, kk, ka, states, dy, bsz, seq, nw, scatter):
    grp = _scan_group(nw)
    tc = _tile(seq, 32, 8)
    nt = seq // tc
    nblk = tc // ROWS
    grid = (bsz, nw // grp, nt)
    ns = len(scatter)

    def body(*refs):
        r_ref, w_ref, k_ref, v_ref, kk_ref, ka_ref, st_ref, dy_ref = refs[:8]
        dr_ref, dw_ref, dk_ref, dv_ref, dkk_ref, dka_ref = refs[8 + ns:14 + ns]
        ds_ref = refs[14 + 2 * ns]
        start, wait = _direct_exchange(refs[8:8 + ns], refs[14 + ns:14 + 2 * ns], *refs[15 + 2 * ns:], scatter=True)
        first, last = _grid_ends(grid)
        pl.when(first)(start)

        @pl.when(pl.program_id(2) == 0)
        def _():
            ds_ref[...] = jnp.zeros_like(ds_ref)

        eye, ones, ones2 = _scan_consts()
        ones_f = _head_ones()

        def step(ii, carry):
            base = pl.multiple_of((nblk - 1 - ii) * ROWS, ROWS)
            rows = pl.ds(base, ROWS)
            for g in range(grp):
                cs = slice(g * LANE, (g + 1) * LANE)
                cols = lambda ref, two: _cols8(ref[rows, cs], eye, ones, ones2, two)
                rm, wm, km, kkm, kam = cols(r_ref, False), cols(w_ref, True), cols(k_ref, False), cols(kk_ref, False), cols(ka_ref, False)
                vt, dyt = v_ref[rows, cs], dy_ref[rows, cs]
                ds = ds_ref[g]
                dvs, sas, p_dr, p_dk, p_dka, p_dw, p_dkk = ([None] * ROWS for _ in range(7))
                for q in reversed(range(ROWS)):
                    sp = st_ref[base + q, g]
                    vrow, dyrow = vt[q:q + 1, :], dyt[q:q + 1, :]
                    sa = _colsum(sp * kkm[q])
                    sas[q] = sa
                    ds = ds + rm[q] * dyrow
                    dvs[q] = _colsum(ds * km[q])
                    dsa = -_colsum(ds * kam[q])
                    p_dr[q] = sp * dyrow
                    p_dk[q] = ds * vrow
                    p_dka[q] = -(ds * sa)
                    p_dw[q] = ds * sp
                    p_dkk[q] = sp * dsa
                    ds = ds * wm[q] + kkm[q] * dsa
                ds_ref[g] = ds
                dv_ref[rows, cs] = _stack_rows(dvs)
                sa_dy = _hsum(_stack_rows(sas) * dyt, ones_f)
                v_dy = _hsum(vt * dyt, ones_f)
                dr_ref[rows, cs] = w_ref[rows, cs] * _rows8(p_dr, eye, ones) - ka_ref[rows, cs] * sa_dy + k_ref[rows, cs] * v_dy
                dk_ref[rows, cs] = _rows8(p_dk, eye, ones)
                dka_ref[rows, cs] = _rows8(p_dka, eye, ones)
                dw_ref[rows, cs] = _rows8(p_dw, eye, ones)
                dkk_ref[rows, cs] = _rows8(p_dkk, eye, ones)
            return carry

        lax.fori_loop(0, nblk, step, 0)
        pl.when(last)(wait)

    tok = pl.BlockSpec((tc, grp * LANE), lambda b, g, t: (b * nt + nt - 1 - t, g))
    n = bsz * seq
    out = pl.pallas_call(
        body, name="rwkv_scan_bwd", grid=grid,
        in_specs=[tok] * 6 + [pl.BlockSpec((tc, grp, HEAD, LANE), lambda b, g, t: (b * nt + nt - 1 - t, g, 0, 0)), tok] + [ANY] * ns,
        out_specs=[tok] * 6 + [ANY] * ns,
        out_shape=[jax.ShapeDtypeStruct((n, nw * LANE), F32)] * 6 + [jax.ShapeDtypeStruct(a.shape, a.dtype) for a in scatter],
        scratch_shapes=[pltpu.VMEM((grp, HEAD, LANE), F32)] + _exchange_sems(ns),
        compiler_params=_cparams(("arbitrary", "arbitrary", "arbitrary")),
    )(r, w, k, v, kk, ka, states, dy, *scatter)
    return out[:6], out[6:]


def _rwkv_post_math(y, r, k2, v, ln_g, ln_b, r_k, ones):
    d = y - _hsum(y, ones) * (1.0 / HEAD)
    rstd = lax.rsqrt(_hsum(d * d, ones) * (1.0 / HEAD) + GN_EPS)
    yn = d * rstd
    s = _hsum(r * k2 * r_k, ones)
    return yn, rstd, s, yn * ln_g + ln_b + s * v


def _rwkv_post_specs(tm, nw):
    tok = pl.BlockSpec((tm, LANE), lambda i, j: (i, j))
    par = pl.BlockSpec((1, LANE), lambda i, j: (0, j))
    return [tok] * 4 + [pl.BlockSpec((tm, LANE), lambda i, j: (i, 7 * nw + j)), par, par, par]


def rwkv_post_fwd(y, r, k2, v, proj, prm, nw):
    n = y.shape[0]
    tm = _tile(n, 512, 8)

    def body(y_ref, r_ref, k_ref, v_ref, g_ref, lg, lb, rk, o_ref):
        ob = _rwkv_post_math(y_ref[...], r_ref[...], k_ref[...], v_ref[...], lg[...], lb[...], rk[...], _head_ones())[3]
        o_ref[...] = (ob * _silu_and_grad(g_ref[...])[0]).astype(BF16)

    return pl.pallas_call(
        body, name="rwkv_post_fwd", grid=(n // tm, nw), in_specs=_rwkv_post_specs(tm, nw),
        out_specs=pl.BlockSpec((tm, LANE), lambda i, j: (i, j)), out_shape=jax.ShapeDtypeStruct((n, nw * LANE), BF16),
        compiler_params=_cparams(("parallel", "parallel")),
    )(y, r, k2, v, proj, prm["ln_g"], prm["ln_b"], prm["r_k"])


def rwkv_post_bwd(y, r, k2, v, proj, prm, dycat, nw):
    n = y.shape[0]
    tm = _tile(n, 512, 8)
    wid = nw * LANE

    def body(y_ref, r_ref, k_ref, v_ref, g_ref, lg, lb, rk, dyc_ref,
             dy_ref, dr_ref, dk_ref, dv_ref, dg_ref, dlg, dlb, drk):
        i, j = pl.program_id(0), pl.program_id(1)
        cols = pl.ds(pl.multiple_of(j * LANE, LANE), LANE)

        @pl.when(jnp.logical_and(i == 0, j == 0))
        def _():
            for ref in (dlg, dlb, drk):
                ref[...] = jnp.zeros_like(ref)

        ones = _head_ones()
        rr, kr, vr = r_ref[...], k_ref[...], v_ref[...]
        yn, rstd, s, ob = _rwkv_post_math(y_ref[...], rr, kr, vr, lg[...], lb[...], rk[...], ones)
        silu, dsilu = _silu_and_grad(g_ref[...])
        dyc = dyc_ref[...].astype(F32)
        dg_ref[...] = (dyc * ob * dsilu).astype(BF16)
        dob = dyc * silu
        dlg[:, cols] += _colsum(dob * yn)
        dlb[:, cols] += _colsum(dob)
        dyn = dob * lg[...]
        dy_ref[...] = rstd * (dyn - _hsum(dyn, ones) * (1.0 / HEAD) - yn * _hsum(dyn * yn, ones) * (1.0 / HEAD))
        dv_ref[...] = dob * s
        dsum = _hsum(dob * vr, ones)
        dr_ref[...] = dsum * kr * rk[...]
        dk_ref[...] = dsum * rr * rk[...]
        drk[:, cols] += _colsum(dsum * rr * kr)

    tok = pl.BlockSpec((tm, LANE), lambda i, j: (i, j))
    whole = pl.BlockSpec((1, wid), lambda i, j: (0, 0))
    return pl.pallas_call(
        body, name="rwkv_post_bwd", grid=(n // tm, nw),
        in_specs=_rwkv_post_specs(tm, nw) + [pl.BlockSpec((tm, LANE), lambda i, j: (i, nw + j))],
        out_specs=[tok] * 5 + [whole] * 3,
        out_shape=[jax.ShapeDtypeStruct((n, wid), F32)] * 4 + [jax.ShapeDtypeStruct((n, wid), BF16)]
        + [jax.ShapeDtypeStruct((1, wid), F32)] * 3,
        compiler_params=_cparams(("arbitrary", "arbitrary")),
    )(y, r, k2, v, proj, prm["ln_g"], prm["ln_b"], prm["r_k"], dycat)


def _sgu_math(blk, ln_g, ln_b, ws_ref, bb_ref, mixed_ref, d, ngr):
    u, v, g = blk[:, 0:d], blk[:, d:2 * d], blk[:, 2 * d:3 * d]
    gu, dgu = _gelu_and_grad(u)
    gv, dgv = _gelu_and_grad(v)
    cen = gv - jnp.mean(gv, axis=-1, keepdims=True)
    rstd = lax.rsqrt(jnp.mean(cen * cen, axis=-1, keepdims=True) + LN_EPS)
    vhat = cen * rstd
    vn = vhat * ln_g + ln_b
    tril = _iota((GMLP_CHUNK, GMLP_CHUNK), 0) >= _iota((GMLP_CHUNK, GMLP_CHUNK), 1)
    for gi in range(ngr):
        cs = slice(gi * LANE, (gi + 1) * LANE)
        wm = jnp.where(tril, ws_ref[gi], 0.0)
        mixed_ref[:, cs] = jnp.dot(wm, vn[:, cs], precision=HI, preferred_element_type=F32) + bb_ref[gi]
    return dict(g=g, gu=gu, dgu=dgu, dgv=dgv, rstd=rstd, vhat=vhat, vn=vn, tril=tril)


def sgu_fwd(proj, ln_g, ln_b, w_s, b_bc):
    n, d3 = proj.shape
    d = d3 // 3
    ngr = d // LANE

    def body(p_ref, lg, lb, ws_ref, bb_ref, o_ref, mixed_ref):
        m = _sgu_math(p_ref[...], lg[...], lb[...], ws_ref, bb_ref, mixed_ref, d, ngr)
        o_ref[...] = (m["gu"] * mixed_ref[...] * _silu_and_grad(m["g"])[0]).astype(BF16)

    return pl.pallas_call(
        body, name="sgu_fwd", grid=(n // GMLP_CHUNK,),
        in_specs=[_row_spec(GMLP_CHUNK, d3), _full_spec((1, d)), _full_spec((1, d)),
                  _full_spec((ngr, GMLP_CHUNK, GMLP_CHUNK)), _full_spec((ngr, GMLP_CHUNK, LANE))],
        out_specs=_row_spec(GMLP_CHUNK, d), out_shape=jax.ShapeDtypeStruct((n, d), BF16),
        scratch_shapes=[pltpu.VMEM((GMLP_CHUNK, d), F32)], compiler_params=_cparams(("parallel",)),
    )(proj, ln_g, ln_b, w_s, b_bc)


def sgu_bwd(proj, ln_g, ln_b, w_s, b_bc, dyin):
    n, d3 = proj.shape
    d = d3 // 3
    ngr = d // LANE
    nsteps = n // GMLP_CHUNK

    def body(p_ref, lg, lb, ws_ref, bb_ref, dy_ref, dp_ref, dws_ref, dbs_ref, dlg_ref, dlb_ref,
             mixed_ref, dvn_ref, dbacc_ref):
        step = pl.program_id(0)

        @pl.when(step == 0)
        def _():
            for ref in (dws_ref, dlg_ref, dlb_ref, dbacc_ref):
                ref[...] = jnp.zeros_like(ref)

        m = _sgu_math(p_ref[...], lg[...], lb[...], ws_ref, bb_ref, mixed_ref, d, ngr)
        silu, dsilu = _silu_and_grad(m["g"])
        dyv = dy_ref[...].astype(F32)
        mixed = mixed_ref[...]
        dp_ref[:, 2 * d:3 * d] = (dyv * m["gu"] * mixed * dsilu).astype(BF16)
        doc = dyv * silu
        dp_ref[:, 0:d] = (doc * mixed * m["dgu"]).astype(BF16)
        dmixed = doc * m["gu"]
        for gi in range(ngr):
            cs = slice(gi * LANE, (gi + 1) * LANE)
            dm = dmixed[:, cs]
            wm = jnp.where(m["tril"], ws_ref[gi], 0.0)
            dws_ref[gi] += jnp.where(m["tril"], lax.dot_general(dm, m["vn"][:, cs], NT_DIMS, precision=HI,
                                                                preferred_element_type=F32), 0.0)
            dbacc_ref[gi] += dm
            dvn_ref[:, cs] = lax.dot_general(wm, dm, TN_DIMS, precision=HI, preferred_element_type=F32)
        dvn = dvn_ref[...]
        dlg_ref[...] += _colsum(dvn * m["vhat"])
        dlb_ref[...] += _colsum(dvn)
        dvh = dvn * lg[...]
        dgv = m["rstd"] * (dvh - jnp.mean(dvh, axis=-1, keepdims=True)
                           - m["vhat"] * jnp.mean(dvh * m["vhat"], axis=-1, keepdims=True))
        dp_ref[:, d:2 * d] = (dgv * m["dgv"]).astype(BF16)

        @pl.when(step == nsteps - 1)
        def _():
            dbs_ref[...] = jnp.sum(dbacc_ref[...], axis=-1, keepdims=True)

    return pl.pallas_call(
        body, name="sgu_bwd", grid=(nsteps,),
        in_specs=[_row_spec(GMLP_CHUNK, d3), _full_spec((1, d)), _full_spec((1, d)),
                  _full_spec((ngr, GMLP_CHUNK, GMLP_CHUNK)), _full_spec((ngr, GMLP_CHUNK, LANE)), _row_spec(GMLP_CHUNK, d)],
        out_specs=[_row_spec(GMLP_CHUNK, d3), _full_spec((ngr, GMLP_CHUNK, GMLP_CHUNK)), _full_spec((ngr, GMLP_CHUNK, 1)),
                   _full_spec((1, d)), _full_spec((1, d))],
        out_shape=[jax.ShapeDtypeStruct((n, d3), BF16), jax.ShapeDtypeStruct((ngr, GMLP_CHUNK, GMLP_CHUNK), F32),
                   jax.ShapeDtypeStruct((ngr, GMLP_CHUNK, 1), F32), jax.ShapeDtypeStruct((1, d), F32),
                   jax.ShapeDtypeStruct((1, d), F32)],
        scratch_shapes=[pltpu.VMEM((GMLP_CHUNK, d), F32), pltpu.VMEM((GMLP_CHUNK, d), F32),
                        pltpu.VMEM((ngr, GMLP_CHUNK, LANE), F32)],
        compiler_params=_cparams(("arbitrary",)),
    )(proj, ln_g, ln_b, w_s, b_bc, dyin)


def _ab_segments(wid, heads):
    return ((0, 3 * wid, 0),
            (3 * wid, 3 * wid + heads, 8 * wid + LANE),
            (3 * wid + heads, 4 * wid + heads, 3 * wid),
            (4 * wid + heads, 7 * wid + heads, 4 * wid),
            (7 * wid + heads, 7 * wid + heads + 2 * LORA, 8 * wid),
            (7 * wid + heads + 2 * LORA, 8 * wid + heads + 2 * LORA, 7 * wid))


def _ab_reorder(shards, wid, heads):
    per = shards[0].shape[1]
    pieces = []
    for o0, o1, _ in sorted(_ab_segments(wid, heads), key=lambda s: s[2]):
        for d, sh in enumerate(shards):
            lo, hi = max(o0, d * per), min(o1, (d + 1) * per)
            if lo < hi:
                pieces.append(sh[:, lo - d * per:hi - d * per])
    pieces.append(jnp.zeros((shards[0].shape[0], LANE - heads), shards[0].dtype))
    return jnp.concatenate(pieces, axis=1)


def _ab_restore(g, wid, heads, n_shards):
    per = (8 * wid + heads + 2 * LORA) // n_shards
    out = []
    for d in range(n_shards):
        pieces = []
        for o0, o1, r0 in _ab_segments(wid, heads):
            lo, hi = max(o0, d * per), min(o1, (d + 1) * per)
            if lo < hi:
                pieces.append(g[:, r0 + lo - o0:r0 + hi - o0])
        out.append(jnp.concatenate(pieces, axis=1))
    return out


ODD_PIECES = ("c_w_in", "c_w_out", "ple_w_gate1", "ple_w_proj1", "ab_w_out", "ple_w_gate0", "ple_w_proj0")
ODD_BY_ROWS = (0, 1, 1, 0, 1, 1, 0)


def local_step(x, p, target, wts, n_shards, odd_shards):
    bsz, seq, d = x.shape
    n = bsz * seq
    wid = d // 2
    nw = wid // LANE
    heads = wid // HEAD
    x2 = x.reshape(n, d)
    tgt = target.reshape(n, d)
    p0, p1 = p[0].reshape(n, -1), p[1].reshape(n, -1)
    row = lambda a: a.reshape(1, -1)
    g_pre0, g_pre1 = row(wts["norm_pre"][0]), row(wts["norm_pre"][1])
    g_post0, g_post1 = row(wts["norm_post"][0]), row(wts["norm_post"][1])
    w_ab = wts["ab_w_in_r"]
    rows_cut = lambda a: a.reshape((n_shards, a.shape[0] // n_shards) + a.shape[1:])
    fbias = jnp.pad(row(wts["fox_f_bias"]), ((0, 0), (0, LANE - heads)))
    mu = row(wts["rwkv_mu"])
    prm = dict(mu_r=mu[:, 0:wid], mu_k=mu[:, wid:2 * wid], mu_v=mu[:, 2 * wid:3 * wid], mu_wa=mu[:, 3 * wid:],
               w0=row(wts["rwkv_w0"]), w2=wts["rwkv_w2"], a0=row(wts["rwkv_a0"]), a2=wts["rwkv_a2"],
               k_k=row(wts["rwkv_k_k"]), k_a=row(wts["rwkv_k_a"]), r_k=row(wts["rwkv_r_k"]),
               ln_g=row(wts["rwkv_ln_g"]), ln_b=row(wts["rwkv_ln_b"]))
    c_ln_g, c_ln_b = row(wts["c_ln_g"]), row(wts["c_ln_b"])
    w_s = wts["c_w_s"]
    b_bc = jnp.broadcast_to(wts["c_b_s"][:, :, None], w_s.shape[:2] + (LANE,))

    xn0 = rms_fwd(x2, g_pre0, name="rms_pre0")
    proj0 = matmul(xn0, w_ab, name="ab_in")
    c = fox_gate_fwd(proj0, fbias, bsz, seq, nw)
    crow = jnp.pad(c.reshape(bsz, seq, LANE)[:, :, :heads].transpose(0, 2, 1).reshape(bsz, nw, 2, seq),
                   ((0, 0), (0, 0), (0, 6), (0, 0)))
    oa, ya = fox_attn_fwd(proj0, c, crow, bsz, seq, nw)
    sr, sw, sk, sv, skk, ska = rwkv_pre_fwd(proj0, prm, bsz, seq, nw)
    ysc, states, odd_w = rwkv_scan_fwd(sr, sw, sk, sv, skk, ska, bsz, seq, nw, [odd_shards[piece] for piece in ODD_PIECES])
    w_c, w_cout, wg1, wp1, w_out, wg0, wp0 = (_join_shards(a, 1 - by_rows) for a, by_rows in zip(odd_w, ODD_BY_ROWS))
    yb = rwkv_post_fwd(ysc, sr, sk, sv, proj0, prm, nw)
    ycat = jnp.concatenate([ya, yb], axis=1)
    y0 = matmul(ycat, w_out, name="ab_out")
    h1, h1b = rms_res_fwd(y0, x2, g_post0, name="rms_post0")
    a0 = matmul(p0, wp0, name="ple_proj0")
    z0 = matmul(h1b, wg0, name="ple_gate0")
    h1p = ple_fwd(h1, a0, z0, name="ple_fwd0")
    xn1 = rms_fwd(h1p, g_pre1, name="rms_pre1")
    proj1 = matmul(xn1, w_c, name="c_in")
    yin = sgu_fwd(proj1, c_ln_g, c_ln_b, w_s, b_bc)
    y1 = matmul(yin, w_cout, name="c_out")
    h2, h2b = rms_res_fwd(y1, h1p, g_post1, name="rms_post1")
    a1 = matmul(p1, wp1, name="ple_proj1")
    z1 = matmul(h2b, wg1, name="ple_gate1")
    h2p = ple_fwd(h2, a1, z1, name="ple_fwd1")
    dh, loss = loss_head(h2p, tgt, name="loss_head")

    g = {}
    da1, dz1 = ple_bwd(dh, a1, z1, name="ple_bwd1")
    g["ple_w_proj1"] = matmul(p1, da1, ta=True, out_dtype=BF16, col_blocks=n_shards, name="d_ple_proj1")
    g["ple_w_gate1"] = rows_cut(matmul(h2b, dz1, ta=True, out_dtype=BF16, name="d_ple_gate1"))
    dh2 = matmul(dz1, wg1, tb=True, add=dh, name="dx_ple_gate1")
    dy1, g_post1_g = rms_bwd(y1, dh2, g_post1, out_dtype=BF16, name="rms_post1_bwd")
    g["c_w_out"] = rows_cut(matmul(yin, dy1, ta=True, out_dtype=BF16, name="d_c_out"))
    dyin = matmul(dy1, w_cout, tb=True, out_dtype=BF16, name="dx_c_out")
    dproj1, g["c_w_s"], g_bs, g["c_ln_g"], g["c_ln_b"] = sgu_bwd(proj1, c_ln_g, c_ln_b, w_s, b_bc, dyin)
    g["c_b_s"] = g_bs.reshape(w_s.shape[:2])
    g["c_w_in"] = matmul(xn1, dproj1, ta=True, out_dtype=BF16, col_blocks=n_shards, name="d_c_in")
    dxn1 = matmul(dproj1, w_c, tb=True, out_dtype=BF16, name="dx_c_in")
    dh1p, g_pre1_g = rms_bwd(h1p, dxn1, g_pre1, res=dh2, name="rms_pre1_bwd")
    da0, dz0 = ple_bwd(dh1p, a0, z0, name="ple_bwd0")
    g["ple_w_proj0"] = matmul(p0, da0, ta=True, out_dtype=BF16, col_blocks=n_shards, name="d_ple_proj0")
    g["ple_w_gate0"] = rows_cut(matmul(h1b, dz0, ta=True, out_dtype=BF16, name="d_ple_gate0"))
    dh1 = matmul(dz0, wg0, tb=True, add=dh1p, name="dx_ple_gate0")
    dy0, g_post0_g = rms_bwd(y0, dh1, g_post0, out_dtype=BF16, name="rms_post0_bwd")
    g["ab_w_out"] = rows_cut(matmul(ycat, dy0, ta=True, out_dtype=BF16, name="d_ab_out"))
    dycat = matmul(dy0, w_out, tb=True, out_dtype=BF16, name="dx_ab_out")
    dysc, dr_b, dk_b, dv_b, dgb, g["rwkv_ln_g"], g["rwkv_ln_b"], g_rk = rwkv_post_bwd(ysc, sr, sk, sv, proj0, prm, dycat, nw)
    g["rwkv_r_k"] = g_rk.reshape(wts["rwkv_r_k"].shape)
    grads_scan, odd_landed = rwkv_scan_bwd(sr, sw, sk, sv, skk, ska, states, dysc, bsz, seq, nw, [g.pop(piece) for piece in ODD_PIECES])
    (dxr, dxk, dxv, dxwa, dmur, dmuk, dmuv, dmuwa, g["rwkv_w0"], g["rwkv_w2"], g["rwkv_a0"], g["rwkv_a2"],
     g["rwkv_k_k"], g["rwkv_k_a"]) = rwkv_pre_bwd(proj0, prm, grads_scan, (dr_b, dk_b, dv_b), bsz, seq, nw)
    g["rwkv_mu"] = jnp.concatenate([dmur, dmuk, dmuv, dmuwa], axis=1)
    dq, dk, dv, dga, dcrow = fox_attn_bwd(proj0, c, crow, oa, dycat, bsz, seq, nw)
    dc = jnp.pad(dcrow[:, :, :2, :].reshape(bsz, heads, seq).transpose(0, 2, 1), ((0, 0), (0, 0), (0, LANE - heads)))
    dfa, g_fb = fox_gate_bwd(proj0, fbias, dc.reshape(n, LANE), bsz, seq, nw)
    g["fox_f_bias"] = g_fb[:, :heads]
    dproj0 = jnp.concatenate([dq, dk, dv, dga, dxr, dxk, dxv, dgb, dxwa, dfa], axis=1)
    g["ab_w_in"] = jnp.stack(_ab_restore(matmul(xn0, dproj0, ta=True, out_dtype=BF16, name="d_ab_in"), wid, heads, n_shards))
    dxn0 = matmul(dproj0, w_ab, tb=True, out_dtype=BF16, name="dx_ab_in")
    dx, g_pre0_g = rms_bwd(x2, dxn0, g_pre0, res=dh1, name="rms_pre0_bwd")

    g["norm_pre"] = jnp.concatenate([g_pre0_g, g_pre1_g], axis=0)
    g["norm_post"] = jnp.concatenate([g_post0_g, g_post1_g], axis=0)
    return loss, dx.reshape(bsz, seq, d), g, dict(zip(ODD_PIECES, odd_landed))


MESH = pl.DeviceIdType.MESH
ANY = pl.BlockSpec(memory_space=pl.ANY)
PACK_COLS = 1024
PACK_ROWS = 16


def _mesh_place():
    xi, yi, ci = lax.axis_index("x"), lax.axis_index("y"), lax.axis_index("c")
    return xi, yi, ci, 4 * xi + 2 * yi + ci


def _peer(xi, yi, ci, m):
    px = 1 - xi if m & 4 else xi
    py = 1 - yi if m & 2 else yi
    pc = 1 - ci if m & 1 else ci
    return (px, py, pc), 4 * px + 2 * py + pc


def _gather(arrays, *, name):
    n = len(arrays)

    def body(*refs):
        ins, outs = refs[:n], refs[n:2 * n]
        send_sems, recv_sems, local_sems = refs[2 * n:]
        xi, yi, ci, me = _mesh_place()
        sibling = (xi, yi, 1 - ci)
        chips = [(1 - xi, yi), (xi, 1 - yi), (1 - xi, 1 - yi)]
        block = lambda px, py, pc: 4 * px + 2 * py + pc

        def copy(a, k, blk, to, src=None):
            return pltpu.make_async_remote_copy(src_ref=outs[a].at[blk] if src is None else src, dst_ref=outs[a].at[blk],
                                                send_sem=send_sems.at[k, a], recv_sem=recv_sems.at[k, a],
                                                device_id=to, device_id_type=MESH)

        started = []
        for a in range(n):
            cp = pltpu.make_async_copy(ins[a], outs[a].at[me], local_sems.at[a])
            cp.start()
            started.append(cp)
        sends = []
        for a in range(n):
            sends.append(copy(a, 0, me, sibling, src=ins[a]))
            sends += [copy(a, 1 + j, me, (*chip, ci), src=ins[a]) for j, chip in enumerate(chips)]
        for cp in sends:
            cp.start()
        for j, chip in enumerate(chips):
            for a in range(n):
                copy(a, 1 + j, block(*chip, ci), (xi, yi, ci)).wait_recv()
                fwd = copy(a, 4 + j, block(*chip, ci), sibling)
                fwd.start()
                sends.append(fwd)
        for a in range(n):
            copy(a, 0, block(xi, yi, 1 - ci), (xi, yi, ci)).wait_recv()
            for j, chip in enumerate(chips):
                copy(a, 4 + j, block(*chip, 1 - ci), (xi, yi, ci)).wait_recv()
        for cp in sends:
            cp.wait_send()
        for cp in started:
            cp.wait()

    return pl.pallas_call(
        body, name=name, in_specs=[ANY] * n, out_specs=[ANY] * n,
        out_shape=[jax.ShapeDtypeStruct((N_DEV,) + a.shape, a.dtype) for a in arrays],
        scratch_shapes=[pltpu.SemaphoreType.DMA((N_DEV - 1, n)), pltpu.SemaphoreType.DMA((N_DEV - 1, n)),
                        pltpu.SemaphoreType.DMA((n,))],
    )(*arrays)


def _scatter(arrays, *, name):
    n = len(arrays)

    def body(*refs):
        start, wait = _direct_exchange(refs[:n], refs[n:2 * n], *refs[2 * n:], scatter=True)
        start()
        wait()

    return pl.pallas_call(
        body, name=name, in_specs=[ANY] * n, out_specs=[ANY] * n, out_shape=[jax.ShapeDtypeStruct(a.shape, a.dtype) for a in arrays],
        scratch_shapes=_exchange_sems(n),
    )(*arrays)


def _exchange_sems(n):
    if n == 0:
        return []
    return [pltpu.SemaphoreType.DMA((N_DEV - 1, n)), pltpu.SemaphoreType.DMA((N_DEV - 1, n)), pltpu.SemaphoreType.DMA((n,))]


def _direct_exchange(ins, outs, send_sems=None, recv_sems=None, local_sems=None, *, scatter):
    n = len(ins)

    def copies():
        xi, yi, ci, me = _mesh_place()
        src = lambda a, idx: ins[a].at[idx] if scatter else ins[a]
        local = [pltpu.make_async_copy(src(a, me), outs[a].at[me], local_sems.at[a]) for a in range(n)]
        sends, arrivals = [], []
        for m in range(1, N_DEV):
            peer, pidx = _peer(xi, yi, ci, m)
            for a in range(n):
                pair = dict(send_sem=send_sems.at[m - 1, a], recv_sem=recv_sems.at[m - 1, a], device_id=peer, device_id_type=MESH)
                sends.append(pltpu.make_async_remote_copy(src_ref=src(a, pidx), dst_ref=outs[a].at[me], **pair))
                arrivals.append(pltpu.make_async_remote_copy(src_ref=src(a, pidx), dst_ref=outs[a].at[pidx], **pair))
        return local, sends, arrivals

    def start():
        local, sends, _ = copies()
        for cp in local + sends:
            cp.start()

    def wait():
        local, sends, arrivals = copies()
        for cp in arrivals:
            cp.wait_recv()
        for cp in sends:
            cp.wait_send()
        for cp in local:
            cp.wait()

    return start, wait


def sum_blocks(x, *, name):
    _, rows, cols = x.shape
    tr = _tile(rows, 256, PACK_ROWS)

    def body(x_ref, o_ref):
        acc = x_ref[0].astype(F32)
        for s in range(1, N_DEV):
            acc = acc + x_ref[s].astype(F32)
        o_ref[...] = acc

    return pl.pallas_call(
        body, name=name, grid=(rows // tr,), in_specs=[pl.BlockSpec((N_DEV, tr, cols), lambda i: (0, i, 0))],
        out_specs=pl.BlockSpec((tr, cols), lambda i: (i, 0)), out_shape=jax.ShapeDtypeStruct((rows, cols), F32),
        compiler_params=_cparams(("parallel",)),
    )(x)


def adamw(w, parts, m, v, *, name):
    rows, cols = w.shape
    nparts = parts.shape[0]
    tr = _tile(rows, 256, 16)
    c1 = 1.0 / (1.0 - ADAM_B1 ** ADAM_STEP)
    c2 = 1.0 / (1.0 - ADAM_B2 ** ADAM_STEP)

    def body(w_ref, p_ref, m_ref, v_ref, g_ref, d_ref, mo_ref, vo_ref):
        gv = p_ref[0].astype(F32)
        for s in range(1, nparts):
            gv = gv + p_ref[s].astype(F32)
        mn = ADAM_B1 * m_ref[...] + (1.0 - ADAM_B1) * gv
        vn = ADAM_B2 * v_ref[...] + (1.0 - ADAM_B2) * (gv * gv)
        g_ref[...] = gv
        d_ref[...] = -ADAM_LR * ((mn * c1) / (jnp.sqrt(vn * c2) + ADAM_EPS) + ADAM_WD * w_ref[...])
        mo_ref[...] = mn
        vo_ref[...] = vn

    spec = pl.BlockSpec((tr, cols), lambda i: (i, 0))
    return pl.pallas_call(
        body, name=name, grid=(rows // tr,), in_specs=[spec, pl.BlockSpec((nparts, tr, cols), lambda i: (0, i, 0)), spec, spec],
        out_specs=[spec] * 4, out_shape=[jax.ShapeDtypeStruct((rows, cols), F32)] * 4, compiler_params=_cparams(("parallel",)),
    )(w, parts, m, v)


def _pack(arrays, dtype):
    flat = jnp.concatenate([a.astype(dtype).reshape(-1) for a in arrays])
    unit = PACK_COLS * PACK_ROWS
    total = -(-flat.shape[0] // unit) * unit
    return jnp.pad(flat, (0, total - flat.shape[0])).reshape(total // PACK_COLS, PACK_COLS)


def _unpack(flat2d, shapes, lead=()):
    flat = flat2d.reshape(lead + (-1,))
    out, off = [], 0
    for shp in shapes:
        size = 1
        for s in shp:
            size *= s
        out.append(flat[..., off:off + size].reshape(lead + tuple(shp)))
        off += size
    return out


def _join_shards(sh, axis):
    return jnp.concatenate([sh[d] for d in range(N_DEV)], axis=axis)


BIG = (("ab_w_in", "ab_w_in", None, False), ("c_w_in", "c_w_in", None, False), ("ab_w_out", "ab_w_out", None, True),
       ("c_w_out", "c_w_out", None, True), ("ple_w_gate0", "ple_w_gate", 0, True), ("ple_w_gate1", "ple_w_gate", 1, True),
       ("ple_w_proj0", "ple_w_proj", 0, False), ("ple_w_proj1", "ple_w_proj", 1, False))
SMALL_SHARDED = (("rwkv_w2", 1), ("rwkv_a2", 1), ("c_ln_g", 0), ("c_ln_b", 0))
REPLICATED = ("norm_pre", "norm_post", "fox_f_bias", "rwkv_mu", "rwkv_w0", "rwkv_a0", "rwkv_k_k", "rwkv_k_a", "rwkv_r_k",
              "rwkv_ln_g", "rwkv_ln_b", "c_w_s", "c_b_s")
WEIGHTS = ("norm_pre", "norm_post", "ab_w_in", "fox_f_bias", "rwkv_mu", "rwkv_w0", "rwkv_w2", "rwkv_a0", "rwkv_a2", "rwkv_k_k",
           "rwkv_k_a", "rwkv_r_k", "rwkv_ln_g", "rwkv_ln_b", "ab_w_out", "c_w_in", "c_ln_g", "c_ln_b", "c_w_s", "c_b_s",
           "c_w_out", "ple_w_proj", "ple_w_gate")
SQUEEZED = ("norm_pre", "norm_post", "ple_w_proj", "ple_w_gate")


def _step(x, p, loss_target, w, mom, vel):
    sq = {k: (a if k in SQUEEZED else a[0]) for k, a in w.items()}
    _, _, _, me = _mesh_place()

    wid, heads = x.shape[-1] // 2, x.shape[-1] // 2 // HEAD
    layer = lambda a, l: a if l is None else a[l]

    shards = {piece: layer(sq[wname], l).astype(BF16) for piece, wname, l, _ in BIG}
    even = [entry for entry in BIG if entry[0] not in ODD_PIECES]
    sends = [shards[piece] for piece, _, _, _ in even]
    sends += [sq[k] if sq[k].ndim == 2 else sq[k].reshape(1, -1) for k, _ in SMALL_SHARDED]
    gathered = _gather(sends, name="gather_weights")
    full = {}
    for (piece, _, _, by_rows), got in zip(even, gathered):
        if piece == "ab_w_in":
            full["ab_w_in_r"] = _ab_reorder([got[d] for d in range(N_DEV)], wid, heads)
        else:
            full[piece] = _join_shards(got, 1 - by_rows)
    for (k, ax), got in zip(SMALL_SHARDED, gathered[len(even):]):
        full[k] = _join_shards(got, 1).reshape(-1) if sq[k].ndim == 1 else _join_shards(got, 1)
    for k in REPLICATED:
        full[k] = sq[k]

    loss, grad_x, g, parts = local_step(x, p, loss_target, full, N_DEV, {piece: shards[piece] for piece in ODD_PIECES})

    landed = _scatter([g[piece] for piece, _, _, _ in even], name="exchange_grads")
    parts.update({piece: a for (piece, _, _, _), a in zip(even, landed)})

    small_names = [k for k, _ in SMALL_SHARDED] + list(REPLICATED)
    small_full_shapes = [full[k].shape for k in small_names] + [(1, 1)]
    partial = _gather([_pack([g[k].reshape(full[k].shape) for k in small_names] + [loss], F32)], name="gather_small_grads")[0]
    summed = _unpack(sum_blocks(partial, name="sum_small_grads"), small_full_shapes)
    small_grads = dict(zip(small_names, summed[:-1]))
    for k, ax in SMALL_SHARDED:
        width = sq[k].shape[ax]
        small_grads[k] = lax.dynamic_slice_in_dim(small_grads[k], me * width, width, axis=ax)
    loss_all = summed[-1][0, 0]

    outs_g, outs_d, outs_m, outs_v = [], [], [], []
    for k in WEIGHTS:
        shape = w[k].shape
        two_d = lambda a: a.reshape(-1, a.shape[-1])
        pieces = [(piece, l) for piece, wname, l, _ in BIG if wname == k]
        if not pieces:
            res = adamw(two_d(w[k]), two_d(small_grads[k].reshape(shape))[None], two_d(mom[k]), two_d(vel[k]), name="adamw_" + k)
        elif pieces[0][1] is None:
            res = adamw(two_d(w[k]), parts[k], two_d(mom[k]), two_d(vel[k]), name="adamw_" + k)
        else:
            per_layer = [adamw(w[k][l], parts[piece], mom[k][l], vel[k][l], name="adamw_" + piece) for piece, l in pieces]
            res = [jnp.stack(r) for r in zip(*per_layer)]
        for out, r in zip((outs_g, outs_d, outs_m, outs_v), res):
            out.append(r.reshape(shape))
    return (loss_all, grad_x, *outs_g, *outs_d, *outs_m, *outs_v)


def kernel(x, p, norm_pre, norm_post, ab_w_in, fox_f_bias, rwkv_mu, rwkv_w0, rwkv_w2, rwkv_a0, rwkv_a2, rwkv_k_k, rwkv_k_a, rwkv_r_k, rwkv_ln_g, rwkv_ln_b, ab_w_out, c_w_in, c_ln_g, c_ln_b, c_w_s, c_b_s, c_w_out, ple_w_proj, ple_w_gate, loss_target, m_norm_pre, m_norm_post, m_ab_w_in, m_fox_f_bias, m_rwkv_mu, m_rwkv_w0, m_rwkv_w2, m_rwkv_a0, m_rwkv_a2, m_rwkv_k_k, m_rwkv_k_a, m_rwkv_r_k, m_rwkv_ln_g, m_rwkv_ln_b, m_ab_w_out, m_c_w_in, m_c_ln_g, m_c_ln_b, m_c_w_s, m_c_b_s, m_c_w_out, m_ple_w_proj, m_ple_w_gate, v_norm_pre, v_norm_post, v_ab_w_in, v_fox_f_bias, v_rwkv_mu, v_rwkv_w0, v_rwkv_w2, v_rwkv_a0, v_rwkv_a2, v_rwkv_k_k, v_rwkv_k_a, v_rwkv_r_k, v_rwkv_ln_g, v_rwkv_ln_b, v_ab_w_out, v_c_w_in, v_c_ln_g, v_c_ln_b, v_c_w_s, v_c_b_s, v_c_w_out, v_ple_w_proj, v_ple_w_gate):
    w = dict(norm_pre=norm_pre, norm_post=norm_post, ab_w_in=ab_w_in, fox_f_bias=fox_f_bias, rwkv_mu=rwkv_mu, rwkv_w0=rwkv_w0, rwkv_w2=rwkv_w2, rwkv_a0=rwkv_a0, rwkv_a2=rwkv_a2, rwkv_k_k=rwkv_k_k, rwkv_k_a=rwkv_k_a, rwkv_r_k=rwkv_r_k, rwkv_ln_g=rwkv_ln_g, rwkv_ln_b=rwkv_ln_b, ab_w_out=ab_w_out, c_w_in=c_w_in, c_ln_g=c_ln_g, c_ln_b=c_ln_b, c_w_s=c_w_s, c_b_s=c_b_s, c_w_out=c_w_out, ple_w_proj=ple_w_proj, ple_w_gate=ple_w_gate)
    mom = dict(norm_pre=m_norm_pre, norm_post=m_norm_post, ab_w_in=m_ab_w_in, fox_f_bias=m_fox_f_bias, rwkv_mu=m_rwkv_mu, rwkv_w0=m_rwkv_w0, rwkv_w2=m_rwkv_w2, rwkv_a0=m_rwkv_a0, rwkv_a2=m_rwkv_a2, rwkv_k_k=m_rwkv_k_k, rwkv_k_a=m_rwkv_k_a, rwkv_r_k=m_rwkv_r_k, rwkv_ln_g=m_rwkv_ln_g, rwkv_ln_b=m_rwkv_ln_b, ab_w_out=m_ab_w_out, c_w_in=m_c_w_in, c_ln_g=m_c_ln_g, c_ln_b=m_c_ln_b, c_w_s=m_c_w_s, c_b_s=m_c_b_s, c_w_out=m_c_w_out, ple_w_proj=m_ple_w_proj, ple_w_gate=m_ple_w_gate)
    vel = dict(norm_pre=v_norm_pre, norm_post=v_norm_post, ab_w_in=v_ab_w_in, fox_f_bias=v_fox_f_bias, rwkv_mu=v_rwkv_mu, rwkv_w0=v_rwkv_w0, rwkv_w2=v_rwkv_w2, rwkv_a0=v_rwkv_a0, rwkv_a2=v_rwkv_a2, rwkv_k_k=v_rwkv_k_k, rwkv_k_a=v_rwkv_k_a, rwkv_r_k=v_rwkv_r_k, rwkv_ln_g=v_rwkv_ln_g, rwkv_ln_b=v_rwkv_ln_b, ab_w_out=v_ab_w_out, c_w_in=v_c_w_in, c_ln_g=v_c_ln_g, c_ln_b=v_c_ln_b, c_w_s=v_c_w_s, c_b_s=v_c_b_s, c_w_out=v_c_w_out, ple_w_proj=v_ple_w_proj, ple_w_gate=v_ple_w_gate)
    return _step(x, p, loss_target, w, mom, vel)
```

```python
import functools

import jax
import jax.numpy as jnp
from jax import lax
from jax.experimental import pallas as pl
from jax.experimental.pallas import tpu as pltpu

F32 = jnp.float32
BF16 = jnp.bfloat16
HI = lax.Precision.HIGHEST

HEAD = 64
LANE = 128
LORA = 64
GMLP_CHUNK = 128
RMS_EPS = 1e-6
LN_EPS = 1e-5
GN_EPS = 64e-5
VMEM_LIMIT = 56 * 1024 * 1024
N_DEV = 8

ADAM_LR = 0.001
ADAM_B1 = 0.9
ADAM_B2 = 0.999
ADAM_EPS = 1e-08
ADAM_WD = 0.01
ADAM_STEP = 10

NT_DIMS = (((1,), (1,)), ((), ()))
TN_DIMS = (((0,), (0,)), ((), ()))


def _cparams(sem):
    return pltpu.CompilerParams(dimension_semantics=sem, vmem_limit_bytes=VMEM_LIMIT)


def _tile(n, target, q=LANE):
    if n <= target:
        return n
    best = None
    for d in range(q, target + 1, q):
        if n % d == 0:
            best = d
    assert best is not None, (n, target, q)
    return best


def _sigmoid(x):
    return 1.0 / (1.0 + jnp.exp(-x))


def _silu_and_grad(x):
    s = _sigmoid(x)
    return x * s, s * (1.0 + x * (1.0 - s))


def _gelu_and_grad(x):
    cdf = 0.5 * (1.0 + lax.erf(x * 0.7071067811865476))
    pdf = jnp.exp(-0.5 * x * x) * 0.3989422804014327
    return x * cdf, cdf + x * pdf


def _iota(shape, dim):
    return lax.broadcasted_iota(jnp.int32, shape, dim)


def _head_ones():
    return (_iota((LANE, LANE), 0) // HEAD == _iota((LANE, LANE), 1) // HEAD).astype(F32)


def _head_eye():
    return (_iota((HEAD, LANE), 0) == _iota((HEAD, LANE), 1) % HEAD).astype(F32)


def _hsum(x, ones):
    return jnp.dot(x, ones, precision=HI, preferred_element_type=F32)


def _colsum(x):
    return jnp.sum(x, axis=0, keepdims=True)


def matmul(a, b, *, name, ta=False, tb=False, add=None, out_dtype=F32, tm=1024, tn=1024, tk=2048, col_blocks=1):
    m, k = (a.shape[1], a.shape[0]) if ta else a.shape
    n = b.shape[0] if tb else b.shape[1]
    assert (b.shape[1] if tb else b.shape[0]) == k
    tm, tn, tk = _tile(m, tm), _tile(n // col_blocks, tn), _tile(k, tk)
    nk = k // tk
    per = n // col_blocks // tn

    def body(*refs):
        if add is None:
            a_ref, b_ref, o_ref, acc_ref = refs
        else:
            a_ref, b_ref, add_ref, o_ref, acc_ref = refs
        kk = pl.program_id(2)

        @pl.when(kk == 0)
        def _():
            acc_ref[...] = jnp.zeros_like(acc_ref)

        dims = (((0 if ta else 1,), (1 if tb else 0,)), ((), ()))
        acc_ref[...] += lax.dot_general(a_ref[...].astype(BF16), b_ref[...].astype(BF16), dims,
                                        preferred_element_type=F32)

        @pl.when(kk == nk - 1)
        def _():
            r = acc_ref[...]
            if add is not None:
                r = r + add_ref[...].astype(F32)
            o_ref[...] = r.astype(out_dtype)

    a_spec = pl.BlockSpec((tk, tm), lambda i, j, kk: (kk, i)) if ta else pl.BlockSpec((tm, tk), lambda i, j, kk: (i, kk))
    b_spec = pl.BlockSpec((tn, tk), lambda i, j, kk: (j, kk)) if tb else pl.BlockSpec((tk, tn), lambda i, j, kk: (kk, j))
    o_spec = pl.BlockSpec((tm, tn), lambda i, j, kk: (i, j))
    in_specs = [a_spec, b_spec] + ([o_spec] if add is not None else [])
    args = (a, b) + ((add,) if add is not None else ())
    out_shape = (m, n)
    if col_blocks > 1:
        assert add is None
        o_spec = pl.BlockSpec((None, tm, tn), lambda i, j, kk: (j // per, i, j % per))
        out_shape = (col_blocks, m, n // col_blocks)
    return pl.pallas_call(
        body, name=name, grid=(m // tm, n // tn, nk), in_specs=in_specs, out_specs=o_spec,
        out_shape=jax.ShapeDtypeStruct(out_shape, out_dtype), scratch_shapes=[pltpu.VMEM((tm, tn), F32)],
        compiler_params=_cparams(("parallel", "parallel", "arbitrary")),
    )(*args)


def _row_spec(tm, width, cb=0):
    return pl.BlockSpec((tm, width), lambda i: (i, cb))


def _full_spec(shape):
    return pl.BlockSpec(shape, lambda i: (0,) * len(shape))


def rms_fwd(x, g, *, name):
    n, d = x.shape
    tm = _tile(n, 256, 8)

    def body(x_ref, g_ref, o_ref):
        xv = x_ref[...]
        r = lax.rsqrt(jnp.mean(xv * xv, axis=-1, keepdims=True) + RMS_EPS)
        o_ref[...] = (xv * r * g_ref[...]).astype(BF16)

    return pl.pallas_call(
        body, name=name, grid=(n // tm,), in_specs=[_row_spec(tm, d), _full_spec((1, d))],
        out_specs=_row_spec(tm, d), out_shape=jax.ShapeDtypeStruct((n, d), BF16),
        compiler_params=_cparams(("parallel",)),
    )(x, g)


def rms_res_fwd(y, res, g, *, name):
    n, d = y.shape
    tm = _tile(n, 256, 8)

    def body(y_ref, res_ref, g_ref, h_ref, hb_ref):
        yv = y_ref[...]
        r = lax.rsqrt(jnp.mean(yv * yv, axis=-1, keepdims=True) + RMS_EPS)
        h = res_ref[...] + yv * r * g_ref[...]
        h_ref[...] = h
        hb_ref[...] = h.astype(BF16)

    return pl.pallas_call(
        body, name=name, grid=(n // tm,), in_specs=[_row_spec(tm, d), _row_spec(tm, d), _full_spec((1, d))],
        out_specs=[_row_spec(tm, d), _row_spec(tm, d)],
        out_shape=[jax.ShapeDtypeStruct((n, d), F32), jax.ShapeDtypeStruct((n, d), BF16)],
        compiler_params=_cparams(("parallel",)),
    )(y, res, g)


def rms_bwd(x, dy, g, *, name, res=None, out_dtype=F32):
    n, d = x.shape
    tm = _tile(n, 256, 8)

    def body(*refs):
        if res is None:
            x_ref, dy_ref, g_ref, dx_ref, dg_ref = refs
        else:
            x_ref, dy_ref, g_ref, res_ref, dx_ref, dg_ref = refs

        @pl.when(pl.program_id(0) == 0)
        def _():
            dg_ref[...] = jnp.zeros_like(dg_ref)

        xv = x_ref[...]
        dyv = dy_ref[...].astype(F32)
        r = lax.rsqrt(jnp.mean(xv * xv, axis=-1, keepdims=True) + RMS_EPS)
        xhat = xv * r
        dxh = dyv * g_ref[...]
        dx = r * (dxh - xhat * jnp.mean(dxh * xhat, axis=-1, keepdims=True))
        if res is not None:
            dx = dx + res_ref[...]
        dx_ref[...] = dx.astype(out_dtype)
        dg_ref[...] += _colsum(dyv * xhat)

    in_specs = [_row_spec(tm, d), _row_spec(tm, d), _full_spec((1, d))] + ([_row_spec(tm, d)] if res is not None else [])
    args = (x, dy, g) + ((res,) if res is not None else ())
    return pl.pallas_call(
        body, name=name, grid=(n // tm,), in_specs=in_specs, out_specs=[_row_spec(tm, d), _full_spec((1, d))],
        out_shape=[jax.ShapeDtypeStruct((n, d), out_dtype), jax.ShapeDtypeStruct((1, d), F32)],
        compiler_params=_cparams(("arbitrary",)),
    )(*args)


def ple_fwd(h, a, z, *, name):
    n, d = h.shape
    tm = _tile(n, 256, 8)

    def body(h_ref, a_ref, z_ref, o_ref):
        o_ref[...] = h_ref[...] + a_ref[...] * _sigmoid(z_ref[...])

    return pl.pallas_call(
        body, name=name, grid=(n // tm,), in_specs=[_row_spec(tm, d)] * 3, out_specs=_row_spec(tm, d),
        out_shape=jax.ShapeDtypeStruct((n, d), F32), compiler_params=_cparams(("parallel",)),
    )(h, a, z)


def ple_bwd(dh, a, z, *, name):
    n, d = dh.shape
    tm = _tile(n, 256, 8)

    def body(dh_ref, a_ref, z_ref, da_ref, dz_ref):
        s = _sigmoid(z_ref[...])
        dhv = dh_ref[...]
        da_ref[...] = (dhv * s).astype(BF16)
        dz_ref[...] = (dhv * a_ref[...] * s * (1.0 - s)).astype(BF16)

    return pl.pallas_call(
        body, name=name, grid=(n // tm,), in_specs=[_row_spec(tm, d)] * 3, out_specs=[_row_spec(tm, d)] * 2,
        out_shape=[jax.ShapeDtypeStruct((n, d), BF16)] * 2, compiler_params=_cparams(("parallel",)),
    )(dh, a, z)


def loss_head(h, target, *, name):
    n, d = h.shape
    tm = _tile(n, 256, 8)

    def body(h_ref, t_ref, dh_ref, loss_ref):
        @pl.when(pl.program_id(0) == 0)
        def _():
            loss_ref[...] = jnp.zeros_like(loss_ref)

        e = h_ref[...] - t_ref[...]
        dh_ref[...] = e * (1.0 / d)
        loss_ref[...] += 0.5 * jnp.sum(jnp.sum(e * e, axis=-1, keepdims=True) * (1.0 / d), axis=0, keepdims=True)

    return pl.pallas_call(
        body, name=name, grid=(n // tm,), in_specs=[_row_spec(tm, d)] * 2,
        out_specs=[_row_spec(tm, d), _full_spec((1, 1))],
        out_shape=[jax.ShapeDtypeStruct((n, d), F32), jax.ShapeDtypeStruct((1, 1), F32)],
        compiler_params=_cparams(("arbitrary",)),
    )(h, target)


def _log_sigmoid(z):
    return jnp.minimum(z, 0.0) - jnp.log1p(jnp.exp(-jnp.abs(z)))


def fox_gate_fwd(proj, fbias, bsz, seq, nw):
    cc = _tile(seq, 256, 8)

    def body(f_ref, b_ref, c_ref):
        low = (_iota((cc, cc), 0) >= _iota((cc, cc), 1)).astype(F32)
        carry = jnp.zeros((1, LANE), F32)
        for ci in range(seq // cc):
            rows = slice(ci * cc, (ci + 1) * cc)
            lf = _log_sigmoid(f_ref[rows, :] + b_ref[...])
            c_ref[rows, :] = jnp.dot(low, lf, precision=HI, preferred_element_type=F32) + carry
            carry = carry + _colsum(lf)

    return pl.pallas_call(
        body, name="fox_gate_fwd", grid=(bsz,),
        in_specs=[pl.BlockSpec((seq, LANE), lambda b: (b, 8 * nw + 1)), _full_spec((1, LANE))],
        out_specs=pl.BlockSpec((seq, LANE), lambda b: (b, 0)),
        out_shape=jax.ShapeDtypeStruct((bsz * seq, LANE), F32), compiler_params=_cparams(("parallel",)),
    )(proj, fbias)


def fox_gate_bwd(proj, fbias, dc, bsz, seq, nw):
    cc = _tile(seq, 256, 8)
    nc = seq // cc

    def body(f_ref, b_ref, dc_ref, df_ref, db_ref):
        @pl.when(pl.program_id(0) == 0)
        def _():
            db_ref[...] = jnp.zeros_like(db_ref)

        upp = (_iota((cc, cc), 0) <= _iota((cc, cc), 1)).astype(F32)
        carry = jnp.zeros((1, LANE), F32)
        dbias = jnp.zeros((1, LANE), F32)
        for ci in reversed(range(nc)):
            rows = slice(ci * cc, (ci + 1) * cc)
            blk = dc_ref[rows, :]
            dlf = jnp.dot(upp, blk, precision=HI, preferred_element_type=F32) + carry
            carry = carry + _colsum(blk)
            df = dlf * _sigmoid(-(f_ref[rows, :] + b_ref[...]))
            df_ref[rows, :] = df.astype(BF16)
            dbias = dbias + _colsum(df)
        db_ref[...] += dbias

    return pl.pallas_call(
        body, name="fox_gate_bwd", grid=(bsz,),
        in_specs=[pl.BlockSpec((seq, LANE), lambda b: (b, 8 * nw + 1)), _full_spec((1, LANE)),
                  pl.BlockSpec((seq, LANE), lambda b: (b, 0))],
        out_specs=[pl.BlockSpec((seq, LANE), lambda b: (b, 0)), _full_spec((1, LANE))],
        out_shape=[jax.ShapeDtypeStruct((bsz * seq, LANE), BF16), jax.ShapeDtypeStruct((1, LANE), F32)],
        compiler_params=_cparams(("arbitrary",)),
    )(proj, fbias, dc)


def _attn_scores(qs, kh, ccol, crow, r0, kend, tq):
    s = lax.dot_general(qs[r0:kend], kh[:kend], NT_DIMS, preferred_element_type=F32)
    s = s + ccol[r0:kend] - crow[:, :kend]
    causal = _iota((tq, kend), 1) <= r0 + _iota((tq, kend), 0)
    return jnp.where(causal, s, -jnp.inf)


def _head_column(c_blk, h):
    return jnp.sum(jnp.where(_iota(c_blk.shape, 1) == h, c_blk, 0.0), axis=1, keepdims=True)


def fox_attn_fwd(proj, c, crow, bsz, seq, nw):
    tq = _tile(seq, 256, 8)
    scale = HEAD ** -0.5

    def body(q_ref, k_ref, v_ref, g_ref, c_ref, crow_ref, o_ref, y_ref):
        hp = pl.program_id(1)
        c_blk = c_ref[...]
        for hh in range(2):
            sl = slice(hh * HEAD, (hh + 1) * HEAD)
            ccol = _head_column(c_blk, 2 * hp + hh)
            crow_h = crow_ref[0, 0, hh:hh + 1, :]
            qs = (q_ref[:, sl] * scale).astype(BF16)
            kh = k_ref[:, sl].astype(BF16)
            vh = v_ref[:, sl].astype(BF16)
            for qi in range(seq // tq):
                r0, kend = qi * tq, (qi + 1) * tq
                s = _attn_scores(qs, kh, ccol, crow_h, r0, kend, tq)
                p = jnp.exp(s - jnp.max(s, axis=-1, keepdims=True))
                o = jnp.dot(p.astype(BF16), vh[:kend], preferred_element_type=F32) / jnp.sum(p, axis=-1, keepdims=True)
                o_ref[r0:kend, sl] = o
                y_ref[r0:kend, sl] = (o * _silu_and_grad(g_ref[r0:kend, sl])[0]).astype(BF16)

    blk = lambda cb: pl.BlockSpec((seq, LANE), lambda b, j: (b, cb * nw + j))
    return pl.pallas_call(
        body, name="fox_attn_fwd", grid=(bsz, nw),
        in_specs=[blk(0), blk(1), blk(2), blk(3), pl.BlockSpec((seq, LANE), lambda b, j: (b, 0)),
                  pl.BlockSpec((1, 1, 8, seq), lambda b, j: (b, j, 0, 0))],
        out_specs=[pl.BlockSpec((seq, LANE), lambda b, j: (b, j))] * 2,
        out_shape=[jax.ShapeDtypeStruct((bsz * seq, nw * LANE), F32), jax.ShapeDtypeStruct((bsz * seq, nw * LANE), BF16)],
        compiler_params=_cparams(("parallel", "parallel")),
    )(proj, proj, proj, proj, c, crow)


def fox_attn_bwd(proj, c, crow, oa, dycat, bsz, seq, nw):
    tq = _tile(seq, 256, 8)
    scale = HEAD ** -0.5

    def body(q_ref, k_ref, v_ref, g_ref, c_ref, crow_ref, o_ref, dy_ref,
             dq_ref, dk_ref, dv_ref, dg_ref, dc_ref, dk_acc, dv_acc, dc_acc):
        hp = pl.program_id(1)
        c_blk = c_ref[...]
        dc_ref[...] = jnp.zeros_like(dc_ref)
        for hh in range(2):
            sl = slice(hh * HEAD, (hh + 1) * HEAD)
            ccol = _head_column(c_blk, 2 * hp + hh)
            crow_h = crow_ref[0, 0, hh:hh + 1, :]
            qs = (q_ref[:, sl] * scale).astype(BF16)
            kh = k_ref[:, sl].astype(BF16)
            vh = v_ref[:, sl].astype(BF16)
            oh = o_ref[:, sl]
            dyh = dy_ref[:, sl].astype(F32)
            silu, dsilu = _silu_and_grad(g_ref[:, sl])
            dg_ref[:, sl] = (dyh * oh * dsilu).astype(BF16)
            do = dyh * silu
            dvec = jnp.sum(do * oh, axis=-1, keepdims=True)
            dob = do.astype(BF16)
            dk_acc[...] = jnp.zeros_like(dk_acc)
            dv_acc[...] = jnp.zeros_like(dv_acc)
            dc_acc[...] = jnp.zeros_like(dc_acc)
            for qi in range(seq // tq):
                r0, kend = qi * tq, (qi + 1) * tq
                s = _attn_scores(qs, kh, ccol, crow_h, r0, kend, tq)
                p = jnp.exp(s - jnp.max(s, axis=-1, keepdims=True))
                p = p / jnp.sum(p, axis=-1, keepdims=True)
                dp = lax.dot_general(dob[r0:kend], vh[:kend], NT_DIMS, preferred_element_type=F32)
                ds = p * (dp - dvec[r0:kend])
                dsb = ds.astype(BF16)
                dq_ref[r0:kend, sl] = (jnp.dot(dsb, kh[:kend], preferred_element_type=F32) * scale).astype(BF16)
                dk_acc[0:kend, :] += lax.dot_general(dsb, qs[r0:kend], TN_DIMS, preferred_element_type=F32)
                dv_acc[0:kend, :] += lax.dot_general(p.astype(BF16), dob[r0:kend], TN_DIMS, preferred_element_type=F32)
                dc_acc[:, 0:kend] += -_colsum(ds)
                ds_lo = (ds - dsb.astype(F32)).astype(BF16)
                rowsum = lax.dot_general(jnp.ones((8, 2 * kend), BF16), jnp.concatenate([dsb, ds_lo], axis=1), NT_DIMS,
                                         preferred_element_type=F32)
                dc_acc[:, r0:kend] += rowsum[0:1, :]
            dk_ref[:, sl] = dk_acc[...].astype(BF16)
            dv_ref[:, sl] = dv_acc[...].astype(BF16)
            dc_ref[0, 0, hh:hh + 1, :] = dc_acc[...]

    blk = lambda cb: pl.BlockSpec((seq, LANE), lambda b, j: (b, cb * nw + j))
    own = pl.BlockSpec((seq, LANE), lambda b, j: (b, j))
    n = bsz * seq
    return pl.pallas_call(
        body, name="fox_attn_bwd", grid=(bsz, nw),
        in_specs=[blk(0), blk(1), blk(2), blk(3), pl.BlockSpec((seq, LANE), lambda b, j: (b, 0)),
                  pl.BlockSpec((1, 1, 8, seq), lambda b, j: (b, j, 0, 0)), own, own],
        out_specs=[own] * 4 + [pl.BlockSpec((1, 1, 8, seq), lambda b, j: (b, j, 0, 0))],
        out_shape=[jax.ShapeDtypeStruct((n, nw * LANE), BF16)] * 4 + [jax.ShapeDtypeStruct((bsz, nw, 8, seq), F32)],
        scratch_shapes=[pltpu.VMEM((seq, HEAD), F32), pltpu.VMEM((seq, HEAD), F32), pltpu.VMEM((1, seq), F32)],
        compiler_params=_cparams(("parallel", "parallel")),
    )(proj, proj, proj, proj, c, crow, oa, dycat)


def _shift_mix(x, mu):
    prev = jnp.where(_iota(x.shape, 0) == 0, 0.0, pltpu.roll(x, 1, 0))
    return x + (prev - x) * mu, prev


def _rwkv_pre_math(r_in, k_in, v_in, wa_in, mu_r, mu_k, mu_v, mu_wa, w0, w2, a0, a2, k_k, k_a, ones):
    r, prev_r = _shift_mix(r_in, mu_r)
    k, prev_k = _shift_mix(k_in, mu_k)
    v, prev_v = _shift_mix(v_in, mu_v)
    wa, prev_wa = _shift_mix(wa_in, mu_wa)
    tw = jnp.tanh(wa[:, :LORA])
    alo = wa[:, LORA:]
    sg = _sigmoid(w0 + jnp.dot(tw, w2, precision=HI, preferred_element_type=F32))
    e = sg * 0.6065306597126334
    w = jnp.exp(-e)
    a = _sigmoid(a0 + jnp.dot(alo, a2, precision=HI, preferred_element_type=F32))
    kkraw = k * k_k
    nrm = jnp.sqrt(_hsum(kkraw * kkraw, ones))
    den = jnp.maximum(nrm, 1e-12)
    kk = kkraw / den
    k2 = k * (1.0 + (a - 1.0) * k_a)
    return dict(r=r, k=k, v=v, prev_r=prev_r, prev_k=prev_k, prev_v=prev_v, prev_wa=prev_wa, tw=tw, alo=alo,
                sg=sg, e=e, w=w, a=a, nrm=nrm, den=den, kk=kk, k2=k2)


def _rwkv_pre_specs(seq, nw):
    tok = lambda cb: pl.BlockSpec((seq, LANE), lambda b, j: (b, cb * nw + j))
    par = pl.BlockSpec((1, LANE), lambda b, j: (0, j))
    lora = pl.BlockSpec((LORA, LANE), lambda b, j: (0, j))
    return [tok(4), tok(5), tok(6), pl.BlockSpec((seq, LANE), lambda b, j: (b, 8 * nw)),
            par, par, par, pl.BlockSpec((1, LANE), lambda b, j: (0, 0)), par, lora, par, lora, par, par]


def rwkv_pre_fwd(proj, prm, bsz, seq, nw):
    def body(r_ref, k_ref, v_ref, wa_ref, mur, muk, muv, muwa, w0, w2, a0, a2, kk_, ka_,
             ro, wo, ko, vo, kko, kao):
        m = _rwkv_pre_math(r_ref[...], k_ref[...], v_ref[...], wa_ref[...], mur[...], muk[...], muv[...], muwa[...],
                           w0[...], w2[...], a0[...], a2[...], kk_[...], ka_[...], _head_ones())
        ro[...] = m["r"]
        wo[...] = m["w"]
        ko[...] = m["k2"]
        vo[...] = m["v"]
        kko[...] = m["kk"]
        kao[...] = m["kk"] * m["a"]

    n = bsz * seq
    own = pl.BlockSpec((seq, LANE), lambda b, j: (b, j))
    return pl.pallas_call(
        body, name="rwkv_pre_fwd", grid=(bsz, nw), in_specs=_rwkv_pre_specs(seq, nw), out_specs=[own] * 6,
        out_shape=[jax.ShapeDtypeStruct((n, nw * LANE), F32)] * 6, compiler_params=_cparams(("parallel", "parallel")),
    )(proj, proj, proj, proj, prm["mu_r"], prm["mu_k"], prm["mu_v"], prm["mu_wa"], prm["w0"], prm["w2"],
      prm["a0"], prm["a2"], prm["k_k"], prm["k_a"])


def rwkv_pre_bwd(proj, prm, grads_scan, grads_bonus, bsz, seq, nw):
    wid = nw * LANE

    def body(r_ref, k_ref, v_ref, wa_ref, mur, muk, muv, muwa, w0, w2, a0, a2, kk_, ka_,
             dr_s, dw_s, dk_s, dv_s, dkk_s, dka_s, dr_b, dk_b, dv_b,
             dxr, dxk, dxv, dxwa, dmur, dmuk, dmuv, dmuwa, dw0, dw2, da0, da2, dkk_p, dka_p, dwa_acc):
        b, j = pl.program_id(0), pl.program_id(1)
        cols = pl.ds(pl.multiple_of(j * LANE, LANE), LANE)
        acc_refs = (dmur, dmuk, dmuv, dmuwa, dw0, dw2, da0, da2, dkk_p, dka_p)

        @pl.when(jnp.logical_and(b == 0, j == 0))
        def _():
            for ref in acc_refs:
                ref[...] = jnp.zeros_like(ref)

        @pl.when(j == 0)
        def _():
            dwa_acc[...] = jnp.zeros_like(dwa_acc)

        ones = _head_ones()
        r_in, k_in, v_in, wa_in = r_ref[...], k_ref[...], v_ref[...], wa_ref[...]
        m = _rwkv_pre_math(r_in, k_in, v_in, wa_in, mur[...], muk[...], muv[...], muwa[...],
                           w0[...], w2[...], a0[...], a2[...], kk_[...], ka_[...], ones)
        k, a, kk = m["k"], m["a"], m["kk"]
        dr = dr_s[...] + dr_b[...]
        dk2 = dk_s[...] + dk_b[...]
        dv = dv_s[...] + dv_b[...]
        dka = dka_s[...]
        da = dka * kk + dk2 * k * ka_[...]
        dkk = dkk_s[...] + dka * a
        dkkraw = jnp.where(m["nrm"] > 1e-12, dkk - kk * _hsum(dkk * kk, ones), dkk) / m["den"]
        dk = dkkraw * kk_[...] + dk2 * (1.0 + (a - 1.0) * ka_[...])
        dkk_p[:, cols] += _colsum(dkkraw * k)
        dka_p[:, cols] += _colsum(dk2 * k * (a - 1.0))
        dza = da * a * (1.0 - a)
        da0[:, cols] += _colsum(dza)
        da2[:, cols] += lax.dot_general(m["alo"], dza, TN_DIMS, precision=HI, preferred_element_type=F32)
        dalo = lax.dot_general(dza, a2[...], NT_DIMS, precision=HI, preferred_element_type=F32)
        dzw = -dw_s[...] * m["w"] * m["e"] * (1.0 - m["sg"])
        dw0[:, cols] += _colsum(dzw)
        dw2[:, cols] += lax.dot_general(m["tw"], dzw, TN_DIMS, precision=HI, preferred_element_type=F32)
        dtw = lax.dot_general(dzw, w2[...], NT_DIMS, precision=HI, preferred_element_type=F32)
        dwa_acc[:, 0:LORA] += dtw * (1.0 - m["tw"] * m["tw"])
        dwa_acc[:, LORA:LANE] += dalo

        def shift_bwd(dxs, x, prev, mu):
            g = dxs * mu
            nxt = jnp.where(_iota(g.shape, 0) == seq - 1, 0.0, pltpu.roll(g, seq - 1, 0))
            return dxs * (1.0 - mu) + nxt, _colsum(dxs * (prev - x))

        for dxs, x, prev, mu, out, dmu in ((dr, r_in, m["prev_r"], mur, dxr, dmur), (dk, k_in, m["prev_k"], muk, dxk, dmuk),
                                           (dv, v_in, m["prev_v"], muv, dxv, dmuv)):
            dx, dm = shift_bwd(dxs, x, prev, mu[...])
            out[...] = dx.astype(BF16)
            dmu[:, cols] += dm

        @pl.when(j == nw - 1)
        def _():
            dx, dm = shift_bwd(dwa_acc[...], wa_in, m["prev_wa"], muwa[...])
            dxwa[...] = dx.astype(BF16)
            dmuwa[...] += dm

    n = bsz * seq
    own = pl.BlockSpec((seq, LANE), lambda b, j: (b, j))
    whole = lambda shape: pl.BlockSpec(shape, lambda b, j: (0,) * len(shape))
    row = jax.ShapeDtypeStruct((1, wid), F32)
    return pl.pallas_call(
        body, name="rwkv_pre_bwd", grid=(bsz, nw), in_specs=_rwkv_pre_specs(seq, nw) + [own] * 9,
        out_specs=[own] * 3 + [pl.BlockSpec((seq, LANE), lambda b, j: (b, 0))]
        + [whole((1, wid))] * 3 + [whole((1, LANE)), whole((1, wid)), whole((LORA, wid)), whole((1, wid)),
                                   whole((LORA, wid)), whole((1, wid)), whole((1, wid))],
        out_shape=[jax.ShapeDtypeStruct((n, wid), BF16)] * 3 + [jax.ShapeDtypeStruct((n, LANE), BF16)]
        + [row] * 3 + [jax.ShapeDtypeStruct((1, LANE), F32), row, jax.ShapeDtypeStruct((LORA, wid), F32), row,
                       jax.ShapeDtypeStruct((LORA, wid), F32), row, row],
        scratch_shapes=[pltpu.VMEM((seq, LANE), F32)],
        compiler_params=_cparams(("arbitrary", "arbitrary")),
    )(proj, proj, proj, proj, prm["mu_r"], prm["mu_k"], prm["mu_v"], prm["mu_wa"], prm["w0"], prm["w2"],
      prm["a0"], prm["a2"], prm["k_k"], prm["k_a"], *grads_scan, *grads_bonus)


SCAN_CHUNK = 64


def _scan_group(nw):
    return _tile(nw, 8, 1)


ROWS = 8


def _scan_consts():
    ones = _head_ones().astype(BF16)
    return _head_eye(), ones, jnp.concatenate([ones, ones], axis=0)


def _cols8(tile, eye, ones, ones2, two_parts):
    hi = tile.astype(BF16).astype(F32)
    place = lambda part, q: (part[q:q + 1, :] * eye).astype(BF16)
    if two_parts:
        lo = tile - hi
        lhs = jnp.concatenate([jnp.concatenate([place(hi, q), place(lo, q)], axis=1) for q in range(ROWS)], axis=0)
        big = jnp.dot(lhs, ones2, preferred_element_type=F32)
    else:
        big = jnp.dot(jnp.concatenate([place(hi, q) for q in range(ROWS)], axis=0), ones, preferred_element_type=F32)
    return [big[q * HEAD:(q + 1) * HEAD, :] for q in range(ROWS)]


def _rows8(mats, eye, ones):
    big = jnp.dot(jnp.concatenate([m.astype(BF16) for m in mats], axis=0), ones, preferred_element_type=F32)
    return _stack_rows([_colsum(big[q * HEAD:(q + 1) * HEAD, :] * eye) for q in range(ROWS)])


def _stack_rows(rows):
    rid = _iota((ROWS, LANE), 0)
    out = jnp.zeros((ROWS, LANE), F32)
    for q in range(ROWS):
        out = jnp.where(rid == q, rows[q], out)
    return out


def _grid_ends(grid):
    ids = [pl.program_id(i) for i in range(len(grid))]
    first = functools.reduce(jnp.logical_and, [i == 0 for i in ids])
    last = functools.reduce(jnp.logical_and, [i == g - 1 for i, g in zip(ids, grid)])
    return first, last


def rwkv_scan_fwd(r, w, k, v, kk, ka, bsz, seq, nw, gather):
    grp = _scan_group(nw)
    tc = _tile(seq, SCAN_CHUNK, 8)
    nt = seq // tc
    grid = (bsz, nw // grp, nt)
    ng = len(gather)

    def body(*refs):
        r_ref, w_ref, k_ref, v_ref, kk_ref, ka_ref = refs[:6]
        y_ref, st_ref = refs[6 + ng:8 + ng]
        s_ref = refs[8 + 2 * ng]
        start, wait = _direct_exchange(refs[6:6 + ng], refs[8 + ng:8 + 2 * ng], *refs[9 + 2 * ng:], scatter=False)
        first, last = _grid_ends(grid)
        pl.when(first)(start)

        @pl.when(pl.program_id(2) == 0)
        def _():
            s_ref[...] = jnp.zeros_like(s_ref)

        eye, ones, ones2 = _scan_consts()

        def step(i8, carry):
            base = pl.multiple_of(i8 * ROWS, ROWS)
            for g in range(grp):
                cs = slice(g * LANE, (g + 1) * LANE)
                cols = lambda ref, two: _cols8(ref[pl.ds(base, ROWS), cs], eye, ones, ones2, two)
                rm, wm, km, kkm, kam = cols(r_ref, False), cols(w_ref, True), cols(k_ref, False), cols(kk_ref, False), cols(ka_ref, False)
                vt = v_ref[pl.ds(base, ROWS), cs]
                s = s_ref[g]
                ys = []
                for q in range(ROWS):
                    st_ref[base + q, g] = s
                    sa = _colsum(s * kkm[q])
                    s = s * wm[q] - kam[q] * sa + km[q] * vt[q:q + 1, :]
                    ys.append(_colsum(s * rm[q]))
                y_ref[pl.ds(base, ROWS), cs] = _stack_rows(ys)
                s_ref[g] = s
            return carry

        lax.fori_loop(0, tc // ROWS, step, 0)
        pl.when(last)(wait)

    tok = pl.BlockSpec((tc, grp * LANE), lambda b, g, t: (b * nt + t, g))
    n = bsz * seq
    out = pl.pallas_call(
        body, name="rwkv_scan_fwd", grid=grid, in_specs=[tok] * 6 + [ANY] * ng,
        out_specs=[tok, pl.BlockSpec((tc, grp, HEAD, LANE), lambda b, g, t: (b * nt + t, g, 0, 0))] + [ANY] * ng,
        out_shape=[jax.ShapeDtypeStruct((n, nw * LANE), F32), jax.ShapeDtypeStruct((n, nw, HEAD, LANE), F32)]
        + [jax.ShapeDtypeStruct((N_DEV,) + a.shape, a.dtype) for a in gather],
        scratch_shapes=[pltpu.VMEM((grp, HEAD, LANE), F32)] + _exchange_sems(ng),
        compiler_params=_cparams(("arbitrary", "arbitrary", "arbitrary")),
    )(r, w, k, v, kk, ka, *gather)
    return out[0], out[1], out[2:]


def rwkv_scan_bwd(r, w, k, v, kk, ka, states, dy, bsz, seq, nw, scatter):
    grp = _scan_group(nw)
    tc = _tile(seq, SCAN_CHUNK, 8)
    nt = seq // tc
    nblk = tc // ROWS
    grid = (bsz, nw // grp, nt)
    ns = len(scatter)

    def body(*refs):
        r_ref, w_ref, k_ref, v_ref, kk_ref, ka_ref, st_ref, dy_ref = refs[:8]
        dr_ref, dw_ref, dk_ref, dv_ref, dkk_ref, dka_ref = refs[8 + ns:14 + ns]
        ds_ref = refs[14 + 2 * ns]
        start, wait = _direct_exchange(refs[8:8 + ns], refs[14 + ns:14 + 2 * ns], *refs[15 + 2 * ns:], scatter=True)
        first, last = _grid_ends(grid)
        pl.when(first)(start)

        @pl.when(pl.program_id(2) == 0)
        def _():
            ds_ref[...] = jnp.zeros_like(ds_ref)

        eye, ones, ones2 = _scan_consts()
        ones_f = _head_ones()

        def step(ii, carry):
            base = pl.multiple_of((nblk - 1 - ii) * ROWS, ROWS)
            rows = pl.ds(base, ROWS)
            for g in range(grp):
                cs = slice(g * LANE, (g + 1) * LANE)
                cols = lambda ref, two: _cols8(ref[rows, cs], eye, ones, ones2, two)
                rm, wm, km, kkm, kam = cols(r_ref, False), cols(w_ref, True), cols(k_ref, False), cols(kk_ref, False), cols(ka_ref, False)
                vt, dyt = v_ref[rows, cs], dy_ref[rows, cs]
                ds = ds_ref[g]
                dvs, sas, p_dr, p_dk, p_dka, p_dw, p_dkk = ([None] * ROWS for _ in range(7))
                for q in reversed(range(ROWS)):
                    sp = st_ref[base + q, g]
                    vrow, dyrow = vt[q:q + 1, :], dyt[q:q + 1, :]
                    sa = _colsum(sp * kkm[q])
                    sas[q] = sa
                    ds = ds + rm[q] * dyrow
                    dvs[q] = _colsum(ds * km[q])
                    dsa = -_colsum(ds * kam[q])
                    p_dr[q] = sp * dyrow
                    p_dk[q] = ds * vrow
                    p_dka[q] = -(ds * sa)
                    p_dw[q] = ds * sp
                    p_dkk[q] = sp * dsa
                    ds = ds * wm[q] + kkm[q] * dsa
                ds_ref[g] = ds
                dv_ref[rows, cs] = _stack_rows(dvs)
                sa_dy = _hsum(_stack_rows(sas) * dyt, ones_f)
                v_dy = _hsum(vt * dyt, ones_f)
                dr_ref[rows, cs] = w_ref[rows, cs] * _rows8(p_dr, eye, ones) - ka_ref[rows, cs] * sa_dy + k_ref[rows, cs] * v_dy
                dk_ref[rows, cs] = _rows8(p_dk, eye, ones)
                dka_ref[rows, cs] = _rows8(p_dka, eye, ones)
                dw_ref[rows, cs] = _rows8(p_dw, eye, ones)
                dkk_ref[rows, cs] = _rows8(p_dkk, eye, ones)
            return carry

        lax.fori_loop(0, nblk, step, 0)
        pl.when(last)(wait)

    tok = pl.BlockSpec((tc, grp * LANE), lambda b, g, t: (b * nt + nt - 1 - t, g))
    n = bsz * seq
    out = pl.pallas_call(
        body, name="rwkv_scan_bwd", grid=grid,
        in_specs=[tok] * 6 + [pl.BlockSpec((tc, grp, HEAD, LANE), lambda b, g, t: (b * nt + nt - 1 - t, g, 0, 0)), tok] + [ANY] * ns,
        out_specs=[tok] * 6 + [ANY] * ns,
        out_shape=[jax.ShapeDtypeStruct((n, nw * LANE), F32)] * 6 + [jax.ShapeDtypeStruct(a.shape, a.dtype) for a in scatter],
        scratch_shapes=[pltpu.VMEM((grp, HEAD, LANE), F32)] + _exchange_sems(ns),
        compiler_params=_cparams(("arbitrary", "arbitrary", "arbitrary")),
    )(r, w, k, v, kk, ka, states, dy, *scatter)
    return out[:6], out[6:]


def _rwkv_post_math(y, r, k2, v, ln_g, ln_b, r_k, ones):
    d = y - _hsum(y, ones) * (1.0 / HEAD)
    rstd = lax.rsqrt(_hsum(d * d, ones) * (1.0 / HEAD) + GN_EPS)
    yn = d * rstd
    s = _hsum(r * k2 * r_k, ones)
    return yn, rstd, s, yn * ln_g + ln_b + s * v


def _rwkv_post_specs(tm, nw):
    tok = pl.BlockSpec((tm, LANE), lambda i, j: (i, j))
    par = pl.BlockSpec((1, LANE), lambda i, j: (0, j))
    return [tok] * 4 + [pl.BlockSpec((tm, LANE), lambda i, j: (i, 7 * nw + j)), par, par, par]


def rwkv_post_fwd(y, r, k2, v, proj, prm, nw):
    n = y.shape[0]
    tm = _tile(n, 512, 8)

    def body(y_ref, r_ref, k_ref, v_ref, g_ref, lg, lb, rk, o_ref):
        ob = _rwkv_post_math(y_ref[...], r_ref[...], k_ref[...], v_ref[...], lg[...], lb[...], rk[...], _head_ones())[3]
        o_ref[...] = (ob * _silu_and_grad(g_ref[...])[0]).astype(BF16)

    return pl.pallas_call(
        body, name="rwkv_post_fwd", grid=(n // tm, nw), in_specs=_rwkv_post_specs(tm, nw),
        out_specs=pl.BlockSpec((tm, LANE), lambda i, j: (i, j)), out_shape=jax.ShapeDtypeStruct((n, nw * LANE), BF16),
        compiler_params=_cparams(("parallel", "parallel")),
    )(y, r, k2, v, proj, prm["ln_g"], prm["ln_b"], prm["r_k"])


def rwkv_post_bwd(y, r, k2, v, proj, prm, dycat, nw):
    n = y.shape[0]
    tm = _tile(n, 512, 8)
    wid = nw * LANE

    def body(y_ref, r_ref, k_ref, v_ref, g_ref, lg, lb, rk, dyc_ref,
             dy_ref, dr_ref, dk_ref, dv_ref, dg_ref, dlg, dlb, drk):
        i, j = pl.program_id(0), pl.program_id(1)
        cols = pl.ds(pl.multiple_of(j * LANE, LANE), LANE)

        @pl.when(jnp.logical_and(i == 0, j == 0))
        def _():
            for ref in (dlg, dlb, drk):
                ref[...] = jnp.zeros_like(ref)

        ones = _head_ones()
        rr, kr, vr = r_ref[...], k_ref[...], v_ref[...]
        yn, rstd, s, ob = _rwkv_post_math(y_ref[...], rr, kr, vr, lg[...], lb[...], rk[...], ones)
        silu, dsilu = _silu_and_grad(g_ref[...])
        dyc = dyc_ref[...].astype(F32)
        dg_ref[...] = (dyc * ob * dsilu).astype(BF16)
        dob = dyc * silu
        dlg[:, cols] += _colsum(dob * yn)
        dlb[:, cols] += _colsum(dob)
        dyn = dob * lg[...]
        dy_ref[...] = rstd * (dyn - _hsum(dyn, ones) * (1.0 / HEAD) - yn * _hsum(dyn * yn, ones) * (1.0 / HEAD))
        dv_ref[...] = dob * s
        dsum = _hsum(dob * vr, ones)
        dr_ref[...] = dsum * kr * rk[...]
        dk_ref[...] = dsum * rr * rk[...]
        drk[:, cols] += _colsum(dsum * rr * kr)

    tok = pl.BlockSpec((tm, LANE), lambda i, j: (i, j))
    whole = pl.BlockSpec((1, wid), lambda i, j: (0, 0))
    return pl.pallas_call(
        body, name="rwkv_post_bwd", grid=(n // tm, nw),
        in_specs=_rwkv_post_specs(tm, nw) + [pl.BlockSpec((tm, LANE), lambda i, j: (i, nw + j))],
        out_specs=[tok] * 5 + [whole] * 3,
        out_shape=[jax.ShapeDtypeStruct((n, wid), F32)] * 4 + [jax.ShapeDtypeStruct((n, wid), BF16)]
        + [jax.ShapeDtypeStruct((1, wid), F32)] * 3,
        compiler_params=_cparams(("arbitrary", "arbitrary")),
    )(y, r, k2, v, proj, prm["ln_g"], prm["ln_b"], prm["r_k"], dycat)


def _sgu_math(blk, ln_g, ln_b, ws_ref, bb_ref, mixed_ref, d, ngr):
    u, v, g = blk[:, 0:d], blk[:, d:2 * d], blk[:, 2 * d:3 * d]
    gu, dgu = _gelu_and_grad(u)
    gv, dgv = _gelu_and_grad(v)
    cen = gv - jnp.mean(gv, axis=-1, keepdims=True)
    rstd = lax.rsqrt(jnp.mean(cen * cen, axis=-1, keepdims=True) + LN_EPS)
    vhat = cen * rstd
    vn = vhat * ln_g + ln_b
    tril = _iota((GMLP_CHUNK, GMLP_CHUNK), 0) >= _iota((GMLP_CHUNK, GMLP_CHUNK), 1)
    for gi in range(ngr):
        cs = slice(gi * LANE, (gi + 1) * LANE)
        wm = jnp.where(tril, ws_ref[gi], 0.0)
        mixed_ref[:, cs] = jnp.dot(wm, vn[:, cs], precision=HI, preferred_element_type=F32) + bb_ref[gi]
    return dict(g=g, gu=gu, dgu=dgu, dgv=dgv, rstd=rstd, vhat=vhat, vn=vn, tril=tril)


def sgu_fwd(proj, ln_g, ln_b, w_s, b_bc):
    n, d3 = proj.shape
    d = d3 // 3
    ngr = d // LANE

    def body(p_ref, lg, lb, ws_ref, bb_ref, o_ref, mixed_ref):
        m = _sgu_math(p_ref[...], lg[...], lb[...], ws_ref, bb_ref, mixed_ref, d, ngr)
        o_ref[...] = (m["gu"] * mixed_ref[...] * _silu_and_grad(m["g"])[0]).astype(BF16)

    return pl.pallas_call(
        body, name="sgu_fwd", grid=(n // GMLP_CHUNK,),
        in_specs=[_row_spec(GMLP_CHUNK, d3), _full_spec((1, d)), _full_spec((1, d)),
                  _full_spec((ngr, GMLP_CHUNK, GMLP_CHUNK)), _full_spec((ngr, GMLP_CHUNK, LANE))],
        out_specs=_row_spec(GMLP_CHUNK, d), out_shape=jax.ShapeDtypeStruct((n, d), BF16),
        scratch_shapes=[pltpu.VMEM((GMLP_CHUNK, d), F32)], compiler_params=_cparams(("parallel",)),
    )(proj, ln_g, ln_b, w_s, b_bc)


def sgu_bwd(proj, ln_g, ln_b, w_s, b_bc, dyin):
    n, d3 = proj.shape
    d = d3 // 3
    ngr = d // LANE
    nsteps = n // GMLP_CHUNK

    def body(p_ref, lg, lb, ws_ref, bb_ref, dy_ref, dp_ref, dws_ref, dbs_ref, dlg_ref, dlb_ref,
             mixed_ref, dvn_ref, dbacc_ref):
        step = pl.program_id(0)

        @pl.when(step == 0)
        def _():
            for ref in (dws_ref, dlg_ref, dlb_ref, dbacc_ref):
                ref[...] = jnp.zeros_like(ref)

        m = _sgu_math(p_ref[...], lg[...], lb[...], ws_ref, bb_ref, mixed_ref, d, ngr)
        silu, dsilu = _silu_and_grad(m["g"])
        dyv = dy_ref[...].astype(F32)
        mixed = mixed_ref[...]
        dp_ref[:, 2 * d:3 * d] = (dyv * m["gu"] * mixed * dsilu).astype(BF16)
        doc = dyv * silu
        dp_ref[:, 0:d] = (doc * mixed * m["dgu"]).astype(BF16)
        dmixed = doc * m["gu"]
        for gi in range(ngr):
            cs = slice(gi * LANE, (gi + 1) * LANE)
            dm = dmixed[:, cs]
            wm = jnp.where(m["tril"], ws_ref[gi], 0.0)
            dws_ref[gi] += jnp.where(m["tril"], lax.dot_general(dm, m["vn"][:, cs], NT_DIMS, precision=HI,
                                                                preferred_element_type=F32), 0.0)
            dbacc_ref[gi] += dm
            dvn_ref[:, cs] = lax.dot_general(wm, dm, TN_DIMS, precision=HI, preferred_element_type=F32)
        dvn = dvn_ref[...]
        dlg_ref[...] += _colsum(dvn * m["vhat"])
        dlb_ref[...] += _colsum(dvn)
        dvh = dvn * lg[...]
        dgv = m["rstd"] * (dvh - jnp.mean(dvh, axis=-1, keepdims=True)
                           - m["vhat"] * jnp.mean(dvh * m["vhat"], axis=-1, keepdims=True))
        dp_ref[:, d:2 * d] = (dgv * m["dgv"]).astype(BF16)

        @pl.when(step == nsteps - 1)
        def _():
            dbs_ref[...] = jnp.sum(dbacc_ref[...], axis=-1, keepdims=True)

    return pl.pallas_call(
        body, name="sgu_bwd", grid=(nsteps,),
        in_specs=[_row_spec(GMLP_CHUNK, d3), _full_spec((1, d)), _full_spec((1, d)),
                  _full_spec((ngr, GMLP_CHUNK, GMLP_CHUNK)), _full_spec((ngr, GMLP_CHUNK, LANE)), _row_spec(GMLP_CHUNK, d)],
        out_specs=[_row_spec(GMLP_CHUNK, d3), _full_spec((ngr, GMLP_CHUNK, GMLP_CHUNK)), _full_spec((ngr, GMLP_CHUNK, 1)),
                   _full_spec((1, d)), _full_spec((1, d))],
        out_shape=[jax.ShapeDtypeStruct((n, d3), BF16), jax.ShapeDtypeStruct((ngr, GMLP_CHUNK, GMLP_CHUNK), F32),
                   jax.ShapeDtypeStruct((ngr, GMLP_CHUNK, 1), F32), jax.ShapeDtypeStruct((1, d), F32),
                   jax.ShapeDtypeStruct((1, d), F32)],
        scratch_shapes=[pltpu.VMEM((GMLP_CHUNK, d), F32), pltpu.VMEM((GMLP_CHUNK, d), F32),
                        pltpu.VMEM((ngr, GMLP_CHUNK, LANE), F32)],
        compiler_params=_cparams(("arbitrary",)),
    )(proj, ln_g, ln_b, w_s, b_bc, dyin)


def _ab_segments(wid, heads):
    return ((0, 3 * wid, 0),
            (3 * wid, 3 * wid + heads, 8 * wid + LANE),
            (3 * wid + heads, 4 * wid + heads, 3 * wid),
            (4 * wid + heads, 7 * wid + heads, 4 * wid),
            (7 * wid + heads, 7 * wid + heads + 2 * LORA, 8 * wid),
            (7 * wid + heads + 2 * LORA, 8 * wid + heads + 2 * LORA, 7 * wid))


def _ab_reorder(shards, wid, heads):
    per = shards[0].shape[1]
    pieces = []
    for o0, o1, _ in sorted(_ab_segments(wid, heads), key=lambda s: s[2]):
        for d, sh in enumerate(shards):
            lo, hi = max(o0, d * per), min(o1, (d + 1) * per)
            if lo < hi:
                pieces.append(sh[:, lo - d * per:hi - d * per])
    pieces.append(jnp.zeros((shards[0].shape[0], LANE - heads), shards[0].dtype))
    return jnp.concatenate(pieces, axis=1)


def _ab_restore(g, wid, heads, n_shards):
    per = (8 * wid + heads + 2 * LORA) // n_shards
    out = []
    for d in range(n_shards):
        pieces = []
        for o0, o1, r0 in _ab_segments(wid, heads):
            lo, hi = max(o0, d * per), min(o1, (d + 1) * per)
            if lo < hi:
                pieces.append(g[:, r0 + lo - o0:r0 + hi - o0])
        out.append(jnp.concatenate(pieces, axis=1))
    return out


ODD_PIECES = ("c_w_in", "c_w_out", "ple_w_gate1", "ple_w_proj1", "ab_w_out", "ple_w_gate0", "ple_w_proj0")
ODD_BY_ROWS = (0, 1, 1, 0, 1, 1, 0)


def local_step(x, p, target, wts, n_shards, odd_shards):
    bsz, seq, d = x.shape
    n = bsz * seq
    wid = d // 2
    nw = wid // LANE
    heads = wid // HEAD
    x2 = x.reshape(n, d)
    tgt = target.reshape(n, d)
    p0, p1 = p[0].reshape(n, -1), p[1].reshape(n, -1)
    row = lambda a: a.reshape(1, -1)
    g_pre0, g_pre1 = row(wts["norm_pre"][0]), row(wts["norm_pre"][1])
    g_post0, g_post1 = row(wts["norm_post"][0]), row(wts["norm_post"][1])
    w_ab = wts["ab_w_in_r"]
    rows_cut = lambda a: a.reshape((n_shards, a.shape[0] // n_shards) + a.shape[1:])
    fbias = jnp.pad(row(wts["fox_f_bias"]), ((0, 0), (0, LANE - heads)))
    mu = row(wts["rwkv_mu"])
    prm = dict(mu_r=mu[:, 0:wid], mu_k=mu[:, wid:2 * wid], mu_v=mu[:, 2 * wid:3 * wid], mu_wa=mu[:, 3 * wid:],
               w0=row(wts["rwkv_w0"]), w2=wts["rwkv_w2"], a0=row(wts["rwkv_a0"]), a2=wts["rwkv_a2"],
               k_k=row(wts["rwkv_k_k"]), k_a=row(wts["rwkv_k_a"]), r_k=row(wts["rwkv_r_k"]),
               ln_g=row(wts["rwkv_ln_g"]), ln_b=row(wts["rwkv_ln_b"]))
    c_ln_g, c_ln_b = row(wts["c_ln_g"]), row(wts["c_ln_b"])
    w_s = wts["c_w_s"]
    b_bc = jnp.broadcast_to(wts["c_b_s"][:, :, None], w_s.shape[:2] + (LANE,))

    xn0 = rms_fwd(x2, g_pre0, name="rms_pre0")
    proj0 = matmul(xn0, w_ab, name="ab_in")
    c = fox_gate_fwd(proj0, fbias, bsz, seq, nw)
    crow = jnp.pad(c.reshape(bsz, seq, LANE)[:, :, :heads].transpose(0, 2, 1).reshape(bsz, nw, 2, seq),
                   ((0, 0), (0, 0), (0, 6), (0, 0)))
    oa, ya = fox_attn_fwd(proj0, c, crow, bsz, seq, nw)
    sr, sw, sk, sv, skk, ska = rwkv_pre_fwd(proj0, prm, bsz, seq, nw)
    ysc, states, odd_w = rwkv_scan_fwd(sr, sw, sk, sv, skk, ska, bsz, seq, nw, [odd_shards[piece] for piece in ODD_PIECES])
    w_c, w_cout, wg1, wp1, w_out, wg0, wp0 = (_join_shards(a, 1 - by_rows) for a, by_rows in zip(odd_w, ODD_BY_ROWS))
    yb = rwkv_post_fwd(ysc, sr, sk, sv, proj0, prm, nw)
    ycat = jnp.concatenate([ya, yb], axis=1)
    y0 = matmul(ycat, w_out, name="ab_out")
    h1, h1b = rms_res_fwd(y0, x2, g_post0, name="rms_post0")
    a0 = matmul(p0, wp0, name="ple_proj0")
    z0 = matmul(h1b, wg0, name="ple_gate0")
    h1p = ple_fwd(h1, a0, z0, name="ple_fwd0")
    xn1 = rms_fwd(h1p, g_pre1, name="rms_pre1")
    proj1 = matmul(xn1, w_c, name="c_in")
    yin = sgu_fwd(proj1, c_ln_g, c_ln_b, w_s, b_bc)
    y1 = matmul(yin, w_cout, name="c_out")
    h2, h2b = rms_res_fwd(y1, h1p, g_post1, name="rms_post1")
    a1 = matmul(p1, wp1, name="ple_proj1")
    z1 = matmul(h2b, wg1, name="ple_gate1")
    h2p = ple_fwd(h2, a1, z1, name="ple_fwd1")
    dh, loss = loss_head(h2p, tgt, name="loss_head")

    g = {}
    da1, dz1 = ple_bwd(dh, a1, z1, name="ple_bwd1")
    g["ple_w_proj1"] = matmul(p1, da1, ta=True, out_dtype=BF16, col_blocks=n_shards, name="d_ple_proj1")
    g["ple_w_gate1"] = rows_cut(matmul(h2b, dz1, ta=True, out_dtype=BF16, name="d_ple_gate1"))
    dh2 = matmul(dz1, wg1, tb=True, add=dh, name="dx_ple_gate1")
    dy1, g_post1_g = rms_bwd(y1, dh2, g_post1, out_dtype=BF16, name="rms_post1_bwd")
    g["c_w_out"] = rows_cut(matmul(yin, dy1, ta=True, out_dtype=BF16, name="d_c_out"))
    dyin = matmul(dy1, w_cout, tb=True, out_dtype=BF16, name="dx_c_out")
    dproj1, g["c_w_s"], g_bs, g["c_ln_g"], g["c_ln_b"] = sgu_bwd(proj1, c_ln_g, c_ln_b, w_s, b_bc, dyin)
    g["c_b_s"] = g_bs.reshape(w_s.shape[:2])
    g["c_w_in"] = matmul(xn1, dproj1, ta=True, out_dtype=BF16, col_blocks=n_shards, name="d_c_in")
    dxn1 = matmul(dproj1, w_c, tb=True, out_dtype=BF16, name="dx_c_in")
    dh1p, g_pre1_g = rms_bwd(h1p, dxn1, g_pre1, res=dh2, name="rms_pre1_bwd")
    da0, dz0 = ple_bwd(dh1p, a0, z0, name="ple_bwd0")
    g["ple_w_proj0"] = matmul(p0, da0, ta=True, out_dtype=BF16, col_blocks=n_shards, name="d_ple_proj0")
    g["ple_w_gate0"] = rows_cut(matmul(h1b, dz0, ta=True, out_dtype=BF16, name="d_ple_gate0"))
    dh1 = matmul(dz0, wg0, tb=True, add=dh1p, name="dx_ple_gate0")
    dy0, g_post0_g = rms_bwd(y0, dh1, g_post0, out_dtype=BF16, name="rms_post0_bwd")
    g["ab_w_out"] = rows_cut(matmul(ycat, dy0, ta=True, out_dtype=BF16, name="d_ab_out"))
    dycat = matmul(dy0, w_out, tb=True, out_dtype=BF16, name="dx_ab_out")
    dysc, dr_b, dk_b, dv_b, dgb, g["rwkv_ln_g"], g["rwkv_ln_b"], g_rk = rwkv_post_bwd(ysc, sr, sk, sv, proj0, prm, dycat, nw)
    g["rwkv_r_k"] = g_rk.reshape(wts["rwkv_r_k"].shape)
    grads_scan, odd_landed = rwkv_scan_bwd(sr, sw, sk, sv, skk, ska, states, dysc, bsz, seq, nw, [g.pop(piece) for piece in ODD_PIECES])
    (dxr, dxk, dxv, dxwa, dmur, dmuk, dmuv, dmuwa, g["rwkv_w0"], g["rwkv_w2"], g["rwkv_a0"], g["rwkv_a2"],
     g["rwkv_k_k"], g["rwkv_k_a"]) = rwkv_pre_bwd(proj0, prm, grads_scan, (dr_b, dk_b, dv_b), bsz, seq, nw)
    g["rwkv_mu"] = jnp.concatenate([dmur, dmuk, dmuv, dmuwa], axis=1)
    dq, dk, dv, dga, dcrow = fox_attn_bwd(proj0, c, crow, oa, dycat, bsz, seq, nw)
    dc = jnp.pad(dcrow[:, :, :2, :].reshape(bsz, heads, seq).transpose(0, 2, 1), ((0, 0), (0, 0), (0, LANE - heads)))
    dfa, g_fb = fox_gate_bwd(proj0, fbias, dc.reshape(n, LANE), bsz, seq, nw)
    g["fox_f_bias"] = g_fb[:, :heads]
    dproj0 = jnp.concatenate([dq, dk, dv, dga, dxr, dxk, dxv, dgb, dxwa, dfa], axis=1)
    g["ab_w_in"] = jnp.stack(_ab_restore(matmul(xn0, dproj0, ta=True, out_dtype=BF16, name="d_ab_in"), wid, heads, n_shards))
    dxn0 = matmul(dproj0, w_ab, tb=True, out_dtype=BF16, name="dx_ab_in")
    dx, g_pre0_g = rms_bwd(x2, dxn0, g_pre0, res=dh1, name="rms_pre0_bwd")

    g["norm_pre"] = jnp.concatenate([g_pre0_g, g_pre1_g], axis=0)
    g["norm_post"] = jnp.concatenate([g_post0_g, g_post1_g], axis=0)
    return loss, dx.reshape(bsz, seq, d), g, dict(zip(ODD_PIECES, odd_landed))


MESH = pl.DeviceIdType.MESH
ANY = pl.BlockSpec(memory_space=pl.ANY)
PACK_COLS = 1024
PACK_ROWS = 16


def _mesh_place():
    xi, yi, ci = lax.axis_index("x"), lax.axis_index("y"), lax.axis_index("c")
    return xi, yi, ci, 4 * xi + 2 * yi + ci


def _peer(xi, yi, ci, m):
    px = 1 - xi if m & 4 else xi
    py = 1 - yi if m & 2 else yi
    pc = 1 - ci if m & 1 else ci
    return (px, py, pc), 4 * px + 2 * py + pc


def _gather(arrays, *, name):
    n = len(arrays)

    def body(*refs):
        ins, outs = refs[:n], refs[n:2 * n]
        send_sems, recv_sems, local_sems = refs[2 * n:]
        xi, yi, ci, me = _mesh_place()
        sibling = (xi, yi, 1 - ci)
        chips = [(1 - xi, yi), (xi, 1 - yi), (1 - xi, 1 - yi)]
        block = lambda px, py, pc: 4 * px + 2 * py + pc

        def copy(a, k, blk, to, src=None):
            return pltpu.make_async_remote_copy(src_ref=outs[a].at[blk] if src is None else src, dst_ref=outs[a].at[blk],
                                                send_sem=send_sems.at[k, a], recv_sem=recv_sems.at[k, a],
                                                device_id=to, device_id_type=MESH)

        started = []
        for a in range(n):
            cp = pltpu.make_async_copy(ins[a], outs[a].at[me], local_sems.at[a])
            cp.start()
            started.append(cp)
        sends = []
        for a in range(n):
            sends.append(copy(a, 0, me, sibling, src=ins[a]))
            sends += [copy(a, 1 + j, me, (*chip, ci), src=ins[a]) for j, chip in enumerate(chips)]
        for cp in sends:
            cp.start()
        for j, chip in enumerate(chips):
            for a in range(n):
                copy(a, 1 + j, block(*chip, ci), (xi, yi, ci)).wait_recv()
                fwd = copy(a, 4 + j, block(*chip, ci), sibling)
                fwd.start()
                sends.append(fwd)
        for a in range(n):
            copy(a, 0, block(xi, yi, 1 - ci), (xi, yi, ci)).wait_recv()
            for j, chip in enumerate(chips):
                copy(a, 4 + j, block(*chip, 1 - ci), (xi, yi, ci)).wait_recv()
        for cp in sends:
            cp.wait_send()
        for cp in started:
            cp.wait()

    return pl.pallas_call(
        body, name=name, in_specs=[ANY] * n, out_specs=[ANY] * n,
        out_shape=[jax.ShapeDtypeStruct((N_DEV,) + a.shape, a.dtype) for a in arrays],
        scratch_shapes=[pltpu.SemaphoreType.DMA((N_DEV - 1, n)), pltpu.SemaphoreType.DMA((N_DEV - 1, n)),
                        pltpu.SemaphoreType.DMA((n,))],
    )(*arrays)


def _scatter(arrays, *, name):
    n = len(arrays)

    def body(*refs):
        start, wait = _direct_exchange(refs[:n], refs[n:2 * n], *refs[2 * n:], scatter=True)
        start()
        wait()

    return pl.pallas_call(
        body, name=name, in_specs=[ANY] * n, out_specs=[ANY] * n, out_shape=[jax.ShapeDtypeStruct(a.shape, a.dtype) for a in arrays],
        scratch_shapes=_exchange_sems(n),
    )(*arrays)


def _exchange_sems(n):
    if n == 0:
        return []
    return [pltpu.SemaphoreType.DMA((N_DEV - 1, n)), pltpu.SemaphoreType.DMA((N_DEV - 1, n)), pltpu.SemaphoreType.DMA((n,))]


def _direct_exchange(ins, outs, send_sems=None, recv_sems=None, local_sems=None, *, scatter):
    n = len(ins)

    def copies():
        xi, yi, ci, me = _mesh_place()
        src = lambda a, idx: ins[a].at[idx] if scatter else ins[a]
        local = [pltpu.make_async_copy(src(a, me), outs[a].at[me], local_sems.at[a]) for a in range(n)]
        sends, arrivals = [], []
        for m in range(1, N_DEV):
            peer, pidx = _peer(xi, yi, ci, m)
            for a in range(n):
                pair = dict(send_sem=send_sems.at[m - 1, a], recv_sem=recv_sems.at[m - 1, a], device_id=peer, device_id_type=MESH)
                sends.append(pltpu.make_async_remote_copy(src_ref=src(a, pidx), dst_ref=outs[a].at[me], **pair))
                arrivals.append(pltpu.make_async_remote_copy(src_ref=src(a, pidx), dst_ref=outs[a].at[pidx], **pair))
        return local, sends, arrivals

    def start():
        local, sends, _ = copies()
        for cp in local + sends:
            cp.start()

    def wait():
        local, sends, arrivals = copies()
        for cp in arrivals:
            cp.wait_recv()
        for cp in sends:
            cp.wait_send()
        for cp in local:
            cp.wait()

    return start, wait


def sum_blocks(x, *, name):
    _, rows, cols = x.shape
    tr = _tile(rows, 256, PACK_ROWS)

    def body(x_ref, o_ref):
        acc = x_ref[0].astype(F32)
        for s in range(1, N_DEV):
            acc = acc + x_ref[s].astype(F32)
        o_ref[...] = acc

    return pl.pallas_call(
        body, name=name, grid=(rows // tr,), in_specs=[pl.BlockSpec((N_DEV, tr, cols), lambda i: (0, i, 0))],
        out_specs=pl.BlockSpec((tr, cols), lambda i: (i, 0)), out_shape=jax.ShapeDtypeStruct((rows, cols), F32),
        compiler_params=_cparams(("parallel",)),
    )(x)


def adamw(w, parts, m, v, *, name):
    rows, cols = w.shape
    nparts = parts.shape[0]
    tr = _tile(rows, 256, 16)
    c1 = 1.0 / (1.0 - ADAM_B1 ** ADAM_STEP)
    c2 = 1.0 / (1.0 - ADAM_B2 ** ADAM_STEP)

    def body(w_ref, p_ref, m_ref, v_ref, g_ref, d_ref, mo_ref, vo_ref):
        gv = p_ref[0].astype(F32)
        for s in range(1, nparts):
            gv = gv + p_ref[s].astype(F32)
        mn = ADAM_B1 * m_ref[...] + (1.0 - ADAM_B1) * gv
        vn = ADAM_B2 * v_ref[...] + (1.0 - ADAM_B2) * (gv * gv)
        g_ref[...] = gv
        d_ref[...] = -ADAM_LR * ((mn * c1) / (jnp.sqrt(vn * c2) + ADAM_EPS) + ADAM_WD * w_ref[...])
        mo_ref[...] = mn
        vo_ref[...] = vn

    spec = pl.BlockSpec((tr, cols), lambda i: (i, 0))
    return pl.pallas_call(
        body, name=name, grid=(rows // tr,), in_specs=[spec, pl.BlockSpec((nparts, tr, cols), lambda i: (0, i, 0)), spec, spec],
        out_specs=[spec] * 4, out_shape=[jax.ShapeDtypeStruct((rows, cols), F32)] * 4, compiler_params=_cparams(("parallel",)),
    )(w, parts, m, v)


def _pack(arrays, dtype):
    flat = jnp.concatenate([a.astype(dtype).reshape(-1) for a in arrays])
    unit = PACK_COLS * PACK_ROWS
    total = -(-flat.shape[0] // unit) * unit
    return jnp.pad(flat, (0, total - flat.shape[0])).reshape(total // PACK_COLS, PACK_COLS)


def _unpack(flat2d, shapes, lead=()):
    flat = flat2d.reshape(lead + (-1,))
    out, off = [], 0
    for shp in shapes:
        size = 1
        for s in shp:
            size *= s
        out.append(flat[..., off:off + size].reshape(lead + tuple(shp)))
        off += size
    return out


def _join_shards(sh, axis):
    return jnp.concatenate([sh[d] for d in range(N_DEV)], axis=axis)


BIG = (("ab_w_in", "ab_w_in", None, False), ("c_w_in", "c_w_in", None, False), ("ab_w_out", "ab_w_out", None, True),
       ("c_w_out", "c_w_out", None, True), ("ple_w_gate0", "ple_w_gate", 0, True), ("ple_w_gate1", "ple_w_gate", 1, True),
       ("ple_w_proj0", "ple_w_proj", 0, False), ("ple_w_proj1", "ple_w_proj", 1, False))
SMALL_SHARDED = (("rwkv_w2", 1), ("rwkv_a2", 1), ("c_ln_g", 0), ("c_ln_b", 0))
REPLICATED = ("norm_pre", "norm_post", "fox_f_bias", "rwkv_mu", "rwkv_w0", "rwkv_a0", "rwkv_k_k", "rwkv_k_a", "rwkv_r_k",
              "rwkv_ln_g", "rwkv_ln_b", "c_w_s", "c_b_s")
WEIGHTS = ("norm_pre", "norm_post", "ab_w_in", "fox_f_bias", "rwkv_mu", "rwkv_w0", "rwkv_w2", "rwkv_a0", "rwkv_a2", "rwkv_k_k",
           "rwkv_k_a", "rwkv_r_k", "rwkv_ln_g", "rwkv_ln_b", "ab_w_out", "c_w_in", "c_ln_g", "c_ln_b", "c_w_s", "c_b_s",
           "c_w_out", "ple_w_proj", "ple_w_gate")
SQUEEZED = ("norm_pre", "norm_post", "ple_w_proj", "ple_w_gate")


def _step(x, p, loss_target, w, mom, vel):
    sq = {k: (a if k in SQUEEZED else a[0]) for k, a in w.items()}
    _, _, _, me = _mesh_place()

    wid, heads = x.shape[-1] // 2, x.shape[-1] // 2 // HEAD
    layer = lambda a, l: a if l is None else a[l]

    shards = {piece: layer(sq[wname], l).astype(BF16) for piece, wname, l, _ in BIG}
    even = [entry for entry in BIG if entry[0] not in ODD_PIECES]
    sends = [shards[piece] for piece, _, _, _ in even]
    sends += [sq[k] if sq[k].ndim == 2 else sq[k].reshape(1, -1) for k, _ in SMALL_SHARDED]
    gathered = _gather(sends, name="gather_weights")
    full = {}
    for (piece, _, _, by_rows), got in zip(even, gathered):
        if piece == "ab_w_in":
            full["ab_w_in_r"] = _ab_reorder([got[d] for d in range(N_DEV)], wid, heads)
        else:
            full[piece] = _join_shards(got, 1 - by_rows)
    for (k, ax), got in zip(SMALL_SHARDED, gathered[len(even):]):
        full[k] = _join_shards(got, 1).reshape(-1) if sq[k].ndim == 1 else _join_shards(got, 1)
    for k in REPLICATED:
        full[k] = sq[k]

    loss, grad_x, g, parts = local_step(x, p, loss_target, full, N_DEV, {piece: shards[piece] for piece in ODD_PIECES})

    landed = _scatter([g[piece] for piece, _, _, _ in even], name="exchange_grads")
    parts.update({piece: a for (piece, _, _, _), a in zip(even, landed)})

    small_names = [k for k, _ in SMALL_SHARDED] + list(REPLICATED)
    small_full_shapes = [full[k].shape for k in small_names] + [(1, 1)]
    partial = _gather([_pack([g[k].reshape(full[k].shape) for k in small_names] + [loss], F32)], name="gather_small_grads")[0]
    summed = _unpack(sum_blocks(partial, name="sum_small_grads"), small_full_shapes)
    small_grads = dict(zip(small_names, summed[:-1]))
    for k, ax in SMALL_SHARDED:
        width = sq[k].shape[ax]
        small_grads[k] = lax.dynamic_slice_in_dim(small_grads[k], me * width, width, axis=ax)
    loss_all = summed[-1][0, 0]

    outs_g, outs_d, outs_m, outs_v = [], [], [], []
    for k in WEIGHTS:
        shape = w[k].shape
        two_d = lambda a: a.reshape(-1, a.shape[-1])
        pieces = [(piece, l) for piece, wname, l, _ in BIG if wname == k]
        if not pieces:
            res = adamw(two_d(w[k]), two_d(small_grads[k].reshape(shape))[None], two_d(mom[k]), two_d(vel[k]), name="adamw_" + k)
        elif pieces[0][1] is None:
            res = adamw(two_d(w[k]), parts[k], two_d(mom[k]), two_d(vel[k]), name="adamw_" + k)
        else:
            per_layer = [adamw(w[k][l], parts[piece], mom[k][l], vel[k][l], name="adamw_" + piece) for piece, l in pieces]
            res = [jnp.stack(r) for r in zip(*per_layer)]
        for out, r in zip((outs_g, outs_d, outs_m, outs_v), res):
            out.append(r.reshape(shape))
    return (loss_all, grad_x, *outs_g, *outs_d, *outs_m, *outs_v)


def kernel(x, p, norm_pre, norm_post, ab_w_in, fox_f_bias, rwkv_mu, rwkv_w0, rwkv_w2, rwkv_a0, rwkv_a2, rwkv_k_k, rwkv_k_a, rwkv_r_k, rwkv_ln_g, rwkv_ln_b, ab_w_out, c_w_in, c_ln_g, c_ln_b, c_w_s, c_b_s, c_w_out, ple_w_proj, ple_w_gate, loss_target, m_norm_pre, m_norm_post, m_ab_w_in, m_fox_f_bias, m_rwkv_mu, m_rwkv_w0, m_rwkv_w2, m_rwkv_a0, m_rwkv_a2, m_rwkv_k_k, m_rwkv_k_a, m_rwkv_r_k, m_rwkv_ln_g, m_rwkv_ln_b, m_ab_w_out, m_c_w_in, m_c_ln_g, m_c_ln_b, m_c_w_s, m_c_b_s, m_c_w_out, m_ple_w_proj, m_ple_w_gate, v_norm_pre, v_norm_post, v_ab_w_in, v_fox_f_bias, v_rwkv_mu, v_rwkv_w0, v_rwkv_w2, v_rwkv_a0, v_rwkv_a2, v_rwkv_k_k, v_rwkv_k_a, v_rwkv_r_k, v_rwkv_ln_g, v_rwkv_ln_b, v_ab_w_out, v_c_w_in, v_c_ln_g, v_c_ln_b, v_c_w_s, v_c_b_s, v_c_w_out, v_ple_w_proj, v_ple_w_gate):
    w = dict(norm_pre=norm_pre, norm_post=norm_post, ab_w_in=ab_w_in, fox_f_bias=fox_f_bias, rwkv_mu=rwkv_mu, rwkv_w0=rwkv_w0, rwkv_w2=rwkv_w2, rwkv_a0=rwkv_a0, rwkv_a2=rwkv_a2, rwkv_k_k=rwkv_k_k, rwkv_k_a=rwkv_k_a, rwkv_r_k=rwkv_r_k, rwkv_ln_g=rwkv_ln_g, rwkv_ln_b=rwkv_ln_b, ab_w_out=ab_w_out, c_w_in=c_w_in, c_ln_g=c_ln_g, c_ln_b=c_ln_b, c_w_s=c_w_s, c_b_s=c_b_s, c_w_out=c_w_out, ple_w_proj=ple_w_proj, ple_w_gate=ple_w_gate)
    mom = dict(norm_pre=m_norm_pre, norm_post=m_norm_post, ab_w_in=m_ab_w_in, fox_f_bias=m_fox_f_bias, rwkv_mu=m_rwkv_mu, rwkv_w0=m_rwkv_w0, rwkv_w2=m_rwkv_w2, rwkv_a0=m_rwkv_a0, rwkv_a2=m_rwkv_a2, rwkv_k_k=m_rwkv_k_k, rwkv_k_a=m_rwkv_k_a, rwkv_r_k=m_rwkv_r_k, rwkv_ln_g=m_rwkv_ln_g, rwkv_ln_b=m_rwkv_ln_b, ab_w_out=m_ab_w_out, c_w_in=m_c_w_in, c_ln_g=m_c_ln_g, c_ln_b=m_c_ln_b, c_w_s=m_c_w_s, c_b_s=m_c_b_s, c_w_out=m_c_w_out, ple_w_proj=m_ple_w_proj, ple_w_gate=m_ple_w_gate)
    vel = dict(norm_pre=v_norm_pre, norm_post=v_norm_post, ab_w_in=v_ab_w_in, fox_f_bias=v_fox_f_bias, rwkv_mu=v_rwkv_mu, rwkv_w0=v_rwkv_w0, rwkv_w2=v_rwkv_w2, rwkv_a0=v_rwkv_a0, rwkv_a2=v_rwkv_a2, rwkv_k_k=v_rwkv_k_k, rwkv_k_a=v_rwkv_k_a, rwkv_r_k=v_rwkv_r_k, rwkv_ln_g=v_rwkv_ln_g, rwkv_ln_b=v_rwkv_ln_b, ab_w_out=v_ab_w_out, c_w_in=v_c_w_in, c_ln_g=v_c_ln_g, c_ln_b=v_c_ln_b, c_w_s=v_c_w_s, c_b_s=v_c_b_s, c_w_out=v_c_w_out, ple_w_proj=v_ple_w_proj, ple_w_gate=v_ple_w_gate)
    return _step(x, p, loss_target, w, mom, vel)
```

```python
import functools

import jax
import jax.numpy as jnp
from jax import lax
from jax.experimental import pallas as pl
from jax.experimental.pallas import tpu as pltpu

F32 = jnp.float32
BF16 = jnp.bfloat16
HI = lax.Precision.HIGHEST

HEAD = 64
LANE = 128
LORA = 64
GMLP_CHUNK = 128
RMS_EPS = 1e-6
LN_EPS = 1e-5
GN_EPS = 64e-5
VMEM_LIMIT = 56 * 1024 * 1024
N_DEV = 8

ADAM_LR = 0.001
ADAM_B1 = 0.9
ADAM_B2 = 0.999
ADAM_EPS = 1e-08
ADAM_WD = 0.01
ADAM_STEP = 10

NT_DIMS = (((1,), (1,)), ((), ()))
TN_DIMS = (((0,), (0,)), ((), ()))


def _cparams(sem):
    return pltpu.CompilerParams(dimension_semantics=sem, vmem_limit_bytes=VMEM_LIMIT)


def _tile(n, target, q=LANE):
    if n <= target:
        return n
    best = None
    for d in range(q, target + 1, q):
        if n % d == 0:
            best = d
    assert best is not None, (n, target, q)
    return best


def _sigmoid(x):
    return 1.0 / (1.0 + jnp.exp(-x))


def _silu_and_grad(x):
    s = _sigmoid(x)
    return x * s, s * (1.0 + x * (1.0 - s))


def _gelu_and_grad(x):
    cdf = 0.5 * (1.0 + lax.erf(x * 0.7071067811865476))
    pdf = jnp.exp(-0.5 * x * x) * 0.3989422804014327
    return x * cdf, cdf + x * pdf


def _iota(shape, dim):
    return lax.broadcasted_iota(jnp.int32, shape, dim)


def _head_ones():
    return (_iota((LANE, LANE), 0) // HEAD == _iota((LANE, LANE), 1) // HEAD).astype(F32)


def _head_eye():
    return (_iota((HEAD, LANE), 0) == _iota((HEAD, LANE), 1) % HEAD).astype(F32)


def _hsum(x, ones):
    hi = x.astype(BF16)
    lo = (x - hi.astype(F32)).astype(BF16)
    ones_b = ones.astype(BF16)
    return jnp.dot(jnp.concatenate([hi, lo], axis=1), jnp.concatenate([ones_b, ones_b], axis=0), preferred_element_type=F32)


def _colsum(x):
    return jnp.sum(x, axis=0, keepdims=True)


def matmul(a, b, *, name, ta=False, tb=False, add=None, out_dtype=F32, tm=1024, tn=1024, tk=2048, col_blocks=1, scatter=()):
    m, k = (a.shape[1], a.shape[0]) if ta else a.shape
    n = b.shape[0] if tb else b.shape[1]
    assert (b.shape[1] if tb else b.shape[0]) == k
    tm, tn, tk = _tile(m, tm), _tile(n // col_blocks, tn), _tile(k, tk)
    nk = k // tk
    per = n // col_blocks // tn
    grid = (m // tm, n // tn, nk)
    ns = len(scatter)
    n_in = 2 + (add is not None)

    def body(*refs):
        a_ref, b_ref = refs[:2]
        add_ref = refs[2] if add is not None else None
        o_ref = refs[n_in + ns]
        acc_ref = refs[n_in + 2 * ns + 1]
        if ns:
            start, wait = _direct_exchange(refs[n_in:n_in + ns], refs[n_in + ns + 1:n_in + 2 * ns + 1], *refs[n_in + 2 * ns + 2:],
                                           scatter=True)
            first, last = _grid_ends(grid)
            pl.when(first)(start)
        kk = pl.program_id(2)

        @pl.when(kk == 0)
        def _():
            acc_ref[...] = jnp.zeros_like(acc_ref)

        dims = (((0 if ta else 1,), (1 if tb else 0,)), ((), ()))
        acc_ref[...] += lax.dot_general(a_ref[...].astype(BF16), b_ref[...].astype(BF16), dims,
                                        preferred_element_type=F32)

        @pl.when(kk == nk - 1)
        def _():
            r = acc_ref[...]
            if add is not None:
                r = r + add_ref[...].astype(F32)
            o_ref[...] = r.astype(out_dtype)

        if ns:
            pl.when(last)(wait)

    a_spec = pl.BlockSpec((tk, tm), lambda i, j, kk: (kk, i)) if ta else pl.BlockSpec((tm, tk), lambda i, j, kk: (i, kk))
    b_spec = pl.BlockSpec((tn, tk), lambda i, j, kk: (j, kk)) if tb else pl.BlockSpec((tk, tn), lambda i, j, kk: (kk, j))
    o_spec = pl.BlockSpec((tm, tn), lambda i, j, kk: (i, j))
    in_specs = [a_spec, b_spec] + ([o_spec] if add is not None else [])
    args = (a, b) + ((add,) if add is not None else ())
    out_shape = (m, n)
    if col_blocks > 1:
        assert add is None
        o_spec = pl.BlockSpec((None, tm, tn), lambda i, j, kk: (j // per, i, j % per))
        out_shape = (col_blocks, m, n // col_blocks)
    out = pl.pallas_call(
        body, name=name, grid=grid, in_specs=in_specs + [ANY] * ns, out_specs=[o_spec] + [ANY] * ns,
        out_shape=[jax.ShapeDtypeStruct(out_shape, out_dtype)] + [jax.ShapeDtypeStruct(s.shape, s.dtype) for s in scatter],
        scratch_shapes=[pltpu.VMEM((tm, tn), F32)] + _exchange_sems(ns),
        compiler_params=_cparams(("arbitrary",) * 3 if ns else ("parallel", "parallel", "arbitrary")),
    )(*args, *scatter)
    return (out[0], out[1:]) if ns else out[0]


def _row_spec(tm, width, cb=0):
    return pl.BlockSpec((tm, width), lambda i: (i, cb))


def _full_spec(shape):
    return pl.BlockSpec(shape, lambda i: (0,) * len(shape))


def rms_fwd(x, g, *, name):
    n, d = x.shape
    tm = _tile(n, 256, 8)

    def body(x_ref, g_ref, o_ref):
        xv = x_ref[...]
        r = lax.rsqrt(jnp.mean(xv * xv, axis=-1, keepdims=True) + RMS_EPS)
        o_ref[...] = (xv * r * g_ref[...]).astype(BF16)

    return pl.pallas_call(
        body, name=name, grid=(n // tm,), in_specs=[_row_spec(tm, d), _full_spec((1, d))],
        out_specs=_row_spec(tm, d), out_shape=jax.ShapeDtypeStruct((n, d), BF16),
        compiler_params=_cparams(("parallel",)),
    )(x, g)


def rms_res_fwd(y, res, g, *, name):
    n, d = y.shape
    tm = _tile(n, 256, 8)

    def body(y_ref, res_ref, g_ref, h_ref, hb_ref):
        yv = y_ref[...]
        r = lax.rsqrt(jnp.mean(yv * yv, axis=-1, keepdims=True) + RMS_EPS)
        h = res_ref[...] + yv * r * g_ref[...]
        h_ref[...] = h
        hb_ref[...] = h.astype(BF16)

    return pl.pallas_call(
        body, name=name, grid=(n // tm,), in_specs=[_row_spec(tm, d), _row_spec(tm, d), _full_spec((1, d))],
        out_specs=[_row_spec(tm, d), _row_spec(tm, d)],
        out_shape=[jax.ShapeDtypeStruct((n, d), F32), jax.ShapeDtypeStruct((n, d), BF16)],
        compiler_params=_cparams(("parallel",)),
    )(y, res, g)


def rms_bwd(x, dy, g, *, name, res=None, out_dtype=F32):
    n, d = x.shape
    tm = _tile(n, 256, 8)

    def body(*refs):
        if res is None:
            x_ref, dy_ref, g_ref, dx_ref, dg_ref = refs
        else:
            x_ref, dy_ref, g_ref, res_ref, dx_ref, dg_ref = refs

        @pl.when(pl.program_id(0) == 0)
        def _():
            dg_ref[...] = jnp.zeros_like(dg_ref)

        xv = x_ref[...]
        dyv = dy_ref[...].astype(F32)
        r = lax.rsqrt(jnp.mean(xv * xv, axis=-1, keepdims=True) + RMS_EPS)
        xhat = xv * r
        dxh = dyv * g_ref[...]
        dx = r * (dxh - xhat * jnp.mean(dxh * xhat, axis=-1, keepdims=True))
        if res is not None:
            dx = dx + res_ref[...]
        dx_ref[...] = dx.astype(out_dtype)
        dg_ref[...] += _colsum(dyv * xhat)

    in_specs = [_row_spec(tm, d), _row_spec(tm, d), _full_spec((1, d))] + ([_row_spec(tm, d)] if res is not None else [])
    args = (x, dy, g) + ((res,) if res is not None else ())
    return pl.pallas_call(
        body, name=name, grid=(n // tm,), in_specs=in_specs, out_specs=[_row_spec(tm, d), _full_spec((1, d))],
        out_shape=[jax.ShapeDtypeStruct((n, d), out_dtype), jax.ShapeDtypeStruct((1, d), F32)],
        compiler_params=_cparams(("arbitrary",)),
    )(*args)


def ple_fwd(h, a, z, *, name):
    n, d = h.shape
    tm = _tile(n, 256, 8)

    def body(h_ref, a_ref, z_ref, o_ref):
        o_ref[...] = h_ref[...] + a_ref[...] * _sigmoid(z_ref[...])

    return pl.pallas_call(
        body, name=name, grid=(n // tm,), in_specs=[_row_spec(tm, d)] * 3, out_specs=_row_spec(tm, d),
        out_shape=jax.ShapeDtypeStruct((n, d), F32), compiler_params=_cparams(("parallel",)),
    )(h, a, z)


def ple_bwd(dh, a, z, *, name):
    n, d = dh.shape
    tm = _tile(n, 256, 8)

    def body(dh_ref, a_ref, z_ref, da_ref, dz_ref):
        s = _sigmoid(z_ref[...])
        dhv = dh_ref[...]
        da_ref[...] = (dhv * s).astype(BF16)
        dz_ref[...] = (dhv * a_ref[...] * s * (1.0 - s)).astype(BF16)

    return pl.pallas_call(
        body, name=name, grid=(n // tm,), in_specs=[_row_spec(tm, d)] * 3, out_specs=[_row_spec(tm, d)] * 2,
        out_shape=[jax.ShapeDtypeStruct((n, d), BF16)] * 2, compiler_params=_cparams(("parallel",)),
    )(dh, a, z)


def loss_head(h, target, *, name):
    n, d = h.shape
    tm = _tile(n, 256, 8)

    def body(h_ref, t_ref, dh_ref, loss_ref):
        @pl.when(pl.program_id(0) == 0)
        def _():
            loss_ref[...] = jnp.zeros_like(loss_ref)

        e = h_ref[...] - t_ref[...]
        dh_ref[...] = e * (1.0 / d)
        loss_ref[...] += 0.5 * jnp.sum(jnp.sum(e * e, axis=-1, keepdims=True) * (1.0 / d), axis=0, keepdims=True)

    return pl.pallas_call(
        body, name=name, grid=(n // tm,), in_specs=[_row_spec(tm, d)] * 2,
        out_specs=[_row_spec(tm, d), _full_spec((1, 1))],
        out_shape=[jax.ShapeDtypeStruct((n, d), F32), jax.ShapeDtypeStruct((1, 1), F32)],
        compiler_params=_cparams(("arbitrary",)),
    )(h, target)


def _log_sigmoid(z):
    return jnp.minimum(z, 0.0) - jnp.log1p(jnp.exp(-jnp.abs(z)))


def fox_gate_fwd(proj, fbias, bsz, seq, nw):
    cc = _tile(seq, 256, 8)

    def body(f_ref, b_ref, c_ref):
        low = (_iota((cc, cc), 0) >= _iota((cc, cc), 1)).astype(F32)
        carry = jnp.zeros((1, LANE), F32)
        for ci in range(seq // cc):
            rows = slice(ci * cc, (ci + 1) * cc)
            lf = _log_sigmoid(f_ref[rows, :] + b_ref[...])
            c_ref[rows, :] = jnp.dot(low, lf, precision=HI, preferred_element_type=F32) + carry
            carry = carry + _colsum(lf)

    return pl.pallas_call(
        body, name="fox_gate_fwd", grid=(bsz,),
        in_specs=[pl.BlockSpec((seq, LANE), lambda b: (b, 8 * nw + 1)), _full_spec((1, LANE))],
        out_specs=pl.BlockSpec((seq, LANE), lambda b: (b, 0)),
        out_shape=jax.ShapeDtypeStruct((bsz * seq, LANE), F32), compiler_params=_cparams(("parallel",)),
    )(proj, fbias)


def fox_gate_bwd(proj, fbias, dc, bsz, seq, nw):
    cc = _tile(seq, 256, 8)
    nc = seq // cc

    def body(f_ref, b_ref, dc_ref, df_ref, db_ref):
        @pl.when(pl.program_id(0) == 0)
        def _():
            db_ref[...] = jnp.zeros_like(db_ref)

        upp = (_iota((cc, cc), 0) <= _iota((cc, cc), 1)).astype(F32)
        carry = jnp.zeros((1, LANE), F32)
        dbias = jnp.zeros((1, LANE), F32)
        for ci in reversed(range(nc)):
            rows = slice(ci * cc, (ci + 1) * cc)
            blk = dc_ref[rows, :]
            dlf = jnp.dot(upp, blk, precision=HI, preferred_element_type=F32) + carry
            carry = carry + _colsum(blk)
            df = dlf * _sigmoid(-(f_ref[rows, :] + b_ref[...]))
            df_ref[rows, :] = df.astype(BF16)
            dbias = dbias + _colsum(df)
        db_ref[...] += dbias

    return pl.pallas_call(
        body, name="fox_gate_bwd", grid=(bsz,),
        in_specs=[pl.BlockSpec((seq, LANE), lambda b: (b, 8 * nw + 1)), _full_spec((1, LANE)),
                  pl.BlockSpec((seq, LANE), lambda b: (b, 0))],
        out_specs=[pl.BlockSpec((seq, LANE), lambda b: (b, 0)), _full_spec((1, LANE))],
        out_shape=[jax.ShapeDtypeStruct((bsz * seq, LANE), BF16), jax.ShapeDtypeStruct((1, LANE), F32)],
        compiler_params=_cparams(("arbitrary",)),
    )(proj, fbias, dc)


def _attn_scores(qs, kh, ccol, crow, r0, kend, tq):
    s = lax.dot_general(qs[r0:kend], kh[:kend], NT_DIMS, preferred_element_type=F32)
    s = s + ccol[r0:kend] - crow[:, :kend]
    causal = _iota((tq, kend), 1) <= r0 + _iota((tq, kend), 0)
    return jnp.where(causal, s, -jnp.inf)


def _head_column(c_blk, h):
    return jnp.sum(jnp.where(_iota(c_blk.shape, 1) == h, c_blk, 0.0), axis=1, keepdims=True)


def fox_attn_fwd(proj, c, crow, bsz, seq, nw):
    tq = _tile(seq, 256, 8)
    scale = HEAD ** -0.5

    def body(q_ref, k_ref, v_ref, g_ref, c_ref, crow_ref, o_ref, y_ref):
        hp = pl.program_id(1)
        c_blk = c_ref[...]
        for hh in range(2):
            sl = slice(hh * HEAD, (hh + 1) * HEAD)
            ccol = _head_column(c_blk, 2 * hp + hh)
            crow_h = crow_ref[0, 0, hh:hh + 1, :]
            qs = (q_ref[:, sl] * scale).astype(BF16)
            kh = k_ref[:, sl].astype(BF16)
            vh = v_ref[:, sl].astype(BF16)
            for qi in range(seq // tq):
                r0, kend = qi * tq, (qi + 1) * tq
                s = _attn_scores(qs, kh, ccol, crow_h, r0, kend, tq)
                p = jnp.exp(s - jnp.max(s, axis=-1, keepdims=True))
                o = jnp.dot(p.astype(BF16), vh[:kend], preferred_element_type=F32) / jnp.sum(p, axis=-1, keepdims=True)
                o_ref[r0:kend, sl] = o
                y_ref[r0:kend, sl] = (o * _silu_and_grad(g_ref[r0:kend, sl])[0]).astype(BF16)

    blk = lambda cb: pl.BlockSpec((seq, LANE), lambda b, j: (b, cb * nw + j))
    return pl.pallas_call(
        body, name="fox_attn_fwd", grid=(bsz, nw),
        in_specs=[blk(0), blk(1), blk(2), blk(3), pl.BlockSpec((seq, LANE), lambda b, j: (b, 0)),
                  pl.BlockSpec((1, 1, 8, seq), lambda b, j: (b, j, 0, 0))],
        out_specs=[pl.BlockSpec((seq, LANE), lambda b, j: (b, j))] * 2,
        out_shape=[jax.ShapeDtypeStruct((bsz * seq, nw * LANE), F32), jax.ShapeDtypeStruct((bsz * seq, nw * LANE), BF16)],
        compiler_params=_cparams(("parallel", "parallel")),
    )(proj, proj, proj, proj, c, crow)


def fox_attn_bwd(proj, c, crow, oa, dycat, bsz, seq, nw):
    tq = _tile(seq, 256, 8)
    scale = HEAD ** -0.5

    def body(q_ref, k_ref, v_ref, g_ref, c_ref, crow_ref, o_ref, dy_ref,
             dq_ref, dk_ref, dv_ref, dg_ref, dc_ref, dk_acc, dv_acc, dc_acc):
        hp = pl.program_id(1)
        c_blk = c_ref[...]
        dc_ref[...] = jnp.zeros_like(dc_ref)
        for hh in range(2):
            sl = slice(hh * HEAD, (hh + 1) * HEAD)
            ccol = _head_column(c_blk, 2 * hp + hh)
            crow_h = crow_ref[0, 0, hh:hh + 1, :]
            qs = (q_ref[:, sl] * scale).astype(BF16)
            kh = k_ref[:, sl].astype(BF16)
            vh = v_ref[:, sl].astype(BF16)
            oh = o_ref[:, sl]
            dyh = dy_ref[:, sl].astype(F32)
            silu, dsilu = _silu_and_grad(g_ref[:, sl])
            dg_ref[:, sl] = (dyh * oh * dsilu).astype(BF16)
            do = dyh * silu
            dvec = jnp.sum(do * oh, axis=-1, keepdims=True)
            dob = do.astype(BF16)
            dk_acc[...] = jnp.zeros_like(dk_acc)
            dv_acc[...] = jnp.zeros_like(dv_acc)
            dc_acc[...] = jnp.zeros_like(dc_acc)
            for qi in range(seq // tq):
                r0, kend = qi * tq, (qi + 1) * tq
                s = _attn_scores(qs, kh, ccol, crow_h, r0, kend, tq)
                p = jnp.exp(s - jnp.max(s, axis=-1, keepdims=True))
                p = p / jnp.sum(p, axis=-1, keepdims=True)
                dp = lax.dot_general(dob[r0:kend], vh[:kend], NT_DIMS, preferred_element_type=F32)
                ds = p * (dp - dvec[r0:kend])
                dsb = ds.astype(BF16)
                dq_ref[r0:kend, sl] = (jnp.dot(dsb, kh[:kend], preferred_element_type=F32) * scale).astype(BF16)
                dk_acc[0:kend, :] += lax.dot_general(dsb, qs[r0:kend], TN_DIMS, preferred_element_type=F32)
                dv_acc[0:kend, :] += lax.dot_general(p.astype(BF16), dob[r0:kend], TN_DIMS, preferred_element_type=F32)
                dc_acc[:, 0:kend] += -_colsum(ds)
                ds_lo = (ds - dsb.astype(F32)).astype(BF16)
                rowsum = lax.dot_general(jnp.ones((8, 2 * kend), BF16), jnp.concatenate([dsb, ds_lo], axis=1), NT_DIMS,
                                         preferred_element_type=F32)
                dc_acc[:, r0:kend] += rowsum[0:1, :]
            dk_ref[:, sl] = dk_acc[...].astype(BF16)
            dv_ref[:, sl] = dv_acc[...].astype(BF16)
            dc_ref[0, 0, hh:hh + 1, :] = dc_acc[...]

    blk = lambda cb: pl.BlockSpec((seq, LANE), lambda b, j: (b, cb * nw + j))
    own = pl.BlockSpec((seq, LANE), lambda b, j: (b, j))
    n = bsz * seq
    return pl.pallas_call(
        body, name="fox_attn_bwd", grid=(bsz, nw),
        in_specs=[blk(0), blk(1), blk(2), blk(3), pl.BlockSpec((seq, LANE), lambda b, j: (b, 0)),
                  pl.BlockSpec((1, 1, 8, seq), lambda b, j: (b, j, 0, 0)), own, own],
        out_specs=[own] * 4 + [pl.BlockSpec((1, 1, 8, seq), lambda b, j: (b, j, 0, 0))],
        out_shape=[jax.ShapeDtypeStruct((n, nw * LANE), BF16)] * 4 + [jax.ShapeDtypeStruct((bsz, nw, 8, seq), F32)],
        scratch_shapes=[pltpu.VMEM((seq, HEAD), F32), pltpu.VMEM((seq, HEAD), F32), pltpu.VMEM((1, seq), F32)],
        compiler_params=_cparams(("parallel", "parallel")),
    )(proj, proj, proj, proj, c, crow, oa, dycat)


def _shift_mix(x, mu):
    prev = jnp.where(_iota(x.shape, 0) == 0, 0.0, pltpu.roll(x, 1, 0))
    return x + (prev - x) * mu, prev


def _rwkv_pre_math(r_in, k_in, v_in, wa_in, mu_r, mu_k, mu_v, mu_wa, w0, w2, a0, a2, k_k, k_a, ones):
    r, prev_r = _shift_mix(r_in, mu_r)
    k, prev_k = _shift_mix(k_in, mu_k)
    v, prev_v = _shift_mix(v_in, mu_v)
    wa, prev_wa = _shift_mix(wa_in, mu_wa)
    tw = jnp.tanh(wa[:, :LORA])
    alo = wa[:, LORA:]
    sg = _sigmoid(w0 + jnp.dot(tw, w2, precision=HI, preferred_element_type=F32))
    e = sg * 0.6065306597126334
    w = jnp.exp(-e)
    a = _sigmoid(a0 + jnp.dot(alo, a2, precision=HI, preferred_element_type=F32))
    kkraw = k * k_k
    nrm = jnp.sqrt(_hsum(kkraw * kkraw, ones))
    den = jnp.maximum(nrm, 1e-12)
    kk = kkraw / den
    k2 = k * (1.0 + (a - 1.0) * k_a)
    return dict(r=r, k=k, v=v, prev_r=prev_r, prev_k=prev_k, prev_v=prev_v, prev_wa=prev_wa, tw=tw, alo=alo,
                sg=sg, e=e, w=w, a=a, nrm=nrm, den=den, kk=kk, k2=k2)


def _rwkv_pre_specs(seq, nw):
    tok = lambda cb: pl.BlockSpec((seq, LANE), lambda b, j: (b, cb * nw + j))
    par = pl.BlockSpec((1, LANE), lambda b, j: (0, j))
    lora = pl.BlockSpec((LORA, LANE), lambda b, j: (0, j))
    return [tok(4), tok(5), tok(6), pl.BlockSpec((seq, LANE), lambda b, j: (b, 8 * nw)),
            par, par, par, pl.BlockSpec((1, LANE), lambda b, j: (0, 0)), par, lora, par, lora, par, par]


def rwkv_pre_fwd(proj, prm, bsz, seq, nw):
    def body(r_ref, k_ref, v_ref, wa_ref, mur, muk, muv, muwa, w0, w2, a0, a2, kk_, ka_,
             ro, wo, ko, vo, kko, kao):
        m = _rwkv_pre_math(r_ref[...], k_ref[...], v_ref[...], wa_ref[...], mur[...], muk[...], muv[...], muwa[...],
                           w0[...], w2[...], a0[...], a2[...], kk_[...], ka_[...], _head_ones())
        ro[...] = m["r"]
        wo[...] = m["w"]
        ko[...] = m["k2"]
        vo[...] = m["v"]
        kko[...] = m["kk"]
        kao[...] = m["kk"] * m["a"]

    n = bsz * seq
    own = pl.BlockSpec((seq, LANE), lambda b, j: (b, j))
    return pl.pallas_call(
        body, name="rwkv_pre_fwd", grid=(bsz, nw), in_specs=_rwkv_pre_specs(seq, nw), out_specs=[own] * 6,
        out_shape=[jax.ShapeDtypeStruct((n, nw * LANE), F32)] * 6, compiler_params=_cparams(("parallel", "parallel")),
    )(proj, proj, proj, proj, prm["mu_r"], prm["mu_k"], prm["mu_v"], prm["mu_wa"], prm["w0"], prm["w2"],
      prm["a0"], prm["a2"], prm["k_k"], prm["k_a"])


def rwkv_pre_bwd(proj, prm, grads_scan, grads_bonus, bsz, seq, nw):
    wid = nw * LANE

    def body(r_ref, k_ref, v_ref, wa_ref, mur, muk, muv, muwa, w0, w2, a0, a2, kk_, ka_,
             dr_s, dw_s, dk_s, dv_s, dkk_s, dka_s, dr_b, dk_b, dv_b,
             dxr, dxk, dxv, dxwa, dmur, dmuk, dmuv, dmuwa, dw0, dw2, da0, da2, dkk_p, dka_p, dwa_acc):
        b, j = pl.program_id(0), pl.program_id(1)
        cols = pl.ds(pl.multiple_of(j * LANE, LANE), LANE)
        acc_refs = (dmur, dmuk, dmuv, dmuwa, dw0, dw2, da0, da2, dkk_p, dka_p)

        @pl.when(jnp.logical_and(b == 0, j == 0))
        def _():
            for ref in acc_refs:
                ref[...] = jnp.zeros_like(ref)

        @pl.when(j == 0)
        def _():
            dwa_acc[...] = jnp.zeros_like(dwa_acc)

        ones = _head_ones()
        r_in, k_in, v_in, wa_in = r_ref[...], k_ref[...], v_ref[...], wa_ref[...]
        m = _rwkv_pre_math(r_in, k_in, v_in, wa_in, mur[...], muk[...], muv[...], muwa[...],
                           w0[...], w2[...], a0[...], a2[...], kk_[...], ka_[...], ones)
        k, a, kk = m["k"], m["a"], m["kk"]
        dr = dr_s[...] + dr_b[...]
        dk2 = dk_s[...] + dk_b[...]
        dv = dv_s[...] + dv_b[...]
        dka = dka_s[...]
        da = dka * kk + dk2 * k * ka_[...]
        dkk = dkk_s[...] + dka * a
        dkkraw = jnp.where(m["nrm"] > 1e-12, dkk - kk * _hsum(dkk * kk, ones), dkk) / m["den"]
        dk = dkkraw * kk_[...] + dk2 * (1.0 + (a - 1.0) * ka_[...])
        dkk_p[:, cols] += _colsum(dkkraw * k)
        dka_p[:, cols] += _colsum(dk2 * k * (a - 1.0))
        dza = da * a * (1.0 - a)
        da0[:, cols] += _colsum(dza)
        da2[:, cols] += lax.dot_general(m["alo"], dza, TN_DIMS, precision=HI, preferred_element_type=F32)
        dalo = lax.dot_general(dza, a2[...], NT_DIMS, precision=HI, preferred_element_type=F32)
        dzw = -dw_s[...] * m["w"] * m["e"] * (1.0 - m["sg"])
        dw0[:, cols] += _colsum(dzw)
        dw2[:, cols] += lax.dot_general(m["tw"], dzw, TN_DIMS, precision=HI, preferred_element_type=F32)
        dtw = lax.dot_general(dzw, w2[...], NT_DIMS, precision=HI, preferred_element_type=F32)
        dwa_acc[:, 0:LORA] += dtw * (1.0 - m["tw"] * m["tw"])
        dwa_acc[:, LORA:LANE] += dalo

        def shift_bwd(dxs, x, prev, mu):
            g = dxs * mu
            nxt = jnp.where(_iota(g.shape, 0) == seq - 1, 0.0, pltpu.roll(g, seq - 1, 0))
            return dxs * (1.0 - mu) + nxt, _colsum(dxs * (prev - x))

        for dxs, x, prev, mu, out, dmu in ((dr, r_in, m["prev_r"], mur, dxr, dmur), (dk, k_in, m["prev_k"], muk, dxk, dmuk),
                                           (dv, v_in, m["prev_v"], muv, dxv, dmuv)):
            dx, dm = shift_bwd(dxs, x, prev, mu[...])
            out[...] = dx.astype(BF16)
            dmu[:, cols] += dm

        @pl.when(j == nw - 1)
        def _():
            dx, dm = shift_bwd(dwa_acc[...], wa_in, m["prev_wa"], muwa[...])
            dxwa[...] = dx.astype(BF16)
            dmuwa[...] += dm

    n = bsz * seq
    own = pl.BlockSpec((seq, LANE), lambda b, j: (b, j))
    whole = lambda shape: pl.BlockSpec(shape, lambda b, j: (0,) * len(shape))
    row = jax.ShapeDtypeStruct((1, wid), F32)
    return pl.pallas_call(
        body, name="rwkv_pre_bwd", grid=(bsz, nw), in_specs=_rwkv_pre_specs(seq, nw) + [own] * 9,
        out_specs=[own] * 3 + [pl.BlockSpec((seq, LANE), lambda b, j: (b, 0))]
        + [whole((1, wid))] * 3 + [whole((1, LANE)), whole((1, wid)), whole((LORA, wid)), whole((1, wid)),
                                   whole((LORA, wid)), whole((1, wid)), whole((1, wid))],
        out_shape=[jax.ShapeDtypeStruct((n, wid), BF16)] * 3 + [jax.ShapeDtypeStruct((n, LANE), BF16)]
        + [row] * 3 + [jax.ShapeDtypeStruct((1, LANE), F32), row, jax.ShapeDtypeStruct((LORA, wid), F32), row,
                       jax.ShapeDtypeStruct((LORA, wid), F32), row, row],
        scratch_shapes=[pltpu.VMEM((seq, LANE), F32)],
        compiler_params=_cparams(("arbitrary", "arbitrary")),
    )(proj, proj, proj, proj, prm["mu_r"], prm["mu_k"], prm["mu_v"], prm["mu_wa"], prm["w0"], prm["w2"],
      prm["a0"], prm["a2"], prm["k_k"], prm["k_a"], *grads_scan, *grads_bonus)


SCAN_CHUNK = 64


def _scan_group(nw):
    return _tile(nw, 8, 1)


ROWS = 8


def _scan_consts():
    ones = _head_ones().astype(BF16)
    return _head_eye(), ones, jnp.concatenate([ones, ones], axis=0)


def _cols8(tile, eye, ones, ones2, two_parts):
    hi = tile.astype(BF16).astype(F32)
    place = lambda part, q: (part[q:q + 1, :] * eye).astype(BF16)
    if two_parts:
        lo = tile - hi
        lhs = jnp.concatenate([jnp.concatenate([place(hi, q), place(lo, q)], axis=1) for q in range(ROWS)], axis=0)
        big = jnp.dot(lhs, ones2, preferred_element_type=F32)
    else:
        big = jnp.dot(jnp.concatenate([place(hi, q) for q in range(ROWS)], axis=0), ones, preferred_element_type=F32)
    return [big[q * HEAD:(q + 1) * HEAD, :] for q in range(ROWS)]


def _rows8(mats, eye, ones):
    big = jnp.dot(jnp.concatenate([m.astype(BF16) for m in mats], axis=0), ones, preferred_element_type=F32)
    return _stack_rows([_colsum(big[q * HEAD:(q + 1) * HEAD, :] * eye) for q in range(ROWS)])


def _stack_rows(rows):
    rid = _iota((ROWS, LANE), 0)
    out = jnp.zeros((ROWS, LANE), F32)
    for q in range(ROWS):
        out = jnp.where(rid == q, rows[q], out)
    return out


def _grid_ends(grid):
    ids = [pl.program_id(i) for i in range(len(grid))]
    first = functools.reduce(jnp.logical_and, [i == 0 for i in ids])
    last = functools.reduce(jnp.logical_and, [i == g - 1 for i, g in zip(ids, grid)])
    return first, last


def rwkv_scan_fwd(r, w, k, v, kk, ka, bsz, seq, nw, gather):
    grp = _scan_group(nw)
    tc = _tile(seq, SCAN_CHUNK, 8)
    nt = seq // tc
    grid = (bsz, nw // grp, nt)
    ng = len(gather)

    def body(*refs):
        r_ref, w_ref, k_ref, v_ref, kk_ref, ka_ref = refs[:6]
        y_ref, st_ref = refs[6 + ng:8 + ng]
        s_ref = refs[8 + 2 * ng]
        start, wait = _direct_exchange(refs[6:6 + ng], refs[8 + ng:8 + 2 * ng], *refs[9 + 2 * ng:], scatter=False)
        first, last = _grid_ends(grid)
        pl.when(first)(start)

        @pl.when(pl.program_id(2) == 0)
        def _():
            s_ref[...] = jnp.zeros_like(s_ref)

        eye, ones, ones2 = _scan_consts()

        def step(i8, carry):
            base = pl.multiple_of(i8 * ROWS, ROWS)
            for g in range(grp):
                cs = slice(g * LANE, (g + 1) * LANE)
                cols = lambda ref, two: _cols8(ref[pl.ds(base, ROWS), cs], eye, ones, ones2, two)
                rm, wm, km, kkm, kam = cols(r_ref, False), cols(w_ref, True), cols(k_ref, False), cols(kk_ref, False), cols(ka_ref, False)
                vt = v_ref[pl.ds(base, ROWS), cs]
                s = s_ref[g]
                ys = []
                for q in range(ROWS):
                    st_ref[base + q, g] = s
                    sa = _colsum(s * kkm[q])
                    s = s * wm[q] - kam[q] * sa + km[q] * vt[q:q + 1, :]
                    ys.append(_colsum(s * rm[q]))
                y_ref[pl.ds(base, ROWS), cs] = _stack_rows(ys)
                s_ref[g] = s
            return carry

        lax.fori_loop(0, tc // ROWS, step, 0)
        pl.when(last)(wait)

    tok = pl.BlockSpec((tc, grp * LANE), lambda b, g, t: (b * nt + t, g))
    n = bsz * seq
    out = pl.pallas_call(
        body, name="rwkv_scan_fwd", grid=grid, in_specs=[tok] * 6 + [ANY] * ng,
        out_specs=[tok, pl.BlockSpec((tc, grp, HEAD, LANE), lambda b, g, t: (b * nt + t, g, 0, 0))] + [ANY] * ng,
        out_shape=[jax.ShapeDtypeStruct((n, nw * LANE), F32), jax.ShapeDtypeStruct((n, nw, HEAD, LANE), F32)]
        + [jax.ShapeDtypeStruct((N_DEV,) + a.shape, a.dtype) for a in gather],
        scratch_shapes=[pltpu.VMEM((grp, HEAD, LANE), F32)] + _exchange_sems(ng),
        compiler_params=_cparams(("arbitrary", "arbitrary", "arbitrary")),
    )(r, w, k, v, kk, ka, *gather)
    return out[0], out[1], out[2:]


def rwkv_scan_bwd(r, w, k, v, kk, ka, states, dy, bsz, seq, nw, scatter):
    grp = _scan_group(nw)
    tc = _tile(seq, SCAN_CHUNK, 8)
    nt = seq // tc
    nblk = tc // ROWS
    grid = (bsz, nw // grp, nt)
    ns = len(scatter)

    def body(*refs):
        r_ref, w_ref, k_ref, v_ref, kk_ref, ka_ref, st_ref, dy_ref = refs[:8]
        dr_ref, dw_ref, dk_ref, dv_ref, dkk_ref, dka_ref = refs[8 + ns:14 + ns]
        ds_ref = refs[14 + 2 * ns]
        start, wait = _direct_exchange(refs[8:8 + ns], refs[14 + ns:14 + 2 * ns], *refs[15 + 2 * ns:], scatter=True)
        first, last = _grid_ends(grid)
        pl.when(first)(start)

        @pl.when(pl.program_id(2) == 0)
        def _():
            ds_ref[...] = jnp.zeros_like(ds_ref)

        eye, ones, ones2 = _scan_consts()
        ones_f = _head_ones()

        def step(ii, carry):
            base = pl.multiple_of((nblk - 1 - ii) * ROWS, ROWS)
            rows = pl.ds(base, ROWS)
            for g in range(grp):
                cs = slice(g * LANE, (g + 1) * LANE)
                cols = lambda ref, two: _cols8(ref[rows, cs], eye, ones, ones2, two)
                rm, wm, km, kkm, kam = cols(r_ref, False), cols(w_ref, True), cols(k_ref, False), cols(kk_ref, False), cols(ka_ref, False)
                vt, dyt = v_ref[rows, cs], dy_ref[rows, cs]
                ds = ds_ref[g]
                dvs, sas, p_dr, p_dk, p_dka, p_dw, p_dkk = ([None] * ROWS for _ in range(7))
                for q in reversed(range(ROWS)):
                    sp = st_ref[base + q, g]
                    vrow, dyrow = vt[q:q + 1, :], dyt[q:q + 1, :]
                    sa = _colsum(sp * kkm[q])
                    sas[q] = sa
                    ds = ds + rm[q] * dyrow
                    dvs[q] = _colsum(ds * km[q])
                    dsa = -_colsum(ds * kam[q])
                    p_dr[q] = sp * dyrow
                    p_dk[q] = ds * vrow
                    p_dka[q] = -(ds * sa)
                    p_dw[q] = ds * sp
                    p_dkk[q] = sp * dsa
                    ds = ds * wm[q] + kkm[q] * dsa
                ds_ref[g] = ds
                dv_ref[rows, cs] = _stack_rows(dvs)
                sa_dy = _hsum(_stack_rows(sas) * dyt, ones_f)
                v_dy = _hsum(vt * dyt, ones_f)
                dr_ref[rows, cs] = w_ref[rows, cs] * _rows8(p_dr, eye, ones) - ka_ref[rows, cs] * sa_dy + k_ref[rows, cs] * v_dy
                dk_ref[rows, cs] = _rows8(p_dk, eye, ones)
                dka_ref[rows, cs] = _rows8(p_dka, eye, ones)
                dw_ref[rows, cs] = _rows8(p_dw, eye, ones)
                dkk_ref[rows, cs] = _rows8(p_dkk, eye, ones)
            return carry

        lax.fori_loop(0, nblk, step, 0)
        pl.when(last)(wait)

    tok = pl.BlockSpec((tc, grp * LANE), lambda b, g, t: (b * nt + nt - 1 - t, g))
    n = bsz * seq
    out = pl.pallas_call(
        body, name="rwkv_scan_bwd", grid=grid,
        in_specs=[tok] * 6 + [pl.BlockSpec((tc, grp, HEAD, LANE), lambda b, g, t: (b * nt + nt - 1 - t, g, 0, 0)), tok] + [ANY] * ns,
        out_specs=[tok] * 6 + [ANY] * ns,
        out_shape=[jax.ShapeDtypeStruct((n, nw * LANE), F32)] * 6 + [jax.ShapeDtypeStruct(a.shape, a.dtype) for a in scatter],
        scratch_shapes=[pltpu.VMEM((grp, HEAD, LANE), F32)] + _exchange_sems(ns),
        compiler_params=_cparams(("arbitrary", "arbitrary", "arbitrary")),
    )(r, w, k, v, kk, ka, states, dy, *scatter)
    return out[:6], out[6:]


def _rwkv_post_math(y, r, k2, v, ln_g, ln_b, r_k, ones):
    d = y - _hsum(y, ones) * (1.0 / HEAD)
    rstd = lax.rsqrt(_hsum(d * d, ones) * (1.0 / HEAD) + GN_EPS)
    yn = d * rstd
    s = _hsum(r * k2 * r_k, ones)
    return yn, rstd, s, yn * ln_g + ln_b + s * v


def _rwkv_post_specs(tm, nw):
    tok = pl.BlockSpec((tm, LANE), lambda i, j: (i, j))
    par = pl.BlockSpec((1, LANE), lambda i, j: (0, j))
    return [tok] * 4 + [pl.BlockSpec((tm, LANE), lambda i, j: (i, 7 * nw + j)), par, par, par]


def rwkv_post_fwd(y, r, k2, v, proj, prm, nw):
    n = y.shape[0]
    tm = _tile(n, 512, 8)

    def body(y_ref, r_ref, k_ref, v_ref, g_ref, lg, lb, rk, o_ref):
        ob = _rwkv_post_math(y_ref[...], r_ref[...], k_ref[...], v_ref[...], lg[...], lb[...], rk[...], _head_ones())[3]
        o_ref[...] = (ob * _silu_and_grad(g_ref[...])[0]).astype(BF16)

    return pl.pallas_call(
        body, name="rwkv_post_fwd", grid=(n // tm, nw), in_specs=_rwkv_post_specs(tm, nw),
        out_specs=pl.BlockSpec((tm, LANE), lambda i, j: (i, j)), out_shape=jax.ShapeDtypeStruct((n, nw * LANE), BF16),
        compiler_params=_cparams(("parallel", "parallel")),
    )(y, r, k2, v, proj, prm["ln_g"], prm["ln_b"], prm["r_k"])


def rwkv_post_bwd(y, r, k2, v, proj, prm, dycat, nw):
    n = y.shape[0]
    tm = _tile(n, 512, 8)
    wid = nw * LANE

    def body(y_ref, r_ref, k_ref, v_ref, g_ref, lg, lb, rk, dyc_ref,
             dy_ref, dr_ref, dk_ref, dv_ref, dg_ref, dlg, dlb, drk):
        i, j = pl.program_id(0), pl.program_id(1)
        cols = pl.ds(pl.multiple_of(j * LANE, LANE), LANE)

        @pl.when(jnp.logical_and(i == 0, j == 0))
        def _():
            for ref in (dlg, dlb, drk):
                ref[...] = jnp.zeros_like(ref)

        ones = _head_ones()
        rr, kr, vr = r_ref[...], k_ref[...], v_ref[...]
        yn, rstd, s, ob = _rwkv_post_math(y_ref[...], rr, kr, vr, lg[...], lb[...], rk[...], ones)
        silu, dsilu = _silu_and_grad(g_ref[...])
        dyc = dyc_ref[...].astype(F32)
        dg_ref[...] = (dyc * ob * dsilu).astype(BF16)
        dob = dyc * silu
        dlg[:, cols] += _colsum(dob * yn)
        dlb[:, cols] += _colsum(dob)
        dyn = dob * lg[...]
        dy_ref[...] = rstd * (dyn - _hsum(dyn, ones) * (1.0 / HEAD) - yn * _hsum(dyn * yn, ones) * (1.0 / HEAD))
        dv_ref[...] = dob * s
        dsum = _hsum(dob * vr, ones)
        dr_ref[...] = dsum * kr * rk[...]
        dk_ref[...] = dsum * rr * rk[...]
        drk[:, cols] += _colsum(dsum * rr * kr)

    tok = pl.BlockSpec((tm, LANE), lambda i, j: (i, j))
    whole = pl.BlockSpec((1, wid), lambda i, j: (0, 0))
    return pl.pallas_call(
        body, name="rwkv_post_bwd", grid=(n // tm, nw),
        in_specs=_rwkv_post_specs(tm, nw) + [pl.BlockSpec((tm, LANE), lambda i, j: (i, nw + j))],
        out_specs=[tok] * 5 + [whole] * 3,
        out_shape=[jax.ShapeDtypeStruct((n, wid), F32)] * 4 + [jax.ShapeDtypeStruct((n, wid), BF16)]
        + [jax.ShapeDtypeStruct((1, wid), F32)] * 3,
        compiler_params=_cparams(("arbitrary", "arbitrary")),
    )(y, r, k2, v, proj, prm["ln_g"], prm["ln_b"], prm["r_k"], dycat)


def _sgu_math(blk, ln_g, ln_b, ws_ref, bb_ref, mixed_ref, d, ngr):
    u, v, g = blk[:, 0:d], blk[:, d:2 * d], blk[:, 2 * d:3 * d]
    gu, dgu = _gelu_and_grad(u)
    gv, dgv = _gelu_and_grad(v)
    cen = gv - jnp.mean(gv, axis=-1, keepdims=True)
    rstd = lax.rsqrt(jnp.mean(cen * cen, axis=-1, keepdims=True) + LN_EPS)
    vhat = cen * rstd
    vn = vhat * ln_g + ln_b
    tril = _iota((GMLP_CHUNK, GMLP_CHUNK), 0) >= _iota((GMLP_CHUNK, GMLP_CHUNK), 1)
    for gi in range(ngr):
        cs = slice(gi * LANE, (gi + 1) * LANE)
        wm = jnp.where(tril, ws_ref[gi], 0.0)
        mixed_ref[:, cs] = jnp.dot(wm, vn[:, cs], precision=HI, preferred_element_type=F32) + bb_ref[gi]
    return dict(g=g, gu=gu, dgu=dgu, dgv=dgv, rstd=rstd, vhat=vhat, vn=vn, tril=tril)


def sgu_fwd(proj, ln_g, ln_b, w_s, b_bc):
    n, d3 = proj.shape
    d = d3 // 3
    ngr = d // LANE

    def body(p_ref, lg, lb, ws_ref, bb_ref, o_ref, mixed_ref):
        m = _sgu_math(p_ref[...], lg[...], lb[...], ws_ref, bb_ref, mixed_ref, d, ngr)
        o_ref[...] = (m["gu"] * mixed_ref[...] * _silu_and_grad(m["g"])[0]).astype(BF16)

    return pl.pallas_call(
        body, name="sgu_fwd", grid=(n // GMLP_CHUNK,),
        in_specs=[_row_spec(GMLP_CHUNK, d3), _full_spec((1, d)), _full_spec((1, d)),
                  _full_spec((ngr, GMLP_CHUNK, GMLP_CHUNK)), _full_spec((ngr, GMLP_CHUNK, LANE))],
        out_specs=_row_spec(GMLP_CHUNK, d), out_shape=jax.ShapeDtypeStruct((n, d), BF16),
        scratch_shapes=[pltpu.VMEM((GMLP_CHUNK, d), F32)], compiler_params=_cparams(("parallel",)),
    )(proj, ln_g, ln_b, w_s, b_bc)


def sgu_bwd(proj, ln_g, ln_b, w_s, b_bc, dyin):
    n, d3 = proj.shape
    d = d3 // 3
    ngr = d // LANE
    nsteps = n // GMLP_CHUNK

    def body(p_ref, lg, lb, ws_ref, bb_ref, dy_ref, dp_ref, dws_ref, dbs_ref, dlg_ref, dlb_ref,
             mixed_ref, dvn_ref, dbacc_ref):
        step = pl.program_id(0)

        @pl.when(step == 0)
        def _():
            for ref in (dws_ref, dlg_ref, dlb_ref, dbacc_ref):
                ref[...] = jnp.zeros_like(ref)

        m = _sgu_math(p_ref[...], lg[...], lb[...], ws_ref, bb_ref, mixed_ref, d, ngr)
        silu, dsilu = _silu_and_grad(m["g"])
        dyv = dy_ref[...].astype(F32)
        mixed = mixed_ref[...]
        dp_ref[:, 2 * d:3 * d] = (dyv * m["gu"] * mixed * dsilu).astype(BF16)
        doc = dyv * silu
        dp_ref[:, 0:d] = (doc * mixed * m["dgu"]).astype(BF16)
        dmixed = doc * m["gu"]
        for gi in range(ngr):
            cs = slice(gi * LANE, (gi + 1) * LANE)
            dm = dmixed[:, cs]
            wm = jnp.where(m["tril"], ws_ref[gi], 0.0)
            dws_ref[gi] += jnp.where(m["tril"], lax.dot_general(dm, m["vn"][:, cs], NT_DIMS, precision=HI,
                                                                preferred_element_type=F32), 0.0)
            dbacc_ref[gi] += dm
            dvn_ref[:, cs] = lax.dot_general(wm, dm, TN_DIMS, precision=HI, preferred_element_type=F32)
        dvn = dvn_ref[...]
        dlg_ref[...] += _colsum(dvn * m["vhat"])
        dlb_ref[...] += _colsum(dvn)
        dvh = dvn * lg[...]
        dgv = m["rstd"] * (dvh - jnp.mean(dvh, axis=-1, keepdims=True)
                           - m["vhat"] * jnp.mean(dvh * m["vhat"], axis=-1, keepdims=True))
        dp_ref[:, d:2 * d] = (dgv * m["dgv"]).astype(BF16)

        @pl.when(step == nsteps - 1)
        def _():
            dbs_ref[...] = jnp.sum(dbacc_ref[...], axis=-1, keepdims=True)

    return pl.pallas_call(
        body, name="sgu_bwd", grid=(nsteps,),
        in_specs=[_row_spec(GMLP_CHUNK, d3), _full_spec((1, d)), _full_spec((1, d)),
                  _full_spec((ngr, GMLP_CHUNK, GMLP_CHUNK)), _full_spec((ngr, GMLP_CHUNK, LANE)), _row_spec(GMLP_CHUNK, d)],
        out_specs=[_row_spec(GMLP_CHUNK, d3), _full_spec((ngr, GMLP_CHUNK, GMLP_CHUNK)), _full_spec((ngr, GMLP_CHUNK, 1)),
                   _full_spec((1, d)), _full_spec((1, d))],
        out_shape=[jax.ShapeDtypeStruct((n, d3), BF16), jax.ShapeDtypeStruct((ngr, GMLP_CHUNK, GMLP_CHUNK), F32),
                   jax.ShapeDtypeStruct((ngr, GMLP_CHUNK, 1), F32), jax.ShapeDtypeStruct((1, d), F32),
                   jax.ShapeDtypeStruct((1, d), F32)],
        scratch_shapes=[pltpu.VMEM((GMLP_CHUNK, d), F32), pltpu.VMEM((GMLP_CHUNK, d), F32),
                        pltpu.VMEM((ngr, GMLP_CHUNK, LANE), F32)],
        compiler_params=_cparams(("arbitrary",)),
    )(proj, ln_g, ln_b, w_s, b_bc, dyin)


def _ab_segments(wid, heads):
    return ((0, 3 * wid, 0),
            (3 * wid, 3 * wid + heads, 8 * wid + LANE),
            (3 * wid + heads, 4 * wid + heads, 3 * wid),
            (4 * wid + heads, 7 * wid + heads, 4 * wid),
            (7 * wid + heads, 7 * wid + heads + 2 * LORA, 8 * wid),
            (7 * wid + heads + 2 * LORA, 8 * wid + heads + 2 * LORA, 7 * wid))


def _ab_reorder(shards, wid, heads):
    per = shards[0].shape[1]
    pieces = []
    for o0, o1, _ in sorted(_ab_segments(wid, heads), key=lambda s: s[2]):
        for d, sh in enumerate(shards):
            lo, hi = max(o0, d * per), min(o1, (d + 1) * per)
            if lo < hi:
                pieces.append(sh[:, lo - d * per:hi - d * per])
    pieces.append(jnp.zeros((shards[0].shape[0], LANE - heads), shards[0].dtype))
    return jnp.concatenate(pieces, axis=1)


def _ab_restore(g, wid, heads, n_shards):
    per = (8 * wid + heads + 2 * LORA) // n_shards
    out = []
    for d in range(n_shards):
        pieces = []
        for o0, o1, r0 in _ab_segments(wid, heads):
            lo, hi = max(o0, d * per), min(o1, (d + 1) * per)
            if lo < hi:
                pieces.append(g[:, r0 + lo - o0:r0 + hi - o0])
        out.append(jnp.concatenate(pieces, axis=1))
    return out


ODD_PIECES = ("c_w_in", "c_w_out", "ple_w_gate1", "ple_w_proj1", "ab_w_out", "ple_w_gate0", "ple_w_proj0")
ODD_BY_ROWS = (0, 1, 1, 0, 1, 1, 0)


def local_step(x, p, target, wts, n_shards, odd_shards):
    bsz, seq, d = x.shape
    n = bsz * seq
    wid = d // 2
    nw = wid // LANE
    heads = wid // HEAD
    x2 = x.reshape(n, d)
    tgt = target.reshape(n, d)
    p0, p1 = p[0].reshape(n, -1), p[1].reshape(n, -1)
    row = lambda a: a.reshape(1, -1)
    g_pre0, g_pre1 = row(wts["norm_pre"][0]), row(wts["norm_pre"][1])
    g_post0, g_post1 = row(wts["norm_post"][0]), row(wts["norm_post"][1])
    w_ab = wts["ab_w_in_r"]
    rows_cut = lambda a: a.reshape((n_shards, a.shape[0] // n_shards) + a.shape[1:])
    fbias = jnp.pad(row(wts["fox_f_bias"]), ((0, 0), (0, LANE - heads)))
    mu = row(wts["rwkv_mu"])
    prm = dict(mu_r=mu[:, 0:wid], mu_k=mu[:, wid:2 * wid], mu_v=mu[:, 2 * wid:3 * wid], mu_wa=mu[:, 3 * wid:],
               w0=row(wts["rwkv_w0"]), w2=wts["rwkv_w2"], a0=row(wts["rwkv_a0"]), a2=wts["rwkv_a2"],
               k_k=row(wts["rwkv_k_k"]), k_a=row(wts["rwkv_k_a"]), r_k=row(wts["rwkv_r_k"]),
               ln_g=row(wts["rwkv_ln_g"]), ln_b=row(wts["rwkv_ln_b"]))
    c_ln_g, c_ln_b = row(wts["c_ln_g"]), row(wts["c_ln_b"])
    w_s = wts["c_w_s"]
    b_bc = jnp.broadcast_to(wts["c_b_s"][:, :, None], w_s.shape[:2] + (LANE,))

    xn0 = rms_fwd(x2, g_pre0, name="rms_pre0")
    proj0 = matmul(xn0, w_ab, name="ab_in")
    c = fox_gate_fwd(proj0, fbias, bsz, seq, nw)
    crow = jnp.pad(c.reshape(bsz, seq, LANE)[:, :, :heads].transpose(0, 2, 1).reshape(bsz, nw, 2, seq),
                   ((0, 0), (0, 0), (0, 6), (0, 0)))
    oa, ya = fox_attn_fwd(proj0, c, crow, bsz, seq, nw)
    sr, sw, sk, sv, skk, ska = rwkv_pre_fwd(proj0, prm, bsz, seq, nw)
    ysc, states, odd_w = rwkv_scan_fwd(sr, sw, sk, sv, skk, ska, bsz, seq, nw, [odd_shards[piece] for piece in ODD_PIECES])
    w_c, w_cout, wg1, wp1, w_out, wg0, wp0 = (_join_shards(a, 1 - by_rows) for a, by_rows in zip(odd_w, ODD_BY_ROWS))
    yb = rwkv_post_fwd(ysc, sr, sk, sv, proj0, prm, nw)
    ycat = jnp.concatenate([ya, yb], axis=1)
    y0 = matmul(ycat, w_out, name="ab_out")
    h1, h1b = rms_res_fwd(y0, x2, g_post0, name="rms_post0")
    a0 = matmul(p0, wp0, name="ple_proj0")
    z0 = matmul(h1b, wg0, name="ple_gate0")
    h1p = ple_fwd(h1, a0, z0, name="ple_fwd0")
    xn1 = rms_fwd(h1p, g_pre1, name="rms_pre1")
    proj1 = matmul(xn1, w_c, name="c_in")
    yin = sgu_fwd(proj1, c_ln_g, c_ln_b, w_s, b_bc)
    y1 = matmul(yin, w_cout, name="c_out")
    h2, h2b = rms_res_fwd(y1, h1p, g_post1, name="rms_post1")
    a1 = matmul(p1, wp1, name="ple_proj1")
    z1 = matmul(h2b, wg1, name="ple_gate1")
    h2p = ple_fwd(h2, a1, z1, name="ple_fwd1")
    dh, loss = loss_head(h2p, tgt, name="loss_head")

    g = {}
    da1, dz1 = ple_bwd(dh, a1, z1, name="ple_bwd1")
    g["ple_w_proj1"] = matmul(p1, da1, ta=True, out_dtype=BF16, col_blocks=n_shards, name="d_ple_proj1")
    g["ple_w_gate1"] = rows_cut(matmul(h2b, dz1, ta=True, out_dtype=BF16, name="d_ple_gate1"))
    dh2 = matmul(dz1, wg1, tb=True, add=dh, name="dx_ple_gate1")
    dy1, g_post1_g = rms_bwd(y1, dh2, g_post1, out_dtype=BF16, name="rms_post1_bwd")
    g["c_w_out"] = rows_cut(matmul(yin, dy1, ta=True, out_dtype=BF16, name="d_c_out"))
    dyin = matmul(dy1, w_cout, tb=True, out_dtype=BF16, name="dx_c_out")
    dproj1, g["c_w_s"], g_bs, g["c_ln_g"], g["c_ln_b"] = sgu_bwd(proj1, c_ln_g, c_ln_b, w_s, b_bc, dyin)
    g["c_b_s"] = g_bs.reshape(w_s.shape[:2])
    g["c_w_in"] = matmul(xn1, dproj1, ta=True, out_dtype=BF16, col_blocks=n_shards, name="d_c_in")
    dxn1 = matmul(dproj1, w_c, tb=True, out_dtype=BF16, name="dx_c_in")
    dh1p, g_pre1_g = rms_bwd(h1p, dxn1, g_pre1, res=dh2, name="rms_pre1_bwd")
    da0, dz0 = ple_bwd(dh1p, a0, z0, name="ple_bwd0")
    g["ple_w_proj0"] = matmul(p0, da0, ta=True, out_dtype=BF16, col_blocks=n_shards, name="d_ple_proj0")
    g["ple_w_gate0"] = rows_cut(matmul(h1b, dz0, ta=True, out_dtype=BF16, name="d_ple_gate0"))
    dh1 = matmul(dz0, wg0, tb=True, add=dh1p, name="dx_ple_gate0")
    dy0, g_post0_g = rms_bwd(y0, dh1, g_post0, out_dtype=BF16, name="rms_post0_bwd")
    g["ab_w_out"] = rows_cut(matmul(ycat, dy0, ta=True, out_dtype=BF16, name="d_ab_out"))
    dycat = matmul(dy0, w_out, tb=True, out_dtype=BF16, name="dx_ab_out")
    dysc, dr_b, dk_b, dv_b, dgb, g["rwkv_ln_g"], g["rwkv_ln_b"], g_rk = rwkv_post_bwd(ysc, sr, sk, sv, proj0, prm, dycat, nw)
    g["rwkv_r_k"] = g_rk.reshape(wts["rwkv_r_k"].shape)
    grads_scan, odd_landed = rwkv_scan_bwd(sr, sw, sk, sv, skk, ska, states, dysc, bsz, seq, nw, [g.pop(piece) for piece in ODD_PIECES])
    (dxr, dxk, dxv, dxwa, dmur, dmuk, dmuv, dmuwa, g["rwkv_w0"], g["rwkv_w2"], g["rwkv_a0"], g["rwkv_a2"],
     g["rwkv_k_k"], g["rwkv_k_a"]) = rwkv_pre_bwd(proj0, prm, grads_scan, (dr_b, dk_b, dv_b), bsz, seq, nw)
    g["rwkv_mu"] = jnp.concatenate([dmur, dmuk, dmuv, dmuwa], axis=1)
    dq, dk, dv, dga, dcrow = fox_attn_bwd(proj0, c, crow, oa, dycat, bsz, seq, nw)
    dc = jnp.pad(dcrow[:, :, :2, :].reshape(bsz, heads, seq).transpose(0, 2, 1), ((0, 0), (0, 0), (0, LANE - heads)))
    dfa, g_fb = fox_gate_bwd(proj0, fbias, dc.reshape(n, LANE), bsz, seq, nw)
    g["fox_f_bias"] = g_fb[:, :heads]
    dproj0 = jnp.concatenate([dq, dk, dv, dga, dxr, dxk, dxv, dgb, dxwa, dfa], axis=1)
    g_ab = jnp.stack(_ab_restore(matmul(xn0, dproj0, ta=True, out_dtype=BF16, name="d_ab_in"), wid, heads, n_shards))
    dxn0, (ab_landed,) = matmul(dproj0, w_ab, tb=True, out_dtype=BF16, scatter=[g_ab], name="dx_ab_in")
    dx, g_pre0_g = rms_bwd(x2, dxn0, g_pre0, res=dh1, name="rms_pre0_bwd")

    g["norm_pre"] = jnp.concatenate([g_pre0_g, g_pre1_g], axis=0)
    g["norm_post"] = jnp.concatenate([g_post0_g, g_post1_g], axis=0)
    return loss, dx.reshape(bsz, seq, d), g, dict(zip(ODD_PIECES + ("ab_w_in",), tuple(odd_landed) + (ab_landed,)))


MESH = pl.DeviceIdType.MESH
ANY = pl.BlockSpec(memory_space=pl.ANY)
PACK_COLS = 1024
PACK_ROWS = 16


def _mesh_place():
    xi, yi, ci = lax.axis_index("x"), lax.axis_index("y"), lax.axis_index("c")
    return xi, yi, ci, 4 * xi + 2 * yi + ci


def _peer(xi, yi, ci, m):
    px = 1 - xi if m & 4 else xi
    py = 1 - yi if m & 2 else yi
    pc = 1 - ci if m & 1 else ci
    return (px, py, pc), 4 * px + 2 * py + pc


def _gather(arrays, *, name):
    n = len(arrays)

    def body(*refs):
        ins, outs = refs[:n], refs[n:2 * n]
        send_sems, recv_sems, local_sems = refs[2 * n:]
        xi, yi, ci, me = _mesh_place()
        sibling = (xi, yi, 1 - ci)
        chips = [(1 - xi, yi), (xi, 1 - yi), (1 - xi, 1 - yi)]
        block = lambda px, py, pc: 4 * px + 2 * py + pc

        def copy(a, k, blk, to, src=None):
            return pltpu.make_async_remote_copy(src_ref=outs[a].at[blk] if src is None else src, dst_ref=outs[a].at[blk],
                                                send_sem=send_sems.at[k, a], recv_sem=recv_sems.at[k, a],
                                                device_id=to, device_id_type=MESH)

        started = []
        for a in range(n):
            cp = pltpu.make_async_copy(ins[a], outs[a].at[me], local_sems.at[a])
            cp.start()
            started.append(cp)
        sends = []
        for a in range(n):
            sends.append(copy(a, 0, me, sibling, src=ins[a]))
            sends += [copy(a, 1 + j, me, (*chip, ci), src=ins[a]) for j, chip in enumerate(chips)]
        for cp in sends:
            cp.start()
        for j, chip in enumerate(chips):
            for a in range(n):
                copy(a, 1 + j, block(*chip, ci), (xi, yi, ci)).wait_recv()
                fwd = copy(a, 4 + j, block(*chip, ci), sibling)
                fwd.start()
                sends.append(fwd)
        for a in range(n):
            copy(a, 0, block(xi, yi, 1 - ci), (xi, yi, ci)).wait_recv()
            for j, chip in enumerate(chips):
                copy(a, 4 + j, block(*chip, 1 - ci), (xi, yi, ci)).wait_recv()
        for cp in sends:
            cp.wait_send()
        for cp in started:
            cp.wait()

    return pl.pallas_call(
        body, name=name, in_specs=[ANY] * n, out_specs=[ANY] * n,
        out_shape=[jax.ShapeDtypeStruct((N_DEV,) + a.shape, a.dtype) for a in arrays],
        scratch_shapes=[pltpu.SemaphoreType.DMA((N_DEV - 1, n)), pltpu.SemaphoreType.DMA((N_DEV - 1, n)),
                        pltpu.SemaphoreType.DMA((n,))],
    )(*arrays)


def _exchange_sems(n):
    if n == 0:
        return []
    return [pltpu.SemaphoreType.DMA((N_DEV - 1, n)), pltpu.SemaphoreType.DMA((N_DEV - 1, n)), pltpu.SemaphoreType.DMA((n,))]


def _direct_exchange(ins, outs, send_sems=None, recv_sems=None, local_sems=None, *, scatter):
    n = len(ins)

    def copies():
        xi, yi, ci, me = _mesh_place()
        src = lambda a, idx: ins[a].at[idx] if scatter else ins[a]
        local = [pltpu.make_async_copy(src(a, me), outs[a].at[me], local_sems.at[a]) for a in range(n)]
        sends, arrivals = [], []
        for m in range(1, N_DEV):
            peer, pidx = _peer(xi, yi, ci, m)
            for a in range(n):
                pair = dict(send_sem=send_sems.at[m - 1, a], recv_sem=recv_sems.at[m - 1, a], device_id=peer, device_id_type=MESH)
                sends.append(pltpu.make_async_remote_copy(src_ref=src(a, pidx), dst_ref=outs[a].at[me], **pair))
                arrivals.append(pltpu.make_async_remote_copy(src_ref=src(a, pidx), dst_ref=outs[a].at[pidx], **pair))
        return local, sends, arrivals

    def start():
        local, sends, _ = copies()
        for cp in local + sends:
            cp.start()

    def wait():
        local, sends, arrivals = copies()
        for cp in arrivals:
            cp.wait_recv()
        for cp in sends:
            cp.wait_send()
        for cp in local:
            cp.wait()

    return start, wait


def sum_blocks(x, *, name):
    _, rows, cols = x.shape
    tr = _tile(rows, 256, PACK_ROWS)

    def body(x_ref, o_ref):
        acc = x_ref[0].astype(F32)
        for s in range(1, N_DEV):
            acc = acc + x_ref[s].astype(F32)
        o_ref[...] = acc

    return pl.pallas_call(
        body, name=name, grid=(rows // tr,), in_specs=[pl.BlockSpec((N_DEV, tr, cols), lambda i: (0, i, 0))],
        out_specs=pl.BlockSpec((tr, cols), lambda i: (i, 0)), out_shape=jax.ShapeDtypeStruct((rows, cols), F32),
        compiler_params=_cparams(("parallel",)),
    )(x)


def adamw(w, parts, m, v, *, name):
    rows, cols = w.shape
    nparts = parts.shape[0]
    tr = _tile(rows, 256, 16)
    c1 = 1.0 / (1.0 - ADAM_B1 ** ADAM_STEP)
    c2 = 1.0 / (1.0 - ADAM_B2 ** ADAM_STEP)

    def body(w_ref, p_ref, m_ref, v_ref, g_ref, d_ref, mo_ref, vo_ref):
        gv = p_ref[0].astype(F32)
        for s in range(1, nparts):
            gv = gv + p_ref[s].astype(F32)
        mn = ADAM_B1 * m_ref[...] + (1.0 - ADAM_B1) * gv
        vn = ADAM_B2 * v_ref[...] + (1.0 - ADAM_B2) * (gv * gv)
        g_ref[...] = gv
        d_ref[...] = -ADAM_LR * ((mn * c1) / (jnp.sqrt(vn * c2) + ADAM_EPS) + ADAM_WD * w_ref[...])
        mo_ref[...] = mn
        vo_ref[...] = vn

    spec = pl.BlockSpec((tr, cols), lambda i: (i, 0))
    return pl.pallas_call(
        body, name=name, grid=(rows // tr,), in_specs=[spec, pl.BlockSpec((nparts, tr, cols), lambda i: (0, i, 0)), spec, spec],
        out_specs=[spec] * 4, out_shape=[jax.ShapeDtypeStruct((rows, cols), F32)] * 4, compiler_params=_cparams(("parallel",)),
    )(w, parts, m, v)


def _pack(arrays, dtype):
    flat = jnp.concatenate([a.astype(dtype).reshape(-1) for a in arrays])
    unit = PACK_COLS * PACK_ROWS
    total = -(-flat.shape[0] // unit) * unit
    return jnp.pad(flat, (0, total - flat.shape[0])).reshape(total // PACK_COLS, PACK_COLS)


def _unpack(flat2d, shapes, lead=()):
    flat = flat2d.reshape(lead + (-1,))
    out, off = [], 0
    for shp in shapes:
        size = 1
        for s in shp:
            size *= s
        out.append(flat[..., off:off + size].reshape(lead + tuple(shp)))
        off += size
    return out


def _join_shards(sh, axis):
    return jnp.concatenate([sh[d] for d in range(N_DEV)], axis=axis)


BIG = (("ab_w_in", "ab_w_in", None, False), ("c_w_in", "c_w_in", None, False), ("ab_w_out", "ab_w_out", None, True),
       ("c_w_out", "c_w_out", None, True), ("ple_w_gate0", "ple_w_gate", 0, True), ("ple_w_gate1", "ple_w_gate", 1, True),
       ("ple_w_proj0", "ple_w_proj", 0, False), ("ple_w_proj1", "ple_w_proj", 1, False))
SMALL_SHARDED = (("rwkv_w2", 1), ("rwkv_a2", 1), ("c_ln_g", 0), ("c_ln_b", 0))
REPLICATED = ("norm_pre", "norm_post", "fox_f_bias", "rwkv_mu", "rwkv_w0", "rwkv_a0", "rwkv_k_k", "rwkv_k_a", "rwkv_r_k",
              "rwkv_ln_g", "rwkv_ln_b", "c_w_s", "c_b_s")
WEIGHTS = ("norm_pre", "norm_post", "ab_w_in", "fox_f_bias", "rwkv_mu", "rwkv_w0", "rwkv_w2", "rwkv_a0", "rwkv_a2", "rwkv_k_k",
           "rwkv_k_a", "rwkv_r_k", "rwkv_ln_g", "rwkv_ln_b", "ab_w_out", "c_w_in", "c_ln_g", "c_ln_b", "c_w_s", "c_b_s",
           "c_w_out", "ple_w_proj", "ple_w_gate")
SQUEEZED = ("norm_pre", "norm_post", "ple_w_proj", "ple_w_gate")


def _step(x, p, loss_target, w, mom, vel):
    sq = {k: (a if k in SQUEEZED else a[0]) for k, a in w.items()}
    _, _, _, me = _mesh_place()

    wid, heads = x.shape[-1] // 2, x.shape[-1] // 2 // HEAD
    layer = lambda a, l: a if l is None else a[l]

    shards = {piece: layer(sq[wname], l).astype(BF16) for piece, wname, l, _ in BIG}
    even = [entry for entry in BIG if entry[0] not in ODD_PIECES]
    sends = [shards[piece] for piece, _, _, _ in even]
    sends += [sq[k] if sq[k].ndim == 2 else sq[k].reshape(1, -1) for k, _ in SMALL_SHARDED]
    gathered = _gather(sends, name="gather_weights")
    full = {}
    for (piece, _, _, by_rows), got in zip(even, gathered):
        if piece == "ab_w_in":
            full["ab_w_in_r"] = _ab_reorder([got[d] for d in range(N_DEV)], wid, heads)
        else:
            full[piece] = _join_shards(got, 1 - by_rows)
    for (k, ax), got in zip(SMALL_SHARDED, gathered[len(even):]):
        full[k] = _join_shards(got, 1).reshape(-1) if sq[k].ndim == 1 else _join_shards(got, 1)
    for k in REPLICATED:
        full[k] = sq[k]

    loss, grad_x, g, parts = local_step(x, p, loss_target, full, N_DEV, {piece: shards[piece] for piece in ODD_PIECES})

    small_names = [k for k, _ in SMALL_SHARDED] + list(REPLICATED)
    small_full_shapes = [full[k].shape for k in small_names] + [(1, 1)]
    partial = _gather([_pack([g[k].reshape(full[k].shape) for k in small_names] + [loss], F32)], name="gather_small_grads")[0]
    summed = _unpack(sum_blocks(partial, name="sum_small_grads"), small_full_shapes)
    small_grads = dict(zip(small_names, summed[:-1]))
    for k, ax in SMALL_SHARDED:
        width = sq[k].shape[ax]
        small_grads[k] = lax.dynamic_slice_in_dim(small_grads[k], me * width, width, axis=ax)
    loss_all = summed[-1][0, 0]

    outs_g, outs_d, outs_m, outs_v = [], [], [], []
    for k in WEIGHTS:
        shape = w[k].shape
        two_d = lambda a: a.reshape(-1, a.shape[-1])
        pieces = [(piece, l) for piece, wname, l, _ in BIG if wname == k]
        if not pieces:
            res = adamw(two_d(w[k]), two_d(small_grads[k].reshape(shape))[None], two_d(mom[k]), two_d(vel[k]), name="adamw_" + k)
        elif pieces[0][1] is None:
            res = adamw(two_d(w[k]), parts[k], two_d(mom[k]), two_d(vel[k]), name="adamw_" + k)
        else:
            per_layer = [adamw(w[k][l], parts[piece], mom[k][l], vel[k][l], name="adamw_" + piece) for piece, l in pieces]
            res = [jnp.stack(r) for r in zip(*per_layer)]
        for out, r in zip((outs_g, outs_d, outs_m, outs_v), res):
            out.append(r.reshape(shape))
    return (loss_all, grad_x, *outs_g, *outs_d, *outs_m, *outs_v)


def kernel(x, p, norm_pre, norm_post, ab_w_in, fox_f_bias, rwkv_mu, rwkv_w0, rwkv_w2, rwkv_a0, rwkv_a2, rwkv_k_k, rwkv_k_a, rwkv_r_k, rwkv_ln_g, rwkv_ln_b, ab_w_out, c_w_in, c_ln_g, c_ln_b, c_w_s, c_b_s, c_w_out, ple_w_proj, ple_w_gate, loss_target, m_norm_pre, m_norm_post, m_ab_w_in, m_fox_f_bias, m_rwkv_mu, m_rwkv_w0, m_rwkv_w2, m_rwkv_a0, m_rwkv_a2, m_rwkv_k_k, m_rwkv_k_a, m_rwkv_r_k, m_rwkv_ln_g, m_rwkv_ln_b, m_ab_w_out, m_c_w_in, m_c_ln_g, m_c_ln_b, m_c_w_s, m_c_b_s, m_c_w_out, m_ple_w_proj, m_ple_w_gate, v_norm_pre, v_norm_post, v_ab_w_in, v_fox_f_bias, v_rwkv_mu, v_rwkv_w0, v_rwkv_w2, v_rwkv_a0, v_rwkv_a2, v_rwkv_k_k, v_rwkv_k_a, v_rwkv_r_k, v_rwkv_ln_g, v_rwkv_ln_b, v_ab_w_out, v_c_w_in, v_c_ln_g, v_c_ln_b, v_c_w_s, v_c_b_s, v_c_w_out, v_ple_w_proj, v_ple_w_gate):
    w = dict(norm_pre=norm_pre, norm_post=norm_post, ab_w_in=ab_w_in, fox_f_bias=fox_f_bias, rwkv_mu=rwkv_mu, rwkv_w0=rwkv_w0, rwkv_w2=rwkv_w2, rwkv_a0=rwkv_a0, rwkv_a2=rwkv_a2, rwkv_k_k=rwkv_k_k, rwkv_k_a=rwkv_k_a, rwkv_r_k=rwkv_r_k, rwkv_ln_g=rwkv_ln_g, rwkv_ln_b=rwkv_ln_b, ab_w_out=ab_w_out, c_w_in=c_w_in, c_ln_g=c_ln_g, c_ln_b=c_ln_b, c_w_s=c_w_s, c_b_s=c_b_s, c_w_out=c_w_out, ple_w_proj=ple_w_proj, ple_w_gate=ple_w_gate)
    mom = dict(norm_pre=m_norm_pre, norm_post=m_norm_post, ab_w_in=m_ab_w_in, fox_f_bias=m_fox_f_bias, rwkv_mu=m_rwkv_mu, rwkv_w0=m_rwkv_w0, rwkv_w2=m_rwkv_w2, rwkv_a0=m_rwkv_a0, rwkv_a2=m_rwkv_a2, rwkv_k_k=m_rwkv_k_k, rwkv_k_a=m_rwkv_k_a, rwkv_r_k=m_rwkv_r_k, rwkv_ln_g=m_rwkv_ln_g, rwkv_ln_b=m_rwkv_ln_b, ab_w_out=m_ab_w_out, c_w_in=m_c_w_in, c_ln_g=m_c_ln_g, c_ln_b=m_c_ln_b, c_w_s=m_c_w_s, c_b_s=m_c_b_s, c_w_out=m_c_w_out, ple_w_proj=m_ple_w_proj, ple_w_gate=m_ple_w_gate)
    vel = dict(norm_pre=v_norm_pre, norm_post=v_norm_post, ab_w_in=v_ab_w_in, fox_f_bias=v_fox_f_bias, rwkv_mu=v_rwkv_mu, rwkv_w0=v_rwkv_w0, rwkv_w2=v_rwkv_w2, rwkv_a0=v_rwkv_a0, rwkv_a2=v_rwkv_a2, rwkv_k_k=v_rwkv_k_k, rwkv_k_a=v_rwkv_k_a, rwkv_r_k=v_rwkv_r_k, rwkv_ln_g=v_rwkv_ln_g, rwkv_ln_b=v_rwkv_ln_b, ab_w_out=v_ab_w_out, c_w_in=v_c_w_in, c_ln_g=v_c_ln_g, c_ln_b=v_c_ln_b, c_w_s=v_c_w_s, c_b_s=v_c_b_s, c_w_out=v_c_w_out, ple_w_proj=v_ple_w_proj, ple_w_gate=v_ple_w_gate)
    return _step(x, p, loss_target, w, mom, vel)
```

```python
import functools

import jax
import jax.numpy as jnp
from jax import lax
from jax.experimental import pallas as pl
from jax.experimental.pallas import tpu as pltpu

F32 = jnp.float32
BF16 = jnp.bfloat16
HI = lax.Precision.HIGHEST

HEAD = 64
LANE = 128
LORA = 64
GMLP_CHUNK = 128
RMS_EPS = 1e-6
LN_EPS = 1e-5
GN_EPS = 64e-5
VMEM_LIMIT = 56 * 1024 * 1024
N_DEV = 8

ADAM_LR = 0.001
ADAM_B1 = 0.9
ADAM_B2 = 0.999
ADAM_EPS = 1e-08
ADAM_WD = 0.01
ADAM_STEP = 10

NT_DIMS = (((1,), (1,)), ((), ()))
TN_DIMS = (((0,), (0,)), ((), ()))


def _cparams(sem):
    return pltpu.CompilerParams(dimension_semantics=sem, vmem_limit_bytes=VMEM_LIMIT)


def _tile(n, target, q=LANE):
    if n <= target:
        return n
    best = None
    for d in range(q, target + 1, q):
        if n % d == 0:
            best = d
    assert best is not None, (n, target, q)
    return best


def _sigmoid(x):
    return 1.0 / (1.0 + jnp.exp(-x))


def _silu_and_grad(x):
    s = _sigmoid(x)
    return x * s, s * (1.0 + x * (1.0 - s))


def _gelu_and_grad(x):
    cdf = 0.5 * (1.0 + lax.erf(x * 0.7071067811865476))
    pdf = jnp.exp(-0.5 * x * x) * 0.3989422804014327
    return x * cdf, cdf + x * pdf


def _iota(shape, dim):
    return lax.broadcasted_iota(jnp.int32, shape, dim)


def _head_ones():
    return (_iota((LANE, LANE), 0) // HEAD == _iota((LANE, LANE), 1) // HEAD).astype(F32)


def _head_eye():
    return (_iota((HEAD, LANE), 0) == _iota((HEAD, LANE), 1) % HEAD).astype(F32)


def _hsum(x, ones):
    hi = x.astype(BF16)
    lo = (x - hi.astype(F32)).astype(BF16)
    ones_b = ones.astype(BF16)
    return jnp.dot(jnp.concatenate([hi, lo], axis=1), jnp.concatenate([ones_b, ones_b], axis=0), preferred_element_type=F32)


def _colsum(x):
    return jnp.sum(x, axis=0, keepdims=True)


def matmul(a, b, *, name, ta=False, tb=False, add=None, out_dtype=F32, tm=1024, tn=1024, tk=2048, col_blocks=1, scatter=()):
    m, k = (a.shape[1], a.shape[0]) if ta else a.shape
    n = b.shape[0] if tb else b.shape[1]
    assert (b.shape[1] if tb else b.shape[0]) == k
    tm, tn, tk = _tile(m, tm), _tile(n // col_blocks, tn), _tile(k, tk)
    nk = k // tk
    per = n // col_blocks // tn
    grid = (m // tm, n // tn, nk)
    ns = len(scatter)
    n_in = 2 + (add is not None)

    def body(*refs):
        a_ref, b_ref = refs[:2]
        add_ref = refs[2] if add is not None else None
        o_ref = refs[n_in + ns]
        acc_ref = refs[n_in + 2 * ns + 1]
        if ns:
            start, wait = _direct_exchange(refs[n_in:n_in + ns], refs[n_in + ns + 1:n_in + 2 * ns + 1], *refs[n_in + 2 * ns + 2:],
                                           scatter=True)
            first, last = _grid_ends(grid)
            pl.when(first)(start)
        kk = pl.program_id(2)

        @pl.when(kk == 0)
        def _():
            acc_ref[...] = jnp.zeros_like(acc_ref)

        dims = (((0 if ta else 1,), (1 if tb else 0,)), ((), ()))
        acc_ref[...] += lax.dot_general(a_ref[...].astype(BF16), b_ref[...].astype(BF16), dims,
                                        preferred_element_type=F32)

        @pl.when(kk == nk - 1)
        def _():
            r = acc_ref[...]
            if add is not None:
                r = r + add_ref[...].astype(F32)
            o_ref[...] = r.astype(out_dtype)

        if ns:
            pl.when(last)(wait)

    a_spec = pl.BlockSpec((tk, tm), lambda i, j, kk: (kk, i)) if ta else pl.BlockSpec((tm, tk), lambda i, j, kk: (i, kk))
    b_spec = pl.BlockSpec((tn, tk), lambda i, j, kk: (j, kk)) if tb else pl.BlockSpec((tk, tn), lambda i, j, kk: (kk, j))
    o_spec = pl.BlockSpec((tm, tn), lambda i, j, kk: (i, j))
    in_specs = [a_spec, b_spec] + ([o_spec] if add is not None else [])
    args = (a, b) + ((add,) if add is not None else ())
    out_shape = (m, n)
    if col_blocks > 1:
        assert add is None
        o_spec = pl.BlockSpec((None, tm, tn), lambda i, j, kk: (j // per, i, j % per))
        out_shape = (col_blocks, m, n // col_blocks)
    out = pl.pallas_call(
        body, name=name, grid=grid, in_specs=in_specs + [ANY] * ns, out_specs=[o_spec] + [ANY] * ns,
        out_shape=[jax.ShapeDtypeStruct(out_shape, out_dtype)] + [jax.ShapeDtypeStruct(s.shape, s.dtype) for s in scatter],
        scratch_shapes=[pltpu.VMEM((tm, tn), F32)] + _exchange_sems(ns),
        compiler_params=_cparams(("arbitrary",) * 3 if ns else ("parallel", "parallel", "arbitrary")),
    )(*args, *scatter)
    return (out[0], out[1:]) if ns else out[0]


def _row_spec(tm, width, cb=0):
    return pl.BlockSpec((tm, width), lambda i: (i, cb))


def _full_spec(shape):
    return pl.BlockSpec(shape, lambda i: (0,) * len(shape))


def rms_fwd(x, g, *, name):
    n, d = x.shape
    tm = _tile(n, 256, 8)

    def body(x_ref, g_ref, o_ref):
        xv = x_ref[...]
        r = lax.rsqrt(jnp.mean(xv * xv, axis=-1, keepdims=True) + RMS_EPS)
        o_ref[...] = (xv * r * g_ref[...]).astype(BF16)

    return pl.pallas_call(
        body, name=name, grid=(n // tm,), in_specs=[_row_spec(tm, d), _full_spec((1, d))],
        out_specs=_row_spec(tm, d), out_shape=jax.ShapeDtypeStruct((n, d), BF16),
        compiler_params=_cparams(("parallel",)),
    )(x, g)


def rms_res_fwd(y, res, g, *, name):
    n, d = y.shape
    tm = _tile(n, 256, 8)

    def body(y_ref, res_ref, g_ref, h_ref, hb_ref):
        yv = y_ref[...]
        r = lax.rsqrt(jnp.mean(yv * yv, axis=-1, keepdims=True) + RMS_EPS)
        h = res_ref[...] + yv * r * g_ref[...]
        h_ref[...] = h
        hb_ref[...] = h.astype(BF16)

    return pl.pallas_call(
        body, name=name, grid=(n // tm,), in_specs=[_row_spec(tm, d), _row_spec(tm, d), _full_spec((1, d))],
        out_specs=[_row_spec(tm, d), _row_spec(tm, d)],
        out_shape=[jax.ShapeDtypeStruct((n, d), F32), jax.ShapeDtypeStruct((n, d), BF16)],
        compiler_params=_cparams(("parallel",)),
    )(y, res, g)


def rms_bwd(x, dy, g, *, name, res=None, out_dtype=F32):
    n, d = x.shape
    tm = _tile(n, 256, 8)

    def body(*refs):
        if res is None:
            x_ref, dy_ref, g_ref, dx_ref, dg_ref = refs
        else:
            x_ref, dy_ref, g_ref, res_ref, dx_ref, dg_ref = refs

        @pl.when(pl.program_id(0) == 0)
        def _():
            dg_ref[...] = jnp.zeros_like(dg_ref)

        xv = x_ref[...]
        dyv = dy_ref[...].astype(F32)
        r = lax.rsqrt(jnp.mean(xv * xv, axis=-1, keepdims=True) + RMS_EPS)
        xhat = xv * r
        dxh = dyv * g_ref[...]
        dx = r * (dxh - xhat * jnp.mean(dxh * xhat, axis=-1, keepdims=True))
        if res is not None:
            dx = dx + res_ref[...]
        dx_ref[...] = dx.astype(out_dtype)
        dg_ref[...] += _colsum(dyv * xhat)

    in_specs = [_row_spec(tm, d), _row_spec(tm, d), _full_spec((1, d))] + ([_row_spec(tm, d)] if res is not None else [])
    args = (x, dy, g) + ((res,) if res is not None else ())
    return pl.pallas_call(
        body, name=name, grid=(n // tm,), in_specs=in_specs, out_specs=[_row_spec(tm, d), _full_spec((1, d))],
        out_shape=[jax.ShapeDtypeStruct((n, d), out_dtype), jax.ShapeDtypeStruct((1, d), F32)],
        compiler_params=_cparams(("arbitrary",)),
    )(*args)


def ple_fwd(h, a, z, *, name):
    n, d = h.shape
    tm = _tile(n, 256, 8)

    def body(h_ref, a_ref, z_ref, o_ref):
        o_ref[...] = h_ref[...] + a_ref[...] * _sigmoid(z_ref[...])

    return pl.pallas_call(
        body, name=name, grid=(n // tm,), in_specs=[_row_spec(tm, d)] * 3, out_specs=_row_spec(tm, d),
        out_shape=jax.ShapeDtypeStruct((n, d), F32), compiler_params=_cparams(("parallel",)),
    )(h, a, z)


def ple_bwd(dh, a, z, *, name):
    n, d = dh.shape
    tm = _tile(n, 256, 8)

    def body(dh_ref, a_ref, z_ref, da_ref, dz_ref):
        s = _sigmoid(z_ref[...])
        dhv = dh_ref[...]
        da_ref[...] = (dhv * s).astype(BF16)
        dz_ref[...] = (dhv * a_ref[...] * s * (1.0 - s)).astype(BF16)

    return pl.pallas_call(
        body, name=name, grid=(n // tm,), in_specs=[_row_spec(tm, d)] * 3, out_specs=[_row_spec(tm, d)] * 2,
        out_shape=[jax.ShapeDtypeStruct((n, d), BF16)] * 2, compiler_params=_cparams(("parallel",)),
    )(dh, a, z)


def loss_head(h, target, *, name):
    n, d = h.shape
    tm = _tile(n, 256, 8)

    def body(h_ref, t_ref, dh_ref, loss_ref):
        @pl.when(pl.program_id(0) == 0)
        def _():
            loss_ref[...] = jnp.zeros_like(loss_ref)

        e = h_ref[...] - t_ref[...]
        dh_ref[...] = e * (1.0 / d)
        loss_ref[...] += 0.5 * jnp.sum(jnp.sum(e * e, axis=-1, keepdims=True) * (1.0 / d), axis=0, keepdims=True)

    return pl.pallas_call(
        body, name=name, grid=(n // tm,), in_specs=[_row_spec(tm, d)] * 2,
        out_specs=[_row_spec(tm, d), _full_spec((1, 1))],
        out_shape=[jax.ShapeDtypeStruct((n, d), F32), jax.ShapeDtypeStruct((1, 1), F32)],
        compiler_params=_cparams(("arbitrary",)),
    )(h, target)


def _log_sigmoid(z):
    return jnp.minimum(z, 0.0) - jnp.log1p(jnp.exp(-jnp.abs(z)))


def fox_gate_fwd(proj, fbias, bsz, seq, nw):
    cc = _tile(seq, 256, 8)

    def body(f_ref, b_ref, c_ref):
        low = (_iota((cc, cc), 0) >= _iota((cc, cc), 1)).astype(F32)
        carry = jnp.zeros((1, LANE), F32)
        for ci in range(seq // cc):
            rows = slice(ci * cc, (ci + 1) * cc)
            lf = _log_sigmoid(f_ref[rows, :] + b_ref[...])
            c_ref[rows, :] = jnp.dot(low, lf, precision=HI, preferred_element_type=F32) + carry
            carry = carry + _colsum(lf)

    return pl.pallas_call(
        body, name="fox_gate_fwd", grid=(bsz,),
        in_specs=[pl.BlockSpec((seq, LANE), lambda b: (b, 8 * nw + 1)), _full_spec((1, LANE))],
        out_specs=pl.BlockSpec((seq, LANE), lambda b: (b, 0)),
        out_shape=jax.ShapeDtypeStruct((bsz * seq, LANE), F32), compiler_params=_cparams(("parallel",)),
    )(proj, fbias)


def fox_gate_bwd(proj, fbias, dc, bsz, seq, nw):
    cc = _tile(seq, 256, 8)
    nc = seq // cc

    def body(f_ref, b_ref, dc_ref, df_ref, db_ref):
        @pl.when(pl.program_id(0) == 0)
        def _():
            db_ref[...] = jnp.zeros_like(db_ref)

        upp = (_iota((cc, cc), 0) <= _iota((cc, cc), 1)).astype(F32)
        carry = jnp.zeros((1, LANE), F32)
        dbias = jnp.zeros((1, LANE), F32)
        for ci in reversed(range(nc)):
            rows = slice(ci * cc, (ci + 1) * cc)
            blk = dc_ref[rows, :]
            dlf = jnp.dot(upp, blk, precision=HI, preferred_element_type=F32) + carry
            carry = carry + _colsum(blk)
            df = dlf * _sigmoid(-(f_ref[rows, :] + b_ref[...]))
            df_ref[rows, :] = df.astype(BF16)
            dbias = dbias + _colsum(df)
        db_ref[...] += dbias

    return pl.pallas_call(
        body, name="fox_gate_bwd", grid=(bsz,),
        in_specs=[pl.BlockSpec((seq, LANE), lambda b: (b, 8 * nw + 1)), _full_spec((1, LANE)),
                  pl.BlockSpec((seq, LANE), lambda b: (b, 0))],
        out_specs=[pl.BlockSpec((seq, LANE), lambda b: (b, 0)), _full_spec((1, LANE))],
        out_shape=[jax.ShapeDtypeStruct((bsz * seq, LANE), BF16), jax.ShapeDtypeStruct((1, LANE), F32)],
        compiler_params=_cparams(("arbitrary",)),
    )(proj, fbias, dc)


def _attn_scores(qs, kh, ccol, crow, r0, kend, tq):
    s = lax.dot_general(qs[r0:kend], kh[:kend], NT_DIMS, preferred_element_type=F32)
    s = s + ccol[r0:kend] - crow[:, :kend]
    causal = _iota((tq, kend), 1) <= r0 + _iota((tq, kend), 0)
    return jnp.where(causal, s, -jnp.inf)


def _head_column(c_blk, h):
    return jnp.sum(jnp.where(_iota(c_blk.shape, 1) == h, c_blk, 0.0), axis=1, keepdims=True)


def fox_attn_fwd(proj, c, crow, bsz, seq, nw):
    tq = _tile(seq, 256, 8)
    scale = HEAD ** -0.5

    def body(q_ref, k_ref, v_ref, g_ref, c_ref, crow_ref, o_ref, y_ref):
        hp = pl.program_id(1)
        c_blk = c_ref[...]
        for hh in range(2):
            sl = slice(hh * HEAD, (hh + 1) * HEAD)
            ccol = _head_column(c_blk, 2 * hp + hh)
            crow_h = crow_ref[0, 0, hh:hh + 1, :]
            qs = (q_ref[:, sl] * scale).astype(BF16)
            kh = k_ref[:, sl].astype(BF16)
            vh = v_ref[:, sl].astype(BF16)
            for qi in range(seq // tq):
                r0, kend = qi * tq, (qi + 1) * tq
                s = _attn_scores(qs, kh, ccol, crow_h, r0, kend, tq)
                p = jnp.exp(s - jnp.max(s, axis=-1, keepdims=True))
                o = jnp.dot(p.astype(BF16), vh[:kend], preferred_element_type=F32) / jnp.sum(p, axis=-1, keepdims=True)
                o_ref[r0:kend, sl] = o
                y_ref[r0:kend, sl] = (o * _silu_and_grad(g_ref[r0:kend, sl])[0]).astype(BF16)

    blk = lambda cb: pl.BlockSpec((seq, LANE), lambda b, j: (b, cb * nw + j))
    return pl.pallas_call(
        body, name="fox_attn_fwd", grid=(bsz, nw),
        in_specs=[blk(0), blk(1), blk(2), blk(3), pl.BlockSpec((seq, LANE), lambda b, j: (b, 0)),
                  pl.BlockSpec((1, 1, 8, seq), lambda b, j: (b, j, 0, 0))],
        out_specs=[pl.BlockSpec((seq, LANE), lambda b, j: (b, j))] * 2,
        out_shape=[jax.ShapeDtypeStruct((bsz * seq, nw * LANE), F32), jax.ShapeDtypeStruct((bsz * seq, nw * LANE), BF16)],
        compiler_params=_cparams(("parallel", "parallel")),
    )(proj, proj, proj, proj, c, crow)


def fox_attn_bwd(proj, c, crow, oa, dycat, bsz, seq, nw):
    tq = _tile(seq, 256, 8)
    scale = HEAD ** -0.5

    def body(q_ref, k_ref, v_ref, g_ref, c_ref, crow_ref, o_ref, dy_ref,
             dq_ref, dk_ref, dv_ref, dg_ref, dc_ref, dk_acc, dv_acc, dc_acc):
        hp = pl.program_id(1)
        c_blk = c_ref[...]
        dc_ref[...] = jnp.zeros_like(dc_ref)
        for hh in range(2):
            sl = slice(hh * HEAD, (hh + 1) * HEAD)
            ccol = _head_column(c_blk, 2 * hp + hh)
            crow_h = crow_ref[0, 0, hh:hh + 1, :]
            qs = (q_ref[:, sl] * scale).astype(BF16)
            kh = k_ref[:, sl].astype(BF16)
            vh = v_ref[:, sl].astype(BF16)
            oh = o_ref[:, sl]
            dyh = dy_ref[:, sl].astype(F32)
            silu, dsilu = _silu_and_grad(g_ref[:, sl])
            dg_ref[:, sl] = (dyh * oh * dsilu).astype(BF16)
            do = dyh * silu
            dvec = jnp.sum(do * oh, axis=-1, keepdims=True)
            dob = do.astype(BF16)
            dk_acc[...] = jnp.zeros_like(dk_acc)
            dv_acc[...] = jnp.zeros_like(dv_acc)
            dc_acc[...] = jnp.zeros_like(dc_acc)
            for qi in range(seq // tq):
                r0, kend = qi * tq, (qi + 1) * tq
                s = _attn_scores(qs, kh, ccol, crow_h, r0, kend, tq)
                p = jnp.exp(s - jnp.max(s, axis=-1, keepdims=True))
                p = p / jnp.sum(p, axis=-1, keepdims=True)
                dp = lax.dot_general(dob[r0:kend], vh[:kend], NT_DIMS, preferred_element_type=F32)
                ds = p * (dp - dvec[r0:kend])
                dsb = ds.astype(BF16)
                dq_ref[r0:kend, sl] = (jnp.dot(dsb, kh[:kend], preferred_element_type=F32) * scale).astype(BF16)
                dk_acc[0:kend, :] += lax.dot_general(dsb, qs[r0:kend], TN_DIMS, preferred_element_type=F32)
                dv_acc[0:kend, :] += lax.dot_general(p.astype(BF16), dob[r0:kend], TN_DIMS, preferred_element_type=F32)
                dc_acc[:, 0:kend] += -_colsum(ds)
                ds_lo = (ds - dsb.astype(F32)).astype(BF16)
                rowsum = lax.dot_general(jnp.ones((8, 2 * kend), BF16), jnp.concatenate([dsb, ds_lo], axis=1), NT_DIMS,
                                         preferred_element_type=F32)
                dc_acc[:, r0:kend] += rowsum[0:1, :]
            dk_ref[:, sl] = dk_acc[...].astype(BF16)
            dv_ref[:, sl] = dv_acc[...].astype(BF16)
            dc_ref[0, 0, hh:hh + 1, :] = dc_acc[...]

    blk = lambda cb: pl.BlockSpec((seq, LANE), lambda b, j: (b, cb * nw + j))
    own = pl.BlockSpec((seq, LANE), lambda b, j: (b, j))
    n = bsz * seq
    return pl.pallas_call(
        body, name="fox_attn_bwd", grid=(bsz, nw),
        in_specs=[blk(0), blk(1), blk(2), blk(3), pl.BlockSpec((seq, LANE), lambda b, j: (b, 0)),
                  pl.BlockSpec((1, 1, 8, seq), lambda b, j: (b, j, 0, 0)), own, own],
        out_specs=[own] * 4 + [pl.BlockSpec((1, 1, 8, seq), lambda b, j: (b, j, 0, 0))],
        out_shape=[jax.ShapeDtypeStruct((n, nw * LANE), BF16)] * 4 + [jax.ShapeDtypeStruct((bsz, nw, 8, seq), F32)],
        scratch_shapes=[pltpu.VMEM((seq, HEAD), F32), pltpu.VMEM((seq, HEAD), F32), pltpu.VMEM((1, seq), F32)],
        compiler_params=_cparams(("parallel", "parallel")),
    )(proj, proj, proj, proj, c, crow, oa, dycat)


def _shift_mix(x, mu):
    prev = jnp.where(_iota(x.shape, 0) == 0, 0.0, pltpu.roll(x, 1, 0))
    return x + (prev - x) * mu, prev


def _rwkv_pre_math(r_in, k_in, v_in, wa_in, mu_r, mu_k, mu_v, mu_wa, w0, w2, a0, a2, k_k, k_a, ones):
    r, prev_r = _shift_mix(r_in, mu_r)
    k, prev_k = _shift_mix(k_in, mu_k)
    v, prev_v = _shift_mix(v_in, mu_v)
    wa, prev_wa = _shift_mix(wa_in, mu_wa)
    tw = jnp.tanh(wa[:, :LORA])
    alo = wa[:, LORA:]
    sg = _sigmoid(w0 + jnp.dot(tw, w2, precision=HI, preferred_element_type=F32))
    e = sg * 0.6065306597126334
    w = jnp.exp(-e)
    a = _sigmoid(a0 + jnp.dot(alo, a2, precision=HI, preferred_element_type=F32))
    kkraw = k * k_k
    nrm = jnp.sqrt(_hsum(kkraw * kkraw, ones))
    den = jnp.maximum(nrm, 1e-12)
    kk = kkraw / den
    k2 = k * (1.0 + (a - 1.0) * k_a)
    return dict(r=r, k=k, v=v, prev_r=prev_r, prev_k=prev_k, prev_v=prev_v, prev_wa=prev_wa, tw=tw, alo=alo,
                sg=sg, e=e, w=w, a=a, nrm=nrm, den=den, kk=kk, k2=k2)


def _rwkv_pre_specs(seq, nw):
    tok = lambda cb: pl.BlockSpec((seq, LANE), lambda b, j: (b, cb * nw + j))
    par = pl.BlockSpec((1, LANE), lambda b, j: (0, j))
    lora = pl.BlockSpec((LORA, LANE), lambda b, j: (0, j))
    return [tok(4), tok(5), tok(6), pl.BlockSpec((seq, LANE), lambda b, j: (b, 8 * nw)),
            par, par, par, pl.BlockSpec((1, LANE), lambda b, j: (0, 0)), par, lora, par, lora, par, par]


def rwkv_pre_fwd(proj, prm, bsz, seq, nw):
    def body(r_ref, k_ref, v_ref, wa_ref, mur, muk, muv, muwa, w0, w2, a0, a2, kk_, ka_,
             ro, wo, ko, vo, kko, kao):
        m = _rwkv_pre_math(r_ref[...], k_ref[...], v_ref[...], wa_ref[...], mur[...], muk[...], muv[...], muwa[...],
                           w0[...], w2[...], a0[...], a2[...], kk_[...], ka_[...], _head_ones())
        ro[...] = m["r"]
        wo[...] = m["w"]
        ko[...] = m["k2"]
        vo[...] = m["v"]
        kko[...] = m["kk"]
        kao[...] = m["kk"] * m["a"]

    n = bsz * seq
    own = pl.BlockSpec((seq, LANE), lambda b, j: (b, j))
    return pl.pallas_call(
        body, name="rwkv_pre_fwd", grid=(bsz, nw), in_specs=_rwkv_pre_specs(seq, nw), out_specs=[own] * 6,
        out_shape=[jax.ShapeDtypeStruct((n, nw * LANE), F32)] * 6, compiler_params=_cparams(("parallel", "parallel")),
    )(proj, proj, proj, proj, prm["mu_r"], prm["mu_k"], prm["mu_v"], prm["mu_wa"], prm["w0"], prm["w2"],
      prm["a0"], prm["a2"], prm["k_k"], prm["k_a"])


def rwkv_pre_bwd(proj, prm, grads_scan, grads_bonus, bsz, seq, nw):
    wid = nw * LANE

    def body(r_ref, k_ref, v_ref, wa_ref, mur, muk, muv, muwa, w0, w2, a0, a2, kk_, ka_,
             dr_s, dw_s, dk_s, dv_s, dkk_s, dka_s, dr_b, dk_b, dv_b,
             dxr, dxk, dxv, dxwa, dmur, dmuk, dmuv, dmuwa, dw0, dw2, da0, da2, dkk_p, dka_p, dwa_acc):
        b, j = pl.program_id(0), pl.program_id(1)
        cols = pl.ds(pl.multiple_of(j * LANE, LANE), LANE)
        acc_refs = (dmur, dmuk, dmuv, dmuwa, dw0, dw2, da0, da2, dkk_p, dka_p)

        @pl.when(jnp.logical_and(b == 0, j == 0))
        def _():
            for ref in acc_refs:
                ref[...] = jnp.zeros_like(ref)

        @pl.when(j == 0)
        def _():
            dwa_acc[...] = jnp.zeros_like(dwa_acc)

        ones = _head_ones()
        r_in, k_in, v_in, wa_in = r_ref[...], k_ref[...], v_ref[...], wa_ref[...]
        m = _rwkv_pre_math(r_in, k_in, v_in, wa_in, mur[...], muk[...], muv[...], muwa[...],
                           w0[...], w2[...], a0[...], a2[...], kk_[...], ka_[...], ones)
        k, a, kk = m["k"], m["a"], m["kk"]
        dr = dr_s[...] + dr_b[...]
        dk2 = dk_s[...] + dk_b[...]
        dv = dv_s[...] + dv_b[...]
        dka = dka_s[...]
        da = dka * kk + dk2 * k * ka_[...]
        dkk = dkk_s[...] + dka * a
        dkkraw = jnp.where(m["nrm"] > 1e-12, dkk - kk * _hsum(dkk * kk, ones), dkk) / m["den"]
        dk = dkkraw * kk_[...] + dk2 * (1.0 + (a - 1.0) * ka_[...])
        dkk_p[:, cols] += _colsum(dkkraw * k)
        dka_p[:, cols] += _colsum(dk2 * k * (a - 1.0))
        dza = da * a * (1.0 - a)
        da0[:, cols] += _colsum(dza)
        da2[:, cols] += lax.dot_general(m["alo"], dza, TN_DIMS, precision=HI, preferred_element_type=F32)
        dalo = lax.dot_general(dza, a2[...], NT_DIMS, precision=HI, preferred_element_type=F32)
        dzw = -dw_s[...] * m["w"] * m["e"] * (1.0 - m["sg"])
        dw0[:, cols] += _colsum(dzw)
        dw2[:, cols] += lax.dot_general(m["tw"], dzw, TN_DIMS, precision=HI, preferred_element_type=F32)
        dtw = lax.dot_general(dzw, w2[...], NT_DIMS, precision=HI, preferred_element_type=F32)
        dwa_acc[:, 0:LORA] += dtw * (1.0 - m["tw"] * m["tw"])
        dwa_acc[:, LORA:LANE] += dalo

        def shift_bwd(dxs, x, prev, mu):
            g = dxs * mu
            nxt = jnp.where(_iota(g.shape, 0) == seq - 1, 0.0, pltpu.roll(g, seq - 1, 0))
            return dxs * (1.0 - mu) + nxt, _colsum(dxs * (prev - x))

        for dxs, x, prev, mu, out, dmu in ((dr, r_in, m["prev_r"], mur, dxr, dmur), (dk, k_in, m["prev_k"], muk, dxk, dmuk),
                                           (dv, v_in, m["prev_v"], muv, dxv, dmuv)):
            dx, dm = shift_bwd(dxs, x, prev, mu[...])
            out[...] = dx.astype(BF16)
            dmu[:, cols] += dm

        @pl.when(j == nw - 1)
        def _():
            dx, dm = shift_bwd(dwa_acc[...], wa_in, m["prev_wa"], muwa[...])
            dxwa[...] = dx.astype(BF16)
            dmuwa[...] += dm

    n = bsz * seq
    own = pl.BlockSpec((seq, LANE), lambda b, j: (b, j))
    whole = lambda shape: pl.BlockSpec(shape, lambda b, j: (0,) * len(shape))
    row = jax.ShapeDtypeStruct((1, wid), F32)
    return pl.pallas_call(
        body, name="rwkv_pre_bwd", grid=(bsz, nw), in_specs=_rwkv_pre_specs(seq, nw) + [own] * 9,
        out_specs=[own] * 3 + [pl.BlockSpec((seq, LANE), lambda b, j: (b, 0))]
        + [whole((1, wid))] * 3 + [whole((1, LANE)), whole((1, wid)), whole((LORA, wid)), whole((1, wid)),
                                   whole((LORA, wid)), whole((1, wid)), whole((1, wid))],
        out_shape=[jax.ShapeDtypeStruct((n, wid), BF16)] * 3 + [jax.ShapeDtypeStruct((n, LANE), BF16)]
        + [row] * 3 + [jax.ShapeDtypeStruct((1, LANE), F32), row, jax.ShapeDtypeStruct((LORA, wid), F32), row,
                       jax.ShapeDtypeStruct((LORA, wid), F32), row, row],
        scratch_shapes=[pltpu.VMEM((seq, LANE), F32)],
        compiler_params=_cparams(("arbitrary", "arbitrary")),
    )(proj, proj, proj, proj, prm["mu_r"], prm["mu_k"], prm["mu_v"], prm["mu_wa"], prm["w0"], prm["w2"],
      prm["a0"], prm["a2"], prm["k_k"], prm["k_a"], *grads_scan, *grads_bonus)


SCAN_CHUNK = 64


def _scan_group(nw):
    return _tile(nw, 8, 1)


ROWS = 8


def _scan_consts():
    ones = _head_ones().astype(BF16)
    return _head_eye(), ones, jnp.concatenate([ones, ones], axis=0)


def _cols8(tile, eye, ones, ones2, two_parts):
    hi = tile.astype(BF16).astype(F32)
    place = lambda part, q: (part[q:q + 1, :] * eye).astype(BF16)
    if two_parts:
        lo = tile - hi
        lhs = jnp.concatenate([jnp.concatenate([place(hi, q), place(lo, q)], axis=1) for q in range(ROWS)], axis=0)
        big = jnp.dot(lhs, ones2, preferred_element_type=F32)
    else:
        big = jnp.dot(jnp.concatenate([place(hi, q) for q in range(ROWS)], axis=0), ones, preferred_element_type=F32)
    return [big[q * HEAD:(q + 1) * HEAD, :] for q in range(ROWS)]


def _rows8(mats, eye, ones):
    big = jnp.dot(jnp.concatenate([m.astype(BF16) for m in mats], axis=0), ones, preferred_element_type=F32)
    return _stack_rows([_colsum(big[q * HEAD:(q + 1) * HEAD, :] * eye) for q in range(ROWS)])


def _stack_rows(rows):
    rid = _iota((ROWS, LANE), 0)
    out = jnp.zeros((ROWS, LANE), F32)
    for q in range(ROWS):
        out = jnp.where(rid == q, rows[q], out)
    return out


def _grid_ends(grid):
    ids = [pl.program_id(i) for i in range(len(grid))]
    first = functools.reduce(jnp.logical_and, [i == 0 for i in ids])
    last = functools.reduce(jnp.logical_and, [i == g - 1 for i, g in zip(ids, grid)])
    return first, last


def rwkv_scan_fwd(r, w, k, v, kk, ka, bsz, seq, nw, gather):
    grp = _scan_group(nw)
    tc = _tile(seq, SCAN_CHUNK, 8)
    nt = seq // tc
    grid = (bsz, nw // grp, nt)
    ng = len(gather)

    def body(*refs):
        r_ref, w_ref, k_ref, v_ref, kk_ref, ka_ref = refs[:6]
        y_ref, st_ref = refs[6 + ng:8 + ng]
        s_ref = refs[8 + 2 * ng]
        start, wait = _direct_exchange(refs[6:6 + ng], refs[8 + ng:8 + 2 * ng], *refs[9 + 2 * ng:], scatter=False)
        first, last = _grid_ends(grid)
        pl.when(first)(start)

        @pl.when(pl.program_id(2) == 0)
        def _():
            s_ref[...] = jnp.zeros_like(s_ref)

        eye, ones, ones2 = _scan_consts()

        def step(i8, carry):
            base = pl.multiple_of(i8 * ROWS, ROWS)
            for g in range(grp):
                cs = slice(g * LANE, (g + 1) * LANE)
                cols = lambda ref, two: _cols8(ref[pl.ds(base, ROWS), cs], eye, ones, ones2, two)
                rm, wm, km, kkm, kam = cols(r_ref, False), cols(w_ref, True), cols(k_ref, False), cols(kk_ref, False), cols(ka_ref, False)
                vt = v_ref[pl.ds(base, ROWS), cs]
                s = s_ref[g]
                ys = []
                for q in range(ROWS):
                    st_ref[base + q, g] = s
                    sa = _colsum(s * kkm[q])
                    s = s * wm[q] - kam[q] * sa + km[q] * vt[q:q + 1, :]
                    ys.append(_colsum(s * rm[q]))
                y_ref[pl.ds(base, ROWS), cs] = _stack_rows(ys)
                s_ref[g] = s
            return carry

        lax.fori_loop(0, tc // ROWS, step, 0)
        pl.when(last)(wait)

    tok = pl.BlockSpec((tc, grp * LANE), lambda b, g, t: (b * nt + t, g))
    n = bsz * seq
    out = pl.pallas_call(
        body, name="rwkv_scan_fwd", grid=grid, in_specs=[tok] * 6 + [ANY] * ng,
        out_specs=[tok, pl.BlockSpec((tc, grp, HEAD, LANE), lambda b, g, t: (b * nt + t, g, 0, 0))] + [ANY] * ng,
        out_shape=[jax.ShapeDtypeStruct((n, nw * LANE), F32), jax.ShapeDtypeStruct((n, nw, HEAD, LANE), F32)]
        + [jax.ShapeDtypeStruct((N_DEV,) + a.shape, a.dtype) for a in gather],
        scratch_shapes=[pltpu.VMEM((grp, HEAD, LANE), F32)] + _exchange_sems(ng),
        compiler_params=_cparams(("arbitrary", "arbitrary", "arbitrary")),
    )(r, w, k, v, kk, ka, *gather)
    return out[0], out[1], out[2:]


def rwkv_scan_bwd(r, w, k, v, kk, ka, states, dy, bsz, seq, nw, scatter):
    grp = _scan_group(nw)
    tc = _tile(seq, SCAN_CHUNK, 8)
    nt = seq // tc
    nblk = tc // ROWS
    grid = (bsz, nw // grp, nt)
    ns = len(scatter)

    def body(*refs):
        r_ref, w_ref, k_ref, v_ref, kk_ref, ka_ref, st_ref, dy_ref = refs[:8]
        dr_ref, dw_ref, dk_ref, dv_ref, dkk_ref, dka_ref = refs[8 + ns:14 + ns]
        ds_ref = refs[14 + 2 * ns]
        start, wait = _direct_exchange(refs[8:8 + ns], refs[14 + ns:14 + 2 * ns], *refs[15 + 2 * ns:], scatter=True)
        first, last = _grid_ends(grid)
        pl.when(first)(start)

        @pl.when(pl.program_id(2) == 0)
        def _():
            ds_ref[...] = jnp.zeros_like(ds_ref)

        eye, ones, ones2 = _scan_consts()
        ones_f = _head_ones()

        def step(ii, carry):
            base = pl.multiple_of((nblk - 1 - ii) * ROWS, ROWS)
            rows = pl.ds(base, ROWS)
            for g in range(grp):
                cs = slice(g * LANE, (g + 1) * LANE)
                cols = lambda ref, two: _cols8(ref[rows, cs], eye, ones, ones2, two)
                rm, wm, km, kkm, kam = cols(r_ref, False), cols(w_ref, True), cols(k_ref, False), cols(kk_ref, False), cols(ka_ref, False)
                vt, dyt = v_ref[rows, cs], dy_ref[rows, cs]
                ds = ds_ref[g]
                dvs, sas, p_dr, p_dk, p_dka, p_dw, p_dkk = ([None] * ROWS for _ in range(7))
                for q in reversed(range(ROWS)):
                    sp = st_ref[base + q, g]
                    vrow, dyrow = vt[q:q + 1, :], dyt[q:q + 1, :]
                    sa = _colsum(sp * kkm[q])
                    sas[q] = sa
                    ds = ds + rm[q] * dyrow
                    dvs[q] = _colsum(ds * km[q])
                    dsa = -_colsum(ds * kam[q])
                    p_dr[q] = sp * dyrow
                    p_dk[q] = ds * vrow
                    p_dka[q] = -(ds * sa)
                    p_dw[q] = ds * sp
                    p_dkk[q] = sp * dsa
                    ds = ds * wm[q] + kkm[q] * dsa
                ds_ref[g] = ds
                dv_ref[rows, cs] = _stack_rows(dvs)
                sa_dy = _hsum(_stack_rows(sas) * dyt, ones_f)
                v_dy = _hsum(vt * dyt, ones_f)
                dr_ref[rows, cs] = w_ref[rows, cs] * _rows8(p_dr, eye, ones) - ka_ref[rows, cs] * sa_dy + k_ref[rows, cs] * v_dy
                dk_ref[rows, cs] = _rows8(p_dk, eye, ones)
                dka_ref[rows, cs] = _rows8(p_dka, eye, ones)
                dw_ref[rows, cs] = _rows8(p_dw, eye, ones)
                dkk_ref[rows, cs] = _rows8(p_dkk, eye, ones)
            return carry

        lax.fori_loop(0, nblk, step, 0)
        pl.when(last)(wait)

    tok = pl.BlockSpec((tc, grp * LANE), lambda b, g, t: (b * nt + nt - 1 - t, g))
    n = bsz * seq
    out = pl.pallas_call(
        body, name="rwkv_scan_bwd", grid=grid,
        in_specs=[tok] * 6 + [pl.BlockSpec((tc, grp, HEAD, LANE), lambda b, g, t: (b * nt + nt - 1 - t, g, 0, 0)), tok] + [ANY] * ns,
        out_specs=[tok] * 6 + [ANY] * ns,
        out_shape=[jax.ShapeDtypeStruct((n, nw * LANE), F32)] * 6 + [jax.ShapeDtypeStruct(a.shape, a.dtype) for a in scatter],
        scratch_shapes=[pltpu.VMEM((grp, HEAD, LANE), F32)] + _exchange_sems(ns),
        compiler_params=_cparams(("arbitrary", "arbitrary", "arbitrary")),
    )(r, w, k, v, kk, ka, states, dy, *scatter)
    return out[:6], out[6:]


def _rwkv_post_math(y, r, k2, v, ln_g, ln_b, r_k, ones):
    d = y - _hsum(y, ones) * (1.0 / HEAD)
    rstd = lax.rsqrt(_hsum(d * d, ones) * (1.0 / HEAD) + GN_EPS)
    yn = d * rstd
    s = _hsum(r * k2 * r_k, ones)
    return yn, rstd, s, yn * ln_g + ln_b + s * v


def _rwkv_post_specs(tm, nw):
    tok = pl.BlockSpec((tm, LANE), lambda i, j: (i, j))
    par = pl.BlockSpec((1, LANE), lambda i, j: (0, j))
    return [tok] * 4 + [pl.BlockSpec((tm, LANE), lambda i, j: (i, 7 * nw + j)), par, par, par]


def rwkv_post_fwd(y, r, k2, v, proj, prm, nw):
    n = y.shape[0]
    tm = _tile(n, 512, 8)

    def body(y_ref, r_ref, k_ref, v_ref, g_ref, lg, lb, rk, o_ref):
        ob = _rwkv_post_math(y_ref[...], r_ref[...], k_ref[...], v_ref[...], lg[...], lb[...], rk[...], _head_ones())[3]
        o_ref[...] = (ob * _silu_and_grad(g_ref[...])[0]).astype(BF16)

    return pl.pallas_call(
        body, name="rwkv_post_fwd", grid=(n // tm, nw), in_specs=_rwkv_post_specs(tm, nw),
        out_specs=pl.BlockSpec((tm, LANE), lambda i, j: (i, j)), out_shape=jax.ShapeDtypeStruct((n, nw * LANE), BF16),
        compiler_params=_cparams(("parallel", "parallel")),
    )(y, r, k2, v, proj, prm["ln_g"], prm["ln_b"], prm["r_k"])


def rwkv_post_bwd(y, r, k2, v, proj, prm, dycat, nw):
    n = y.shape[0]
    tm = _tile(n, 512, 8)
    wid = nw * LANE

    def body(y_ref, r_ref, k_ref, v_ref, g_ref, lg, lb, rk, dyc_ref,
             dy_ref, dr_ref, dk_ref, dv_ref, dg_ref, dlg, dlb, drk):
        i, j = pl.program_id(0), pl.program_id(1)
        cols = pl.ds(pl.multiple_of(j * LANE, LANE), LANE)

        @pl.when(jnp.logical_and(i == 0, j == 0))
        def _():
            for ref in (dlg, dlb, drk):
                ref[...] = jnp.zeros_like(ref)

        ones = _head_ones()
        rr, kr, vr = r_ref[...], k_ref[...], v_ref[...]
        yn, rstd, s, ob = _rwkv_post_math(y_ref[...], rr, kr, vr, lg[...], lb[...], rk[...], ones)
        silu, dsilu = _silu_and_grad(g_ref[...])
        dyc = dyc_ref[...].astype(F32)
        dg_ref[...] = (dyc * ob * dsilu).astype(BF16)
        dob = dyc * silu
        dlg[:, cols] += _colsum(dob * yn)
        dlb[:, cols] += _colsum(dob)
        dyn = dob * lg[...]
        dy_ref[...] = rstd * (dyn - _hsum(dyn, ones) * (1.0 / HEAD) - yn * _hsum(dyn * yn, ones) * (1.0 / HEAD))
        dv_ref[...] = dob * s
        dsum = _hsum(dob * vr, ones)
        dr_ref[...] = dsum * kr * rk[...]
        dk_ref[...] = dsum * rr * rk[...]
        drk[:, cols] += _colsum(dsum * rr * kr)

    tok = pl.BlockSpec((tm, LANE), lambda i, j: (i, j))
    whole = pl.BlockSpec((1, wid), lambda i, j: (0, 0))
    return pl.pallas_call(
        body, name="rwkv_post_bwd", grid=(n // tm, nw),
        in_specs=_rwkv_post_specs(tm, nw) + [pl.BlockSpec((tm, LANE), lambda i, j: (i, nw + j))],
        out_specs=[tok] * 5 + [whole] * 3,
        out_shape=[jax.ShapeDtypeStruct((n, wid), F32)] * 4 + [jax.ShapeDtypeStruct((n, wid), BF16)]
        + [jax.ShapeDtypeStruct((1, wid), F32)] * 3,
        compiler_params=_cparams(("arbitrary", "arbitrary")),
    )(y, r, k2, v, proj, prm["ln_g"], prm["ln_b"], prm["r_k"], dycat)


def _sgu_math(blk, ln_g, ln_b, ws_ref, bb_ref, mixed_ref, d, ngr):
    u, v, g = blk[:, 0:d], blk[:, d:2 * d], blk[:, 2 * d:3 * d]
    gu, dgu = _gelu_and_grad(u)
    gv, dgv = _gelu_and_grad(v)
    cen = gv - jnp.mean(gv, axis=-1, keepdims=True)
    rstd = lax.rsqrt(jnp.mean(cen * cen, axis=-1, keepdims=True) + LN_EPS)
    vhat = cen * rstd
    vn = vhat * ln_g + ln_b
    tril = _iota((GMLP_CHUNK, GMLP_CHUNK), 0) >= _iota((GMLP_CHUNK, GMLP_CHUNK), 1)
    for gi in range(ngr):
        cs = slice(gi * LANE, (gi + 1) * LANE)
        wm = jnp.where(tril, ws_ref[gi], 0.0)
        mixed_ref[:, cs] = jnp.dot(wm, vn[:, cs], precision=HI, preferred_element_type=F32) + bb_ref[gi]
    return dict(g=g, gu=gu, dgu=dgu, dgv=dgv, rstd=rstd, vhat=vhat, vn=vn, tril=tril)


def sgu_fwd(proj, ln_g, ln_b, w_s, b_bc):
    n, d3 = proj.shape
    d = d3 // 3
    ngr = d // LANE

    def body(p_ref, lg, lb, ws_ref, bb_ref, o_ref, mixed_ref):
        m = _sgu_math(p_ref[...], lg[...], lb[...], ws_ref, bb_ref, mixed_ref, d, ngr)
        o_ref[...] = (m["gu"] * mixed_ref[...] * _silu_and_grad(m["g"])[0]).astype(BF16)

    return pl.pallas_call(
        body, name="sgu_fwd", grid=(n // GMLP_CHUNK,),
        in_specs=[_row_spec(GMLP_CHUNK, d3), _full_spec((1, d)), _full_spec((1, d)),
                  _full_spec((ngr, GMLP_CHUNK, GMLP_CHUNK)), _full_spec((ngr, GMLP_CHUNK, LANE))],
        out_specs=_row_spec(GMLP_CHUNK, d), out_shape=jax.ShapeDtypeStruct((n, d), BF16),
        scratch_shapes=[pltpu.VMEM((GMLP_CHUNK, d), F32)], compiler_params=_cparams(("parallel",)),
    )(proj, ln_g, ln_b, w_s, b_bc)


def sgu_bwd(proj, ln_g, ln_b, w_s, b_bc, dyin):
    n, d3 = proj.shape
    d = d3 // 3
    ngr = d // LANE
    nsteps = n // GMLP_CHUNK

    def body(p_ref, lg, lb, ws_ref, bb_ref, dy_ref, dp_ref, dws_ref, dbs_ref, dlg_ref, dlb_ref,
             mixed_ref, dvn_ref, dbacc_ref):
        step = pl.program_id(0)

        @pl.when(step == 0)
        def _():
            for ref in (dws_ref, dlg_ref, dlb_ref, dbacc_ref):
                ref[...] = jnp.zeros_like(ref)

        m = _sgu_math(p_ref[...], lg[...], lb[...], ws_ref, bb_ref, mixed_ref, d, ngr)
        silu, dsilu = _silu_and_grad(m["g"])
        dyv = dy_ref[...].astype(F32)
        mixed = mixed_ref[...]
        dp_ref[:, 2 * d:3 * d] = (dyv * m["gu"] * mixed * dsilu).astype(BF16)
        doc = dyv * silu
        dp_ref[:, 0:d] = (doc * mixed * m["dgu"]).astype(BF16)
        dmixed = doc * m["gu"]
        for gi in range(ngr):
            cs = slice(gi * LANE, (gi + 1) * LANE)
            dm = dmixed[:, cs]
            wm = jnp.where(m["tril"], ws_ref[gi], 0.0)
            dws_ref[gi] += jnp.where(m["tril"], lax.dot_general(dm, m["vn"][:, cs], NT_DIMS, precision=HI,
                                                                preferred_element_type=F32), 0.0)
            dbacc_ref[gi] += dm
            dvn_ref[:, cs] = lax.dot_general(wm, dm, TN_DIMS, precision=HI, preferred_element_type=F32)
        dvn = dvn_ref[...]
        dlg_ref[...] += _colsum(dvn * m["vhat"])
        dlb_ref[...] += _colsum(dvn)
        dvh = dvn * lg[...]
        dgv = m["rstd"] * (dvh - jnp.mean(dvh, axis=-1, keepdims=True)
                           - m["vhat"] * jnp.mean(dvh * m["vhat"], axis=-1, keepdims=True))
        dp_ref[:, d:2 * d] = (dgv * m["dgv"]).astype(BF16)

        @pl.when(step == nsteps - 1)
        def _():
            dbs_ref[...] = jnp.sum(dbacc_ref[...], axis=-1, keepdims=True)

    return pl.pallas_call(
        body, name="sgu_bwd", grid=(nsteps,),
        in_specs=[_row_spec(GMLP_CHUNK, d3), _full_spec((1, d)), _full_spec((1, d)),
                  _full_spec((ngr, GMLP_CHUNK, GMLP_CHUNK)), _full_spec((ngr, GMLP_CHUNK, LANE)), _row_spec(GMLP_CHUNK, d)],
        out_specs=[_row_spec(GMLP_CHUNK, d3), _full_spec((ngr, GMLP_CHUNK, GMLP_CHUNK)), _full_spec((ngr, GMLP_CHUNK, 1)),
                   _full_spec((1, d)), _full_spec((1, d))],
        out_shape=[jax.ShapeDtypeStruct((n, d3), BF16), jax.ShapeDtypeStruct((ngr, GMLP_CHUNK, GMLP_CHUNK), F32),
                   jax.ShapeDtypeStruct((ngr, GMLP_CHUNK, 1), F32), jax.ShapeDtypeStruct((1, d), F32),
                   jax.ShapeDtypeStruct((1, d), F32)],
        scratch_shapes=[pltpu.VMEM((GMLP_CHUNK, d), F32), pltpu.VMEM((GMLP_CHUNK, d), F32),
                        pltpu.VMEM((ngr, GMLP_CHUNK, LANE), F32)],
        compiler_params=_cparams(("arbitrary",)),
    )(proj, ln_g, ln_b, w_s, b_bc, dyin)


def _ab_segments(wid, heads):
    return ((0, 3 * wid, 0),
            (3 * wid, 3 * wid + heads, 8 * wid + LANE),
            (3 * wid + heads, 4 * wid + heads, 3 * wid),
            (4 * wid + heads, 7 * wid + heads, 4 * wid),
            (7 * wid + heads, 7 * wid + heads + 2 * LORA, 8 * wid),
            (7 * wid + heads + 2 * LORA, 8 * wid + heads + 2 * LORA, 7 * wid))


def _ab_reorder(shards, wid, heads):
    per = shards[0].shape[1]
    pieces = []
    for o0, o1, _ in sorted(_ab_segments(wid, heads), key=lambda s: s[2]):
        for d, sh in enumerate(shards):
            lo, hi = max(o0, d * per), min(o1, (d + 1) * per)
            if lo < hi:
                pieces.append(sh[:, lo - d * per:hi - d * per])
    pieces.append(jnp.zeros((shards[0].shape[0], LANE - heads), shards[0].dtype))
    return jnp.concatenate(pieces, axis=1)


def _ab_restore(g, wid, heads, n_shards):
    per = (8 * wid + heads + 2 * LORA) // n_shards
    out = []
    for d in range(n_shards):
        pieces = []
        for o0, o1, r0 in _ab_segments(wid, heads):
            lo, hi = max(o0, d * per), min(o1, (d + 1) * per)
            if lo < hi:
                pieces.append(g[:, r0 + lo - o0:r0 + hi - o0])
        out.append(jnp.concatenate(pieces, axis=1))
    return out


ODD_PIECES = ("c_w_in", "c_w_out", "ple_w_gate1", "ple_w_proj1", "ab_w_out", "ple_w_gate0", "ple_w_proj0")
ODD_BY_ROWS = (0, 1, 1, 0, 1, 1, 0)


def local_step(x, p, target, wts, n_shards, odd_shards):
    bsz, seq, d = x.shape
    n = bsz * seq
    wid = d // 2
    nw = wid // LANE
    heads = wid // HEAD
    x2 = x.reshape(n, d)
    tgt = target.reshape(n, d)
    p0, p1 = p[0].reshape(n, -1), p[1].reshape(n, -1)
    row = lambda a: a.reshape(1, -1)
    g_pre0, g_pre1 = row(wts["norm_pre"][0]), row(wts["norm_pre"][1])
    g_post0, g_post1 = row(wts["norm_post"][0]), row(wts["norm_post"][1])
    w_ab = wts["ab_w_in_r"]
    rows_cut = lambda a: a.reshape((n_shards, a.shape[0] // n_shards) + a.shape[1:])
    fbias = jnp.pad(row(wts["fox_f_bias"]), ((0, 0), (0, LANE - heads)))
    mu = row(wts["rwkv_mu"])
    prm = dict(mu_r=mu[:, 0:wid], mu_k=mu[:, wid:2 * wid], mu_v=mu[:, 2 * wid:3 * wid], mu_wa=mu[:, 3 * wid:],
               w0=row(wts["rwkv_w0"]), w2=wts["rwkv_w2"], a0=row(wts["rwkv_a0"]), a2=wts["rwkv_a2"],
               k_k=row(wts["rwkv_k_k"]), k_a=row(wts["rwkv_k_a"]), r_k=row(wts["rwkv_r_k"]),
               ln_g=row(wts["rwkv_ln_g"]), ln_b=row(wts["rwkv_ln_b"]))
    c_ln_g, c_ln_b = row(wts["c_ln_g"]), row(wts["c_ln_b"])
    w_s = wts["c_w_s"]
    b_bc = jnp.broadcast_to(wts["c_b_s"][:, :, None], w_s.shape[:2] + (LANE,))

    xn0 = rms_fwd(x2, g_pre0, name="rms_pre0")
    proj0 = matmul(xn0, w_ab, name="ab_in")
    c = fox_gate_fwd(proj0, fbias, bsz, seq, nw)
    crow = jnp.pad(c.reshape(bsz, seq, LANE)[:, :, :heads].transpose(0, 2, 1).reshape(bsz, nw, 2, seq),
                   ((0, 0), (0, 0), (0, 6), (0, 0)))
    oa, ya = fox_attn_fwd(proj0, c, crow, bsz, seq, nw)
    sr, sw, sk, sv, skk, ska = rwkv_pre_fwd(proj0, prm, bsz, seq, nw)
    ysc, states, odd_w = rwkv_scan_fwd(sr, sw, sk, sv, skk, ska, bsz, seq, nw, [odd_shards[piece] for piece in ODD_PIECES])
    w_c, w_cout, wg1, wp1, w_out, wg0, wp0 = (_join_shards(a, 1 - by_rows) for a, by_rows in zip(odd_w, ODD_BY_ROWS))
    yb = rwkv_post_fwd(ysc, sr, sk, sv, proj0, prm, nw)
    ycat = jnp.concatenate([ya, yb], axis=1)
    y0 = matmul(ycat, w_out, name="ab_out")
    h1, h1b = rms_res_fwd(y0, x2, g_post0, name="rms_post0")
    a0 = matmul(p0, wp0, name="ple_proj0")
    z0 = matmul(h1b, wg0, name="ple_gate0")
    h1p = ple_fwd(h1, a0, z0, name="ple_fwd0")
    xn1 = rms_fwd(h1p, g_pre1, name="rms_pre1")
    proj1 = matmul(xn1, w_c, name="c_in")
    yin = sgu_fwd(proj1, c_ln_g, c_ln_b, w_s, b_bc)
    y1 = matmul(yin, w_cout, name="c_out")
    h2, h2b = rms_res_fwd(y1, h1p, g_post1, name="rms_post1")
    a1 = matmul(p1, wp1, name="ple_proj1")
    z1 = matmul(h2b, wg1, name="ple_gate1")
    h2p = ple_fwd(h2, a1, z1, name="ple_fwd1")
    dh, loss = loss_head(h2p, tgt, name="loss_head")

    g = {}
    da1, dz1 = ple_bwd(dh, a1, z1, name="ple_bwd1")
    g["ple_w_proj1"] = matmul(p1, da1, ta=True, out_dtype=BF16, col_blocks=n_shards, name="d_ple_proj1")
    g["ple_w_gate1"] = rows_cut(matmul(h2b, dz1, ta=True, out_dtype=BF16, name="d_ple_gate1"))
    dh2 = matmul(dz1, wg1, tb=True, add=dh, name="dx_ple_gate1")
    dy1, g_post1_g = rms_bwd(y1, dh2, g_post1, out_dtype=BF16, name="rms_post1_bwd")
    g["c_w_out"] = rows_cut(matmul(yin, dy1, ta=True, out_dtype=BF16, name="d_c_out"))
    dyin = matmul(dy1, w_cout, tb=True, out_dtype=BF16, name="dx_c_out")
    dproj1, g["c_w_s"], g_bs, g["c_ln_g"], g["c_ln_b"] = sgu_bwd(proj1, c_ln_g, c_ln_b, w_s, b_bc, dyin)
    g["c_b_s"] = g_bs.reshape(w_s.shape[:2])
    g["c_w_in"] = matmul(xn1, dproj1, ta=True, out_dtype=BF16, col_blocks=n_shards, name="d_c_in")
    dxn1 = matmul(dproj1, w_c, tb=True, out_dtype=BF16, name="dx_c_in")
    dh1p, g_pre1_g = rms_bwd(h1p, dxn1, g_pre1, res=dh2, name="rms_pre1_bwd")
    da0, dz0 = ple_bwd(dh1p, a0, z0, name="ple_bwd0")
    g["ple_w_proj0"] = matmul(p0, da0, ta=True, out_dtype=BF16, col_blocks=n_shards, name="d_ple_proj0")
    g["ple_w_gate0"] = rows_cut(matmul(h1b, dz0, ta=True, out_dtype=BF16, name="d_ple_gate0"))
    dh1 = matmul(dz0, wg0, tb=True, add=dh1p, name="dx_ple_gate0")
    dy0, g_post0_g = rms_bwd(y0, dh1, g_post0, out_dtype=BF16, name="rms_post0_bwd")
    g["ab_w_out"] = rows_cut(matmul(ycat, dy0, ta=True, out_dtype=BF16, name="d_ab_out"))
    dycat = matmul(dy0, w_out, tb=True, out_dtype=BF16, name="dx_ab_out")
    dysc, dr_b, dk_b, dv_b, dgb, g["rwkv_ln_g"], g["rwkv_ln_b"], g_rk = rwkv_post_bwd(ysc, sr, sk, sv, proj0, prm, dycat, nw)
    g["rwkv_r_k"] = g_rk.reshape(wts["rwkv_r_k"].shape)
    grads_scan, odd_landed = rwkv_scan_bwd(sr, sw, sk, sv, skk, ska, states, dysc, bsz, seq, nw, [g.pop(piece) for piece in ODD_PIECES])
    (dxr, dxk, dxv, dxwa, dmur, dmuk, dmuv, dmuwa, g["rwkv_w0"], g["rwkv_w2"], g["rwkv_a0"], g["rwkv_a2"],
     g["rwkv_k_k"], g["rwkv_k_a"]) = rwkv_pre_bwd(proj0, prm, grads_scan, (dr_b, dk_b, dv_b), bsz, seq, nw)
    g["rwkv_mu"] = jnp.concatenate([dmur, dmuk, dmuv, dmuwa], axis=1)
    dq, dk, dv, dga, dcrow = fox_attn_bwd(proj0, c, crow, oa, dycat, bsz, seq, nw)
    dc = jnp.pad(dcrow[:, :, :2, :].reshape(bsz, heads, seq).transpose(0, 2, 1), ((0, 0), (0, 0), (0, LANE - heads)))
    dfa, g_fb = fox_gate_bwd(proj0, fbias, dc.reshape(n, LANE), bsz, seq, nw)
    g["fox_f_bias"] = g_fb[:, :heads]
    dproj0 = jnp.concatenate([dq, dk, dv, dga, dxr, dxk, dxv, dgb, dxwa, dfa], axis=1)
    cut = 3 * d // 8
    shard_major = lambda full: jnp.stack(_ab_restore(full, wid, heads, n_shards))
    g_lo = shard_major(matmul(xn0[:, :cut], dproj0, ta=True, out_dtype=BF16, name="d_ab_in_lo"))
    g_hi, (lo_landed,) = matmul(xn0[:, cut:], dproj0, ta=True, out_dtype=BF16, scatter=[g_lo], name="d_ab_in_hi")
    dxn0, (hi_landed,) = matmul(dproj0, w_ab, tb=True, out_dtype=BF16, scatter=[shard_major(g_hi)], name="dx_ab_in")
    ab_landed = jnp.concatenate([lo_landed, hi_landed], axis=1)
    dx, g_pre0_g = rms_bwd(x2, dxn0, g_pre0, res=dh1, name="rms_pre0_bwd")

    g["norm_pre"] = jnp.concatenate([g_pre0_g, g_pre1_g], axis=0)
    g["norm_post"] = jnp.concatenate([g_post0_g, g_post1_g], axis=0)
    return loss, dx.reshape(bsz, seq, d), g, dict(zip(ODD_PIECES + ("ab_w_in",), tuple(odd_landed) + (ab_landed,)))


MESH = pl.DeviceIdType.MESH
ANY = pl.BlockSpec(memory_space=pl.ANY)
PACK_COLS = 1024
PACK_ROWS = 16


def _mesh_place():
    xi, yi, ci = lax.axis_index("x"), lax.axis_index("y"), lax.axis_index("c")
    return xi, yi, ci, 4 * xi + 2 * yi + ci


def _peer(xi, yi, ci, m):
    px = 1 - xi if m & 4 else xi
    py = 1 - yi if m & 2 else yi
    pc = 1 - ci if m & 1 else ci
    return (px, py, pc), 4 * px + 2 * py + pc


def _gather(arrays, *, name):
    n = len(arrays)

    def body(*refs):
        ins, outs = refs[:n], refs[n:2 * n]
        send_sems, recv_sems, local_sems = refs[2 * n:]
        xi, yi, ci, me = _mesh_place()
        sibling = (xi, yi, 1 - ci)
        chips = [(1 - xi, yi), (xi, 1 - yi), (1 - xi, 1 - yi)]
        block = lambda px, py, pc: 4 * px + 2 * py + pc

        def copy(a, k, blk, to, src=None):
            return pltpu.make_async_remote_copy(src_ref=outs[a].at[blk] if src is None else src, dst_ref=outs[a].at[blk],
                                                send_sem=send_sems.at[k, a], recv_sem=recv_sems.at[k, a],
                                                device_id=to, device_id_type=MESH)

        started = []
        for a in range(n):
            cp = pltpu.make_async_copy(ins[a], outs[a].at[me], local_sems.at[a])
            cp.start()
            started.append(cp)
        sends = []
        for a in range(n):
            sends.append(copy(a, 0, me, sibling, src=ins[a]))
            sends += [copy(a, 1 + j, me, (*chip, ci), src=ins[a]) for j, chip in enumerate(chips)]
        for cp in sends:
            cp.start()
        for j, chip in enumerate(chips):
            for a in range(n):
                copy(a, 1 + j, block(*chip, ci), (xi, yi, ci)).wait_recv()
                fwd = copy(a, 4 + j, block(*chip, ci), sibling)
                fwd.start()
                sends.append(fwd)
        for a in range(n):
            copy(a, 0, block(xi, yi, 1 - ci), (xi, yi, ci)).wait_recv()
            for j, chip in enumerate(chips):
                copy(a, 4 + j, block(*chip, 1 - ci), (xi, yi, ci)).wait_recv()
        for cp in sends:
            cp.wait_send()
        for cp in started:
            cp.wait()

    return pl.pallas_call(
        body, name=name, in_specs=[ANY] * n, out_specs=[ANY] * n,
        out_shape=[jax.ShapeDtypeStruct((N_DEV,) + a.shape, a.dtype) for a in arrays],
        scratch_shapes=[pltpu.SemaphoreType.DMA((N_DEV - 1, n)), pltpu.SemaphoreType.DMA((N_DEV - 1, n)),
                        pltpu.SemaphoreType.DMA((n,))],
    )(*arrays)


def _exchange_sems(n):
    if n == 0:
        return []
    return [pltpu.SemaphoreType.DMA((N_DEV - 1, n)), pltpu.SemaphoreType.DMA((N_DEV - 1, n)), pltpu.SemaphoreType.DMA((n,))]


def _direct_exchange(ins, outs, send_sems=None, recv_sems=None, local_sems=None, *, scatter):
    n = len(ins)

    def copies():
        xi, yi, ci, me = _mesh_place()
        src = lambda a, idx: ins[a].at[idx] if scatter else ins[a]
        local = [pltpu.make_async_copy(src(a, me), outs[a].at[me], local_sems.at[a]) for a in range(n)]
        sends, arrivals = [], []
        for m in range(1, N_DEV):
            peer, pidx = _peer(xi, yi, ci, m)
            for a in range(n):
                pair = dict(send_sem=send_sems.at[m - 1, a], recv_sem=recv_sems.at[m - 1, a], device_id=peer, device_id_type=MESH)
                sends.append(pltpu.make_async_remote_copy(src_ref=src(a, pidx), dst_ref=outs[a].at[me], **pair))
                arrivals.append(pltpu.make_async_remote_copy(src_ref=src(a, pidx), dst_ref=outs[a].at[pidx], **pair))
        return local, sends, arrivals

    def start():
        local, sends, _ = copies()
        for cp in local + sends:
            cp.start()

    def wait():
        local, sends, arrivals = copies()
        for cp in arrivals:
            cp.wait_recv()
        for cp in sends:
            cp.wait_send()
        for cp in local:
            cp.wait()

    return start, wait


def sum_blocks(x, *, name):
    _, rows, cols = x.shape
    tr = _tile(rows, 256, PACK_ROWS)

    def body(x_ref, o_ref):
        acc = x_ref[0].astype(F32)
        for s in range(1, N_DEV):
            acc = acc + x_ref[s].astype(F32)
        o_ref[...] = acc

    return pl.pallas_call(
        body, name=name, grid=(rows // tr,), in_specs=[pl.BlockSpec((N_DEV, tr, cols), lambda i: (0, i, 0))],
        out_specs=pl.BlockSpec((tr, cols), lambda i: (i, 0)), out_shape=jax.ShapeDtypeStruct((rows, cols), F32),
        compiler_params=_cparams(("parallel",)),
    )(x)


def adamw(w, parts, m, v, *, name):
    rows, cols = w.shape
    nparts = parts.shape[0]
    tr = _tile(rows, 256, 16)
    c1 = 1.0 / (1.0 - ADAM_B1 ** ADAM_STEP)
    c2 = 1.0 / (1.0 - ADAM_B2 ** ADAM_STEP)

    def body(w_ref, p_ref, m_ref, v_ref, g_ref, d_ref, mo_ref, vo_ref):
        gv = p_ref[0].astype(F32)
        for s in range(1, nparts):
            gv = gv + p_ref[s].astype(F32)
        mn = ADAM_B1 * m_ref[...] + (1.0 - ADAM_B1) * gv
        vn = ADAM_B2 * v_ref[...] + (1.0 - ADAM_B2) * (gv * gv)
        g_ref[...] = gv
        d_ref[...] = -ADAM_LR * ((mn * c1) / (jnp.sqrt(vn * c2) + ADAM_EPS) + ADAM_WD * w_ref[...])
        mo_ref[...] = mn
        vo_ref[...] = vn

    spec = pl.BlockSpec((tr, cols), lambda i: (i, 0))
    return pl.pallas_call(
        body, name=name, grid=(rows // tr,), in_specs=[spec, pl.BlockSpec((nparts, tr, cols), lambda i: (0, i, 0)), spec, spec],
        out_specs=[spec] * 4, out_shape=[jax.ShapeDtypeStruct((rows, cols), F32)] * 4, compiler_params=_cparams(("parallel",)),
    )(w, parts, m, v)


def _pack(arrays, dtype):
    flat = jnp.concatenate([a.astype(dtype).reshape(-1) for a in arrays])
    unit = PACK_COLS * PACK_ROWS
    total = -(-flat.shape[0] // unit) * unit
    return jnp.pad(flat, (0, total - flat.shape[0])).reshape(total // PACK_COLS, PACK_COLS)


def _unpack(flat2d, shapes, lead=()):
    flat = flat2d.reshape(lead + (-1,))
    out, off = [], 0
    for shp in shapes:
        size = 1
        for s in shp:
            size *= s
        out.append(flat[..., off:off + size].reshape(lead + tuple(shp)))
        off += size
    return out


def _join_shards(sh, axis):
    return jnp.concatenate([sh[d] for d in range(N_DEV)], axis=axis)


BIG = (("ab_w_in", "ab_w_in", None, False), ("c_w_in", "c_w_in", None, False), ("ab_w_out", "ab_w_out", None, True),
       ("c_w_out", "c_w_out", None, True), ("ple_w_gate0", "ple_w_gate", 0, True), ("ple_w_gate1", "ple_w_gate", 1, True),
       ("ple_w_proj0", "ple_w_proj", 0, False), ("ple_w_proj1", "ple_w_proj", 1, False))
SMALL_SHARDED = (("rwkv_w2", 1), ("rwkv_a2", 1), ("c_ln_g", 0), ("c_ln_b", 0))
REPLICATED = ("norm_pre", "norm_post", "fox_f_bias", "rwkv_mu", "rwkv_w0", "rwkv_a0", "rwkv_k_k", "rwkv_k_a", "rwkv_r_k",
              "rwkv_ln_g", "rwkv_ln_b", "c_w_s", "c_b_s")
WEIGHTS = ("norm_pre", "norm_post", "ab_w_in", "fox_f_bias", "rwkv_mu", "rwkv_w0", "rwkv_w2", "rwkv_a0", "rwkv_a2", "rwkv_k_k",
           "rwkv_k_a", "rwkv_r_k", "rwkv_ln_g", "rwkv_ln_b", "ab_w_out", "c_w_in", "c_ln_g", "c_ln_b", "c_w_s", "c_b_s",
           "c_w_out", "ple_w_proj", "ple_w_gate")
SQUEEZED = ("norm_pre", "norm_post", "ple_w_proj", "ple_w_gate")


def _step(x, p, loss_target, w, mom, vel):
    sq = {k: (a if k in SQUEEZED else a[0]) for k, a in w.items()}
    _, _, _, me = _mesh_place()

    wid, heads = x.shape[-1] // 2, x.shape[-1] // 2 // HEAD
    layer = lambda a, l: a if l is None else a[l]

    shards = {piece: layer(sq[wname], l).astype(BF16) for piece, wname, l, _ in BIG}
    even = [entry for entry in BIG if entry[0] not in ODD_PIECES]
    sends = [shards[piece] for piece, _, _, _ in even]
    sends += [sq[k] if sq[k].ndim == 2 else sq[k].reshape(1, -1) for k, _ in SMALL_SHARDED]
    gathered = _gather(sends, name="gather_weights")
    full = {}
    for (piece, _, _, by_rows), got in zip(even, gathered):
        if piece == "ab_w_in":
            full["ab_w_in_r"] = _ab_reorder([got[d] for d in range(N_DEV)], wid, heads)
        else:
            full[piece] = _join_shards(got, 1 - by_rows)
    for (k, ax), got in zip(SMALL_SHARDED, gathered[len(even):]):
        full[k] = _join_shards(got, 1).reshape(-1) if sq[k].ndim == 1 else _join_shards(got, 1)
    for k in REPLICATED:
        full[k] = sq[k]

    loss, grad_x, g, parts = local_step(x, p, loss_target, full, N_DEV, {piece: shards[piece] for piece in ODD_PIECES})

    small_names = [k for k, _ in SMALL_SHARDED] + list(REPLICATED)
    small_full_shapes = [full[k].shape for k in small_names] + [(1, 1)]
    partial = _gather([_pack([g[k].reshape(full[k].shape) for k in small_names] + [loss], F32)], name="gather_small_grads")[0]
    summed = _unpack(sum_blocks(partial, name="sum_small_grads"), small_full_shapes)
    small_grads = dict(zip(small_names, summed[:-1]))
    for k, ax in SMALL_SHARDED:
        width = sq[k].shape[ax]
        small_grads[k] = lax.dynamic_slice_in_dim(small_grads[k], me * width, width, axis=ax)
    loss_all = summed[-1][0, 0]

    outs_g, outs_d, outs_m, outs_v = [], [], [], []
    for k in WEIGHTS:
        shape = w[k].shape
        two_d = lambda a: a.reshape(-1, a.shape[-1])
        pieces = [(piece, l) for piece, wname, l, _ in BIG if wname == k]
        if not pieces:
            res = adamw(two_d(w[k]), two_d(small_grads[k].reshape(shape))[None], two_d(mom[k]), two_d(vel[k]), name="adamw_" + k)
        elif pieces[0][1] is None:
            res = adamw(two_d(w[k]), parts[k], two_d(mom[k]), two_d(vel[k]), name="adamw_" + k)
        else:
            per_layer = [adamw(w[k][l], parts[piece], mom[k][l], vel[k][l], name="adamw_" + piece) for piece, l in pieces]
            res = [jnp.stack(r) for r in zip(*per_layer)]
        for out, r in zip((outs_g, outs_d, outs_m, outs_v), res):
            out.append(r.reshape(shape))
    return (loss_all, grad_x, *outs_g, *outs_d, *outs_m, *outs_v)


def kernel(x, p, norm_pre, norm_post, ab_w_in, fox_f_bias, rwkv_mu, rwkv_w0, rwkv_w2, rwkv_a0, rwkv_a2, rwkv_k_k, rwkv_k_a, rwkv_r_k, rwkv_ln_g, rwkv_ln_b, ab_w_out, c_w_in, c_ln_g, c_ln_b, c_w_s, c_b_s, c_w_out, ple_w_proj, ple_w_gate, loss_target, m_norm_pre, m_norm_post, m_ab_w_in, m_fox_f_bias, m_rwkv_mu, m_rwkv_w0, m_rwkv_w2, m_rwkv_a0, m_rwkv_a2, m_rwkv_k_k, m_rwkv_k_a, m_rwkv_r_k, m_rwkv_ln_g, m_rwkv_ln_b, m_ab_w_out, m_c_w_in, m_c_ln_g, m_c_ln_b, m_c_w_s, m_c_b_s, m_c_w_out, m_ple_w_proj, m_ple_w_gate, v_norm_pre, v_norm_post, v_ab_w_in, v_fox_f_bias, v_rwkv_mu, v_rwkv_w0, v_rwkv_w2, v_rwkv_a0, v_rwkv_a2, v_rwkv_k_k, v_rwkv_k_a, v_rwkv_r_k, v_rwkv_ln_g, v_rwkv_ln_b, v_ab_w_out, v_c_w_in, v_c_ln_g, v_c_ln_b, v_c_w_s, v_c_b_s, v_c_w_out, v_ple_w_proj, v_ple_w_gate):
    w = dict(norm_pre=norm_pre, norm_post=norm_post, ab_w_in=ab_w_in, fox_f_bias=fox_f_bias, rwkv_mu=rwkv_mu, rwkv_w0=rwkv_w0, rwkv_w2=rwkv_w2, rwkv_a0=rwkv_a0, rwkv_a2=rwkv_a2, rwkv_k_k=rwkv_k_k, rwkv_k_a=rwkv_k_a, rwkv_r_k=rwkv_r_k, rwkv_ln_g=rwkv_ln_g, rwkv_ln_b=rwkv_ln_b, ab_w_out=ab_w_out, c_w_in=c_w_in, c_ln_g=c_ln_g, c_ln_b=c_ln_b, c_w_s=c_w_s, c_b_s=c_b_s, c_w_out=c_w_out, ple_w_proj=ple_w_proj, ple_w_gate=ple_w_gate)
    mom = dict(norm_pre=m_norm_pre, norm_post=m_norm_post, ab_w_in=m_ab_w_in, fox_f_bias=m_fox_f_bias, rwkv_mu=m_rwkv_mu, rwkv_w0=m_rwkv_w0, rwkv_w2=m_rwkv_w2, rwkv_a0=m_rwkv_a0, rwkv_a2=m_rwkv_a2, rwkv_k_k=m_rwkv_k_k, rwkv_k_a=m_rwkv_k_a, rwkv_r_k=m_rwkv_r_k, rwkv_ln_g=m_rwkv_ln_g, rwkv_ln_b=m_rwkv_ln_b, ab_w_out=m_ab_w_out, c_w_in=m_c_w_in, c_ln_g=m_c_ln_g, c_ln_b=m_c_ln_b, c_w_s=m_c_w_s, c_b_s=m_c_b_s, c_w_out=m_c_w_out, ple_w_proj=m_ple_w_proj, ple_w_gate=m_ple_w_gate)
    vel = dict(norm_pre=v_norm_pre, norm_post=v_norm_post, ab_w_in=v_ab_w_in, fox_f_bias=v_fox_f_bias, rwkv_mu=v_rwkv_mu, rwkv_w0=v_rwkv_w0, rwkv_w2=v_rwkv_w2, rwkv_a0=v_rwkv_a0, rwkv_a2=v_rwkv_a2, rwkv_k_k=v_rwkv_k_k, rwkv_k_a=v_rwkv_k_a, rwkv_r_k=v_rwkv_r_k, rwkv_ln_g=v_rwkv_ln_g, rwkv_ln_b=v_rwkv_ln_b, ab_w_out=v_ab_w_out, c_w_in=v_c_w_in, c_ln_g=v_c_ln_g, c_ln_b=v_c_ln_b, c_w_s=v_c_w_s, c_b_s=v_c_b_s, c_w_out=v_c_w_out, ple_w_proj=v_ple_w_proj, ple_w_gate=v_ple_w_gate)
    return _step(x, p, loss_target, w, mom, vel)
```

```python
import functools

import jax
import jax.numpy as jnp
from jax import lax
from jax.experimental import pallas as pl
from jax.experimental.pallas import tpu as pltpu

F32 = jnp.float32
BF16 = jnp.bfloat16
HI = lax.Precision.HIGHEST
HI3 = lax.Precision.HIGH

HEAD = 64
LANE = 128
LORA = 64
GMLP_CHUNK = 128
RMS_EPS = 1e-6
LN_EPS = 1e-5
GN_EPS = 64e-5
VMEM_LIMIT = 56 * 1024 * 1024
N_DEV = 8

ADAM_LR = 0.001
ADAM_B1 = 0.9
ADAM_B2 = 0.999
ADAM_EPS = 1e-08
ADAM_WD = 0.01
ADAM_STEP = 10

NT_DIMS = (((1,), (1,)), ((), ()))
TN_DIMS = (((0,), (0,)), ((), ()))


def _cparams(sem):
    return pltpu.CompilerParams(dimension_semantics=sem, vmem_limit_bytes=VMEM_LIMIT)


def _tile(n, target, q=LANE):
    if n <= target:
        return n
    best = None
    for d in range(q, target + 1, q):
        if n % d == 0:
            best = d
    assert best is not None, (n, target, q)
    return best


def _sigmoid(x):
    return 1.0 / (1.0 + jnp.exp(-x))


def _silu_and_grad(x):
    s = _sigmoid(x)
    return x * s, s * (1.0 + x * (1.0 - s))


def _gelu_and_grad(x):
    cdf = 0.5 * (1.0 + lax.erf(x * 0.7071067811865476))
    pdf = jnp.exp(-0.5 * x * x) * 0.3989422804014327
    return x * cdf, cdf + x * pdf


def _iota(shape, dim):
    return lax.broadcasted_iota(jnp.int32, shape, dim)


def _head_ones():
    return (_iota((LANE, LANE), 0) // HEAD == _iota((LANE, LANE), 1) // HEAD).astype(F32)


def _head_eye():
    return (_iota((HEAD, LANE), 0) == _iota((HEAD, LANE), 1) % HEAD).astype(F32)


def _hsum(x, ones):
    hi = x.astype(BF16)
    lo = (x - hi.astype(F32)).astype(BF16)
    ones_b = ones.astype(BF16)
    return jnp.dot(jnp.concatenate([hi, lo], axis=1), jnp.concatenate([ones_b, ones_b], axis=0), preferred_element_type=F32)


def _colsum(x):
    return jnp.sum(x, axis=0, keepdims=True)


def matmul(a, b, *, name, ta=False, tb=False, add=None, out_dtype=F32, tm=1024, tn=1024, tk=2048, col_blocks=1, scatter=()):
    m, k = (a.shape[1], a.shape[0]) if ta else a.shape
    n = b.shape[0] if tb else b.shape[1]
    assert (b.shape[1] if tb else b.shape[0]) == k
    tm, tn, tk = _tile(m, tm), _tile(n // col_blocks, tn), _tile(k, tk)
    nk = k // tk
    per = n // col_blocks // tn
    grid = (m // tm, n // tn, nk)
    ns = len(scatter)
    n_in = 2 + (add is not None)

    def body(*refs):
        a_ref, b_ref = refs[:2]
        add_ref = refs[2] if add is not None else None
        o_ref = refs[n_in + ns]
        acc_ref = refs[n_in + 2 * ns + 1]
        if ns:
            start, wait = _direct_exchange(refs[n_in:n_in + ns], refs[n_in + ns + 1:n_in + 2 * ns + 1], *refs[n_in + 2 * ns + 2:],
                                           scatter=True)
            first, last = _grid_ends(grid)
            pl.when(first)(start)
        kk = pl.program_id(2)

        @pl.when(kk == 0)
        def _():
            acc_ref[...] = jnp.zeros_like(acc_ref)

        dims = (((0 if ta else 1,), (1 if tb else 0,)), ((), ()))
        acc_ref[...] += lax.dot_general(a_ref[...].astype(BF16), b_ref[...].astype(BF16), dims,
                                        preferred_element_type=F32)

        @pl.when(kk == nk - 1)
        def _():
            r = acc_ref[...]
            if add is not None:
                r = r + add_ref[...].astype(F32)
            o_ref[...] = r.astype(out_dtype)

        if ns:
            pl.when(last)(wait)

    a_spec = pl.BlockSpec((tk, tm), lambda i, j, kk: (kk, i)) if ta else pl.BlockSpec((tm, tk), lambda i, j, kk: (i, kk))
    b_spec = pl.BlockSpec((tn, tk), lambda i, j, kk: (j, kk)) if tb else pl.BlockSpec((tk, tn), lambda i, j, kk: (kk, j))
    o_spec = pl.BlockSpec((tm, tn), lambda i, j, kk: (i, j))
    in_specs = [a_spec, b_spec] + ([o_spec] if add is not None else [])
    args = (a, b) + ((add,) if add is not None else ())
    out_shape = (m, n)
    if col_blocks > 1:
        assert add is None
        o_spec = pl.BlockSpec((None, tm, tn), lambda i, j, kk: (j // per, i, j % per))
        out_shape = (col_blocks, m, n // col_blocks)
    out = pl.pallas_call(
        body, name=name, grid=grid, in_specs=in_specs + [ANY] * ns, out_specs=[o_spec] + [ANY] * ns,
        out_shape=[jax.ShapeDtypeStruct(out_shape, out_dtype)] + [jax.ShapeDtypeStruct(s.shape, s.dtype) for s in scatter],
        scratch_shapes=[pltpu.VMEM((tm, tn), F32)] + _exchange_sems(ns),
        compiler_params=_cparams(("arbitrary",) * 3 if ns else ("parallel", "parallel", "arbitrary")),
    )(*args, *scatter)
    return (out[0], out[1:]) if ns else out[0]


def _row_spec(tm, width, cb=0):
    return pl.BlockSpec((tm, width), lambda i: (i, cb))


def _full_spec(shape):
    return pl.BlockSpec(shape, lambda i: (0,) * len(shape))


def rms_fwd(x, g, *, name):
    n, d = x.shape
    tm = _tile(n, 256, 8)

    def body(x_ref, g_ref, o_ref):
        xv = x_ref[...]
        r = lax.rsqrt(jnp.mean(xv * xv, axis=-1, keepdims=True) + RMS_EPS)
        o_ref[...] = (xv * r * g_ref[...]).astype(BF16)

    return pl.pallas_call(
        body, name=name, grid=(n // tm,), in_specs=[_row_spec(tm, d), _full_spec((1, d))],
        out_specs=_row_spec(tm, d), out_shape=jax.ShapeDtypeStruct((n, d), BF16),
        compiler_params=_cparams(("parallel",)),
    )(x, g)


def rms_res_fwd(y, res, g, *, name):
    n, d = y.shape
    tm = _tile(n, 256, 8)

    def body(y_ref, res_ref, g_ref, h_ref, hb_ref):
        yv = y_ref[...]
        r = lax.rsqrt(jnp.mean(yv * yv, axis=-1, keepdims=True) + RMS_EPS)
        h = res_ref[...] + yv * r * g_ref[...]
        h_ref[...] = h
        hb_ref[...] = h.astype(BF16)

    return pl.pallas_call(
        body, name=name, grid=(n // tm,), in_specs=[_row_spec(tm, d), _row_spec(tm, d), _full_spec((1, d))],
        out_specs=[_row_spec(tm, d), _row_spec(tm, d)],
        out_shape=[jax.ShapeDtypeStruct((n, d), F32), jax.ShapeDtypeStruct((n, d), BF16)],
        compiler_params=_cparams(("parallel",)),
    )(y, res, g)


def rms_bwd(x, dy, g, *, name, res=None, out_dtype=F32):
    n, d = x.shape
    tm = _tile(n, 256, 8)

    def body(*refs):
        if res is None:
            x_ref, dy_ref, g_ref, dx_ref, dg_ref = refs
        else:
            x_ref, dy_ref, g_ref, res_ref, dx_ref, dg_ref = refs

        @pl.when(pl.program_id(0) == 0)
        def _():
            dg_ref[...] = jnp.zeros_like(dg_ref)

        xv = x_ref[...]
        dyv = dy_ref[...].astype(F32)
        r = lax.rsqrt(jnp.mean(xv * xv, axis=-1, keepdims=True) + RMS_EPS)
        xhat = xv * r
        dxh = dyv * g_ref[...]
        dx = r * (dxh - xhat * jnp.mean(dxh * xhat, axis=-1, keepdims=True))
        if res is not None:
            dx = dx + res_ref[...]
        dx_ref[...] = dx.astype(out_dtype)
        dg_ref[...] += _colsum(dyv * xhat)

    in_specs = [_row_spec(tm, d), _row_spec(tm, d), _full_spec((1, d))] + ([_row_spec(tm, d)] if res is not None else [])
    args = (x, dy, g) + ((res,) if res is not None else ())
    return pl.pallas_call(
        body, name=name, grid=(n // tm,), in_specs=in_specs, out_specs=[_row_spec(tm, d), _full_spec((1, d))],
        out_shape=[jax.ShapeDtypeStruct((n, d), out_dtype), jax.ShapeDtypeStruct((1, d), F32)],
        compiler_params=_cparams(("arbitrary",)),
    )(*args)


def ple_fwd(h, a, z, *, name):
    n, d = h.shape
    tm = _tile(n, 256, 8)

    def body(h_ref, a_ref, z_ref, o_ref):
        o_ref[...] = h_ref[...] + a_ref[...] * _sigmoid(z_ref[...])

    return pl.pallas_call(
        body, name=name, grid=(n // tm,), in_specs=[_row_spec(tm, d)] * 3, out_specs=_row_spec(tm, d),
        out_shape=jax.ShapeDtypeStruct((n, d), F32), compiler_params=_cparams(("parallel",)),
    )(h, a, z)


def ple_bwd(dh, a, z, *, name):
    n, d = dh.shape
    tm = _tile(n, 256, 8)

    def body(dh_ref, a_ref, z_ref, da_ref, dz_ref):
        s = _sigmoid(z_ref[...])
        dhv = dh_ref[...]
        da_ref[...] = (dhv * s).astype(BF16)
        dz_ref[...] = (dhv * a_ref[...] * s * (1.0 - s)).astype(BF16)

    return pl.pallas_call(
        body, name=name, grid=(n // tm,), in_specs=[_row_spec(tm, d)] * 3, out_specs=[_row_spec(tm, d)] * 2,
        out_shape=[jax.ShapeDtypeStruct((n, d), BF16)] * 2, compiler_params=_cparams(("parallel",)),
    )(dh, a, z)


def loss_head(h, target, *, name):
    n, d = h.shape
    tm = _tile(n, 256, 8)

    def body(h_ref, t_ref, dh_ref, loss_ref):
        @pl.when(pl.program_id(0) == 0)
        def _():
            loss_ref[...] = jnp.zeros_like(loss_ref)

        e = h_ref[...] - t_ref[...]
        dh_ref[...] = e * (1.0 / d)
        loss_ref[...] += 0.5 * jnp.sum(jnp.sum(e * e, axis=-1, keepdims=True) * (1.0 / d), axis=0, keepdims=True)

    return pl.pallas_call(
        body, name=name, grid=(n // tm,), in_specs=[_row_spec(tm, d)] * 2,
        out_specs=[_row_spec(tm, d), _full_spec((1, 1))],
        out_shape=[jax.ShapeDtypeStruct((n, d), F32), jax.ShapeDtypeStruct((1, 1), F32)],
        compiler_params=_cparams(("arbitrary",)),
    )(h, target)


def _log_sigmoid(z):
    return jnp.minimum(z, 0.0) - jnp.log1p(jnp.exp(-jnp.abs(z)))


def fox_gate_fwd(proj, fbias, bsz, seq, nw):
    cc = _tile(seq, 256, 8)

    def body(f_ref, b_ref, c_ref):
        low = (_iota((cc, cc), 0) >= _iota((cc, cc), 1)).astype(F32)
        carry = jnp.zeros((1, LANE), F32)
        for ci in range(seq // cc):
            rows = slice(ci * cc, (ci + 1) * cc)
            lf = _log_sigmoid(f_ref[rows, :] + b_ref[...])
            c_ref[rows, :] = jnp.dot(low, lf, precision=HI, preferred_element_type=F32) + carry
            carry = carry + _colsum(lf)

    return pl.pallas_call(
        body, name="fox_gate_fwd", grid=(bsz,),
        in_specs=[pl.BlockSpec((seq, LANE), lambda b: (b, 8 * nw + 1)), _full_spec((1, LANE))],
        out_specs=pl.BlockSpec((seq, LANE), lambda b: (b, 0)),
        out_shape=jax.ShapeDtypeStruct((bsz * seq, LANE), F32), compiler_params=_cparams(("parallel",)),
    )(proj, fbias)


def fox_gate_bwd(proj, fbias, dc, bsz, seq, nw):
    cc = _tile(seq, 256, 8)
    nc = seq // cc

    def body(f_ref, b_ref, dc_ref, df_ref, db_ref):
        @pl.when(pl.program_id(0) == 0)
        def _():
            db_ref[...] = jnp.zeros_like(db_ref)

        upp = (_iota((cc, cc), 0) <= _iota((cc, cc), 1)).astype(F32)
        carry = jnp.zeros((1, LANE), F32)
        dbias = jnp.zeros((1, LANE), F32)
        for ci in reversed(range(nc)):
            rows = slice(ci * cc, (ci + 1) * cc)
            blk = dc_ref[rows, :]
            dlf = jnp.dot(upp, blk, precision=HI, preferred_element_type=F32) + carry
            carry = carry + _colsum(blk)
            df = dlf * _sigmoid(-(f_ref[rows, :] + b_ref[...]))
            df_ref[rows, :] = df.astype(BF16)
            dbias = dbias + _colsum(df)
        db_ref[...] += dbias

    return pl.pallas_call(
        body, name="fox_gate_bwd", grid=(bsz,),
        in_specs=[pl.BlockSpec((seq, LANE), lambda b: (b, 8 * nw + 1)), _full_spec((1, LANE)),
                  pl.BlockSpec((seq, LANE), lambda b: (b, 0))],
        out_specs=[pl.BlockSpec((seq, LANE), lambda b: (b, 0)), _full_spec((1, LANE))],
        out_shape=[jax.ShapeDtypeStruct((bsz * seq, LANE), BF16), jax.ShapeDtypeStruct((1, LANE), F32)],
        compiler_params=_cparams(("arbitrary",)),
    )(proj, fbias, dc)


def _attn_scores(qs, kh, ccol, crow, r0, kend, tq):
    s = lax.dot_general(qs[r0:kend], kh[:kend], NT_DIMS, preferred_element_type=F32)
    s = s + ccol[r0:kend] - crow[:, :kend]
    causal = _iota((tq, kend), 1) <= r0 + _iota((tq, kend), 0)
    return jnp.where(causal, s, -jnp.inf)


def _head_column(c_blk, h):
    return jnp.sum(jnp.where(_iota(c_blk.shape, 1) == h, c_blk, 0.0), axis=1, keepdims=True)


def fox_attn_fwd(proj, c, crow, bsz, seq, nw):
    tq = _tile(seq, 256, 8)
    scale = HEAD ** -0.5

    def body(q_ref, k_ref, v_ref, g_ref, c_ref, crow_ref, o_ref, y_ref):
        hp = pl.program_id(1)
        c_blk = c_ref[...]
        for hh in range(2):
            sl = slice(hh * HEAD, (hh + 1) * HEAD)
            ccol = _head_column(c_blk, 2 * hp + hh)
            crow_h = crow_ref[0, 0, hh:hh + 1, :]
            qs = (q_ref[:, sl] * scale).astype(BF16)
            kh = k_ref[:, sl].astype(BF16)
            vh = v_ref[:, sl].astype(BF16)
            for qi in range(seq // tq):
                r0, kend = qi * tq, (qi + 1) * tq
                s = _attn_scores(qs, kh, ccol, crow_h, r0, kend, tq)
                p = jnp.exp(s - jnp.max(s, axis=-1, keepdims=True))
                o = jnp.dot(p.astype(BF16), vh[:kend], preferred_element_type=F32) / jnp.sum(p, axis=-1, keepdims=True)
                o_ref[r0:kend, sl] = o
                y_ref[r0:kend, sl] = (o * _silu_and_grad(g_ref[r0:kend, sl])[0]).astype(BF16)

    blk = lambda cb: pl.BlockSpec((seq, LANE), lambda b, j: (b, cb * nw + j))
    return pl.pallas_call(
        body, name="fox_attn_fwd", grid=(bsz, nw),
        in_specs=[blk(0), blk(1), blk(2), blk(3), pl.BlockSpec((seq, LANE), lambda b, j: (b, 0)),
                  pl.BlockSpec((1, 1, 8, seq), lambda b, j: (b, j, 0, 0))],
        out_specs=[pl.BlockSpec((seq, LANE), lambda b, j: (b, j))] * 2,
        out_shape=[jax.ShapeDtypeStruct((bsz * seq, nw * LANE), F32), jax.ShapeDtypeStruct((bsz * seq, nw * LANE), BF16)],
        compiler_params=_cparams(("parallel", "parallel")),
    )(proj, proj, proj, proj, c, crow)


def fox_attn_bwd(proj, c, crow, oa, dycat, bsz, seq, nw):
    tq = _tile(seq, 256, 8)
    scale = HEAD ** -0.5

    def body(q_ref, k_ref, v_ref, g_ref, c_ref, crow_ref, o_ref, dy_ref,
             dq_ref, dk_ref, dv_ref, dg_ref, dc_ref, dk_acc, dv_acc, dc_acc):
        hp = pl.program_id(1)
        c_blk = c_ref[...]
        dc_ref[...] = jnp.zeros_like(dc_ref)
        for hh in range(2):
            sl = slice(hh * HEAD, (hh + 1) * HEAD)
            ccol = _head_column(c_blk, 2 * hp + hh)
            crow_h = crow_ref[0, 0, hh:hh + 1, :]
            qs = (q_ref[:, sl] * scale).astype(BF16)
            kh = k_ref[:, sl].astype(BF16)
            vh = v_ref[:, sl].astype(BF16)
            oh = o_ref[:, sl]
            dyh = dy_ref[:, sl].astype(F32)
            silu, dsilu = _silu_and_grad(g_ref[:, sl])
            dg_ref[:, sl] = (dyh * oh * dsilu).astype(BF16)
            do = dyh * silu
            dvec = jnp.sum(do * oh, axis=-1, keepdims=True)
            dob = do.astype(BF16)
            dk_acc[...] = jnp.zeros_like(dk_acc)
            dv_acc[...] = jnp.zeros_like(dv_acc)
            dc_acc[...] = jnp.zeros_like(dc_acc)
            for qi in range(seq // tq):
                r0, kend = qi * tq, (qi + 1) * tq
                s = _attn_scores(qs, kh, ccol, crow_h, r0, kend, tq)
                p = jnp.exp(s - jnp.max(s, axis=-1, keepdims=True))
                p = p / jnp.sum(p, axis=-1, keepdims=True)
                dp = lax.dot_general(dob[r0:kend], vh[:kend], NT_DIMS, preferred_element_type=F32)
                ds = p * (dp - dvec[r0:kend])
                dsb = ds.astype(BF16)
                dq_ref[r0:kend, sl] = (jnp.dot(dsb, kh[:kend], preferred_element_type=F32) * scale).astype(BF16)
                dk_acc[0:kend, :] += lax.dot_general(dsb, qs[r0:kend], TN_DIMS, preferred_element_type=F32)
                dv_acc[0:kend, :] += lax.dot_general(p.astype(BF16), dob[r0:kend], TN_DIMS, preferred_element_type=F32)
                dc_acc[:, 0:kend] += -_colsum(ds)
                ds_lo = (ds - dsb.astype(F32)).astype(BF16)
                rowsum = lax.dot_general(jnp.ones((8, 2 * kend), BF16), jnp.concatenate([dsb, ds_lo], axis=1), NT_DIMS,
                                         preferred_element_type=F32)
                dc_acc[:, r0:kend] += rowsum[0:1, :]
            dk_ref[:, sl] = dk_acc[...].astype(BF16)
            dv_ref[:, sl] = dv_acc[...].astype(BF16)
            dc_ref[0, 0, hh:hh + 1, :] = dc_acc[...]

    blk = lambda cb: pl.BlockSpec((seq, LANE), lambda b, j: (b, cb * nw + j))
    own = pl.BlockSpec((seq, LANE), lambda b, j: (b, j))
    n = bsz * seq
    return pl.pallas_call(
        body, name="fox_attn_bwd", grid=(bsz, nw),
        in_specs=[blk(0), blk(1), blk(2), blk(3), pl.BlockSpec((seq, LANE), lambda b, j: (b, 0)),
                  pl.BlockSpec((1, 1, 8, seq), lambda b, j: (b, j, 0, 0)), own, own],
        out_specs=[own] * 4 + [pl.BlockSpec((1, 1, 8, seq), lambda b, j: (b, j, 0, 0))],
        out_shape=[jax.ShapeDtypeStruct((n, nw * LANE), BF16)] * 4 + [jax.ShapeDtypeStruct((bsz, nw, 8, seq), F32)],
        scratch_shapes=[pltpu.VMEM((seq, HEAD), F32), pltpu.VMEM((seq, HEAD), F32), pltpu.VMEM((1, seq), F32)],
        compiler_params=_cparams(("parallel", "parallel")),
    )(proj, proj, proj, proj, c, crow, oa, dycat)


def _shift_mix(x, mu):
    prev = jnp.where(_iota(x.shape, 0) == 0, 0.0, pltpu.roll(x, 1, 0))
    return x + (prev - x) * mu, prev


def _rwkv_pre_math(r_in, k_in, v_in, wa_in, mu_r, mu_k, mu_v, mu_wa, w0, w2, a0, a2, k_k, k_a, ones):
    r, prev_r = _shift_mix(r_in, mu_r)
    k, prev_k = _shift_mix(k_in, mu_k)
    v, prev_v = _shift_mix(v_in, mu_v)
    wa, prev_wa = _shift_mix(wa_in, mu_wa)
    tw = jnp.tanh(wa[:, :LORA])
    alo = wa[:, LORA:]
    sg = _sigmoid(w0 + jnp.dot(tw, w2, precision=HI, preferred_element_type=F32))
    e = sg * 0.6065306597126334
    w = jnp.exp(-e)
    a = _sigmoid(a0 + jnp.dot(alo, a2, precision=HI, preferred_element_type=F32))
    kkraw = k * k_k
    nrm = jnp.sqrt(_hsum(kkraw * kkraw, ones))
    den = jnp.maximum(nrm, 1e-12)
    kk = kkraw / den
    k2 = k * (1.0 + (a - 1.0) * k_a)
    return dict(r=r, k=k, v=v, prev_r=prev_r, prev_k=prev_k, prev_v=prev_v, prev_wa=prev_wa, tw=tw, alo=alo,
                sg=sg, e=e, w=w, a=a, nrm=nrm, den=den, kk=kk, k2=k2)


def _rwkv_pre_specs(seq, nw):
    tok = lambda cb: pl.BlockSpec((seq, LANE), lambda b, j: (b, cb * nw + j))
    par = pl.BlockSpec((1, LANE), lambda b, j: (0, j))
    lora = pl.BlockSpec((LORA, LANE), lambda b, j: (0, j))
    return [tok(4), tok(5), tok(6), pl.BlockSpec((seq, LANE), lambda b, j: (b, 8 * nw)),
            par, par, par, pl.BlockSpec((1, LANE), lambda b, j: (0, 0)), par, lora, par, lora, par, par]


def rwkv_pre_fwd(proj, prm, bsz, seq, nw):
    def body(r_ref, k_ref, v_ref, wa_ref, mur, muk, muv, muwa, w0, w2, a0, a2, kk_, ka_,
             ro, wo, ko, vo, kko, kao):
        m = _rwkv_pre_math(r_ref[...], k_ref[...], v_ref[...], wa_ref[...], mur[...], muk[...], muv[...], muwa[...],
                           w0[...], w2[...], a0[...], a2[...], kk_[...], ka_[...], _head_ones())
        ro[...] = m["r"]
        wo[...] = m["w"]
        ko[...] = m["k2"]
        vo[...] = m["v"]
        kko[...] = m["kk"]
        kao[...] = m["kk"] * m["a"]

    n = bsz * seq
    own = pl.BlockSpec((seq, LANE), lambda b, j: (b, j))
    return pl.pallas_call(
        body, name="rwkv_pre_fwd", grid=(bsz, nw), in_specs=_rwkv_pre_specs(seq, nw), out_specs=[own] * 6,
        out_shape=[jax.ShapeDtypeStruct((n, nw * LANE), F32)] * 6, compiler_params=_cparams(("parallel", "parallel")),
    )(proj, proj, proj, proj, prm["mu_r"], prm["mu_k"], prm["mu_v"], prm["mu_wa"], prm["w0"], prm["w2"],
      prm["a0"], prm["a2"], prm["k_k"], prm["k_a"])


def rwkv_pre_bwd(proj, prm, grads_scan, grads_bonus, bsz, seq, nw):
    wid = nw * LANE

    def body(r_ref, k_ref, v_ref, wa_ref, mur, muk, muv, muwa, w0, w2, a0, a2, kk_, ka_,
             dr_s, dw_s, dk_s, dv_s, dkk_s, dka_s, dr_b, dk_b, dv_b,
             dxr, dxk, dxv, dxwa, dmur, dmuk, dmuv, dmuwa, dw0, dw2, da0, da2, dkk_p, dka_p, dwa_acc):
        b, j = pl.program_id(0), pl.program_id(1)
        cols = pl.ds(pl.multiple_of(j * LANE, LANE), LANE)
        acc_refs = (dmur, dmuk, dmuv, dmuwa, dw0, dw2, da0, da2, dkk_p, dka_p)

        @pl.when(jnp.logical_and(b == 0, j == 0))
        def _():
            for ref in acc_refs:
                ref[...] = jnp.zeros_like(ref)

        @pl.when(j == 0)
        def _():
            dwa_acc[...] = jnp.zeros_like(dwa_acc)

        ones = _head_ones()
        r_in, k_in, v_in, wa_in = r_ref[...], k_ref[...], v_ref[...], wa_ref[...]
        m = _rwkv_pre_math(r_in, k_in, v_in, wa_in, mur[...], muk[...], muv[...], muwa[...],
                           w0[...], w2[...], a0[...], a2[...], kk_[...], ka_[...], ones)
        k, a, kk = m["k"], m["a"], m["kk"]
        dr = dr_s[...] + dr_b[...]
        dk2 = dk_s[...] + dk_b[...]
        dv = dv_s[...] + dv_b[...]
        dka = dka_s[...]
        da = dka * kk + dk2 * k * ka_[...]
        dkk = dkk_s[...] + dka * a
        dkkraw = jnp.where(m["nrm"] > 1e-12, dkk - kk * _hsum(dkk * kk, ones), dkk) / m["den"]
        dk = dkkraw * kk_[...] + dk2 * (1.0 + (a - 1.0) * ka_[...])
        dkk_p[:, cols] += _colsum(dkkraw * k)
        dka_p[:, cols] += _colsum(dk2 * k * (a - 1.0))
        dza = da * a * (1.0 - a)
        da0[:, cols] += _colsum(dza)
        da2[:, cols] += lax.dot_general(m["alo"], dza, TN_DIMS, precision=HI, preferred_element_type=F32)
        dalo = lax.dot_general(dza, a2[...], NT_DIMS, precision=HI, preferred_element_type=F32)
        dzw = -dw_s[...] * m["w"] * m["e"] * (1.0 - m["sg"])
        dw0[:, cols] += _colsum(dzw)
        dw2[:, cols] += lax.dot_general(m["tw"], dzw, TN_DIMS, precision=HI, preferred_element_type=F32)
        dtw = lax.dot_general(dzw, w2[...], NT_DIMS, precision=HI, preferred_element_type=F32)
        dwa_acc[:, 0:LORA] += dtw * (1.0 - m["tw"] * m["tw"])
        dwa_acc[:, LORA:LANE] += dalo

        def shift_bwd(dxs, x, prev, mu):
            g = dxs * mu
            nxt = jnp.where(_iota(g.shape, 0) == seq - 1, 0.0, pltpu.roll(g, seq - 1, 0))
            return dxs * (1.0 - mu) + nxt, _colsum(dxs * (prev - x))

        for dxs, x, prev, mu, out, dmu in ((dr, r_in, m["prev_r"], mur, dxr, dmur), (dk, k_in, m["prev_k"], muk, dxk, dmuk),
                                           (dv, v_in, m["prev_v"], muv, dxv, dmuv)):
            dx, dm = shift_bwd(dxs, x, prev, mu[...])
            out[...] = dx.astype(BF16)
            dmu[:, cols] += dm

        @pl.when(j == nw - 1)
        def _():
            dx, dm = shift_bwd(dwa_acc[...], wa_in, m["prev_wa"], muwa[...])
            dxwa[...] = dx.astype(BF16)
            dmuwa[...] += dm

    n = bsz * seq
    own = pl.BlockSpec((seq, LANE), lambda b, j: (b, j))
    whole = lambda shape: pl.BlockSpec(shape, lambda b, j: (0,) * len(shape))
    row = jax.ShapeDtypeStruct((1, wid), F32)
    return pl.pallas_call(
        body, name="rwkv_pre_bwd", grid=(bsz, nw), in_specs=_rwkv_pre_specs(seq, nw) + [own] * 9,
        out_specs=[own] * 3 + [pl.BlockSpec((seq, LANE), lambda b, j: (b, 0))]
        + [whole((1, wid))] * 3 + [whole((1, LANE)), whole((1, wid)), whole((LORA, wid)), whole((1, wid)),
                                   whole((LORA, wid)), whole((1, wid)), whole((1, wid))],
        out_shape=[jax.ShapeDtypeStruct((n, wid), BF16)] * 3 + [jax.ShapeDtypeStruct((n, LANE), BF16)]
        + [row] * 3 + [jax.ShapeDtypeStruct((1, LANE), F32), row, jax.ShapeDtypeStruct((LORA, wid), F32), row,
                       jax.ShapeDtypeStruct((LORA, wid), F32), row, row],
        scratch_shapes=[pltpu.VMEM((seq, LANE), F32)],
        compiler_params=_cparams(("arbitrary", "arbitrary")),
    )(proj, proj, proj, proj, prm["mu_r"], prm["mu_k"], prm["mu_v"], prm["mu_wa"], prm["w0"], prm["w2"],
      prm["a0"], prm["a2"], prm["k_k"], prm["k_a"], *grads_scan, *grads_bonus)


SCAN_CHUNK = 64


def _scan_group(nw):
    return _tile(nw, 8, 1)


ROWS = 8


def _scan_consts():
    ones = _head_ones().astype(BF16)
    return _head_eye(), ones, jnp.concatenate([ones, ones], axis=0)


def _cols8(tile, eye, ones, ones2, two_parts):
    hi = tile.astype(BF16).astype(F32)
    place = lambda part, q: (part[q:q + 1, :] * eye).astype(BF16)
    if two_parts:
        lo = tile - hi
        lhs = jnp.concatenate([jnp.concatenate([place(hi, q), place(lo, q)], axis=1) for q in range(ROWS)], axis=0)
        big = jnp.dot(lhs, ones2, preferred_element_type=F32)
    else:
        big = jnp.dot(jnp.concatenate([place(hi, q) for q in range(ROWS)], axis=0), ones, preferred_element_type=F32)
    return [big[q * HEAD:(q + 1) * HEAD, :] for q in range(ROWS)]


def _rows8(mats, eye, ones):
    big = jnp.dot(jnp.concatenate([m.astype(BF16) for m in mats], axis=0), ones, preferred_element_type=F32)
    return _stack_rows([_colsum(big[q * HEAD:(q + 1) * HEAD, :] * eye) for q in range(ROWS)])


def _stack_rows(rows):
    rid = _iota((ROWS, LANE), 0)
    out = jnp.zeros((ROWS, LANE), F32)
    for q in range(ROWS):
        out = jnp.where(rid == q, rows[q], out)
    return out


def _grid_ends(grid):
    ids = [pl.program_id(i) for i in range(len(grid))]
    first = functools.reduce(jnp.logical_and, [i == 0 for i in ids])
    last = functools.reduce(jnp.logical_and, [i == g - 1 for i, g in zip(ids, grid)])
    return first, last


def rwkv_scan_fwd(r, w, k, v, kk, ka, bsz, seq, nw, gather):
    grp = _scan_group(nw)
    tc = _tile(seq, SCAN_CHUNK, 8)
    nt = seq // tc
    grid = (bsz, nw // grp, nt)
    ng = len(gather)

    def body(*refs):
        r_ref, w_ref, k_ref, v_ref, kk_ref, ka_ref = refs[:6]
        y_ref, st_ref = refs[6 + ng:8 + ng]
        s_ref = refs[8 + 2 * ng]
        start, wait = _direct_exchange(refs[6:6 + ng], refs[8 + ng:8 + 2 * ng], *refs[9 + 2 * ng:], scatter=False)
        first, last = _grid_ends(grid)
        pl.when(first)(start)

        @pl.when(pl.program_id(2) == 0)
        def _():
            s_ref[...] = jnp.zeros_like(s_ref)

        eye, ones, ones2 = _scan_consts()

        def step(i8, carry):
            base = pl.multiple_of(i8 * ROWS, ROWS)
            for g in range(grp):
                cs = slice(g * LANE, (g + 1) * LANE)
                cols = lambda ref, two: _cols8(ref[pl.ds(base, ROWS), cs], eye, ones, ones2, two)
                rm, wm, km, kkm, kam = cols(r_ref, False), cols(w_ref, True), cols(k_ref, False), cols(kk_ref, False), cols(ka_ref, False)
                vt = v_ref[pl.ds(base, ROWS), cs]
                s = s_ref[g]
                ys = []
                for q in range(ROWS):
                    st_ref[base + q, g] = s
                    sa = _colsum(s * kkm[q])
                    s = s * wm[q] - kam[q] * sa + km[q] * vt[q:q + 1, :]
                    ys.append(_colsum(s * rm[q]))
                y_ref[pl.ds(base, ROWS), cs] = _stack_rows(ys)
                s_ref[g] = s
            return carry

        lax.fori_loop(0, tc // ROWS, step, 0)
        pl.when(last)(wait)

    tok = pl.BlockSpec((tc, grp * LANE), lambda b, g, t: (b * nt + t, g))
    n = bsz * seq
    out = pl.pallas_call(
        body, name="rwkv_scan_fwd", grid=grid, in_specs=[tok] * 6 + [ANY] * ng,
        out_specs=[tok, pl.BlockSpec((tc, grp, HEAD, LANE), lambda b, g, t: (b * nt + t, g, 0, 0))] + [ANY] * ng,
        out_shape=[jax.ShapeDtypeStruct((n, nw * LANE), F32), jax.ShapeDtypeStruct((n, nw, HEAD, LANE), F32)]
        + [jax.ShapeDtypeStruct((N_DEV,) + a.shape, a.dtype) for a in gather],
        scratch_shapes=[pltpu.VMEM((grp, HEAD, LANE), F32)] + _exchange_sems(ng),
        compiler_params=_cparams(("arbitrary", "arbitrary", "arbitrary")),
    )(r, w, k, v, kk, ka, *gather)
    return out[0], out[1], out[2:]


def rwkv_scan_bwd(r, w, k, v, kk, ka, states, dy, bsz, seq, nw, scatter):
    grp = _scan_group(nw)
    tc = _tile(seq, SCAN_CHUNK, 8)
    nt = seq // tc
    nblk = tc // ROWS
    grid = (bsz, nw // grp, nt)
    ns = len(scatter)

    def body(*refs):
        r_ref, w_ref, k_ref, v_ref, kk_ref, ka_ref, st_ref, dy_ref = refs[:8]
        dr_ref, dw_ref, dk_ref, dv_ref, dkk_ref, dka_ref = refs[8 + ns:14 + ns]
        ds_ref = refs[14 + 2 * ns]
        start, wait = _direct_exchange(refs[8:8 + ns], refs[14 + ns:14 + 2 * ns], *refs[15 + 2 * ns:], scatter=True)
        first, last = _grid_ends(grid)
        pl.when(first)(start)

        @pl.when(pl.program_id(2) == 0)
        def _():
            ds_ref[...] = jnp.zeros_like(ds_ref)

        eye, ones, ones2 = _scan_consts()
        ones_f = _head_ones()

        def step(ii, carry):
            base = pl.multiple_of((nblk - 1 - ii) * ROWS, ROWS)
            rows = pl.ds(base, ROWS)
            for g in range(grp):
                cs = slice(g * LANE, (g + 1) * LANE)
                cols = lambda ref, two: _cols8(ref[rows, cs], eye, ones, ones2, two)
                rm, wm, km, kkm, kam = cols(r_ref, False), cols(w_ref, True), cols(k_ref, False), cols(kk_ref, False), cols(ka_ref, False)
                vt, dyt = v_ref[rows, cs], dy_ref[rows, cs]
                ds = ds_ref[g]
                dvs, sas, p_dr, p_dk, p_dka, p_dw, p_dkk = ([None] * ROWS for _ in range(7))
                for q in reversed(range(ROWS)):
                    sp = st_ref[base + q, g]
                    vrow, dyrow = vt[q:q + 1, :], dyt[q:q + 1, :]
                    sa = _colsum(sp * kkm[q])
                    sas[q] = sa
                    ds = ds + rm[q] * dyrow
                    dvs[q] = _colsum(ds * km[q])
                    dsa = -_colsum(ds * kam[q])
                    p_dr[q] = sp * dyrow
                    p_dk[q] = ds * vrow
                    p_dka[q] = -(ds * sa)
                    p_dw[q] = ds * sp
                    p_dkk[q] = sp * dsa
                    ds = ds * wm[q] + kkm[q] * dsa
                ds_ref[g] = ds
                dv_ref[rows, cs] = _stack_rows(dvs)
                sa_dy = _hsum(_stack_rows(sas) * dyt, ones_f)
                v_dy = _hsum(vt * dyt, ones_f)
                dr_ref[rows, cs] = w_ref[rows, cs] * _rows8(p_dr, eye, ones) - ka_ref[rows, cs] * sa_dy + k_ref[rows, cs] * v_dy
                dk_ref[rows, cs] = _rows8(p_dk, eye, ones)
                dka_ref[rows, cs] = _rows8(p_dka, eye, ones)
                dw_ref[rows, cs] = _rows8(p_dw, eye, ones)
                dkk_ref[rows, cs] = _rows8(p_dkk, eye, ones)
            return carry

        lax.fori_loop(0, nblk, step, 0)
        pl.when(last)(wait)

    tok = pl.BlockSpec((tc, grp * LANE), lambda b, g, t: (b * nt + nt - 1 - t, g))
    n = bsz * seq
    out = pl.pallas_call(
        body, name="rwkv_scan_bwd", grid=grid,
        in_specs=[tok] * 6 + [pl.BlockSpec((tc, grp, HEAD, LANE), lambda b, g, t: (b * nt + nt - 1 - t, g, 0, 0)), tok] + [ANY] * ns,
        out_specs=[tok] * 6 + [ANY] * ns,
        out_shape=[jax.ShapeDtypeStruct((n, nw * LANE), F32)] * 6 + [jax.ShapeDtypeStruct(a.shape, a.dtype) for a in scatter],
        scratch_shapes=[pltpu.VMEM((grp, HEAD, LANE), F32)] + _exchange_sems(ns),
        compiler_params=_cparams(("arbitrary", "arbitrary", "arbitrary")),
    )(r, w, k, v, kk, ka, states, dy, *scatter)
    return out[:6], out[6:]


def _rwkv_post_math(y, r, k2, v, ln_g, ln_b, r_k, ones):
    d = y - _hsum(y, ones) * (1.0 / HEAD)
    rstd = lax.rsqrt(_hsum(d * d, ones) * (1.0 / HEAD) + GN_EPS)
    yn = d * rstd
    s = _hsum(r * k2 * r_k, ones)
    return yn, rstd, s, yn * ln_g + ln_b + s * v


def _rwkv_post_specs(tm, nw):
    tok = pl.BlockSpec((tm, LANE), lambda i, j: (i, j))
    par = pl.BlockSpec((1, LANE), lambda i, j: (0, j))
    return [tok] * 4 + [pl.BlockSpec((tm, LANE), lambda i, j: (i, 7 * nw + j)), par, par, par]


def rwkv_post_fwd(y, r, k2, v, proj, prm, nw):
    n = y.shape[0]
    tm = _tile(n, 512, 8)

    def body(y_ref, r_ref, k_ref, v_ref, g_ref, lg, lb, rk, o_ref):
        ob = _rwkv_post_math(y_ref[...], r_ref[...], k_ref[...], v_ref[...], lg[...], lb[...], rk[...], _head_ones())[3]
        o_ref[...] = (ob * _silu_and_grad(g_ref[...])[0]).astype(BF16)

    return pl.pallas_call(
        body, name="rwkv_post_fwd", grid=(n // tm, nw), in_specs=_rwkv_post_specs(tm, nw),
        out_specs=pl.BlockSpec((tm, LANE), lambda i, j: (i, j)), out_shape=jax.ShapeDtypeStruct((n, nw * LANE), BF16),
        compiler_params=_cparams(("parallel", "parallel")),
    )(y, r, k2, v, proj, prm["ln_g"], prm["ln_b"], prm["r_k"])


def rwkv_post_bwd(y, r, k2, v, proj, prm, dycat, nw):
    n = y.shape[0]
    tm = _tile(n, 512, 8)
    wid = nw * LANE

    def body(y_ref, r_ref, k_ref, v_ref, g_ref, lg, lb, rk, dyc_ref,
             dy_ref, dr_ref, dk_ref, dv_ref, dg_ref, dlg, dlb, drk):
        i, j = pl.program_id(0), pl.program_id(1)
        cols = pl.ds(pl.multiple_of(j * LANE, LANE), LANE)

        @pl.when(jnp.logical_and(i == 0, j == 0))
        def _():
            for ref in (dlg, dlb, drk):
                ref[...] = jnp.zeros_like(ref)

        ones = _head_ones()
        rr, kr, vr = r_ref[...], k_ref[...], v_ref[...]
        yn, rstd, s, ob = _rwkv_post_math(y_ref[...], rr, kr, vr, lg[...], lb[...], rk[...], ones)
        silu, dsilu = _silu_and_grad(g_ref[...])
        dyc = dyc_ref[...].astype(F32)
        dg_ref[...] = (dyc * ob * dsilu).astype(BF16)
        dob = dyc * silu
        dlg[:, cols] += _colsum(dob * yn)
        dlb[:, cols] += _colsum(dob)
        dyn = dob * lg[...]
        dy_ref[...] = rstd * (dyn - _hsum(dyn, ones) * (1.0 / HEAD) - yn * _hsum(dyn * yn, ones) * (1.0 / HEAD))
        dv_ref[...] = dob * s
        dsum = _hsum(dob * vr, ones)
        dr_ref[...] = dsum * kr * rk[...]
        dk_ref[...] = dsum * rr * rk[...]
        drk[:, cols] += _colsum(dsum * rr * kr)

    tok = pl.BlockSpec((tm, LANE), lambda i, j: (i, j))
    whole = pl.BlockSpec((1, wid), lambda i, j: (0, 0))
    return pl.pallas_call(
        body, name="rwkv_post_bwd", grid=(n // tm, nw),
        in_specs=_rwkv_post_specs(tm, nw) + [pl.BlockSpec((tm, LANE), lambda i, j: (i, nw + j))],
        out_specs=[tok] * 5 + [whole] * 3,
        out_shape=[jax.ShapeDtypeStruct((n, wid), F32)] * 4 + [jax.ShapeDtypeStruct((n, wid), BF16)]
        + [jax.ShapeDtypeStruct((1, wid), F32)] * 3,
        compiler_params=_cparams(("arbitrary", "arbitrary")),
    )(y, r, k2, v, proj, prm["ln_g"], prm["ln_b"], prm["r_k"], dycat)


def _sgu_math(blk, ln_g, ln_b, ws_ref, bb_ref, mixed_ref, d, ngr):
    u, v, g = blk[:, 0:d], blk[:, d:2 * d], blk[:, 2 * d:3 * d]
    gu, dgu = _gelu_and_grad(u)
    gv, dgv = _gelu_and_grad(v)
    cen = gv - jnp.mean(gv, axis=-1, keepdims=True)
    rstd = lax.rsqrt(jnp.mean(cen * cen, axis=-1, keepdims=True) + LN_EPS)
    vhat = cen * rstd
    vn = vhat * ln_g + ln_b
    tril = _iota((GMLP_CHUNK, GMLP_CHUNK), 0) >= _iota((GMLP_CHUNK, GMLP_CHUNK), 1)
    for gi in range(ngr):
        cs = slice(gi * LANE, (gi + 1) * LANE)
        wm = jnp.where(tril, ws_ref[gi], 0.0)
        mixed_ref[:, cs] = jnp.dot(wm, vn[:, cs], precision=HI3, preferred_element_type=F32) + bb_ref[gi]
    return dict(g=g, gu=gu, dgu=dgu, dgv=dgv, rstd=rstd, vhat=vhat, vn=vn, tril=tril)


def sgu_fwd(proj, ln_g, ln_b, w_s, b_bc):
    n, d3 = proj.shape
    d = d3 // 3
    ngr = d // LANE

    def body(p_ref, lg, lb, ws_ref, bb_ref, o_ref, mixed_ref):
        m = _sgu_math(p_ref[...], lg[...], lb[...], ws_ref, bb_ref, mixed_ref, d, ngr)
        o_ref[...] = (m["gu"] * mixed_ref[...] * _silu_and_grad(m["g"])[0]).astype(BF16)

    return pl.pallas_call(
        body, name="sgu_fwd", grid=(n // GMLP_CHUNK,),
        in_specs=[_row_spec(GMLP_CHUNK, d3), _full_spec((1, d)), _full_spec((1, d)),
                  _full_spec((ngr, GMLP_CHUNK, GMLP_CHUNK)), _full_spec((ngr, GMLP_CHUNK, LANE))],
        out_specs=_row_spec(GMLP_CHUNK, d), out_shape=jax.ShapeDtypeStruct((n, d), BF16),
        scratch_shapes=[pltpu.VMEM((GMLP_CHUNK, d), F32)], compiler_params=_cparams(("parallel",)),
    )(proj, ln_g, ln_b, w_s, b_bc)


def sgu_bwd(proj, ln_g, ln_b, w_s, b_bc, dyin):
    n, d3 = proj.shape
    d = d3 // 3
    ngr = d // LANE
    nsteps = n // GMLP_CHUNK

    def body(p_ref, lg, lb, ws_ref, bb_ref, dy_ref, dp_ref, dws_ref, dbs_ref, dlg_ref, dlb_ref,
             mixed_ref, dvn_ref, dbacc_ref):
        step = pl.program_id(0)

        @pl.when(step == 0)
        def _():
            for ref in (dws_ref, dlg_ref, dlb_ref, dbacc_ref):
                ref[...] = jnp.zeros_like(ref)

        m = _sgu_math(p_ref[...], lg[...], lb[...], ws_ref, bb_ref, mixed_ref, d, ngr)
        silu, dsilu = _silu_and_grad(m["g"])
        dyv = dy_ref[...].astype(F32)
        mixed = mixed_ref[...]
        dp_ref[:, 2 * d:3 * d] = (dyv * m["gu"] * mixed * dsilu).astype(BF16)
        doc = dyv * silu
        dp_ref[:, 0:d] = (doc * mixed * m["dgu"]).astype(BF16)
        dmixed = doc * m["gu"]
        for gi in range(ngr):
            cs = slice(gi * LANE, (gi + 1) * LANE)
            dm = dmixed[:, cs]
            wm = jnp.where(m["tril"], ws_ref[gi], 0.0)
            dws_ref[gi] += jnp.where(m["tril"], lax.dot_general(dm, m["vn"][:, cs], NT_DIMS, precision=HI3,
                                                                preferred_element_type=F32), 0.0)
            dbacc_ref[gi] += dm
            dvn_ref[:, cs] = lax.dot_general(wm, dm, TN_DIMS, precision=HI3, preferred_element_type=F32)
        dvn = dvn_ref[...]
        dlg_ref[...] += _colsum(dvn * m["vhat"])
        dlb_ref[...] += _colsum(dvn)
        dvh = dvn * lg[...]
        dgv = m["rstd"] * (dvh - jnp.mean(dvh, axis=-1, keepdims=True)
                           - m["vhat"] * jnp.mean(dvh * m["vhat"], axis=-1, keepdims=True))
        dp_ref[:, d:2 * d] = (dgv * m["dgv"]).astype(BF16)

        @pl.when(step == nsteps - 1)
        def _():
            dbs_ref[...] = jnp.sum(dbacc_ref[...], axis=-1, keepdims=True)

    return pl.pallas_call(
        body, name="sgu_bwd", grid=(nsteps,),
        in_specs=[_row_spec(GMLP_CHUNK, d3), _full_spec((1, d)), _full_spec((1, d)),
                  _full_spec((ngr, GMLP_CHUNK, GMLP_CHUNK)), _full_spec((ngr, GMLP_CHUNK, LANE)), _row_spec(GMLP_CHUNK, d)],
        out_specs=[_row_spec(GMLP_CHUNK, d3), _full_spec((ngr, GMLP_CHUNK, GMLP_CHUNK)), _full_spec((ngr, GMLP_CHUNK, 1)),
                   _full_spec((1, d)), _full_spec((1, d))],
        out_shape=[jax.ShapeDtypeStruct((n, d3), BF16), jax.ShapeDtypeStruct((ngr, GMLP_CHUNK, GMLP_CHUNK), F32),
                   jax.ShapeDtypeStruct((ngr, GMLP_CHUNK, 1), F32), jax.ShapeDtypeStruct((1, d), F32),
                   jax.ShapeDtypeStruct((1, d), F32)],
        scratch_shapes=[pltpu.VMEM((GMLP_CHUNK, d), F32), pltpu.VMEM((GMLP_CHUNK, d), F32),
                        pltpu.VMEM((ngr, GMLP_CHUNK, LANE), F32)],
        compiler_params=_cparams(("arbitrary",)),
    )(proj, ln_g, ln_b, w_s, b_bc, dyin)


def _ab_segments(wid, heads):
    return ((0, 3 * wid, 0),
            (3 * wid, 3 * wid + heads, 8 * wid + LANE),
            (3 * wid + heads, 4 * wid + heads, 3 * wid),
            (4 * wid + heads, 7 * wid + heads, 4 * wid),
            (7 * wid + heads, 7 * wid + heads + 2 * LORA, 8 * wid),
            (7 * wid + heads + 2 * LORA, 8 * wid + heads + 2 * LORA, 7 * wid))


def _ab_reorder(shards, wid, heads):
    per = shards[0].shape[1]
    pieces = []
    for o0, o1, _ in sorted(_ab_segments(wid, heads), key=lambda s: s[2]):
        for d, sh in enumerate(shards):
            lo, hi = max(o0, d * per), min(o1, (d + 1) * per)
            if lo < hi:
                pieces.append(sh[:, lo - d * per:hi - d * per])
    pieces.append(jnp.zeros((shards[0].shape[0], LANE - heads), shards[0].dtype))
    return jnp.concatenate(pieces, axis=1)


def _ab_restore(g, wid, heads, n_shards):
    per = (8 * wid + heads + 2 * LORA) // n_shards
    out = []
    for d in range(n_shards):
        pieces = []
        for o0, o1, r0 in _ab_segments(wid, heads):
            lo, hi = max(o0, d * per), min(o1, (d + 1) * per)
            if lo < hi:
                pieces.append(g[:, r0 + lo - o0:r0 + hi - o0])
        out.append(jnp.concatenate(pieces, axis=1))
    return out


ODD_PIECES = ("c_w_in", "c_w_out", "ple_w_gate1", "ple_w_proj1", "ab_w_out", "ple_w_gate0", "ple_w_proj0")
ODD_BY_ROWS = (0, 1, 1, 0, 1, 1, 0)


def local_step(x, p, target, wts, n_shards, odd_shards):
    bsz, seq, d = x.shape
    n = bsz * seq
    wid = d // 2
    nw = wid // LANE
    heads = wid // HEAD
    x2 = x.reshape(n, d)
    tgt = target.reshape(n, d)
    p0, p1 = p[0].reshape(n, -1), p[1].reshape(n, -1)
    row = lambda a: a.reshape(1, -1)
    g_pre0, g_pre1 = row(wts["norm_pre"][0]), row(wts["norm_pre"][1])
    g_post0, g_post1 = row(wts["norm_post"][0]), row(wts["norm_post"][1])
    w_ab = wts["ab_w_in_r"]
    rows_cut = lambda a: a.reshape((n_shards, a.shape[0] // n_shards) + a.shape[1:])
    fbias = jnp.pad(row(wts["fox_f_bias"]), ((0, 0), (0, LANE - heads)))
    mu = row(wts["rwkv_mu"])
    prm = dict(mu_r=mu[:, 0:wid], mu_k=mu[:, wid:2 * wid], mu_v=mu[:, 2 * wid:3 * wid], mu_wa=mu[:, 3 * wid:],
               w0=row(wts["rwkv_w0"]), w2=wts["rwkv_w2"], a0=row(wts["rwkv_a0"]), a2=wts["rwkv_a2"],
               k_k=row(wts["rwkv_k_k"]), k_a=row(wts["rwkv_k_a"]), r_k=row(wts["rwkv_r_k"]),
               ln_g=row(wts["rwkv_ln_g"]), ln_b=row(wts["rwkv_ln_b"]))
    c_ln_g, c_ln_b = row(wts["c_ln_g"]), row(wts["c_ln_b"])
    w_s = wts["c_w_s"]
    b_bc = jnp.broadcast_to(wts["c_b_s"][:, :, None], w_s.shape[:2] + (LANE,))

    xn0 = rms_fwd(x2, g_pre0, name="rms_pre0")
    proj0 = matmul(xn0, w_ab, name="ab_in")
    c = fox_gate_fwd(proj0, fbias, bsz, seq, nw)
    crow = jnp.pad(c.reshape(bsz, seq, LANE)[:, :, :heads].transpose(0, 2, 1).reshape(bsz, nw, 2, seq),
                   ((0, 0), (0, 0), (0, 6), (0, 0)))
    oa, ya = fox_attn_fwd(proj0, c, crow, bsz, seq, nw)
    sr, sw, sk, sv, skk, ska = rwkv_pre_fwd(proj0, prm, bsz, seq, nw)
    ysc, states, odd_w = rwkv_scan_fwd(sr, sw, sk, sv, skk, ska, bsz, seq, nw, [odd_shards[piece] for piece in ODD_PIECES])
    w_c, w_cout, wg1, wp1, w_out, wg0, wp0 = (_join_shards(a, 1 - by_rows) for a, by_rows in zip(odd_w, ODD_BY_ROWS))
    yb = rwkv_post_fwd(ysc, sr, sk, sv, proj0, prm, nw)
    ycat = jnp.concatenate([ya, yb], axis=1)
    y0 = matmul(ycat, w_out, name="ab_out")
    h1, h1b = rms_res_fwd(y0, x2, g_post0, name="rms_post0")
    a0 = matmul(p0, wp0, name="ple_proj0")
    z0 = matmul(h1b, wg0, name="ple_gate0")
    h1p = ple_fwd(h1, a0, z0, name="ple_fwd0")
    xn1 = rms_fwd(h1p, g_pre1, name="rms_pre1")
    proj1 = matmul(xn1, w_c, name="c_in")
    yin = sgu_fwd(proj1, c_ln_g, c_ln_b, w_s, b_bc)
    y1 = matmul(yin, w_cout, name="c_out")
    h2, h2b = rms_res_fwd(y1, h1p, g_post1, name="rms_post1")
    a1 = matmul(p1, wp1, name="ple_proj1")
    z1 = matmul(h2b, wg1, name="ple_gate1")
    h2p = ple_fwd(h2, a1, z1, name="ple_fwd1")
    dh, loss = loss_head(h2p, tgt, name="loss_head")

    g = {}
    da1, dz1 = ple_bwd(dh, a1, z1, name="ple_bwd1")
    g["ple_w_proj1"] = matmul(p1, da1, ta=True, out_dtype=BF16, col_blocks=n_shards, name="d_ple_proj1")
    g["ple_w_gate1"] = rows_cut(matmul(h2b, dz1, ta=True, out_dtype=BF16, name="d_ple_gate1"))
    dh2 = matmul(dz1, wg1, tb=True, add=dh, name="dx_ple_gate1")
    dy1, g_post1_g = rms_bwd(y1, dh2, g_post1, out_dtype=BF16, name="rms_post1_bwd")
    g["c_w_out"] = rows_cut(matmul(yin, dy1, ta=True, out_dtype=BF16, name="d_c_out"))
    dyin = matmul(dy1, w_cout, tb=True, out_dtype=BF16, name="dx_c_out")
    dproj1, g["c_w_s"], g_bs, g["c_ln_g"], g["c_ln_b"] = sgu_bwd(proj1, c_ln_g, c_ln_b, w_s, b_bc, dyin)
    g["c_b_s"] = g_bs.reshape(w_s.shape[:2])
    g["c_w_in"] = matmul(xn1, dproj1, ta=True, out_dtype=BF16, col_blocks=n_shards, name="d_c_in")
    dxn1 = matmul(dproj1, w_c, tb=True, out_dtype=BF16, name="dx_c_in")
    dh1p, g_pre1_g = rms_bwd(h1p, dxn1, g_pre1, res=dh2, name="rms_pre1_bwd")
    da0, dz0 = ple_bwd(dh1p, a0, z0, name="ple_bwd0")
    g["ple_w_proj0"] = matmul(p0, da0, ta=True, out_dtype=BF16, col_blocks=n_shards, name="d_ple_proj0")
    g["ple_w_gate0"] = rows_cut(matmul(h1b, dz0, ta=True, out_dtype=BF16, name="d_ple_gate0"))
    dh1 = matmul(dz0, wg0, tb=True, add=dh1p, name="dx_ple_gate0")
    dy0, g_post0_g = rms_bwd(y0, dh1, g_post0, out_dtype=BF16, name="rms_post0_bwd")
    g["ab_w_out"] = rows_cut(matmul(ycat, dy0, ta=True, out_dtype=BF16, name="d_ab_out"))
    dycat = matmul(dy0, w_out, tb=True, out_dtype=BF16, name="dx_ab_out")
    dysc, dr_b, dk_b, dv_b, dgb, g["rwkv_ln_g"], g["rwkv_ln_b"], g_rk = rwkv_post_bwd(ysc, sr, sk, sv, proj0, prm, dycat, nw)
    g["rwkv_r_k"] = g_rk.reshape(wts["rwkv_r_k"].shape)
    grads_scan, odd_landed = rwkv_scan_bwd(sr, sw, sk, sv, skk, ska, states, dysc, bsz, seq, nw, [g.pop(piece) for piece in ODD_PIECES])
    (dxr, dxk, dxv, dxwa, dmur, dmuk, dmuv, dmuwa, g["rwkv_w0"], g["rwkv_w2"], g["rwkv_a0"], g["rwkv_a2"],
     g["rwkv_k_k"], g["rwkv_k_a"]) = rwkv_pre_bwd(proj0, prm, grads_scan, (dr_b, dk_b, dv_b), bsz, seq, nw)
    g["rwkv_mu"] = jnp.concatenate([dmur, dmuk, dmuv, dmuwa], axis=1)
    dq, dk, dv, dga, dcrow = fox_attn_bwd(proj0, c, crow, oa, dycat, bsz, seq, nw)
    dc = jnp.pad(dcrow[:, :, :2, :].reshape(bsz, heads, seq).transpose(0, 2, 1), ((0, 0), (0, 0), (0, LANE - heads)))
    dfa, g_fb = fox_gate_bwd(proj0, fbias, dc.reshape(n, LANE), bsz, seq, nw)
    g["fox_f_bias"] = g_fb[:, :heads]
    dproj0 = jnp.concatenate([dq, dk, dv, dga, dxr, dxk, dxv, dgb, dxwa, dfa], axis=1)
    cut = 3 * d // 8
    shard_major = lambda full: jnp.stack(_ab_restore(full, wid, heads, n_shards))
    g_lo = shard_major(matmul(xn0[:, :cut], dproj0, ta=True, out_dtype=BF16, name="d_ab_in_lo"))
    g_hi, (lo_landed,) = matmul(xn0[:, cut:], dproj0, ta=True, out_dtype=BF16, scatter=[g_lo], name="d_ab_in_hi")
    dxn0, (hi_landed,) = matmul(dproj0, w_ab, tb=True, out_dtype=BF16, scatter=[shard_major(g_hi)], name="dx_ab_in")
    ab_landed = jnp.concatenate([lo_landed, hi_landed], axis=1)
    dx, g_pre0_g = rms_bwd(x2, dxn0, g_pre0, res=dh1, name="rms_pre0_bwd")

    g["norm_pre"] = jnp.concatenate([g_pre0_g, g_pre1_g], axis=0)
    g["norm_post"] = jnp.concatenate([g_post0_g, g_post1_g], axis=0)
    return loss, dx.reshape(bsz, seq, d), g, dict(zip(ODD_PIECES + ("ab_w_in",), tuple(odd_landed) + (ab_landed,)))


MESH = pl.DeviceIdType.MESH
ANY = pl.BlockSpec(memory_space=pl.ANY)
PACK_COLS = 1024
PACK_ROWS = 16


def _mesh_place():
    xi, yi, ci = lax.axis_index("x"), lax.axis_index("y"), lax.axis_index("c")
    return xi, yi, ci, 4 * xi + 2 * yi + ci


def _peer(xi, yi, ci, m):
    px = 1 - xi if m & 4 else xi
    py = 1 - yi if m & 2 else yi
    pc = 1 - ci if m & 1 else ci
    return (px, py, pc), 4 * px + 2 * py + pc


def _gather(arrays, *, name):
    n = len(arrays)

    def body(*refs):
        ins, outs = refs[:n], refs[n:2 * n]
        send_sems, recv_sems, local_sems = refs[2 * n:]
        xi, yi, ci, me = _mesh_place()
        sibling = (xi, yi, 1 - ci)
        chips = [(1 - xi, yi), (xi, 1 - yi), (1 - xi, 1 - yi)]
        block = lambda px, py, pc: 4 * px + 2 * py + pc

        def copy(a, k, blk, to, src=None):
            return pltpu.make_async_remote_copy(src_ref=outs[a].at[blk] if src is None else src, dst_ref=outs[a].at[blk],
                                                send_sem=send_sems.at[k, a], recv_sem=recv_sems.at[k, a],
                                                device_id=to, device_id_type=MESH)

        started = []
        for a in range(n):
            cp = pltpu.make_async_copy(ins[a], outs[a].at[me], local_sems.at[a])
            cp.start()
            started.append(cp)
        sends = []
        for a in range(n):
            sends.append(copy(a, 0, me, sibling, src=ins[a]))
            sends += [copy(a, 1 + j, me, (*chip, ci), src=ins[a]) for j, chip in enumerate(chips)]
        for cp in sends:
            cp.start()
        for j, chip in enumerate(chips):
            for a in range(n):
                copy(a, 1 + j, block(*chip, ci), (xi, yi, ci)).wait_recv()
                fwd = copy(a, 4 + j, block(*chip, ci), sibling)
                fwd.start()
                sends.append(fwd)
        for a in range(n):
            copy(a, 0, block(xi, yi, 1 - ci), (xi, yi, ci)).wait_recv()
            for j, chip in enumerate(chips):
                copy(a, 4 + j, block(*chip, 1 - ci), (xi, yi, ci)).wait_recv()
        for cp in sends:
            cp.wait_send()
        for cp in started:
            cp.wait()

    return pl.pallas_call(
        body, name=name, in_specs=[ANY] * n, out_specs=[ANY] * n,
        out_shape=[jax.ShapeDtypeStruct((N_DEV,) + a.shape, a.dtype) for a in arrays],
        scratch_shapes=[pltpu.SemaphoreType.DMA((N_DEV - 1, n)), pltpu.SemaphoreType.DMA((N_DEV - 1, n)),
                        pltpu.SemaphoreType.DMA((n,))],
    )(*arrays)


def _exchange_sems(n):
    if n == 0:
        return []
    return [pltpu.SemaphoreType.DMA((N_DEV - 1, n)), pltpu.SemaphoreType.DMA((N_DEV - 1, n)), pltpu.SemaphoreType.DMA((n,))]


def _direct_exchange(ins, outs, send_sems=None, recv_sems=None, local_sems=None, *, scatter):
    n = len(ins)

    def copies():
        xi, yi, ci, me = _mesh_place()
        src = lambda a, idx: ins[a].at[idx] if scatter else ins[a]
        local = [pltpu.make_async_copy(src(a, me), outs[a].at[me], local_sems.at[a]) for a in range(n)]
        sends, arrivals = [], []
        for m in range(1, N_DEV):
            peer, pidx = _peer(xi, yi, ci, m)
            for a in range(n):
                pair = dict(send_sem=send_sems.at[m - 1, a], recv_sem=recv_sems.at[m - 1, a], device_id=peer, device_id_type=MESH)
                sends.append(pltpu.make_async_remote_copy(src_ref=src(a, pidx), dst_ref=outs[a].at[me], **pair))
                arrivals.append(pltpu.make_async_remote_copy(src_ref=src(a, pidx), dst_ref=outs[a].at[pidx], **pair))
        return local, sends, arrivals

    def start():
        local, sends, _ = copies()
        for cp in local + sends:
            cp.start()

    def wait():
        local, sends, arrivals = copies()
        for cp in arrivals:
            cp.wait_recv()
        for cp in sends:
            cp.wait_send()
        for cp in local:
            cp.wait()

    return start, wait


def sum_blocks(x, *, name):
    _, rows, cols = x.shape
    tr = _tile(rows, 256, PACK_ROWS)

    def body(x_ref, o_ref):
        acc = x_ref[0].astype(F32)
        for s in range(1, N_DEV):
            acc = acc + x_ref[s].astype(F32)
        o_ref[...] = acc

    return pl.pallas_call(
        body, name=name, grid=(rows // tr,), in_specs=[pl.BlockSpec((N_DEV, tr, cols), lambda i: (0, i, 0))],
        out_specs=pl.BlockSpec((tr, cols), lambda i: (i, 0)), out_shape=jax.ShapeDtypeStruct((rows, cols), F32),
        compiler_params=_cparams(("parallel",)),
    )(x)


def adamw(w, parts, m, v, *, name):
    rows, cols = w.shape
    nparts = parts.shape[0]
    tr = _tile(rows, 256, 16)
    c1 = 1.0 / (1.0 - ADAM_B1 ** ADAM_STEP)
    c2 = 1.0 / (1.0 - ADAM_B2 ** ADAM_STEP)

    def body(w_ref, p_ref, m_ref, v_ref, g_ref, d_ref, mo_ref, vo_ref):
        gv = p_ref[0].astype(F32)
        for s in range(1, nparts):
            gv = gv + p_ref[s].astype(F32)
        mn = ADAM_B1 * m_ref[...] + (1.0 - ADAM_B1) * gv
        vn = ADAM_B2 * v_ref[...] + (1.0 - ADAM_B2) * (gv * gv)
        g_ref[...] = gv
        d_ref[...] = -ADAM_LR * ((mn * c1) / (jnp.sqrt(vn * c2) + ADAM_EPS) + ADAM_WD * w_ref[...])
        mo_ref[...] = mn
        vo_ref[...] = vn

    spec = pl.BlockSpec((tr, cols), lambda i: (i, 0))
    return pl.pallas_call(
        body, name=name, grid=(rows // tr,), in_specs=[spec, pl.BlockSpec((nparts, tr, cols), lambda i: (0, i, 0)), spec, spec],
        out_specs=[spec] * 4, out_shape=[jax.ShapeDtypeStruct((rows, cols), F32)] * 4, compiler_params=_cparams(("parallel",)),
    )(w, parts, m, v)


def _pack(arrays, dtype):
    flat = jnp.concatenate([a.astype(dtype).reshape(-1) for a in arrays])
    unit = PACK_COLS * PACK_ROWS
    total = -(-flat.shape[0] // unit) * unit
    return jnp.pad(flat, (0, total - flat.shape[0])).reshape(total // PACK_COLS, PACK_COLS)


def _unpack(flat2d, shapes, lead=()):
    flat = flat2d.reshape(lead + (-1,))
    out, off = [], 0
    for shp in shapes:
        size = 1
        for s in shp:
            size *= s
        out.append(flat[..., off:off + size].reshape(lead + tuple(shp)))
        off += size
    return out


def _join_shards(sh, axis):
    return jnp.concatenate([sh[d] for d in range(N_DEV)], axis=axis)


BIG = (("ab_w_in", "ab_w_in", None, False), ("c_w_in", "c_w_in", None, False), ("ab_w_out", "ab_w_out", None, True),
       ("c_w_out", "c_w_out", None, True), ("ple_w_gate0", "ple_w_gate", 0, True), ("ple_w_gate1", "ple_w_gate", 1, True),
       ("ple_w_proj0", "ple_w_proj", 0, False), ("ple_w_proj1", "ple_w_proj", 1, False))
SMALL_SHARDED = (("rwkv_w2", 1), ("rwkv_a2", 1), ("c_ln_g", 0), ("c_ln_b", 0))
REPLICATED = ("norm_pre", "norm_post", "fox_f_bias", "rwkv_mu", "rwkv_w0", "rwkv_a0", "rwkv_k_k", "rwkv_k_a", "rwkv_r_k",
              "rwkv_ln_g", "rwkv_ln_b", "c_w_s", "c_b_s")
WEIGHTS = ("norm_pre", "norm_post", "ab_w_in", "fox_f_bias", "rwkv_mu", "rwkv_w0", "rwkv_w2", "rwkv_a0", "rwkv_a2", "rwkv_k_k",
           "rwkv_k_a", "rwkv_r_k", "rwkv_ln_g", "rwkv_ln_b", "ab_w_out", "c_w_in", "c_ln_g", "c_ln_b", "c_w_s", "c_b_s",
           "c_w_out", "ple_w_proj", "ple_w_gate")
SQUEEZED = ("norm_pre", "norm_post", "ple_w_proj", "ple_w_gate")


def _step(x, p, loss_target, w, mom, vel):
    sq = {k: (a if k in SQUEEZED else a[0]) for k, a in w.items()}
    _, _, _, me = _mesh_place()

    wid, heads = x.shape[-1] // 2, x.shape[-1] // 2 // HEAD
    layer = lambda a, l: a if l is None else a[l]

    shards = {piece: layer(sq[wname], l).astype(BF16) for piece, wname, l, _ in BIG}
    even = [entry for entry in BIG if entry[0] not in ODD_PIECES]
    sends = [shards[piece] for piece, _, _, _ in even]
    sends += [sq[k] if sq[k].ndim == 2 else sq[k].reshape(1, -1) for k, _ in SMALL_SHARDED]
    gathered = _gather(sends, name="gather_weights")
    full = {}
    for (piece, _, _, by_rows), got in zip(even, gathered):
        if piece == "ab_w_in":
            full["ab_w_in_r"] = _ab_reorder([got[d] for d in range(N_DEV)], wid, heads)
        else:
            full[piece] = _join_shards(got, 1 - by_rows)
    for (k, ax), got in zip(SMALL_SHARDED, gathered[len(even):]):
        full[k] = _join_shards(got, 1).reshape(-1) if sq[k].ndim == 1 else _join_shards(got, 1)
    for k in REPLICATED:
        full[k] = sq[k]

    loss, grad_x, g, parts = local_step(x, p, loss_target, full, N_DEV, {piece: shards[piece] for piece in ODD_PIECES})

    small_names = [k for k, _ in SMALL_SHARDED] + list(REPLICATED)
    small_full_shapes = [full[k].shape for k in small_names] + [(1, 1)]
    partial = _gather([_pack([g[k].reshape(full[k].shape) for k in small_names] + [loss], F32)], name="gather_small_grads")[0]
    summed = _unpack(sum_blocks(partial, name="sum_small_grads"), small_full_shapes)
    small_grads = dict(zip(small_names, summed[:-1]))
    for k, ax in SMALL_SHARDED:
        width = sq[k].shape[ax]
        small_grads[k] = lax.dynamic_slice_in_dim(small_grads[k], me * width, width, axis=ax)
    loss_all = summed[-1][0, 0]

    outs_g, outs_d, outs_m, outs_v = [], [], [], []
    for k in WEIGHTS:
        shape = w[k].shape
        two_d = lambda a: a.reshape(-1, a.shape[-1])
        pieces = [(piece, l) for piece, wname, l, _ in BIG if wname == k]
        if not pieces:
            res = adamw(two_d(w[k]), two_d(small_grads[k].reshape(shape))[None], two_d(mom[k]), two_d(vel[k]), name="adamw_" + k)
        elif pieces[0][1] is None:
            res = adamw(two_d(w[k]), parts[k], two_d(mom[k]), two_d(vel[k]), name="adamw_" + k)
        else:
            per_layer = [adamw(w[k][l], parts[piece], mom[k][l], vel[k][l], name="adamw_" + piece) for piece, l in pieces]
            res = [jnp.stack(r) for r in zip(*per_layer)]
        for out, r in zip((outs_g, outs_d, outs_m, outs_v), res):
            out.append(r.reshape(shape))
    return (loss_all, grad_x, *outs_g, *outs_d, *outs_m, *outs_v)


def kernel(x, p, norm_pre, norm_post, ab_w_in, fox_f_bias, rwkv_mu, rwkv_w0, rwkv_w2, rwkv_a0, rwkv_a2, rwkv_k_k, rwkv_k_a, rwkv_r_k, rwkv_ln_g, rwkv_ln_b, ab_w_out, c_w_in, c_ln_g, c_ln_b, c_w_s, c_b_s, c_w_out, ple_w_proj, ple_w_gate, loss_target, m_norm_pre, m_norm_post, m_ab_w_in, m_fox_f_bias, m_rwkv_mu, m_rwkv_w0, m_rwkv_w2, m_rwkv_a0, m_rwkv_a2, m_rwkv_k_k, m_rwkv_k_a, m_rwkv_r_k, m_rwkv_ln_g, m_rwkv_ln_b, m_ab_w_out, m_c_w_in, m_c_ln_g, m_c_ln_b, m_c_w_s, m_c_b_s, m_c_w_out, m_ple_w_proj, m_ple_w_gate, v_norm_pre, v_norm_post, v_ab_w_in, v_fox_f_bias, v_rwkv_mu, v_rwkv_w0, v_rwkv_w2, v_rwkv_a0, v_rwkv_a2, v_rwkv_k_k, v_rwkv_k_a, v_rwkv_r_k, v_rwkv_ln_g, v_rwkv_ln_b, v_ab_w_out, v_c_w_in, v_c_ln_g, v_c_ln_b, v_c_w_s, v_c_b_s, v_c_w_out, v_ple_w_proj, v_ple_w_gate):
    w = dict(norm_pre=norm_pre, norm_post=norm_post, ab_w_in=ab_w_in, fox_f_bias=fox_f_bias, rwkv_mu=rwkv_mu, rwkv_w0=rwkv_w0, rwkv_w2=rwkv_w2, rwkv_a0=rwkv_a0, rwkv_a2=rwkv_a2, rwkv_k_k=rwkv_k_k, rwkv_k_a=rwkv_k_a, rwkv_r_k=rwkv_r_k, rwkv_ln_g=rwkv_ln_g, rwkv_ln_b=rwkv_ln_b, ab_w_out=ab_w_out, c_w_in=c_w_in, c_ln_g=c_ln_g, c_ln_b=c_ln_b, c_w_s=c_w_s, c_b_s=c_b_s, c_w_out=c_w_out, ple_w_proj=ple_w_proj, ple_w_gate=ple_w_gate)
    mom = dict(norm_pre=m_norm_pre, norm_post=m_norm_post, ab_w_in=m_ab_w_in, fox_f_bias=m_fox_f_bias, rwkv_mu=m_rwkv_mu, rwkv_w0=m_rwkv_w0, rwkv_w2=m_rwkv_w2, rwkv_a0=m_rwkv_a0, rwkv_a2=m_rwkv_a2, rwkv_k_k=m_rwkv_k_k, rwkv_k_a=m_rwkv_k_a, rwkv_r_k=m_rwkv_r_k, rwkv_ln_g=m_rwkv_ln_g, rwkv_ln_b=m_rwkv_ln_b, ab_w_out=m_ab_w_out, c_w_in=m_c_w_in, c_ln_g=m_c_ln_g, c_ln_b=m_c_ln_b, c_w_s=m_c_w_s, c_b_s=m_c_b_s, c_w_out=m_c_w_out, ple_w_proj=m_ple_w_proj, ple_w_gate=m_ple_w_gate)
    vel = dict(norm_pre=v_norm_pre, norm_post=v_norm_post, ab_w_in=v_ab_w_in, fox_f_bias=v_fox_f_bias, rwkv_mu=v_rwkv_mu, rwkv_w0=v_rwkv_w0, rwkv_w2=v_rwkv_w2, rwkv_a0=v_rwkv_a0, rwkv_a2=v_rwkv_a2, rwkv_k_k=v_rwkv_k_k, rwkv_k_a=v_rwkv_k_a, rwkv_r_k=v_rwkv_r_k, rwkv_ln_g=v_rwkv_ln_g, rwkv_ln_b=v_rwkv_ln_b, ab_w_out=v_ab_w_out, c_w_in=v_c_w_in, c_ln_g=v_c_ln_g, c_ln_b=v_c_ln_b, c_w_s=v_c_w_s, c_b_s=v_c_b_s, c_w_out=v_c_w_out, ple_w_proj=v_ple_w_proj, ple_w_gate=v_ple_w_gate)
    return _step(x, p, loss_target, w, mom, vel)
```

```python
import functools

import jax
import jax.numpy as jnp
from jax import lax
from jax.experimental import pallas as pl
from jax.experimental.pallas import tpu as pltpu

F32 = jnp.float32
BF16 = jnp.bfloat16
HI = lax.Precision.HIGHEST
HI3 = lax.Precision.HIGH

HEAD = 64
LANE = 128
LORA = 64
GMLP_CHUNK = 128
RMS_EPS = 1e-6
LN_EPS = 1e-5
GN_EPS = 64e-5
VMEM_LIMIT = 56 * 1024 * 1024
N_DEV = 8

ADAM_LR = 0.001
ADAM_B1 = 0.9
ADAM_B2 = 0.999
ADAM_EPS = 1e-08
ADAM_WD = 0.01
ADAM_STEP = 10

NT_DIMS = (((1,), (1,)), ((), ()))
TN_DIMS = (((0,), (0,)), ((), ()))


def _cparams(sem):
    return pltpu.CompilerParams(dimension_semantics=sem, vmem_limit_bytes=VMEM_LIMIT)


def _tile(n, target, q=LANE):
    if n <= target:
        return n
    best = None
    for d in range(q, target + 1, q):
        if n % d == 0:
            best = d
    assert best is not None, (n, target, q)
    return best


def _sigmoid(x):
    return 1.0 / (1.0 + jnp.exp(-x))


def _silu_and_grad(x):
    s = _sigmoid(x)
    return x * s, s * (1.0 + x * (1.0 - s))


def _gelu_and_grad(x):
    cdf = 0.5 * (1.0 + lax.erf(x * 0.7071067811865476))
    pdf = jnp.exp(-0.5 * x * x) * 0.3989422804014327
    return x * cdf, cdf + x * pdf


def _iota(shape, dim):
    return lax.broadcasted_iota(jnp.int32, shape, dim)


def _head_ones():
    return (_iota((LANE, LANE), 0) // HEAD == _iota((LANE, LANE), 1) // HEAD).astype(F32)


def _head_eye():
    return (_iota((HEAD, LANE), 0) == _iota((HEAD, LANE), 1) % HEAD).astype(F32)


def _hsum(x, ones):
    hi = x.astype(BF16)
    lo = (x - hi.astype(F32)).astype(BF16)
    ones_b = ones.astype(BF16)
    return jnp.dot(jnp.concatenate([hi, lo], axis=1), jnp.concatenate([ones_b, ones_b], axis=0), preferred_element_type=F32)


def _colsum(x):
    return jnp.sum(x, axis=0, keepdims=True)


def matmul(a, b, *, name, ta=False, tb=False, add=None, out_dtype=F32, tm=1024, tn=1024, tk=2048, col_blocks=1, scatter=()):
    m, k = (a.shape[1], a.shape[0]) if ta else a.shape
    n = b.shape[0] if tb else b.shape[1]
    assert (b.shape[1] if tb else b.shape[0]) == k
    tm, tn, tk = _tile(m, tm), _tile(n // col_blocks, tn), _tile(k, tk)
    nk = k // tk
    per = n // col_blocks // tn
    grid = (m // tm, n // tn, nk)
    ns = len(scatter)
    n_in = 2 + (add is not None)

    def body(*refs):
        a_ref, b_ref = refs[:2]
        add_ref = refs[2] if add is not None else None
        o_ref = refs[n_in + ns]
        acc_ref = refs[n_in + 2 * ns + 1]
        if ns:
            start, wait = _direct_exchange(refs[n_in:n_in + ns], refs[n_in + ns + 1:n_in + 2 * ns + 1], *refs[n_in + 2 * ns + 2:],
                                           scatter=True)
            first, last = _grid_ends(grid)
            pl.when(first)(start)
        kk = pl.program_id(2)

        @pl.when(kk == 0)
        def _():
            acc_ref[...] = jnp.zeros_like(acc_ref)

        dims = (((0 if ta else 1,), (1 if tb else 0,)), ((), ()))
        acc_ref[...] += lax.dot_general(a_ref[...].astype(BF16), b_ref[...].astype(BF16), dims,
                                        preferred_element_type=F32)

        @pl.when(kk == nk - 1)
        def _():
            r = acc_ref[...]
            if add is not None:
                r = r + add_ref[...].astype(F32)
            o_ref[...] = r.astype(out_dtype)

        if ns:
            pl.when(last)(wait)

    a_spec = pl.BlockSpec((tk, tm), lambda i, j, kk: (kk, i)) if ta else pl.BlockSpec((tm, tk), lambda i, j, kk: (i, kk))
    b_spec = pl.BlockSpec((tn, tk), lambda i, j, kk: (j, kk)) if tb else pl.BlockSpec((tk, tn), lambda i, j, kk: (kk, j))
    o_spec = pl.BlockSpec((tm, tn), lambda i, j, kk: (i, j))
    in_specs = [a_spec, b_spec] + ([o_spec] if add is not None else [])
    args = (a, b) + ((add,) if add is not None else ())
    out_shape = (m, n)
    if col_blocks > 1:
        assert add is None
        o_spec = pl.BlockSpec((None, tm, tn), lambda i, j, kk: (j // per, i, j % per))
        out_shape = (col_blocks, m, n // col_blocks)
    out = pl.pallas_call(
        body, name=name, grid=grid, in_specs=in_specs + [ANY] * ns, out_specs=[o_spec] + [ANY] * ns,
        out_shape=[jax.ShapeDtypeStruct(out_shape, out_dtype)] + [jax.ShapeDtypeStruct(s.shape, s.dtype) for s in scatter],
        scratch_shapes=[pltpu.VMEM((tm, tn), F32)] + _exchange_sems(ns),
        compiler_params=_cparams(("arbitrary",) * 3 if ns else ("parallel", "parallel", "arbitrary")),
    )(*args, *scatter)
    return (out[0], out[1:]) if ns else out[0]


def _row_spec(tm, width, cb=0):
    return pl.BlockSpec((tm, width), lambda i: (i, cb))


def _full_spec(shape):
    return pl.BlockSpec(shape, lambda i: (0,) * len(shape))


def rms_fwd(x, g, *, name):
    n, d = x.shape
    tm = _tile(n, 256, 8)

    def body(x_ref, g_ref, o_ref):
        xv = x_ref[...]
        r = lax.rsqrt(jnp.mean(xv * xv, axis=-1, keepdims=True) + RMS_EPS)
        o_ref[...] = (xv * r * g_ref[...]).astype(BF16)

    return pl.pallas_call(
        body, name=name, grid=(n // tm,), in_specs=[_row_spec(tm, d), _full_spec((1, d))],
        out_specs=_row_spec(tm, d), out_shape=jax.ShapeDtypeStruct((n, d), BF16),
        compiler_params=_cparams(("parallel",)),
    )(x, g)


def rms_res_fwd(y, res, g, *, name):
    n, d = y.shape
    tm = _tile(n, 256, 8)

    def body(y_ref, res_ref, g_ref, h_ref, hb_ref):
        yv = y_ref[...]
        r = lax.rsqrt(jnp.mean(yv * yv, axis=-1, keepdims=True) + RMS_EPS)
        h = res_ref[...] + yv * r * g_ref[...]
        h_ref[...] = h
        hb_ref[...] = h.astype(BF16)

    return pl.pallas_call(
        body, name=name, grid=(n // tm,), in_specs=[_row_spec(tm, d), _row_spec(tm, d), _full_spec((1, d))],
        out_specs=[_row_spec(tm, d), _row_spec(tm, d)],
        out_shape=[jax.ShapeDtypeStruct((n, d), F32), jax.ShapeDtypeStruct((n, d), BF16)],
        compiler_params=_cparams(("parallel",)),
    )(y, res, g)


def rms_bwd(x, dy, g, *, name, res=None, out_dtype=F32):
    n, d = x.shape
    tm = _tile(n, 256, 8)

    def body(*refs):
        if res is None:
            x_ref, dy_ref, g_ref, dx_ref, dg_ref = refs
        else:
            x_ref, dy_ref, g_ref, res_ref, dx_ref, dg_ref = refs

        @pl.when(pl.program_id(0) == 0)
        def _():
            dg_ref[...] = jnp.zeros_like(dg_ref)

        xv = x_ref[...]
        dyv = dy_ref[...].astype(F32)
        r = lax.rsqrt(jnp.mean(xv * xv, axis=-1, keepdims=True) + RMS_EPS)
        xhat = xv * r
        dxh = dyv * g_ref[...]
        dx = r * (dxh - xhat * jnp.mean(dxh * xhat, axis=-1, keepdims=True))
        if res is not None:
            dx = dx + res_ref[...]
        dx_ref[...] = dx.astype(out_dtype)
        dg_ref[...] += _colsum(dyv * xhat)

    in_specs = [_row_spec(tm, d), _row_spec(tm, d), _full_spec((1, d))] + ([_row_spec(tm, d)] if res is not None else [])
    args = (x, dy, g) + ((res,) if res is not None else ())
    return pl.pallas_call(
        body, name=name, grid=(n // tm,), in_specs=in_specs, out_specs=[_row_spec(tm, d), _full_spec((1, d))],
        out_shape=[jax.ShapeDtypeStruct((n, d), out_dtype), jax.ShapeDtypeStruct((1, d), F32)],
        compiler_params=_cparams(("arbitrary",)),
    )(*args)


def ple_fwd(h, a, z, *, name):
    n, d = h.shape
    tm = _tile(n, 256, 8)

    def body(h_ref, a_ref, z_ref, o_ref):
        o_ref[...] = h_ref[...] + a_ref[...] * _sigmoid(z_ref[...])

    return pl.pallas_call(
        body, name=name, grid=(n // tm,), in_specs=[_row_spec(tm, d)] * 3, out_specs=_row_spec(tm, d),
        out_shape=jax.ShapeDtypeStruct((n, d), F32), compiler_params=_cparams(("parallel",)),
    )(h, a, z)


def ple_bwd(dh, a, z, *, name):
    n, d = dh.shape
    tm = _tile(n, 256, 8)

    def body(dh_ref, a_ref, z_ref, da_ref, dz_ref):
        s = _sigmoid(z_ref[...])
        dhv = dh_ref[...]
        da_ref[...] = (dhv * s).astype(BF16)
        dz_ref[...] = (dhv * a_ref[...] * s * (1.0 - s)).astype(BF16)

    return pl.pallas_call(
        body, name=name, grid=(n // tm,), in_specs=[_row_spec(tm, d)] * 3, out_specs=[_row_spec(tm, d)] * 2,
        out_shape=[jax.ShapeDtypeStruct((n, d), BF16)] * 2, compiler_params=_cparams(("parallel",)),
    )(dh, a, z)


def loss_head(h, target, *, name):
    n, d = h.shape
    tm = _tile(n, 256, 8)

    def body(h_ref, t_ref, dh_ref, loss_ref):
        @pl.when(pl.program_id(0) == 0)
        def _():
            loss_ref[...] = jnp.zeros_like(loss_ref)

        e = h_ref[...] - t_ref[...]
        dh_ref[...] = e * (1.0 / d)
        loss_ref[...] += 0.5 * jnp.sum(jnp.sum(e * e, axis=-1, keepdims=True) * (1.0 / d), axis=0, keepdims=True)

    return pl.pallas_call(
        body, name=name, grid=(n // tm,), in_specs=[_row_spec(tm, d)] * 2,
        out_specs=[_row_spec(tm, d), _full_spec((1, 1))],
        out_shape=[jax.ShapeDtypeStruct((n, d), F32), jax.ShapeDtypeStruct((1, 1), F32)],
        compiler_params=_cparams(("arbitrary",)),
    )(h, target)


def _log_sigmoid(z):
    return jnp.minimum(z, 0.0) - jnp.log1p(jnp.exp(-jnp.abs(z)))


def fox_gate_fwd(proj, fbias, bsz, seq, nw):
    cc = _tile(seq, 256, 8)

    def body(f_ref, b_ref, c_ref):
        low = (_iota((cc, cc), 0) >= _iota((cc, cc), 1)).astype(F32)
        carry = jnp.zeros((1, LANE), F32)
        for ci in range(seq // cc):
            rows = slice(ci * cc, (ci + 1) * cc)
            lf = _log_sigmoid(f_ref[rows, :] + b_ref[...])
            c_ref[rows, :] = jnp.dot(low, lf, precision=HI, preferred_element_type=F32) + carry
            carry = carry + _colsum(lf)

    return pl.pallas_call(
        body, name="fox_gate_fwd", grid=(bsz,),
        in_specs=[pl.BlockSpec((seq, LANE), lambda b: (b, 8 * nw + 1)), _full_spec((1, LANE))],
        out_specs=pl.BlockSpec((seq, LANE), lambda b: (b, 0)),
        out_shape=jax.ShapeDtypeStruct((bsz * seq, LANE), F32), compiler_params=_cparams(("parallel",)),
    )(proj, fbias)


def fox_gate_bwd(proj, fbias, dc, bsz, seq, nw):
    cc = _tile(seq, 256, 8)
    nc = seq // cc

    def body(f_ref, b_ref, dc_ref, df_ref, db_ref):
        @pl.when(pl.program_id(0) == 0)
        def _():
            db_ref[...] = jnp.zeros_like(db_ref)

        upp = (_iota((cc, cc), 0) <= _iota((cc, cc), 1)).astype(F32)
        carry = jnp.zeros((1, LANE), F32)
        dbias = jnp.zeros((1, LANE), F32)
        for ci in reversed(range(nc)):
            rows = slice(ci * cc, (ci + 1) * cc)
            blk = dc_ref[rows, :]
            dlf = jnp.dot(upp, blk, precision=HI, preferred_element_type=F32) + carry
            carry = carry + _colsum(blk)
            df = dlf * _sigmoid(-(f_ref[rows, :] + b_ref[...]))
            df_ref[rows, :] = df.astype(BF16)
            dbias = dbias + _colsum(df)
        db_ref[...] += dbias

    return pl.pallas_call(
        body, name="fox_gate_bwd", grid=(bsz,),
        in_specs=[pl.BlockSpec((seq, LANE), lambda b: (b, 8 * nw + 1)), _full_spec((1, LANE)),
                  pl.BlockSpec((seq, LANE), lambda b: (b, 0))],
        out_specs=[pl.BlockSpec((seq, LANE), lambda b: (b, 0)), _full_spec((1, LANE))],
        out_shape=[jax.ShapeDtypeStruct((bsz * seq, LANE), BF16), jax.ShapeDtypeStruct((1, LANE), F32)],
        compiler_params=_cparams(("arbitrary",)),
    )(proj, fbias, dc)


def _attn_scores(qs, kh, ccol, crow, r0, kend, tq):
    s = lax.dot_general(qs[r0:kend], kh[:kend], NT_DIMS, preferred_element_type=F32)
    s = s + ccol[r0:kend] - crow[:, :kend]
    causal = _iota((tq, kend), 1) <= r0 + _iota((tq, kend), 0)
    return jnp.where(causal, s, -jnp.inf)


def _head_column(c_blk, h):
    return jnp.sum(jnp.where(_iota(c_blk.shape, 1) == h, c_blk, 0.0), axis=1, keepdims=True)


def fox_attn_fwd(proj, c, crow, bsz, seq, nw):
    tq = _tile(seq, 256, 8)
    scale = HEAD ** -0.5

    def body(q_ref, k_ref, v_ref, g_ref, c_ref, crow_ref, o_ref, y_ref):
        hp = pl.program_id(1)
        c_blk = c_ref[...]
        for hh in range(2):
            sl = slice(hh * HEAD, (hh + 1) * HEAD)
            ccol = _head_column(c_blk, 2 * hp + hh)
            crow_h = crow_ref[0, 0, hh:hh + 1, :]
            qs = (q_ref[:, sl] * scale).astype(BF16)
            kh = k_ref[:, sl].astype(BF16)
            vh = v_ref[:, sl].astype(BF16)
            for qi in range(seq // tq):
                r0, kend = qi * tq, (qi + 1) * tq
                s = _attn_scores(qs, kh, ccol, crow_h, r0, kend, tq)
                p = jnp.exp(s - jnp.max(s, axis=-1, keepdims=True))
                o = jnp.dot(p.astype(BF16), vh[:kend], preferred_element_type=F32) / jnp.sum(p, axis=-1, keepdims=True)
                o_ref[r0:kend, sl] = o
                y_ref[r0:kend, sl] = (o * _silu_and_grad(g_ref[r0:kend, sl])[0]).astype(BF16)

    blk = lambda cb: pl.BlockSpec((seq, LANE), lambda b, j: (b, cb * nw + j))
    return pl.pallas_call(
        body, name="fox_attn_fwd", grid=(bsz, nw),
        in_specs=[blk(0), blk(1), blk(2), blk(3), pl.BlockSpec((seq, LANE), lambda b, j: (b, 0)),
                  pl.BlockSpec((1, 1, 8, seq), lambda b, j: (b, j, 0, 0))],
        out_specs=[pl.BlockSpec((seq, LANE), lambda b, j: (b, j))] * 2,
        out_shape=[jax.ShapeDtypeStruct((bsz * seq, nw * LANE), F32), jax.ShapeDtypeStruct((bsz * seq, nw * LANE), BF16)],
        compiler_params=_cparams(("parallel", "parallel")),
    )(proj, proj, proj, proj, c, crow)


def fox_attn_bwd(proj, c, crow, oa, dycat, bsz, seq, nw):
    tq = _tile(seq, 256, 8)
    scale = HEAD ** -0.5

    def body(q_ref, k_ref, v_ref, g_ref, c_ref, crow_ref, o_ref, dy_ref,
             dq_ref, dk_ref, dv_ref, dg_ref, dc_ref, dk_acc, dv_acc, dc_acc):
        hp = pl.program_id(1)
        c_blk = c_ref[...]
        dc_ref[...] = jnp.zeros_like(dc_ref)
        for hh in range(2):
            sl = slice(hh * HEAD, (hh + 1) * HEAD)
            ccol = _head_column(c_blk, 2 * hp + hh)
            crow_h = crow_ref[0, 0, hh:hh + 1, :]
            qs = (q_ref[:, sl] * scale).astype(BF16)
            kh = k_ref[:, sl].astype(BF16)
            vh = v_ref[:, sl].astype(BF16)
            oh = o_ref[:, sl]
            dyh = dy_ref[:, sl].astype(F32)
            silu, dsilu = _silu_and_grad(g_ref[:, sl])
            dg_ref[:, sl] = (dyh * oh * dsilu).astype(BF16)
            do = dyh * silu
            dvec = jnp.sum(do * oh, axis=-1, keepdims=True)
            dob = do.astype(BF16)
            dk_acc[...] = jnp.zeros_like(dk_acc)
            dv_acc[...] = jnp.zeros_like(dv_acc)
            dc_acc[...] = jnp.zeros_like(dc_acc)
            for qi in range(seq // tq):
                r0, kend = qi * tq, (qi + 1) * tq
                s = _attn_scores(qs, kh, ccol, crow_h, r0, kend, tq)
                p = jnp.exp(s - jnp.max(s, axis=-1, keepdims=True))
                p = p / jnp.sum(p, axis=-1, keepdims=True)
                dp = lax.dot_general(dob[r0:kend], vh[:kend], NT_DIMS, preferred_element_type=F32)
                ds = p * (dp - dvec[r0:kend])
                dsb = ds.astype(BF16)
                dq_ref[r0:kend, sl] = (jnp.dot(dsb, kh[:kend], preferred_element_type=F32) * scale).astype(BF16)
                dk_acc[0:kend, :] += lax.dot_general(dsb, qs[r0:kend], TN_DIMS, preferred_element_type=F32)
                dv_acc[0:kend, :] += lax.dot_general(p.astype(BF16), dob[r0:kend], TN_DIMS, preferred_element_type=F32)
                dc_acc[:, 0:kend] += -_colsum(ds)
                ds_lo = (ds - dsb.astype(F32)).astype(BF16)
                rowsum = lax.dot_general(jnp.ones((8, 2 * kend), BF16), jnp.concatenate([dsb, ds_lo], axis=1), NT_DIMS,
                                         preferred_element_type=F32)
                dc_acc[:, r0:kend] += rowsum[0:1, :]
            dk_ref[:, sl] = dk_acc[...].astype(BF16)
            dv_ref[:, sl] = dv_acc[...].astype(BF16)
            dc_ref[0, 0, hh:hh + 1, :] = dc_acc[...]

    blk = lambda cb: pl.BlockSpec((seq, LANE), lambda b, j: (b, cb * nw + j))
    own = pl.BlockSpec((seq, LANE), lambda b, j: (b, j))
    n = bsz * seq
    return pl.pallas_call(
        body, name="fox_attn_bwd", grid=(bsz, nw),
        in_specs=[blk(0), blk(1), blk(2), blk(3), pl.BlockSpec((seq, LANE), lambda b, j: (b, 0)),
                  pl.BlockSpec((1, 1, 8, seq), lambda b, j: (b, j, 0, 0)), own, own],
        out_specs=[own] * 4 + [pl.BlockSpec((1, 1, 8, seq), lambda b, j: (b, j, 0, 0))],
        out_shape=[jax.ShapeDtypeStruct((n, nw * LANE), BF16)] * 4 + [jax.ShapeDtypeStruct((bsz, nw, 8, seq), F32)],
        scratch_shapes=[pltpu.VMEM((seq, HEAD), F32), pltpu.VMEM((seq, HEAD), F32), pltpu.VMEM((1, seq), F32)],
        compiler_params=_cparams(("parallel", "parallel")),
    )(proj, proj, proj, proj, c, crow, oa, dycat)


def _shift_mix(x, mu):
    prev = jnp.where(_iota(x.shape, 0) == 0, 0.0, pltpu.roll(x, 1, 0))
    return x + (prev - x) * mu, prev


def _rwkv_pre_math(r_in, k_in, v_in, wa_in, mu_r, mu_k, mu_v, mu_wa, w0, w2, a0, a2, k_k, k_a, ones):
    r, prev_r = _shift_mix(r_in, mu_r)
    k, prev_k = _shift_mix(k_in, mu_k)
    v, prev_v = _shift_mix(v_in, mu_v)
    wa, prev_wa = _shift_mix(wa_in, mu_wa)
    tw = jnp.tanh(wa[:, :LORA])
    alo = wa[:, LORA:]
    sg = _sigmoid(w0 + jnp.dot(tw, w2, precision=HI, preferred_element_type=F32))
    e = sg * 0.6065306597126334
    w = jnp.exp(-e)
    a = _sigmoid(a0 + jnp.dot(alo, a2, precision=HI, preferred_element_type=F32))
    kkraw = k * k_k
    nrm = jnp.sqrt(_hsum(kkraw * kkraw, ones))
    den = jnp.maximum(nrm, 1e-12)
    kk = kkraw / den
    k2 = k * (1.0 + (a - 1.0) * k_a)
    return dict(r=r, k=k, v=v, prev_r=prev_r, prev_k=prev_k, prev_v=prev_v, prev_wa=prev_wa, tw=tw, alo=alo,
                sg=sg, e=e, w=w, a=a, nrm=nrm, den=den, kk=kk, k2=k2)


def _rwkv_pre_specs(seq, nw):
    tok = lambda cb: pl.BlockSpec((seq, LANE), lambda b, j: (b, cb * nw + j))
    par = pl.BlockSpec((1, LANE), lambda b, j: (0, j))
    lora = pl.BlockSpec((LORA, LANE), lambda b, j: (0, j))
    return [tok(4), tok(5), tok(6), pl.BlockSpec((seq, LANE), lambda b, j: (b, 8 * nw)),
            par, par, par, pl.BlockSpec((1, LANE), lambda b, j: (0, 0)), par, lora, par, lora, par, par]


def rwkv_pre_fwd(proj, prm, bsz, seq, nw):
    def body(r_ref, k_ref, v_ref, wa_ref, mur, muk, muv, muwa, w0, w2, a0, a2, kk_, ka_,
             ro, wo, ko, vo, kko, kao):
        m = _rwkv_pre_math(r_ref[...], k_ref[...], v_ref[...], wa_ref[...], mur[...], muk[...], muv[...], muwa[...],
                           w0[...], w2[...], a0[...], a2[...], kk_[...], ka_[...], _head_ones())
        ro[...] = m["r"]
        wo[...] = m["w"]
        ko[...] = m["k2"]
        vo[...] = m["v"]
        kko[...] = m["kk"]
        kao[...] = m["kk"] * m["a"]

    n = bsz * seq
    own = pl.BlockSpec((seq, LANE), lambda b, j: (b, j))
    return pl.pallas_call(
        body, name="rwkv_pre_fwd", grid=(bsz, nw), in_specs=_rwkv_pre_specs(seq, nw), out_specs=[own] * 6,
        out_shape=[jax.ShapeDtypeStruct((n, nw * LANE), F32)] * 6, compiler_params=_cparams(("parallel", "parallel")),
    )(proj, proj, proj, proj, prm["mu_r"], prm["mu_k"], prm["mu_v"], prm["mu_wa"], prm["w0"], prm["w2"],
      prm["a0"], prm["a2"], prm["k_k"], prm["k_a"])


def rwkv_pre_bwd(proj, prm, grads_scan, grads_bonus, bsz, seq, nw):
    wid = nw * LANE

    def body(r_ref, k_ref, v_ref, wa_ref, mur, muk, muv, muwa, w0, w2, a0, a2, kk_, ka_,
             dr_s, dw_s, dk_s, dv_s, dkk_s, dka_s, dr_b, dk_b, dv_b,
             dxr, dxk, dxv, dxwa, dmur, dmuk, dmuv, dmuwa, dw0, dw2, da0, da2, dkk_p, dka_p, dwa_acc):
        b, j = pl.program_id(0), pl.program_id(1)
        cols = pl.ds(pl.multiple_of(j * LANE, LANE), LANE)
        acc_refs = (dmur, dmuk, dmuv, dmuwa, dw0, dw2, da0, da2, dkk_p, dka_p)

        @pl.when(jnp.logical_and(b == 0, j == 0))
        def _():
            for ref in acc_refs:
                ref[...] = jnp.zeros_like(ref)

        @pl.when(j == 0)
        def _():
            dwa_acc[...] = jnp.zeros_like(dwa_acc)

        ones = _head_ones()
        r_in, k_in, v_in, wa_in = r_ref[...], k_ref[...], v_ref[...], wa_ref[...]
        m = _rwkv_pre_math(r_in, k_in, v_in, wa_in, mur[...], muk[...], muv[...], muwa[...],
                           w0[...], w2[...], a0[...], a2[...], kk_[...], ka_[...], ones)
        k, a, kk = m["k"], m["a"], m["kk"]
        dr = dr_s[...] + dr_b[...]
        dk2 = dk_s[...] + dk_b[...]
        dv = dv_s[...] + dv_b[...]
        dka = dka_s[...]
        da = dka * kk + dk2 * k * ka_[...]
        dkk = dkk_s[...] + dka * a
        dkkraw = jnp.where(m["nrm"] > 1e-12, dkk - kk * _hsum(dkk * kk, ones), dkk) / m["den"]
        dk = dkkraw * kk_[...] + dk2 * (1.0 + (a - 1.0) * ka_[...])
        dkk_p[:, cols] += _colsum(dkkraw * k)
        dka_p[:, cols] += _colsum(dk2 * k * (a - 1.0))
        dza = da * a * (1.0 - a)
        da0[:, cols] += _colsum(dza)
        da2[:, cols] += lax.dot_general(m["alo"], dza, TN_DIMS, precision=HI, preferred_element_type=F32)
        dalo = lax.dot_general(dza, a2[...], NT_DIMS, precision=HI, preferred_element_type=F32)
        dzw = -dw_s[...] * m["w"] * m["e"] * (1.0 - m["sg"])
        dw0[:, cols] += _colsum(dzw)
        dw2[:, cols] += lax.dot_general(m["tw"], dzw, TN_DIMS, precision=HI, preferred_element_type=F32)
        dtw = lax.dot_general(dzw, w2[...], NT_DIMS, precision=HI, preferred_element_type=F32)
        dwa_acc[:, 0:LORA] += dtw * (1.0 - m["tw"] * m["tw"])
        dwa_acc[:, LORA:LANE] += dalo

        def shift_bwd(dxs, x, prev, mu):
            g = dxs * mu
            nxt = jnp.where(_iota(g.shape, 0) == seq - 1, 0.0, pltpu.roll(g, seq - 1, 0))
            return dxs * (1.0 - mu) + nxt, _colsum(dxs * (prev - x))

        for dxs, x, prev, mu, out, dmu in ((dr, r_in, m["prev_r"], mur, dxr, dmur), (dk, k_in, m["prev_k"], muk, dxk, dmuk),
                                           (dv, v_in, m["prev_v"], muv, dxv, dmuv)):
            dx, dm = shift_bwd(dxs, x, prev, mu[...])
            out[...] = dx.astype(BF16)
            dmu[:, cols] += dm

        @pl.when(j == nw - 1)
        def _():
            dx, dm = shift_bwd(dwa_acc[...], wa_in, m["prev_wa"], muwa[...])
            dxwa[...] = dx.astype(BF16)
            dmuwa[...] += dm

    n = bsz * seq
    own = pl.BlockSpec((seq, LANE), lambda b, j: (b, j))
    whole = lambda shape: pl.BlockSpec(shape, lambda b, j: (0,) * len(shape))
    row = jax.ShapeDtypeStruct((1, wid), F32)
    return pl.pallas_call(
        body, name="rwkv_pre_bwd", grid=(bsz, nw), in_specs=_rwkv_pre_specs(seq, nw) + [own] * 9,
        out_specs=[own] * 3 + [pl.BlockSpec((seq, LANE), lambda b, j: (b, 0))]
        + [whole((1, wid))] * 3 + [whole((1, LANE)), whole((1, wid)), whole((LORA, wid)), whole((1, wid)),
                                   whole((LORA, wid)), whole((1, wid)), whole((1, wid))],
        out_shape=[jax.ShapeDtypeStruct((n, wid), BF16)] * 3 + [jax.ShapeDtypeStruct((n, LANE), BF16)]
        + [row] * 3 + [jax.ShapeDtypeStruct((1, LANE), F32), row, jax.ShapeDtypeStruct((LORA, wid), F32), row,
                       jax.ShapeDtypeStruct((LORA, wid), F32), row, row],
        scratch_shapes=[pltpu.VMEM((seq, LANE), F32)],
        compiler_params=_cparams(("arbitrary", "arbitrary")),
    )(proj, proj, proj, proj, prm["mu_r"], prm["mu_k"], prm["mu_v"], prm["mu_wa"], prm["w0"], prm["w2"],
      prm["a0"], prm["a2"], prm["k_k"], prm["k_a"], *grads_scan, *grads_bonus)


SCAN_CHUNK = 64


def _scan_group(nw):
    return _tile(nw, 8, 1)


ROWS = 8


def _scan_consts():
    ones = _head_ones().astype(BF16)
    return _head_eye(), ones, jnp.concatenate([ones, ones], axis=0)


HALVES = (tuple(range(0, ROWS // 2)), tuple(range(ROWS // 2, ROWS)))


def _cols8(tile, eye, ones, ones2, two_parts, qs):
    hi = tile.astype(BF16).astype(F32)
    place = lambda part, q: (part[q:q + 1, :] * eye).astype(BF16)
    if two_parts:
        lo = tile - hi
        lhs = jnp.concatenate([jnp.concatenate([place(hi, q), place(lo, q)], axis=1) for q in qs], axis=0)
        big = jnp.dot(lhs, ones2, preferred_element_type=F32)
    else:
        big = jnp.dot(jnp.concatenate([place(hi, q) for q in qs], axis=0), ones, preferred_element_type=F32)
    return {q: big[n * HEAD:(n + 1) * HEAD, :] for n, q in enumerate(qs)}


def _rows8(mats, eye, ones):
    big = jnp.dot(jnp.concatenate([m.astype(BF16) for m in mats], axis=0), ones, preferred_element_type=F32)
    return _stack_rows([_colsum(big[q * HEAD:(q + 1) * HEAD, :] * eye) for q in range(ROWS)])


def _stack_rows(rows):
    rid = _iota((ROWS, LANE), 0)
    out = jnp.zeros((ROWS, LANE), F32)
    for q in range(ROWS):
        out = jnp.where(rid == q, rows[q], out)
    return out


def _grid_ends(grid):
    ids = [pl.program_id(i) for i in range(len(grid))]
    first = functools.reduce(jnp.logical_and, [i == 0 for i in ids])
    last = functools.reduce(jnp.logical_and, [i == g - 1 for i, g in zip(ids, grid)])
    return first, last


def rwkv_scan_fwd(r, w, k, v, kk, ka, bsz, seq, nw, gather):
    grp = _scan_group(nw)
    tc = _tile(seq, SCAN_CHUNK, 8)
    nt = seq // tc
    grid = (bsz, nw // grp, nt)
    ng = len(gather)

    def body(*refs):
        r_ref, w_ref, k_ref, v_ref, kk_ref, ka_ref = refs[:6]
        y_ref, st_ref = refs[6 + ng:8 + ng]
        s_ref = refs[8 + 2 * ng]
        start, wait = _direct_exchange(refs[6:6 + ng], refs[8 + ng:8 + 2 * ng], *refs[9 + 2 * ng:], scatter=False)
        first, last = _grid_ends(grid)
        pl.when(first)(start)

        @pl.when(pl.program_id(2) == 0)
        def _():
            s_ref[...] = jnp.zeros_like(s_ref)

        eye, ones, ones2 = _scan_consts()

        def step(i8, carry):
            base = pl.multiple_of(i8 * ROWS, ROWS)
            for g in range(grp):
                cs = slice(g * LANE, (g + 1) * LANE)
                tile = lambda ref: ref[pl.ds(base, ROWS), cs]
                rt, wt, kt, kkt, kat, vt = tile(r_ref), tile(w_ref), tile(k_ref), tile(kk_ref), tile(ka_ref), tile(v_ref)
                s = s_ref[g]
                ys = []
                for half in HALVES:
                    cols = lambda t, two: _cols8(t, eye, ones, ones2, two, half)
                    rm, wm, km, kkm, kam = cols(rt, False), cols(wt, True), cols(kt, False), cols(kkt, False), cols(kat, False)
                    for q in half:
                        st_ref[base + q, g] = s
                        sa = _colsum(s * kkm[q])
                        s = s * wm[q] - kam[q] * sa + km[q] * vt[q:q + 1, :]
                        ys.append(_colsum(s * rm[q]))
                y_ref[pl.ds(base, ROWS), cs] = _stack_rows(ys)
                s_ref[g] = s
            return carry

        lax.fori_loop(0, tc // ROWS, step, 0)
        pl.when(last)(wait)

    tok = pl.BlockSpec((tc, grp * LANE), lambda b, g, t: (b * nt + t, g))
    n = bsz * seq
    out = pl.pallas_call(
        body, name="rwkv_scan_fwd", grid=grid, in_specs=[tok] * 6 + [ANY] * ng,
        out_specs=[tok, pl.BlockSpec((tc, grp, HEAD, LANE), lambda b, g, t: (b * nt + t, g, 0, 0))] + [ANY] * ng,
        out_shape=[jax.ShapeDtypeStruct((n, nw * LANE), F32), jax.ShapeDtypeStruct((n, nw, HEAD, LANE), F32)]
        + [jax.ShapeDtypeStruct((N_DEV,) + a.shape, a.dtype) for a in gather],
        scratch_shapes=[pltpu.VMEM((grp, HEAD, LANE), F32)] + _exchange_sems(ng),
        compiler_params=_cparams(("arbitrary", "arbitrary", "arbitrary")),
    )(r, w, k, v, kk, ka, *gather)
    return out[0], out[1], out[2:]


def rwkv_scan_bwd(r, w, k, v, kk, ka, states, dy, bsz, seq, nw, scatter):
    grp = _scan_group(nw)
    tc = _tile(seq, SCAN_CHUNK, 8)
    nt = seq // tc
    nblk = tc // ROWS
    grid = (bsz, nw // grp, nt)
    ns = len(scatter)

    def body(*refs):
        r_ref, w_ref, k_ref, v_ref, kk_ref, ka_ref, st_ref, dy_ref = refs[:8]
        dr_ref, dw_ref, dk_ref, dv_ref, dkk_ref, dka_ref = refs[8 + ns:14 + ns]
        ds_ref = refs[14 + 2 * ns]
        start, wait = _direct_exchange(refs[8:8 + ns], refs[14 + ns:14 + 2 * ns], *refs[15 + 2 * ns:], scatter=True)
        first, last = _grid_ends(grid)
        pl.when(first)(start)

        @pl.when(pl.program_id(2) == 0)
        def _():
            ds_ref[...] = jnp.zeros_like(ds_ref)

        eye, ones, ones2 = _scan_consts()
        ones_f = _head_ones()

        def step(ii, carry):
            base = pl.multiple_of((nblk - 1 - ii) * ROWS, ROWS)
            rows = pl.ds(base, ROWS)
            for g in range(grp):
                cs = slice(g * LANE, (g + 1) * LANE)
                rt, wt, kt, kkt, kat = r_ref[rows, cs], w_ref[rows, cs], k_ref[rows, cs], kk_ref[rows, cs], ka_ref[rows, cs]
                vt, dyt = v_ref[rows, cs], dy_ref[rows, cs]
                ds = ds_ref[g]
                dvs, sas, p_dr, p_dk, p_dka, p_dw, p_dkk = ([None] * ROWS for _ in range(7))
                for half in reversed(HALVES):
                    cols = lambda t, two: _cols8(t, eye, ones, ones2, two, half)
                    rm, wm, km, kkm, kam = cols(rt, False), cols(wt, True), cols(kt, False), cols(kkt, False), cols(kat, False)
                    for q in reversed(half):
                        sp = st_ref[base + q, g]
                        vrow, dyrow = vt[q:q + 1, :], dyt[q:q + 1, :]
                        sa = _colsum(sp * kkm[q])
                        sas[q] = sa
                        ds = ds + rm[q] * dyrow
                        dvs[q] = _colsum(ds * km[q])
                        dsa = -_colsum(ds * kam[q])
                        p_dr[q] = sp * dyrow
                        p_dk[q] = ds * vrow
                        p_dka[q] = -(ds * sa)
                        p_dw[q] = ds * sp
                        p_dkk[q] = sp * dsa
                        ds = ds * wm[q] + kkm[q] * dsa
                ds_ref[g] = ds
                dv_ref[rows, cs] = _stack_rows(dvs)
                sa_dy = _hsum(_stack_rows(sas) * dyt, ones_f)
                v_dy = _hsum(vt * dyt, ones_f)
                dr_ref[rows, cs] = w_ref[rows, cs] * _rows8(p_dr, eye, ones) - ka_ref[rows, cs] * sa_dy + k_ref[rows, cs] * v_dy
                dk_ref[rows, cs] = _rows8(p_dk, eye, ones)
                dka_ref[rows, cs] = _rows8(p_dka, eye, ones)
                dw_ref[rows, cs] = _rows8(p_dw, eye, ones)
                dkk_ref[rows, cs] = _rows8(p_dkk, eye, ones)
            return carry

        lax.fori_loop(0, nblk, step, 0)
        pl.when(last)(wait)

    tok = pl.BlockSpec((tc, grp * LANE), lambda b, g, t: (b * nt + nt - 1 - t, g))
    n = bsz * seq
    out = pl.pallas_call(
        body, name="rwkv_scan_bwd", grid=grid,
        in_specs=[tok] * 6 + [pl.BlockSpec((tc, grp, HEAD, LANE), lambda b, g, t: (b * nt + nt - 1 - t, g, 0, 0)), tok] + [ANY] * ns,
        out_specs=[tok] * 6 + [ANY] * ns,
        out_shape=[jax.ShapeDtypeStruct((n, nw * LANE), F32)] * 6 + [jax.ShapeDtypeStruct(a.shape, a.dtype) for a in scatter],
        scratch_shapes=[pltpu.VMEM((grp, HEAD, LANE), F32)] + _exchange_sems(ns),
        compiler_params=_cparams(("arbitrary", "arbitrary", "arbitrary")),
    )(r, w, k, v, kk, ka, states, dy, *scatter)
    return out[:6], out[6:]


def _rwkv_post_math(y, r, k2, v, ln_g, ln_b, r_k, ones):
    d = y - _hsum(y, ones) * (1.0 / HEAD)
    rstd = lax.rsqrt(_hsum(d * d, ones) * (1.0 / HEAD) + GN_EPS)
    yn = d * rstd
    s = _hsum(r * k2 * r_k, ones)
    return yn, rstd, s, yn * ln_g + ln_b + s * v


def _rwkv_post_specs(tm, nw):
    tok = pl.BlockSpec((tm, LANE), lambda i, j: (i, j))
    par = pl.BlockSpec((1, LANE), lambda i, j: (0, j))
    return [tok] * 4 + [pl.BlockSpec((tm, LANE), lambda i, j: (i, 7 * nw + j)), par, par, par]


def rwkv_post_fwd(y, r, k2, v, proj, prm, nw):
    n = y.shape[0]
    tm = _tile(n, 512, 8)

    def body(y_ref, r_ref, k_ref, v_ref, g_ref, lg, lb, rk, o_ref):
        ob = _rwkv_post_math(y_ref[...], r_ref[...], k_ref[...], v_ref[...], lg[...], lb[...], rk[...], _head_ones())[3]
        o_ref[...] = (ob * _silu_and_grad(g_ref[...])[0]).astype(BF16)

    return pl.pallas_call(
        body, name="rwkv_post_fwd", grid=(n // tm, nw), in_specs=_rwkv_post_specs(tm, nw),
        out_specs=pl.BlockSpec((tm, LANE), lambda i, j: (i, j)), out_shape=jax.ShapeDtypeStruct((n, nw * LANE), BF16),
        compiler_params=_cparams(("parallel", "parallel")),
    )(y, r, k2, v, proj, prm["ln_g"], prm["ln_b"], prm["r_k"])


def rwkv_post_bwd(y, r, k2, v, proj, prm, dycat, nw):
    n = y.shape[0]
    tm = _tile(n, 512, 8)
    wid = nw * LANE

    def body(y_ref, r_ref, k_ref, v_ref, g_ref, lg, lb, rk, dyc_ref,
             dy_ref, dr_ref, dk_ref, dv_ref, dg_ref, dlg, dlb, drk):
        i, j = pl.program_id(0), pl.program_id(1)
        cols = pl.ds(pl.multiple_of(j * LANE, LANE), LANE)

        @pl.when(jnp.logical_and(i == 0, j == 0))
        def _():
            for ref in (dlg, dlb, drk):
                ref[...] = jnp.zeros_like(ref)

        ones = _head_ones()
        rr, kr, vr = r_ref[...], k_ref[...], v_ref[...]
        yn, rstd, s, ob = _rwkv_post_math(y_ref[...], rr, kr, vr, lg[...], lb[...], rk[...], ones)
        silu, dsilu = _silu_and_grad(g_ref[...])
        dyc = dyc_ref[...].astype(F32)
        dg_ref[...] = (dyc * ob * dsilu).astype(BF16)
        dob = dyc * silu
        dlg[:, cols] += _colsum(dob * yn)
        dlb[:, cols] += _colsum(dob)
        dyn = dob * lg[...]
        dy_ref[...] = rstd * (dyn - _hsum(dyn, ones) * (1.0 / HEAD) - yn * _hsum(dyn * yn, ones) * (1.0 / HEAD))
        dv_ref[...] = dob * s
        dsum = _hsum(dob * vr, ones)
        dr_ref[...] = dsum * kr * rk[...]
        dk_ref[...] = dsum * rr * rk[...]
        drk[:, cols] += _colsum(dsum * rr * kr)

    tok = pl.BlockSpec((tm, LANE), lambda i, j: (i, j))
    whole = pl.BlockSpec((1, wid), lambda i, j: (0, 0))
    return pl.pallas_call(
        body, name="rwkv_post_bwd", grid=(n // tm, nw),
        in_specs=_rwkv_post_specs(tm, nw) + [pl.BlockSpec((tm, LANE), lambda i, j: (i, nw + j))],
        out_specs=[tok] * 5 + [whole] * 3,
        out_shape=[jax.ShapeDtypeStruct((n, wid), F32)] * 4 + [jax.ShapeDtypeStruct((n, wid), BF16)]
        + [jax.ShapeDtypeStruct((1, wid), F32)] * 3,
        compiler_params=_cparams(("arbitrary", "arbitrary")),
    )(y, r, k2, v, proj, prm["ln_g"], prm["ln_b"], prm["r_k"], dycat)


def _sgu_math(blk, ln_g, ln_b, ws_ref, bb_ref, mixed_ref, d, ngr):
    u, v, g = blk[:, 0:d], blk[:, d:2 * d], blk[:, 2 * d:3 * d]
    gu, dgu = _gelu_and_grad(u)
    gv, dgv = _gelu_and_grad(v)
    cen = gv - jnp.mean(gv, axis=-1, keepdims=True)
    rstd = lax.rsqrt(jnp.mean(cen * cen, axis=-1, keepdims=True) + LN_EPS)
    vhat = cen * rstd
    vn = vhat * ln_g + ln_b
    tril = _iota((GMLP_CHUNK, GMLP_CHUNK), 0) >= _iota((GMLP_CHUNK, GMLP_CHUNK), 1)
    for gi in range(ngr):
        cs = slice(gi * LANE, (gi + 1) * LANE)
        wm = jnp.where(tril, ws_ref[gi], 0.0)
        mixed_ref[:, cs] = jnp.dot(wm, vn[:, cs], precision=HI3, preferred_element_type=F32) + bb_ref[gi]
    return dict(g=g, gu=gu, dgu=dgu, dgv=dgv, rstd=rstd, vhat=vhat, vn=vn, tril=tril)


def sgu_fwd(proj, ln_g, ln_b, w_s, b_bc):
    n, d3 = proj.shape
    d = d3 // 3
    ngr = d // LANE

    def body(p_ref, lg, lb, ws_ref, bb_ref, o_ref, mixed_ref):
        m = _sgu_math(p_ref[...], lg[...], lb[...], ws_ref, bb_ref, mixed_ref, d, ngr)
        o_ref[...] = (m["gu"] * mixed_ref[...] * _silu_and_grad(m["g"])[0]).astype(BF16)

    return pl.pallas_call(
        body, name="sgu_fwd", grid=(n // GMLP_CHUNK,),
        in_specs=[_row_spec(GMLP_CHUNK, d3), _full_spec((1, d)), _full_spec((1, d)),
                  _full_spec((ngr, GMLP_CHUNK, GMLP_CHUNK)), _full_spec((ngr, GMLP_CHUNK, LANE))],
        out_specs=_row_spec(GMLP_CHUNK, d), out_shape=jax.ShapeDtypeStruct((n, d), BF16),
        scratch_shapes=[pltpu.VMEM((GMLP_CHUNK, d), F32)], compiler_params=_cparams(("parallel",)),
    )(proj, ln_g, ln_b, w_s, b_bc)


def sgu_bwd(proj, ln_g, ln_b, w_s, b_bc, dyin):
    n, d3 = proj.shape
    d = d3 // 3
    ngr = d // LANE
    nsteps = n // GMLP_CHUNK

    def body(p_ref, lg, lb, ws_ref, bb_ref, dy_ref, dp_ref, dws_ref, dbs_ref, dlg_ref, dlb_ref,
             mixed_ref, dvn_ref, dbacc_ref):
        step = pl.program_id(0)

        @pl.when(step == 0)
        def _():
            for ref in (dws_ref, dlg_ref, dlb_ref, dbacc_ref):
                ref[...] = jnp.zeros_like(ref)

        m = _sgu_math(p_ref[...], lg[...], lb[...], ws_ref, bb_ref, mixed_ref, d, ngr)
        silu, dsilu = _silu_and_grad(m["g"])
        dyv = dy_ref[...].astype(F32)
        mixed = mixed_ref[...]
        dp_ref[:, 2 * d:3 * d] = (dyv * m["gu"] * mixed * dsilu).astype(BF16)
        doc = dyv * silu
        dp_ref[:, 0:d] = (doc * mixed * m["dgu"]).astype(BF16)
        dmixed = doc * m["gu"]
        for gi in range(ngr):
            cs = slice(gi * LANE, (gi + 1) * LANE)
            dm = dmixed[:, cs]
            wm = jnp.where(m["tril"], ws_ref[gi], 0.0)
            dws_ref[gi] += jnp.where(m["tril"], lax.dot_general(dm, m["vn"][:, cs], NT_DIMS, precision=HI3,
                                                                preferred_element_type=F32), 0.0)
            dbacc_ref[gi] += dm
            dvn_ref[:, cs] = lax.dot_general(wm, dm, TN_DIMS, precision=HI3, preferred_element_type=F32)
        dvn = dvn_ref[...]
        dlg_ref[...] += _colsum(dvn * m["vhat"])
        dlb_ref[...] += _colsum(dvn)
        dvh = dvn * lg[...]
        dgv = m["rstd"] * (dvh - jnp.mean(dvh, axis=-1, keepdims=True)
                           - m["vhat"] * jnp.mean(dvh * m["vhat"], axis=-1, keepdims=True))
        dp_ref[:, d:2 * d] = (dgv * m["dgv"]).astype(BF16)

        @pl.when(step == nsteps - 1)
        def _():
            dbs_ref[...] = jnp.sum(dbacc_ref[...], axis=-1, keepdims=True)

    return pl.pallas_call(
        body, name="sgu_bwd", grid=(nsteps,),
        in_specs=[_row_spec(GMLP_CHUNK, d3), _full_spec((1, d)), _full_spec((1, d)),
                  _full_spec((ngr, GMLP_CHUNK, GMLP_CHUNK)), _full_spec((ngr, GMLP_CHUNK, LANE)), _row_spec(GMLP_CHUNK, d)],
        out_specs=[_row_spec(GMLP_CHUNK, d3), _full_spec((ngr, GMLP_CHUNK, GMLP_CHUNK)), _full_spec((ngr, GMLP_CHUNK, 1)),
                   _full_spec((1, d)), _full_spec((1, d))],
        out_shape=[jax.ShapeDtypeStruct((n, d3), BF16), jax.ShapeDtypeStruct((ngr, GMLP_CHUNK, GMLP_CHUNK), F32),
                   jax.ShapeDtypeStruct((ngr, GMLP_CHUNK, 1), F32), jax.ShapeDtypeStruct((1, d), F32),
                   jax.ShapeDtypeStruct((1, d), F32)],
        scratch_shapes=[pltpu.VMEM((GMLP_CHUNK, d), F32), pltpu.VMEM((GMLP_CHUNK, d), F32),
                        pltpu.VMEM((ngr, GMLP_CHUNK, LANE), F32)],
        compiler_params=_cparams(("arbitrary",)),
    )(proj, ln_g, ln_b, w_s, b_bc, dyin)


def _ab_segments(wid, heads):
    return ((0, 3 * wid, 0),
            (3 * wid, 3 * wid + heads, 8 * wid + LANE),
            (3 * wid + heads, 4 * wid + heads, 3 * wid),
            (4 * wid + heads, 7 * wid + heads, 4 * wid),
            (7 * wid + heads, 7 * wid + heads + 2 * LORA, 8 * wid),
            (7 * wid + heads + 2 * LORA, 8 * wid + heads + 2 * LORA, 7 * wid))


def _ab_reorder(shards, wid, heads):
    per = shards[0].shape[1]
    pieces = []
    for o0, o1, _ in sorted(_ab_segments(wid, heads), key=lambda s: s[2]):
        for d, sh in enumerate(shards):
            lo, hi = max(o0, d * per), min(o1, (d + 1) * per)
            if lo < hi:
                pieces.append(sh[:, lo - d * per:hi - d * per])
    pieces.append(jnp.zeros((shards[0].shape[0], LANE - heads), shards[0].dtype))
    return jnp.concatenate(pieces, axis=1)


def _ab_restore(g, wid, heads, n_shards):
    per = (8 * wid + heads + 2 * LORA) // n_shards
    out = []
    for d in range(n_shards):
        pieces = []
        for o0, o1, r0 in _ab_segments(wid, heads):
            lo, hi = max(o0, d * per), min(o1, (d + 1) * per)
            if lo < hi:
                pieces.append(g[:, r0 + lo - o0:r0 + hi - o0])
        out.append(jnp.concatenate(pieces, axis=1))
    return out


ODD_PIECES = ("c_w_in", "c_w_out", "ple_w_gate1", "ple_w_proj1", "ab_w_out", "ple_w_gate0", "ple_w_proj0")
ODD_BY_ROWS = (0, 1, 1, 0, 1, 1, 0)


def local_step(x, p, target, wts, n_shards, odd_shards):
    bsz, seq, d = x.shape
    n = bsz * seq
    wid = d // 2
    nw = wid // LANE
    heads = wid // HEAD
    x2 = x.reshape(n, d)
    tgt = target.reshape(n, d)
    p0, p1 = p[0].reshape(n, -1), p[1].reshape(n, -1)
    row = lambda a: a.reshape(1, -1)
    g_pre0, g_pre1 = row(wts["norm_pre"][0]), row(wts["norm_pre"][1])
    g_post0, g_post1 = row(wts["norm_post"][0]), row(wts["norm_post"][1])
    w_ab = wts["ab_w_in_r"]
    rows_cut = lambda a: a.reshape((n_shards, a.shape[0] // n_shards) + a.shape[1:])
    fbias = jnp.pad(row(wts["fox_f_bias"]), ((0, 0), (0, LANE - heads)))
    mu = row(wts["rwkv_mu"])
    prm = dict(mu_r=mu[:, 0:wid], mu_k=mu[:, wid:2 * wid], mu_v=mu[:, 2 * wid:3 * wid], mu_wa=mu[:, 3 * wid:],
               w0=row(wts["rwkv_w0"]), w2=wts["rwkv_w2"], a0=row(wts["rwkv_a0"]), a2=wts["rwkv_a2"],
               k_k=row(wts["rwkv_k_k"]), k_a=row(wts["rwkv_k_a"]), r_k=row(wts["rwkv_r_k"]),
               ln_g=row(wts["rwkv_ln_g"]), ln_b=row(wts["rwkv_ln_b"]))
    c_ln_g, c_ln_b = row(wts["c_ln_g"]), row(wts["c_ln_b"])
    w_s = wts["c_w_s"]
    b_bc = jnp.broadcast_to(wts["c_b_s"][:, :, None], w_s.shape[:2] + (LANE,))

    xn0 = rms_fwd(x2, g_pre0, name="rms_pre0")
    proj0 = matmul(xn0, w_ab, name="ab_in")
    c = fox_gate_fwd(proj0, fbias, bsz, seq, nw)
    crow = jnp.pad(c.reshape(bsz, seq, LANE)[:, :, :heads].transpose(0, 2, 1).reshape(bsz, nw, 2, seq),
                   ((0, 0), (0, 0), (0, 6), (0, 0)))
    oa, ya = fox_attn_fwd(proj0, c, crow, bsz, seq, nw)
    sr, sw, sk, sv, skk, ska = rwkv_pre_fwd(proj0, prm, bsz, seq, nw)
    ysc, states, odd_w = rwkv_scan_fwd(sr, sw, sk, sv, skk, ska, bsz, seq, nw, [odd_shards[piece] for piece in ODD_PIECES])
    w_c, w_cout, wg1, wp1, w_out, wg0, wp0 = (_join_shards(a, 1 - by_rows) for a, by_rows in zip(odd_w, ODD_BY_ROWS))
    yb = rwkv_post_fwd(ysc, sr, sk, sv, proj0, prm, nw)
    ycat = jnp.concatenate([ya, yb], axis=1)
    y0 = matmul(ycat, w_out, name="ab_out")
    h1, h1b = rms_res_fwd(y0, x2, g_post0, name="rms_post0")
    a0 = matmul(p0, wp0, name="ple_proj0")
    z0 = matmul(h1b, wg0, name="ple_gate0")
    h1p = ple_fwd(h1, a0, z0, name="ple_fwd0")
    xn1 = rms_fwd(h1p, g_pre1, name="rms_pre1")
    proj1 = matmul(xn1, w_c, name="c_in")
    yin = sgu_fwd(proj1, c_ln_g, c_ln_b, w_s, b_bc)
    y1 = matmul(yin, w_cout, name="c_out")
    h2, h2b = rms_res_fwd(y1, h1p, g_post1, name="rms_post1")
    a1 = matmul(p1, wp1, name="ple_proj1")
    z1 = matmul(h2b, wg1, name="ple_gate1")
    h2p = ple_fwd(h2, a1, z1, name="ple_fwd1")
    dh, loss = loss_head(h2p, tgt, name="loss_head")

    g = {}
    da1, dz1 = ple_bwd(dh, a1, z1, name="ple_bwd1")
    g["ple_w_proj1"] = matmul(p1, da1, ta=True, out_dtype=BF16, col_blocks=n_shards, name="d_ple_proj1")
    g["ple_w_gate1"] = rows_cut(matmul(h2b, dz1, ta=True, out_dtype=BF16, name="d_ple_gate1"))
    dh2 = matmul(dz1, wg1, tb=True, add=dh, name="dx_ple_gate1")
    dy1, g_post1_g = rms_bwd(y1, dh2, g_post1, out_dtype=BF16, name="rms_post1_bwd")
    g["c_w_out"] = rows_cut(matmul(yin, dy1, ta=True, out_dtype=BF16, name="d_c_out"))
    dyin = matmul(dy1, w_cout, tb=True, out_dtype=BF16, name="dx_c_out")
    dproj1, g["c_w_s"], g_bs, g["c_ln_g"], g["c_ln_b"] = sgu_bwd(proj1, c_ln_g, c_ln_b, w_s, b_bc, dyin)
    g["c_b_s"] = g_bs.reshape(w_s.shape[:2])
    g["c_w_in"] = matmul(xn1, dproj1, ta=True, out_dtype=BF16, col_blocks=n_shards, name="d_c_in")
    dxn1 = matmul(dproj1, w_c, tb=True, out_dtype=BF16, name="dx_c_in")
    dh1p, g_pre1_g = rms_bwd(h1p, dxn1, g_pre1, res=dh2, name="rms_pre1_bwd")
    da0, dz0 = ple_bwd(dh1p, a0, z0, name="ple_bwd0")
    g["ple_w_proj0"] = matmul(p0, da0, ta=True, out_dtype=BF16, col_blocks=n_shards, name="d_ple_proj0")
    g["ple_w_gate0"] = rows_cut(matmul(h1b, dz0, ta=True, out_dtype=BF16, name="d_ple_gate0"))
    dh1 = matmul(dz0, wg0, tb=True, add=dh1p, name="dx_ple_gate0")
    dy0, g_post0_g = rms_bwd(y0, dh1, g_post0, out_dtype=BF16, name="rms_post0_bwd")
    g["ab_w_out"] = rows_cut(matmul(ycat, dy0, ta=True, out_dtype=BF16, name="d_ab_out"))
    dycat = matmul(dy0, w_out, tb=True, out_dtype=BF16, name="dx_ab_out")
    dysc, dr_b, dk_b, dv_b, dgb, g["rwkv_ln_g"], g["rwkv_ln_b"], g_rk = rwkv_post_bwd(ysc, sr, sk, sv, proj0, prm, dycat, nw)
    g["rwkv_r_k"] = g_rk.reshape(wts["rwkv_r_k"].shape)
    grads_scan, odd_landed = rwkv_scan_bwd(sr, sw, sk, sv, skk, ska, states, dysc, bsz, seq, nw, [g.pop(piece) for piece in ODD_PIECES])
    (dxr, dxk, dxv, dxwa, dmur, dmuk, dmuv, dmuwa, g["rwkv_w0"], g["rwkv_w2"], g["rwkv_a0"], g["rwkv_a2"],
     g["rwkv_k_k"], g["rwkv_k_a"]) = rwkv_pre_bwd(proj0, prm, grads_scan, (dr_b, dk_b, dv_b), bsz, seq, nw)
    g["rwkv_mu"] = jnp.concatenate([dmur, dmuk, dmuv, dmuwa], axis=1)
    dq, dk, dv, dga, dcrow = fox_attn_bwd(proj0, c, crow, oa, dycat, bsz, seq, nw)
    dc = jnp.pad(dcrow[:, :, :2, :].reshape(bsz, heads, seq).transpose(0, 2, 1), ((0, 0), (0, 0), (0, LANE - heads)))
    dfa, g_fb = fox_gate_bwd(proj0, fbias, dc.reshape(n, LANE), bsz, seq, nw)
    g["fox_f_bias"] = g_fb[:, :heads]
    dproj0 = jnp.concatenate([dq, dk, dv, dga, dxr, dxk, dxv, dgb, dxwa, dfa], axis=1)
    cut = 3 * d // 8
    shard_major = lambda full: jnp.stack(_ab_restore(full, wid, heads, n_shards))
    g_lo = shard_major(matmul(xn0[:, :cut], dproj0, ta=True, out_dtype=BF16, name="d_ab_in_lo"))
    g_hi, (lo_landed,) = matmul(xn0[:, cut:], dproj0, ta=True, out_dtype=BF16, scatter=[g_lo], name="d_ab_in_hi")
    dxn0, (hi_landed,) = matmul(dproj0, w_ab, tb=True, out_dtype=BF16, scatter=[shard_major(g_hi)], name="dx_ab_in")
    ab_landed = jnp.concatenate([lo_landed, hi_landed], axis=1)
    dx, g_pre0_g = rms_bwd(x2, dxn0, g_pre0, res=dh1, name="rms_pre0_bwd")

    g["norm_pre"] = jnp.concatenate([g_pre0_g, g_pre1_g], axis=0)
    g["norm_post"] = jnp.concatenate([g_post0_g, g_post1_g], axis=0)
    return loss, dx.reshape(bsz, seq, d), g, dict(zip(ODD_PIECES + ("ab_w_in",), tuple(odd_landed) + (ab_landed,)))


MESH = pl.DeviceIdType.MESH
ANY = pl.BlockSpec(memory_space=pl.ANY)
PACK_COLS = 1024
PACK_ROWS = 16


def _mesh_place():
    xi, yi, ci = lax.axis_index("x"), lax.axis_index("y"), lax.axis_index("c")
    return xi, yi, ci, 4 * xi + 2 * yi + ci


def _peer(xi, yi, ci, m):
    px = 1 - xi if m & 4 else xi
    py = 1 - yi if m & 2 else yi
    pc = 1 - ci if m & 1 else ci
    return (px, py, pc), 4 * px + 2 * py + pc


def _gather(arrays, *, name):
    n = len(arrays)

    def body(*refs):
        ins, outs = refs[:n], refs[n:2 * n]
        send_sems, recv_sems, local_sems = refs[2 * n:]
        xi, yi, ci, me = _mesh_place()
        sibling = (xi, yi, 1 - ci)
        chips = [(1 - xi, yi), (xi, 1 - yi), (1 - xi, 1 - yi)]
        block = lambda px, py, pc: 4 * px + 2 * py + pc

        def copy(a, k, blk, to, src=None):
            return pltpu.make_async_remote_copy(src_ref=outs[a].at[blk] if src is None else src, dst_ref=outs[a].at[blk],
                                                send_sem=send_sems.at[k, a], recv_sem=recv_sems.at[k, a],
                                                device_id=to, device_id_type=MESH)

        started = []
        for a in range(n):
            cp = pltpu.make_async_copy(ins[a], outs[a].at[me], local_sems.at[a])
            cp.start()
            started.append(cp)
        sends = []
        for a in range(n):
            sends.append(copy(a, 0, me, sibling, src=ins[a]))
            sends += [copy(a, 1 + j, me, (*chip, ci), src=ins[a]) for j, chip in enumerate(chips)]
        for cp in sends:
            cp.start()
        for j, chip in enumerate(chips):
            for a in range(n):
                copy(a, 1 + j, block(*chip, ci), (xi, yi, ci)).wait_recv()
                fwd = copy(a, 4 + j, block(*chip, ci), sibling)
                fwd.start()
                sends.append(fwd)
        for a in range(n):
            copy(a, 0, block(xi, yi, 1 - ci), (xi, yi, ci)).wait_recv()
            for j, chip in enumerate(chips):
                copy(a, 4 + j, block(*chip, 1 - ci), (xi, yi, ci)).wait_recv()
        for cp in sends:
            cp.wait_send()
        for cp in started:
            cp.wait()

    return pl.pallas_call(
        body, name=name, in_specs=[ANY] * n, out_specs=[ANY] * n,
        out_shape=[jax.ShapeDtypeStruct((N_DEV,) + a.shape, a.dtype) for a in arrays],
        scratch_shapes=[pltpu.SemaphoreType.DMA((N_DEV - 1, n)), pltpu.SemaphoreType.DMA((N_DEV - 1, n)),
                        pltpu.SemaphoreType.DMA((n,))],
    )(*arrays)


def _exchange_sems(n):
    if n == 0:
        return []
    return [pltpu.SemaphoreType.DMA((N_DEV - 1, n)), pltpu.SemaphoreType.DMA((N_DEV - 1, n)), pltpu.SemaphoreType.DMA((n,))]


def _direct_exchange(ins, outs, send_sems=None, recv_sems=None, local_sems=None, *, scatter):
    n = len(ins)

    def copies():
        xi, yi, ci, me = _mesh_place()
        src = lambda a, idx: ins[a].at[idx] if scatter else ins[a]
        local = [pltpu.make_async_copy(src(a, me), outs[a].at[me], local_sems.at[a]) for a in range(n)]
        sends, arrivals = [], []
        for m in range(1, N_DEV):
            peer, pidx = _peer(xi, yi, ci, m)
            for a in range(n):
                pair = dict(send_sem=send_sems.at[m - 1, a], recv_sem=recv_sems.at[m - 1, a], device_id=peer, device_id_type=MESH)
                sends.append(pltpu.make_async_remote_copy(src_ref=src(a, pidx), dst_ref=outs[a].at[me], **pair))
                arrivals.append(pltpu.make_async_remote_copy(src_ref=src(a, pidx), dst_ref=outs[a].at[pidx], **pair))
        return local, sends, arrivals

    def start():
        local, sends, _ = copies()
        for cp in local + sends:
            cp.start()

    def wait():
        local, sends, arrivals = copies()
        for cp in arrivals:
            cp.wait_recv()
        for cp in sends:
            cp.wait_send()
        for cp in local:
            cp.wait()

    return start, wait


def sum_blocks(x, *, name):
    _, rows, cols = x.shape
    tr = _tile(rows, 256, PACK_ROWS)

    def body(x_ref, o_ref):
        acc = x_ref[0].astype(F32)
        for s in range(1, N_DEV):
            acc = acc + x_ref[s].astype(F32)
        o_ref[...] = acc

    return pl.pallas_call(
        body, name=name, grid=(rows // tr,), in_specs=[pl.BlockSpec((N_DEV, tr, cols), lambda i: (0, i, 0))],
        out_specs=pl.BlockSpec((tr, cols), lambda i: (i, 0)), out_shape=jax.ShapeDtypeStruct((rows, cols), F32),
        compiler_params=_cparams(("parallel",)),
    )(x)


def adamw(w, parts, m, v, *, name):
    rows, cols = w.shape
    nparts = parts.shape[0]
    tr = _tile(rows, 256, 16)
    c1 = 1.0 / (1.0 - ADAM_B1 ** ADAM_STEP)
    c2 = 1.0 / (1.0 - ADAM_B2 ** ADAM_STEP)

    def body(w_ref, p_ref, m_ref, v_ref, g_ref, d_ref, mo_ref, vo_ref):
        gv = p_ref[0].astype(F32)
        for s in range(1, nparts):
            gv = gv + p_ref[s].astype(F32)
        mn = ADAM_B1 * m_ref[...] + (1.0 - ADAM_B1) * gv
        vn = ADAM_B2 * v_ref[...] + (1.0 - ADAM_B2) * (gv * gv)
        g_ref[...] = gv
        d_ref[...] = -ADAM_LR * ((mn * c1) / (jnp.sqrt(vn * c2) + ADAM_EPS) + ADAM_WD * w_ref[...])
        mo_ref[...] = mn
        vo_ref[...] = vn

    spec = pl.BlockSpec((tr, cols), lambda i: (i, 0))
    return pl.pallas_call(
        body, name=name, grid=(rows // tr,), in_specs=[spec, pl.BlockSpec((nparts, tr, cols), lambda i: (0, i, 0)), spec, spec],
        out_specs=[spec] * 4, out_shape=[jax.ShapeDtypeStruct((rows, cols), F32)] * 4, compiler_params=_cparams(("parallel",)),
    )(w, parts, m, v)


def _pack(arrays, dtype):
    flat = jnp.concatenate([a.astype(dtype).reshape(-1) for a in arrays])
    unit = PACK_COLS * PACK_ROWS
    total = -(-flat.shape[0] // unit) * unit
    return jnp.pad(flat, (0, total - flat.shape[0])).reshape(total // PACK_COLS, PACK_COLS)


def _unpack(flat2d, shapes, lead=()):
    flat = flat2d.reshape(lead + (-1,))
    out, off = [], 0
    for shp in shapes:
        size = 1
        for s in shp:
            size *= s
        out.append(flat[..., off:off + size].reshape(lead + tuple(shp)))
        off += size
    return out


def _join_shards(sh, axis):
    return jnp.concatenate([sh[d] for d in range(N_DEV)], axis=axis)


BIG = (("ab_w_in", "ab_w_in", None, False), ("c_w_in", "c_w_in", None, False), ("ab_w_out", "ab_w_out", None, True),
       ("c_w_out", "c_w_out", None, True), ("ple_w_gate0", "ple_w_gate", 0, True), ("ple_w_gate1", "ple_w_gate", 1, True),
       ("ple_w_proj0", "ple_w_proj", 0, False), ("ple_w_proj1", "ple_w_proj", 1, False))
SMALL_SHARDED = (("rwkv_w2", 1), ("rwkv_a2", 1), ("c_ln_g", 0), ("c_ln_b", 0))
REPLICATED = ("norm_pre", "norm_post", "fox_f_bias", "rwkv_mu", "rwkv_w0", "rwkv_a0", "rwkv_k_k", "rwkv_k_a", "rwkv_r_k",
              "rwkv_ln_g", "rwkv_ln_b", "c_w_s", "c_b_s")
WEIGHTS = ("norm_pre", "norm_post", "ab_w_in", "fox_f_bias", "rwkv_mu", "rwkv_w0", "rwkv_w2", "rwkv_a0", "rwkv_a2", "rwkv_k_k",
           "rwkv_k_a", "rwkv_r_k", "rwkv_ln_g", "rwkv_ln_b", "ab_w_out", "c_w_in", "c_ln_g", "c_ln_b", "c_w_s", "c_b_s",
           "c_w_out", "ple_w_proj", "ple_w_gate")
SQUEEZED = ("norm_pre", "norm_post", "ple_w_proj", "ple_w_gate")


def _step(x, p, loss_target, w, mom, vel):
    sq = {k: (a if k in SQUEEZED else a[0]) for k, a in w.items()}
    _, _, _, me = _mesh_place()

    wid, heads = x.shape[-1] // 2, x.shape[-1] // 2 // HEAD
    layer = lambda a, l: a if l is None else a[l]

    shards = {piece: layer(sq[wname], l).astype(BF16) for piece, wname, l, _ in BIG}
    even = [entry for entry in BIG if entry[0] not in ODD_PIECES]
    sends = [shards[piece] for piece, _, _, _ in even]
    sends += [sq[k] if sq[k].ndim == 2 else sq[k].reshape(1, -1) for k, _ in SMALL_SHARDED]
    gathered = _gather(sends, name="gather_weights")
    full = {}
    for (piece, _, _, by_rows), got in zip(even, gathered):
        if piece == "ab_w_in":
            full["ab_w_in_r"] = _ab_reorder([got[d] for d in range(N_DEV)], wid, heads)
        else:
            full[piece] = _join_shards(got, 1 - by_rows)
    for (k, ax), got in zip(SMALL_SHARDED, gathered[len(even):]):
        full[k] = _join_shards(got, 1).reshape(-1) if sq[k].ndim == 1 else _join_shards(got, 1)
    for k in REPLICATED:
        full[k] = sq[k]

    loss, grad_x, g, parts = local_step(x, p, loss_target, full, N_DEV, {piece: shards[piece] for piece in ODD_PIECES})

    small_names = [k for k, _ in SMALL_SHARDED] + list(REPLICATED)
    small_full_shapes = [full[k].shape for k in small_names] + [(1, 1)]
    partial = _gather([_pack([g[k].reshape(full[k].shape) for k in small_names] + [loss], F32)], name="gather_small_grads")[0]
    summed = _unpack(sum_blocks(partial, name="sum_small_grads"), small_full_shapes)
    small_grads = dict(zip(small_names, summed[:-1]))
    for k, ax in SMALL_SHARDED:
        width = sq[k].shape[ax]
        small_grads[k] = lax.dynamic_slice_in_dim(small_grads[k], me * width, width, axis=ax)
    loss_all = summed[-1][0, 0]

    outs_g, outs_d, outs_m, outs_v = [], [], [], []
    for k in WEIGHTS:
        shape = w[k].shape
        two_d = lambda a: a.reshape(-1, a.shape[-1])
        pieces = [(piece, l) for piece, wname, l, _ in BIG if wname == k]
        if not pieces:
            res = adamw(two_d(w[k]), two_d(small_grads[k].reshape(shape))[None], two_d(mom[k]), two_d(vel[k]), name="adamw_" + k)
        elif pieces[0][1] is None:
            res = adamw(two_d(w[k]), parts[k], two_d(mom[k]), two_d(vel[k]), name="adamw_" + k)
        else:
            per_layer = [adamw(w[k][l], parts[piece], mom[k][l], vel[k][l], name="adamw_" + piece) for piece, l in pieces]
            res = [jnp.stack(r) for r in zip(*per_layer)]
        for out, r in zip((outs_g, outs_d, outs_m, outs_v), res):
            out.append(r.reshape(shape))
    return (loss_all, grad_x, *outs_g, *outs_d, *outs_m, *outs_v)


def kernel(x, p, norm_pre, norm_post, ab_w_in, fox_f_bias, rwkv_mu, rwkv_w0, rwkv_w2, rwkv_a0, rwkv_a2, rwkv_k_k, rwkv_k_a, rwkv_r_k, rwkv_ln_g, rwkv_ln_b, ab_w_out, c_w_in, c_ln_g, c_ln_b, c_w_s, c_b_s, c_w_out, ple_w_proj, ple_w_gate, loss_target, m_norm_pre, m_norm_post, m_ab_w_in, m_fox_f_bias, m_rwkv_mu, m_rwkv_w0, m_rwkv_w2, m_rwkv_a0, m_rwkv_a2, m_rwkv_k_k, m_rwkv_k_a, m_rwkv_r_k, m_rwkv_ln_g, m_rwkv_ln_b, m_ab_w_out, m_c_w_in, m_c_ln_g, m_c_ln_b, m_c_w_s, m_c_b_s, m_c_w_out, m_ple_w_proj, m_ple_w_gate, v_norm_pre, v_norm_post, v_ab_w_in, v_fox_f_bias, v_rwkv_mu, v_rwkv_w0, v_rwkv_w2, v_rwkv_a0, v_rwkv_a2, v_rwkv_k_k, v_rwkv_k_a, v_rwkv_r_k, v_rwkv_ln_g, v_rwkv_ln_b, v_ab_w_out, v_c_w_in, v_c_ln_g, v_c_ln_b, v_c_w_s, v_c_b_s, v_c_w_out, v_ple_w_proj, v_ple_w_gate):
    w = dict(norm_pre=norm_pre, norm_post=norm_post, ab_w_in=ab_w_in, fox_f_bias=fox_f_bias, rwkv_mu=rwkv_mu, rwkv_w0=rwkv_w0, rwkv_w2=rwkv_w2, rwkv_a0=rwkv_a0, rwkv_a2=rwkv_a2, rwkv_k_k=rwkv_k_k, rwkv_k_a=rwkv_k_a, rwkv_r_k=rwkv_r_k, rwkv_ln_g=rwkv_ln_g, rwkv_ln_b=rwkv_ln_b, ab_w_out=ab_w_out, c_w_in=c_w_in, c_ln_g=c_ln_g, c_ln_b=c_ln_b, c_w_s=c_w_s, c_b_s=c_b_s, c_w_out=c_w_out, ple_w_proj=ple_w_proj, ple_w_gate=ple_w_gate)
    mom = dict(norm_pre=m_norm_pre, norm_post=m_norm_post, ab_w_in=m_ab_w_in, fox_f_bias=m_fox_f_bias, rwkv_mu=m_rwkv_mu, rwkv_w0=m_rwkv_w0, rwkv_w2=m_rwkv_w2, rwkv_a0=m_rwkv_a0, rwkv_a2=m_rwkv_a2, rwkv_k_k=m_rwkv_k_k, rwkv_k_a=m_rwkv_k_a, rwkv_r_k=m_rwkv_r_k, rwkv_ln_g=m_rwkv_ln_g, rwkv_ln_b=m_rwkv_ln_b, ab_w_out=m_ab_w_out, c_w_in=m_c_w_in, c_ln_g=m_c_ln_g, c_ln_b=m_c_ln_b, c_w_s=m_c_w_s, c_b_s=m_c_b_s, c_w_out=m_c_w_out, ple_w_proj=m_ple_w_proj, ple_w_gate=m_ple_w_gate)
    vel = dict(norm_pre=v_norm_pre, norm_post=v_norm_post, ab_w_in=v_ab_w_in, fox_f_bias=v_fox_f_bias, rwkv_mu=v_rwkv_mu, rwkv_w0=v_rwkv_w0, rwkv_w2=v_rwkv_w2, rwkv_a0=v_rwkv_a0, rwkv_a2=v_rwkv_a2, rwkv_k_k=v_rwkv_k_k, rwkv_k_a=v_rwkv_k_a, rwkv_r_k=v_rwkv_r_k, rwkv_ln_g=v_rwkv_ln_g, rwkv_ln_b=v_rwkv_ln_b, ab_w_out=v_ab_w_out, c_w_in=v_c_w_in, c_ln_g=v_c_ln_g, c_ln_b=v_c_ln_b, c_w_s=v_c_w_s, c_b_s=v_c_b_s, c_w_out=v_c_w_out, ple_w_proj=v_ple_w_proj, ple_w_gate=v_ple_w_gate)
    return _step(x, p, loss_target, w, mom, vel)
```
